```python
import math
import jax, jax.numpy as jnp
from jax import lax
import numpy as np

D_MODEL = 1024
BATCH = 16
SEQ = 256
DEPTH = 4
DEC_BATCH = 2
DEC_SEQ = 1024
PAST_LEN = 512

GRID_W = 64
HEAD_DIM = 64
NA_HEADS = 4
NA_KH = 8
NA_KW = 16
DIFF_HEADS = 4
DIFF_QK_DIM = 32
DIFF_V_DIM = 2 * DIFF_QK_DIM
FNET_GROUPS = 4
FNET_GROUP_DIM = 64
SWA_HEADS = 4
SWA_KV_HEADS = 2
SWA_WINDOW = 128
D_FF = 4 * D_MODEL
Q_BLOCK = 128
ROPE_BASE = 10000.0
NORM_EPS = 1e-6
NEG_INF = -1e30

NA_WIDTH = NA_HEADS * HEAD_DIM
DIFF_QK_WIDTH = DIFF_HEADS * 2 * DIFF_QK_DIM
DIFF_WIDTH = DIFF_HEADS * DIFF_V_DIM
FNET_WIDTH = FNET_GROUPS * FNET_GROUP_DIM
SWA_WIDTH = SWA_HEADS * HEAD_DIM
SWA_KV_WIDTH = SWA_KV_HEADS * HEAD_DIM
MIX_WIDTH = NA_WIDTH + DIFF_WIDTH + FNET_WIDTH + SWA_WIDTH
IN_WIDTH = 3 * NA_WIDTH + 2 * DIFF_QK_WIDTH + DIFF_WIDTH + FNET_WIDTH + SWA_WIDTH + 2 * SWA_KV_WIDTH

kernel_name = 'hybrid_diffusion_prefix_trunk_step'


def rmsnorm(x, g):
    xf = x.astype(jnp.float32)
    y = xf * lax.rsqrt(jnp.mean(xf * xf, axis=-1, keepdims=True) + NORM_EPS)
    return (y * g.astype(jnp.float32)).astype(x.dtype)


def modulate(h, shift, scale):
    return h * (1.0 + scale) + shift


def adaln(cond, w, b):
    m = jax.nn.silu(cond) @ w + b
    return jnp.split(m, 6, axis=-1)


def rope_1d(x, pos):
    half = x.shape[-1] // 2
    inv = ROPE_BASE ** (-jnp.arange(half, dtype=jnp.float32) / half)
    ang = pos.astype(jnp.float32)[:, None] * inv[None, :]
    cos = jnp.cos(ang)[:, None, :]
    sin = jnp.sin(ang)[:, None, :]
    xf = x.astype(jnp.float32)
    x1, x2 = xf[..., :half], xf[..., half:]
    return jnp.concatenate([x1 * cos - x2 * sin, x2 * cos + x1 * sin], axis=-1).astype(x.dtype)


def rope_2d(x):
    t = jnp.arange(x.shape[1])
    n = x.shape[-1] // 2
    return jnp.concatenate([rope_1d(x[..., :n], t // GRID_W), rope_1d(x[..., n:], t % GRID_W)], axis=-1)


def split_proj(z):
    B, L = z.shape[:2]
    sizes = (NA_WIDTH, NA_WIDTH, NA_WIDTH, DIFF_QK_WIDTH, DIFF_QK_WIDTH, DIFF_WIDTH,
             FNET_WIDTH, SWA_WIDTH, SWA_KV_WIDTH, SWA_KV_WIDTH)
    cuts = [int(c) for c in np.cumsum(sizes)[:-1]]
    na_q, na_k, na_v, dq, dk, dv, fc, sq, sk, sv = jnp.split(z, cuts, axis=-1)
    return (na_q.reshape(B, L, NA_HEADS, HEAD_DIM), na_k.reshape(B, L, NA_HEADS, HEAD_DIM),
            na_v.reshape(B, L, NA_HEADS, HEAD_DIM),
            dq.reshape(B, L, DIFF_HEADS, 2 * DIFF_QK_DIM), dk.reshape(B, L, DIFF_HEADS, 2 * DIFF_QK_DIM),
            dv.reshape(B, L, DIFF_HEADS, DIFF_V_DIM), fc,
            sq.reshape(B, L, SWA_HEADS, HEAD_DIM), sk.reshape(B, L, SWA_KV_HEADS, HEAD_DIM),
            sv.reshape(B, L, SWA_KV_HEADS, HEAD_DIM))


def dense_attn(q, k, v, sink):
    B, Lq, Hq, d = q.shape
    Hkv = k.shape[2]
    G = Hq // Hkv
    nb = Lq // Q_BLOCK
    scale = d ** -0.5
    qb = q.reshape(B, nb, Q_BLOCK, Hkv, G, d).transpose(1, 0, 2, 3, 4, 5)

    def one(qblk):
        s = jnp.einsum('bqhgd,bkhd->bhgqk', qblk, k).astype(jnp.float32) * scale
        if sink is not None:
            s_sink = jnp.broadcast_to(sink.astype(jnp.float32).reshape(1, Hkv, G, 1, 1), s.shape[:-1] + (1,))
            s = jnp.concatenate([s, s_sink], axis=-1)
        p = jax.nn.softmax(s, axis=-1)
        if sink is not None:
            p = p[..., :-1]
        return jnp.einsum('bhgqk,bkhd->bqhgd', p.astype(v.dtype), v)

    out = lax.map(one, qb)
    return out.transpose(1, 0, 2, 3, 4, 5).reshape(B, Lq, Hq, v.shape[-1])


def diff_lambda(lq1, lk1, lq2, lk2, lambda_init):
    f = jnp.float32
    return (jnp.exp(jnp.sum(lq1.astype(f) * lk1.astype(f)))
            - jnp.exp(jnp.sum(lq2.astype(f) * lk2.astype(f))) + lambda_init)


def diff_attn(q1, q2, k1, k2, v, lam):
    B, L, H, d = q1.shape
    nb = L // Q_BLOCK
    scale = d ** -0.5

    def to_blocks(t):
        return t.reshape(B, nb, Q_BLOCK, H, d).transpose(1, 0, 2, 3, 4)

    def one(qs):
        a, b = qs
        p1 = jax.nn.softmax(jnp.einsum('bqhd,bkhd->bhqk', a, k1).astype(jnp.float32) * scale, axis=-1)
        p2 = jax.nn.softmax(jnp.einsum('bqhd,bkhd->bhqk', b, k2).astype(jnp.float32) * scale, axis=-1)
        p = p1 - lam * p2
        return jnp.einsum('bhqk,bkhe->bqhe', p.astype(v.dtype), v)

    out = lax.map(one, (to_blocks(q1), to_blocks(q2)))
    return out.transpose(1, 0, 2, 3, 4).reshape(B, L, H, v.shape[-1])


def fourier_mix(h, w):
    B, L, _ = h.shape
    hf = h.astype(jnp.float32).reshape(B, L, FNET_GROUPS, FNET_GROUP_DIM)
    y = jnp.fft.fft2(hf, axes=(1, 3), norm='ortho').real.astype(h.dtype)
    return y.reshape(B, L, FNET_WIDTH) @ w


def na_latent(q, k, v, kc, vc, rpb):
    B, N, H, d = q.shape
    rows = N // GRID_W
    kh = min(NA_KH, rows)
    scale = d ** -0.5
    qg = q.reshape(B, rows, GRID_W, H, d)
    kg = k.reshape(B, rows, GRID_W, H, d)
    vg = v.reshape(B, rows, GRID_W, H, d)
    r = jnp.arange(rows)
    rstart = jnp.clip(r - kh // 2, 0, rows - kh)
    ridx = rstart[:, None] + jnp.arange(kh)[None, :]
    kr = kg[:, ridx]
    vr = vg[:, ridx]
    col = jnp.arange(GRID_W)
    cstart = jnp.clip(col - NA_KW // 2, 0, GRID_W - NA_KW)
    cmask = (col[None, :] >= cstart[:, None]) & (col[None, :] < cstart[:, None] + NA_KW)
    dr_i = ridx - r[:, None] + (NA_KH - 1)
    dc_i = jnp.clip(col[None, :] - col[:, None] + (NA_KW - 1), 0, 2 * NA_KW - 2)
    bias = rpb.astype(jnp.float32)[:, dr_i[:, None, :, None], dc_i[None, :, None, :]]
    s_loc = jnp.einsum('brchd,brkwhd->bhrckw', qg, kr).astype(jnp.float32) * scale + bias[None]
    n_loc = kh * GRID_W
    s_loc = jnp.where(cmask[:, None, :], s_loc, NEG_INF).reshape(B, H, rows, GRID_W, n_loc)
    s_ctx = jnp.einsum('brchd,bjhd->bhrcj', qg, kc).astype(jnp.float32) * scale
    p = jax.nn.softmax(jnp.concatenate([s_loc, s_ctx], axis=-1), axis=-1)
    p_loc = p[..., :n_loc].reshape(B, H, rows, GRID_W, kh, GRID_W).astype(v.dtype)
    p_ctx = p[..., n_loc:].astype(v.dtype)
    out = (jnp.einsum('bhrckw,brkwhd->brchd', p_loc, vr)
           + jnp.einsum('bhrcj,bjhd->brchd', p_ctx, vc))
    return out.reshape(B, N, H, d)


def swa_latent(q, k, v, kc, vc, sink):
    B, N, Hq, d = q.shape
    Hkv = k.shape[2]
    G = Hq // Hkv
    W = SWA_WINDOW
    nb = N // W
    Lc = kc.shape[1]
    scale = d ** -0.5
    qb = q.reshape(B, nb, W, Hkv, G, d)
    pad = ((0, 0), (W, W), (0, 0), (0, 0))
    kp = jnp.pad(k, pad)
    vp = jnp.pad(v, pad)
    idx = jnp.arange(nb)[:, None] * W + jnp.arange(3 * W)[None, :]
    kb = kp[:, idx]
    vb = vp[:, idx]
    qpos = jnp.arange(nb)[:, None] * W + jnp.arange(W)[None, :]
    kpos = idx - W
    rel = qpos[:, :, None] - kpos[:, None, :]
    valid = (jnp.abs(rel) <= W) & (kpos[:, None, :] >= 0) & (kpos[:, None, :] < N)
    s_loc = jnp.einsum('bnqhgd,bnkhd->bnhgqk', qb, kb).astype(jnp.float32) * scale
    s_loc = jnp.where(valid[None, :, None, None], s_loc, NEG_INF)
    s_ctx = jnp.einsum('bnqhgd,bjhd->bnhgqj', qb, kc).astype(jnp.float32) * scale
    s_sink = jnp.broadcast_to(sink.astype(jnp.float32).reshape(1, 1, Hkv, G, 1, 1), s_loc.shape[:-1] + (1,))
    p = jax.nn.softmax(jnp.concatenate([s_loc, s_ctx, s_sink], axis=-1), axis=-1)
    p_loc = p[..., :3 * W].astype(v.dtype)
    p_ctx = p[..., 3 * W:3 * W + Lc].astype(v.dtype)
    out = (jnp.einsum('bnhgqk,bnkhd->bnqhgd', p_loc, vb)
           + jnp.einsum('bnhgqj,bjhd->bnqhgd', p_ctx, vc))
    return out.reshape(B, N, Hq, d)


def merge_heads(o_a, o_b, o_c, o_d, w_out):
    B, L = o_c.shape[:2]
    cat = jnp.concatenate([o_a.reshape(B, L, NA_WIDTH), o_b.reshape(B, L, DIFF_WIDTH), o_c,
                           o_d.reshape(B, L, SWA_WIDTH)], axis=-1)
    return cat @ w_out


def mix_context(h, lw):
    na_q, na_k, na_v, dq, dk, dv, fc, sq, sk, sv = split_proj(h @ lw['w_in'])
    o_a = dense_attn(na_q, na_k, na_v, None)
    o_b = diff_attn(dq[..., :DIFF_QK_DIM], dq[..., DIFF_QK_DIM:], dk[..., :DIFF_QK_DIM], dk[..., DIFF_QK_DIM:],
                    dv, lw['lam'])
    o_b = rmsnorm(o_b, lw['subln_g']) * (1.0 - lw['lam_init'])
    o_c = fourier_mix(fc, lw['w_fourier'])
    o_d = dense_attn(sq, sk, sv, lw['sink'])
    return merge_heads(o_a, o_b, o_c, o_d, lw['w_out']), (na_k, na_v, dk, dv, sk, sv)


def mix_latent(h, lw, ck):
    c_na_k, c_na_v, c_diff_k, c_diff_v, c_swa_k, c_swa_v = ck
    na_q, na_k, na_v, dq, dk, dv, fc, sq, sk, sv = split_proj(h @ lw['w_in'])
    o_a = na_latent(na_q, na_k, na_v, c_na_k, c_na_v, lw['rpb'])
    q1 = rope_2d(dq[..., :DIFF_QK_DIM])
    q2 = rope_2d(dq[..., DIFF_QK_DIM:])
    k1 = jnp.concatenate([rope_2d(dk[..., :DIFF_QK_DIM]), c_diff_k[..., :DIFF_QK_DIM]], axis=1)
    k2 = jnp.concatenate([rope_2d(dk[..., DIFF_QK_DIM:]), c_diff_k[..., DIFF_QK_DIM:]], axis=1)
    va = jnp.concatenate([dv, c_diff_v], axis=1)
    o_b = diff_attn(q1, q2, k1, k2, va, lw['lam'])
    o_b = rmsnorm(o_b, lw['subln_g']) * (1.0 - lw['lam_init'])
    o_c = fourier_mix(fc, lw['w_fourier'])
    o_d = swa_latent(rope_2d(sq), rope_2d(sk), sv, c_swa_k, c_swa_v, lw['sink'])
    return merge_heads(o_a, o_b, o_c, o_d, lw['w_out']), ()


def trunk_block(x, mods, g1, g2, w1, w2, mixer):
    sh1, sc1, gt1, sh2, sc2, gt2 = mods
    mixed, ctx_tensors = mixer(modulate(rmsnorm(x, g1), sh1, sc1))
    x = x + gt1 * mixed
    hh = modulate(rmsnorm(x, g2), sh2, sc2)
    x = x + gt2 * (jnp.square(jax.nn.relu(hh @ w1)) @ w2)
    return x, ctx_tensors


def setup_inputs(seed: int = 0) -> dict:
    key = jax.random.key(seed)
    ks = jax.random.split(key, 32)
    D = D_MODEL

    def nrm(k, shape, scale=1.0):
        return jax.random.normal(k, shape, jnp.float32) * scale

    return {
        'x_prompt': nrm(ks[0], (BATCH, SEQ, D)),
        'x_sample': nrm(ks[1], (DEC_BATCH, DEC_SEQ, D)),
        'cache_na_k': nrm(ks[2], (DEC_BATCH, DEPTH, PAST_LEN, NA_HEADS, HEAD_DIM)),
        'cache_na_v': nrm(ks[3], (DEC_BATCH, DEPTH, PAST_LEN, NA_HEADS, HEAD_DIM)),
        'cache_diff_k': nrm(ks[4], (DEC_BATCH, DEPTH, PAST_LEN, DIFF_HEADS, 2 * DIFF_QK_DIM)),
        'cache_diff_v': nrm(ks[5], (DEC_BATCH, DEPTH, PAST_LEN, DIFF_HEADS, DIFF_V_DIM)),
        'cache_swa_k': nrm(ks[6], (DEC_BATCH, DEPTH, PAST_LEN, SWA_KV_HEADS, HEAD_DIM)),
        'cache_swa_v': nrm(ks[7], (DEC_BATCH, DEPTH, PAST_LEN, SWA_KV_HEADS, HEAD_DIM)),
        'c': nrm(ks[8], (DEC_BATCH, D)),
        'c_ctx': nrm(ks[9], (D,)),
        'w_ada': nrm(ks[10], (DEPTH, D, 6 * D), D ** -0.5),
        'b_ada': nrm(ks[11], (DEPTH, 6 * D), 0.02),
        'norm1_g': 1.0 + nrm(ks[12], (DEPTH, D), 0.02),
        'norm2_g': 1.0 + nrm(ks[13], (DEPTH, D), 0.02),
        'w_in': nrm(ks[14], (DEPTH, D, IN_WIDTH), D ** -0.5),
        'na_rpb': nrm(ks[15], (DEPTH, NA_HEADS, 2 * NA_KH - 1, 2 * NA_KW - 1), 0.1),
        'diff_lq1': nrm(ks[16], (DEPTH, DIFF_QK_DIM), 0.1),
        'diff_lk1': nrm(ks[17], (DEPTH, DIFF_QK_DIM), 0.1),
        'diff_lq2': nrm(ks[18], (DEPTH, DIFF_QK_DIM), 0.1),
        'diff_lk2': nrm(ks[19], (DEPTH, DIFF_QK_DIM), 0.1),
        'diff_subln_g': 1.0 + nrm(ks[20], (DEPTH, DIFF_V_DIM), 0.02),
        'w_fourier': nrm(ks[21], (DEPTH, FNET_WIDTH, FNET_WIDTH), FNET_WIDTH ** -0.5),
        'swa_sink': nrm(ks[22], (DEPTH, SWA_HEADS), 0.5),
        'w_out': nrm(ks[23], (DEPTH, MIX_WIDTH, D), MIX_WIDTH ** -0.5),
        'w_mlp1': nrm(ks[24], (DEPTH, D, D_FF), D ** -0.5),
        'w_mlp2': nrm(ks[25], (DEPTH, D_FF, D), D_FF ** -0.5),
        'final_g': 1.0 + nrm(ks[26], (D,), 0.02),
    }


def reference(x_prompt, x_sample, cache_na_k, cache_na_v, cache_diff_k, cache_diff_v, cache_swa_k, cache_swa_v,
              c, c_ctx, w_ada, b_ada, norm1_g, norm2_g, w_in, na_rpb, diff_lq1, diff_lk1, diff_lq2, diff_lk2,
              diff_subln_g, w_fourier, swa_sink, w_out, w_mlp1, w_mlp2, final_g):
    xp = x_prompt
    xs = x_sample
    ctx_out = []
    for l in range(DEPTH):
        lam_init = 0.8 - 0.6 * math.exp(-0.3 * l)
        lw = {
            'w_in': w_in[l], 'rpb': na_rpb[l],
            'lam': diff_lambda(diff_lq1[l], diff_lk1[l], diff_lq2[l], diff_lk2[l], lam_init),
            'lam_init': lam_init, 'subln_g': diff_subln_g[l], 'w_fourier': w_fourier[l],
            'sink': swa_sink[l], 'w_out': w_out[l],
        }
        mods_p = adaln(c_ctx, w_ada[l], b_ada[l])
        xp, kv = trunk_block(xp, mods_p, norm1_g[l], norm2_g[l], w_mlp1[l], w_mlp2[l],
                             lambda h: mix_context(h, lw))
        ctx_out.append(kv)
        mods_s = [m[:, None, :] for m in adaln(c, w_ada[l], b_ada[l])]
        ck = (cache_na_k[:, l], cache_na_v[:, l], cache_diff_k[:, l], cache_diff_v[:, l],
              cache_swa_k[:, l], cache_swa_v[:, l])
        xs, _ = trunk_block(xs, mods_s, norm1_g[l], norm2_g[l], w_mlp1[l], w_mlp2[l],
                            lambda h: mix_latent(h, lw, ck))
    y_prompt = rmsnorm(xp, final_g)
    y_sample = rmsnorm(xs, final_g)
    new_na_k = jnp.stack([t[0] for t in ctx_out], axis=1)
    new_na_v = jnp.stack([t[1] for t in ctx_out], axis=1)
    new_diff_k = jnp.stack([t[2] for t in ctx_out], axis=1)
    new_diff_v = jnp.stack([t[3] for t in ctx_out], axis=1)
    new_swa_k = jnp.stack([t[4] for t in ctx_out], axis=1)
    new_swa_v = jnp.stack([t[5] for t in ctx_out], axis=1)
    return (y_prompt, y_sample, new_na_k, new_na_v, new_diff_k, new_diff_v, new_swa_k, new_swa_v)
```

```python
import functools
import math

import numpy as np
import jax
import jax.numpy as jnp
from jax import lax
from jax.experimental import pallas as pl
from jax.experimental.pallas import tpu as pltpu

D_MODEL = 1024
BATCH = 16
SEQ = 256
DEPTH = 4
DEC_BATCH = 2
DEC_SEQ = 1024
PAST_LEN = 512
GRID_W = 64
GRID_ROWS = DEC_SEQ // GRID_W
HEAD_DIM = 64
NA_KH = 8
NA_KW = 16
DIFF_QK_DIM = 32
SWA_WINDOW = 128
D_FF = 4 * D_MODEL
ROPE_BASE = 10000.0
NORM_EPS = 1e-6
NEG_INF = -1e30
IN_WIDTH = 2304
MIX_WIDTH = 1024

N_CTX_TOK = BATCH * SEQ
N_LAT_TOK = DEC_BATCH * DEC_SEQ
N_TOK = N_CTX_TOK + N_LAT_TOK
ROW_TILE = 256
N_ROW_TILES = N_TOK // ROW_TILE
N_CTX_TILES = N_CTX_TOK // ROW_TILE
LAT_TILES_PER_REQ = DEC_SEQ // ROW_TILE
MOD_ROWS = 8

C_NA_Q, C_NA_K, C_NA_V = 0, 256, 512
C_DQ, C_DK, C_DV = 768, 1024, 1280
C_FC = 1536
C_SQ, C_SK, C_SV = 1792, 2048, 2176

BF = jnp.bfloat16
F32 = jnp.float32
VMEM_LIMIT = 56 * 1024 * 1024


def _dot(a, b):
    return jnp.dot(a, b, preferred_element_type=F32)


def _dot_nt(a, b):
    return lax.dot_general(a, b, (((1,), (1,)), ((), ())), preferred_element_type=F32)


def _rmsnorm(x, g):
    ms = jnp.mean(x * x, axis=-1, keepdims=True)
    return x * lax.rsqrt(ms + NORM_EPS) * g


def _mod_row(i):
    return jnp.where(i < N_CTX_TILES, 0, 1 + (i - N_CTX_TILES) // LAT_TILES_PER_REQ)


def _params(n_grid):
    return pltpu.CompilerParams(dimension_semantics=("arbitrary",) * n_grid, vmem_limit_bytes=VMEM_LIMIT)


def _ada_kernel(cond_ref, w_ref, b_ref, o_ref):
    cnd = cond_ref[...]
    s = cnd / (1.0 + jnp.exp(-cnd))
    o_ref[...] = _dot(s.astype(BF), w_ref[...].astype(BF)) + b_ref[...]


def _ada(cond, w_ada, b_ada):
    tn = 1024
    return pl.pallas_call(
        _ada_kernel,
        grid=(DEPTH, 6 * D_MODEL // tn),
        in_specs=[
            pl.BlockSpec((MOD_ROWS, D_MODEL), lambda l, j: (0, 0)),
            pl.BlockSpec((None, D_MODEL, tn), lambda l, j: (l, 0, j)),
            pl.BlockSpec((None, 1, tn), lambda l, j: (l, 0, j)),
        ],
        out_specs=pl.BlockSpec((None, MOD_ROWS, tn), lambda l, j: (l, 0, j)),
        out_shape=jax.ShapeDtypeStruct((DEPTH, MOD_ROWS, 6 * D_MODEL), F32),
        compiler_params=_params(2),
        name="ada",
    )(cond, w_ada, b_ada.reshape(DEPTH, 1, 6 * D_MODEL))


def _proj_in_kernel(x_ref, m_ref, g_ref, w_ref, z_ref):
    y = _rmsnorm(x_ref[...], g_ref[...])
    h = y * (1.0 + m_ref[:, D_MODEL:2 * D_MODEL]) + m_ref[:, 0:D_MODEL]
    z_ref[...] = _dot(h.astype(BF), w_ref[...])


def _proj_in(x, mods_l, g1, w_in_bf):
    return pl.pallas_call(
        _proj_in_kernel,
        grid=(N_ROW_TILES,),
        in_specs=[
            pl.BlockSpec((ROW_TILE, D_MODEL), lambda i: (i, 0)),
            pl.BlockSpec((None, 1, 6 * D_MODEL), lambda i: (_mod_row(i), 0, 0)),
            pl.BlockSpec((1, D_MODEL), lambda i: (0, 0)),
            pl.BlockSpec((D_MODEL, IN_WIDTH), lambda i: (0, 0)),
        ],
        out_specs=pl.BlockSpec((ROW_TILE, IN_WIDTH), lambda i: (i, 0)),
        out_shape=jax.ShapeDtypeStruct((N_TOK, IN_WIDTH), F32),
        compiler_params=_params(1),
        name="proj_in",
    )(x, mods_l, g1.reshape(1, D_MODEL), w_in_bf)


def _diff_lambda(lamp_ref, lam_init):
    a = jnp.sum(lamp_ref[0:1, :] * lamp_ref[1:2, :], axis=-1, keepdims=True)
    b = jnp.sum(lamp_ref[2:3, :] * lamp_ref[3:4, :], axis=-1, keepdims=True)
    return jnp.exp(a) - jnp.exp(b) + lam_init


def _subln(o, g, lam_init):
    return _rmsnorm(o, g) * (1.0 - lam_init)


def _fourier(x_bf, cl, sl, cbd, sbd, wf):
    xc = _dot(x_bf, cbd).astype(BF)
    xs = _dot(x_bf, sbd).astype(BF)
    y = _dot(cl, xc) - _dot(sl, xs)
    return _dot(y.astype(BF), wf)


def _ctx_mix_kernel(lam_init, sink_ref, z_ref, lamp_ref, subg_ref, cl_ref, sl_ref, cbd_ref, sbd_ref, wf_ref,
                    o_ref):
    for h in range(4):
        q = (z_ref[:, C_NA_Q + 64 * h:C_NA_Q + 64 * h + 64] * 0.125).astype(BF)
        k = z_ref[:, C_NA_K + 64 * h:C_NA_K + 64 * h + 64].astype(BF)
        v = z_ref[:, C_NA_V + 64 * h:C_NA_V + 64 * h + 64].astype(BF)
        s = _dot_nt(q, k)
        e = jnp.exp(s - jnp.max(s, axis=-1, keepdims=True))
        o = _dot(e.astype(BF), v) / jnp.sum(e, axis=-1, keepdims=True)
        o_ref[:, 64 * h:64 * h + 64] = o.astype(BF)

    lam = _diff_lambda(lamp_ref, lam_init)
    dscale = DIFF_QK_DIM ** -0.5
    for h in range(4):
        c0 = 64 * h
        q1 = (z_ref[:, C_DQ + c0:C_DQ + c0 + 32] * dscale).astype(BF)
        q2 = (z_ref[:, C_DQ + c0 + 32:C_DQ + c0 + 64] * dscale).astype(BF)
        k1 = z_ref[:, C_DK + c0:C_DK + c0 + 32].astype(BF)
        k2 = z_ref[:, C_DK + c0 + 32:C_DK + c0 + 64].astype(BF)
        v = z_ref[:, C_DV + c0:C_DV + c0 + 64].astype(BF)
        s1 = _dot_nt(q1, k1)
        s2 = _dot_nt(q2, k2)
        e1 = jnp.exp(s1 - jnp.max(s1, axis=-1, keepdims=True))
        e2 = jnp.exp(s2 - jnp.max(s2, axis=-1, keepdims=True))
        r1 = 1.0 / jnp.sum(e1, axis=-1, keepdims=True)
        r2 = lam / jnp.sum(e2, axis=-1, keepdims=True)
        p = e1 * r1 - e2 * r2
        o = _subln(_dot(p.astype(BF), v), subg_ref[...], lam_init)
        o_ref[:, 256 + c0:256 + c0 + 64] = o.astype(BF)

    x_bf = z_ref[:, C_FC:C_FC + 256].astype(BF)
    o_c = _fourier(x_bf, cl_ref[...], sl_ref[...], cbd_ref[...], sbd_ref[...], wf_ref[...])
    o_ref[:, 512:768] = o_c.astype(BF)

    for h in range(4):
        g = h // 2
        q = (z_ref[:, C_SQ + 64 * h:C_SQ + 64 * h + 64] * 0.125).astype(BF)
        k = z_ref[:, C_SK + 64 * g:C_SK + 64 * g + 64].astype(BF)
        v = z_ref[:, C_SV + 64 * g:C_SV + 64 * g + 64].astype(BF)
        sink = sink_ref[h]
        s = _dot_nt(q, k)
        m = jnp.maximum(jnp.max(s, axis=-1, keepdims=True), sink)
        e = jnp.exp(s - m)
        l = jnp.sum(e, axis=-1, keepdims=True) + jnp.exp(sink - m)
        o = _dot(e.astype(BF), v) / l
        o_ref[:, 768 + 64 * h:768 + 64 * h + 64] = o.astype(BF)


def _ctx_mix(z, sink, lamp, subg, cl, sl, cbd, sbd, wf_bf, lam_init):
    full = lambda shape: pl.BlockSpec(shape, lambda b, s: (0,) * len(shape))
    grid_spec = pltpu.PrefetchScalarGridSpec(
        num_scalar_prefetch=1,
        grid=(BATCH,),
        in_specs=[
            pl.BlockSpec((SEQ, IN_WIDTH), lambda b, s: (b, 0)),
            full((4, DIFF_QK_DIM)),
            full((1, HEAD_DIM)),
            full((SEQ, SEQ)), full((SEQ, SEQ)),
            full((256, 256)), full((256, 256)), full((256, 256)),
        ],
        out_specs=pl.BlockSpec((SEQ, MIX_WIDTH), lambda b, s: (b, 0)),
    )
    return pl.pallas_call(
        functools.partial(_ctx_mix_kernel, lam_init),
        grid_spec=grid_spec,
        out_shape=jax.ShapeDtypeStruct((N_CTX_TOK, MIX_WIDTH), BF),
        compiler_params=_params(1),
        name="ctx_mix",
    )(sink, z, lamp, subg, cl, sl, cbd, sbd, wf_bf)


def _rope(x, cos, sin_signed, shift, first):
    partner = jnp.where(first, pltpu.roll(x, 128 - shift, 1), pltpu.roll(x, shift, 1))
    return x * cos + partner * sin_signed


def _lat_na_kernel(z_ref, ck_ref, cv_ref, nb_ref, o_ref, qb, kb, vb, ckb, cvb):
    qb[...] = (z_ref[:, 0:256] * 0.125).astype(BF)
    kb[...] = z_ref[:, 256:512].astype(BF)
    vb[...] = z_ref[:, 512:768].astype(BF)
    ckb[...] = ck_ref[...].astype(BF)
    cvb[...] = cv_ref[...].astype(BF)
    n_loc = NA_KH * GRID_W
    for h in range(4):
        cs = slice(64 * h, 64 * h + 64)

        def body(r, carry):
            rs = jnp.clip(r - NA_KH // 2, 0, GRID_ROWS - NA_KH)
            q0 = pl.multiple_of(r * GRID_W, GRID_W)
            k0 = pl.multiple_of(rs * GRID_W, GRID_W)
            q = qb[pl.ds(q0, GRID_W), cs]
            s_loc = _dot_nt(q, kb[pl.ds(k0, n_loc), cs]) + nb_ref[h, rs - r + NA_KH - 1]
            s_ctx = _dot_nt(q, ckb[:, cs])
            m = jnp.maximum(jnp.max(s_loc, axis=-1, keepdims=True), jnp.max(s_ctx, axis=-1, keepdims=True))
            e_loc = jnp.exp(s_loc - m)
            e_ctx = jnp.exp(s_ctx - m)
            l = jnp.sum(e_loc, axis=-1, keepdims=True) + jnp.sum(e_ctx, axis=-1, keepdims=True)
            o = _dot(e_loc.astype(BF), vb[pl.ds(k0, n_loc), cs]) + _dot(e_ctx.astype(BF), cvb[:, cs])
            o_ref[pl.ds(q0, GRID_W), cs] = (o / l).astype(BF)
            return carry

        lax.fori_loop(0, GRID_ROWS, body, 0)


def _lat_na(z, cache_k, cache_v, nb, layer):
    n_loc = NA_KH * GRID_W
    return pl.pallas_call(
        _lat_na_kernel,
        grid=(DEC_BATCH,),
        in_specs=[
            pl.BlockSpec((DEC_SEQ, 768), lambda b: (N_CTX_TOK // DEC_SEQ + b, 0)),
            pl.BlockSpec((None, None, PAST_LEN, 256), lambda b: (b, layer, 0, 0)),
            pl.BlockSpec((None, None, PAST_LEN, 256), lambda b: (b, layer, 0, 0)),
            pl.BlockSpec((4, NA_KH, GRID_W, n_loc), lambda b: (0, 0, 0, 0)),
        ],
        out_specs=pl.BlockSpec((DEC_SEQ, 256), lambda b: (b, 0)),
        out_shape=jax.ShapeDtypeStruct((N_LAT_TOK, 256), BF),
        scratch_shapes=[pltpu.VMEM((DEC_SEQ, 256), BF)] * 3 + [pltpu.VMEM((PAST_LEN, 256), BF)] * 2,
        compiler_params=_params(1),
        name="lat_na",
    )(z, cache_k, cache_v, nb)


def _lat_diff_kernel(lam_init, z_ref, ck_ref, cv_ref, cos_ref, sin_ref, lamp_ref, subg_ref, o_ref,
                     qb, kb, vb, ckb, cvb):
    lane = lax.broadcasted_iota(jnp.int32, (DEC_SEQ, 128), 1)
    first = (lane % 16) < 8
    dscale = DIFF_QK_DIM ** -0.5
    for j in range(2):
        cs = slice(128 * j, 128 * j + 128)
        qb[:, cs] = (_rope(z_ref[:, cs], cos_ref[...], sin_ref[...], 8, first) * dscale).astype(BF)
        kb[:, cs] = _rope(z_ref[:, 256 + 128 * j:256 + 128 * j + 128], cos_ref[...], sin_ref[...], 8,
                          first).astype(BF)
    vb[...] = z_ref[:, 512:768].astype(BF)
    ckb[...] = ck_ref[...].astype(BF)
    cvb[...] = cv_ref[...].astype(BF)
    lam = _diff_lambda(lamp_ref, lam_init)
    qblk = 256
    for h in range(4):
        c0 = 64 * h
        s1c = slice(c0, c0 + 32)
        s2c = slice(c0 + 32, c0 + 64)
        vc = slice(c0, c0 + 64)

        def body(i, carry):
            q0 = pl.multiple_of(i * qblk, qblk)
            q1 = qb[pl.ds(q0, qblk), s1c]
            q2 = qb[pl.ds(q0, qblk), s2c]
            s1a = _dot_nt(q1, kb[:, s1c])
            s1b = _dot_nt(q1, ckb[:, s1c])
            s2a = _dot_nt(q2, kb[:, s2c])
            s2b = _dot_nt(q2, ckb[:, s2c])
            m1 = jnp.maximum(jnp.max(s1a, axis=-1, keepdims=True), jnp.max(s1b, axis=-1, keepdims=True))
            m2 = jnp.maximum(jnp.max(s2a, axis=-1, keepdims=True), jnp.max(s2b, axis=-1, keepdims=True))
            e1a = jnp.exp(s1a - m1)
            e1b = jnp.exp(s1b - m1)
            e2a = jnp.exp(s2a - m2)
            e2b = jnp.exp(s2b - m2)
            r1 = 1.0 / (jnp.sum(e1a, axis=-1, keepdims=True) + jnp.sum(e1b, axis=-1, keepdims=True))
            r2 = lam / (jnp.sum(e2a, axis=-1, keepdims=True) + jnp.sum(e2b, axis=-1, keepdims=True))
            pa = (e1a * r1 - e2a * r2).astype(BF)
            pb = (e1b * r1 - e2b * r2).astype(BF)
            o = _dot(pa, vb[:, vc]) + _dot(pb, cvb[:, vc])
            o_ref[pl.ds(q0, qblk), vc] = _subln(o, subg_ref[...], lam_init).astype(BF)
            return carry

        lax.fori_loop(0, DEC_SEQ // qblk, body, 0)


def _lat_diff(z, cache_k, cache_v, cos_t, sin_t, lamp, subg, lam_init, layer):
    full = lambda shape: pl.BlockSpec(shape, lambda b: (0,) * len(shape))
    return pl.pallas_call(
        functools.partial(_lat_diff_kernel, lam_init),
        grid=(DEC_BATCH,),
        in_specs=[
            pl.BlockSpec((DEC_SEQ, 768), lambda b: (N_CTX_TOK // DEC_SEQ + b, 1)),
            pl.BlockSpec((None, None, PAST_LEN, 256), lambda b: (b, layer, 0, 0)),
            pl.BlockSpec((None, None, PAST_LEN, 256), lambda b: (b, layer, 0, 0)),
            full((DEC_SEQ, 128)), full((DEC_SEQ, 128)),
            full((4, DIFF_QK_DIM)), full((1, HEAD_DIM)),
        ],
        out_specs=pl.BlockSpec((DEC_SEQ, 256), lambda b: (b, 0)),
        out_shape=jax.ShapeDtypeStruct((N_LAT_TOK, 256), BF),
        scratch_shapes=[pltpu.VMEM((DEC_SEQ, 256), BF)] * 3 + [pltpu.VMEM((PAST_LEN, 256), BF)] * 2,
        compiler_params=_params(1),
        name="lat_diff",
    )(z, cache_k, cache_v, cos_t, sin_t, lamp, subg)


def _lat_fnet_kernel(z_ref, cl_ref, sl_ref, cbd_ref, sbd_ref, wf_ref, o_ref):
    o = _fourier(z_ref[...].astype(BF), cl_ref[...], sl_ref[...], cbd_ref[...], sbd_ref[...], wf_ref[...])
    o_ref[...] = o.astype(BF)


def _lat_fnet(z, cl, sl, cbd, sbd, wf_bf):
    full = lambda shape: pl.BlockSpec(shape, lambda b: (0,) * len(shape))
    return pl.pallas_call(
        _lat_fnet_kernel,
        grid=(DEC_BATCH,),
        in_specs=[
            pl.BlockSpec((DEC_SEQ, 256), lambda b: (N_CTX_TOK // DEC_SEQ + b, C_FC // 256)),
            full((DEC_SEQ, DEC_SEQ)), full((DEC_SEQ, DEC_SEQ)),
            full((256, 256)), full((256, 256)), full((256, 256)),
        ],
        out_specs=pl.BlockSpec((DEC_SEQ, 256), lambda b: (b, 0)),
        out_shape=jax.ShapeDtypeStruct((N_LAT_TOK, 256), BF),
        compiler_params=_params(1),
        name="lat_fnet",
    )(z, cl, sl, cbd, sbd, wf_bf)


def _lat_swa_kernel(sink_ref, zq_ref, zkv_ref, ck_ref, cv_ref, cos_ref, sin_ref, o_ref, qb, kb, vb, ckb, cvb):
    lane = lax.broadcasted_iota(jnp.int32, (DEC_SEQ, 128), 1)
    first = (lane % 32) < 16
    for j in range(2):
        cs = slice(128 * j, 128 * j + 128)
        qb[:, cs] = (_rope(zq_ref[:, cs], cos_ref[...], sin_ref[...], 16, first) * 0.125).astype(BF)
    kb[...] = _rope(zkv_ref[:, 0:128], cos_ref[...], sin_ref[...], 16, first).astype(BF)
    vb[...] = zkv_ref[:, 128:256].astype(BF)
    ckb[...] = ck_ref[...].astype(BF)
    cvb[...] = cv_ref[...].astype(BF)
    W = SWA_WINDOW
    n_win = 3 * W
    for h in range(4):
        g = h // 2
        qs = slice(64 * h, 64 * h + 64)
        ks = slice(64 * g, 64 * g + 64)
        sink = sink_ref[h]

        def body(n, carry):
            q0 = pl.multiple_of(n * W, W)
            w0 = pl.multiple_of(jnp.clip((n - 1) * W, 0, DEC_SEQ - n_win), W)
            q = qb[pl.ds(q0, W), qs]
            s_loc = _dot_nt(q, kb[pl.ds(w0, n_win), ks])
            qpos = q0 + lax.broadcasted_iota(jnp.int32, (W, n_win), 0)
            kpos = w0 + lax.broadcasted_iota(jnp.int32, (W, n_win), 1)
            s_loc = jnp.where(jnp.abs(qpos - kpos) <= W, s_loc, NEG_INF)
            s_ctx = _dot_nt(q, ckb[:, ks])
            m = jnp.maximum(jnp.max(s_loc, axis=-1, keepdims=True), jnp.max(s_ctx, axis=-1, keepdims=True))
            m = jnp.maximum(m, sink)
            e_loc = jnp.exp(s_loc - m)
            e_ctx = jnp.exp(s_ctx - m)
            l = (jnp.sum(e_loc, axis=-1, keepdims=True) + jnp.sum(e_ctx, axis=-1, keepdims=True)
                 + jnp.exp(sink - m))
            o = _dot(e_loc.astype(BF), vb[pl.ds(w0, n_win), ks]) + _dot(e_ctx.astype(BF), cvb[:, ks])
            o_ref[pl.ds(q0, W), qs] = (o / l).astype(BF)
            return carry

        lax.fori_loop(0, DEC_SEQ // W, body, 0)


def _lat_swa(z, cache_k, cache_v, cos_t, sin_t, sink, layer):
    full = lambda shape: pl.BlockSpec(shape, lambda b, s: (0,) * len(shape))
    grid_spec = pltpu.PrefetchScalarGridSpec(
        num_scalar_prefetch=1,
        grid=(DEC_BATCH,),
        in_specs=[
            pl.BlockSpec((DEC_SEQ, 256), lambda b, s: (N_CTX_TOK // DEC_SEQ + b, C_SQ // 256)),
            pl.BlockSpec((DEC_SEQ, 256), lambda b, s: (N_CTX_TOK // DEC_SEQ + b, C_SK // 256)),
            pl.BlockSpec((None, None, PAST_LEN, 128), lambda b, s: (b, layer, 0, 0)),
            pl.BlockSpec((None, None, PAST_LEN, 128), lambda b, s: (b, layer, 0, 0)),
            full((DEC_SEQ, 128)), full((DEC_SEQ, 128)),
        ],
        out_specs=pl.BlockSpec((DEC_SEQ, 256), lambda b, s: (b, 0)),
        scratch_shapes=[pltpu.VMEM((DEC_SEQ, 256), BF), pltpu.VMEM((DEC_SEQ, 128), BF),
                        pltpu.VMEM((DEC_SEQ, 128), BF), pltpu.VMEM((PAST_LEN, 128), BF),
                        pltpu.VMEM((PAST_LEN, 128), BF)],
    )
    return pl.pallas_call(
        _lat_swa_kernel,
        grid_spec=grid_spec,
        out_shape=jax.ShapeDtypeStruct((N_LAT_TOK, 256), BF),
        compiler_params=_params(1),
        name="lat_swa",
    )(sink, z, z, cache_k, cache_v, cos_t, sin_t)


FF_CHUNK = 1024


def _out_mlp_kernel(x_ref, cat_ref, m_ref, g_ref, wo_ref, w1_ref, w2_ref, o_ref):
    gt1 = m_ref[:, 2 * D_MODEL:3 * D_MODEL]
    sh2 = m_ref[:, 3 * D_MODEL:4 * D_MODEL]
    sc2 = m_ref[:, 4 * D_MODEL:5 * D_MODEL]
    gt2 = m_ref[:, 5 * D_MODEL:6 * D_MODEL]
    x1 = x_ref[...] + gt1 * _dot(cat_ref[...], wo_ref[...])
    hh = (_rmsnorm(x1, g_ref[...]) * (1.0 + sc2) + sh2).astype(BF)
    acc = jnp.zeros((ROW_TILE, D_MODEL), F32)
    for c in range(D_FF // FF_CHUNK):
        a = jnp.maximum(_dot(hh, w1_ref[:, c * FF_CHUNK:(c + 1) * FF_CHUNK]), 0.0)
        acc = acc + _dot((a * a).astype(BF), w2_ref[c * FF_CHUNK:(c + 1) * FF_CHUNK, :])
    o_ref[...] = x1 + gt2 * acc


def _out_mlp(x, cat, mods_l, g2, wo_bf, w1_bf, w2_bf):
    resident = lambda shape: pl.BlockSpec(shape, lambda i: (0,) * len(shape), pipeline_mode=pl.Buffered(1))
    return pl.pallas_call(
        _out_mlp_kernel,
        grid=(N_ROW_TILES,),
        in_specs=[
            pl.BlockSpec((ROW_TILE, D_MODEL), lambda i: (i, 0)),
            pl.BlockSpec((ROW_TILE, MIX_WIDTH), lambda i: (i, 0)),
            pl.BlockSpec((None, 1, 6 * D_MODEL), lambda i: (_mod_row(i), 0, 0)),
            pl.BlockSpec((1, D_MODEL), lambda i: (0, 0)),
            resident((MIX_WIDTH, D_MODEL)),
            resident((D_MODEL, D_FF)),
            resident((D_FF, D_MODEL)),
        ],
        out_specs=pl.BlockSpec((ROW_TILE, D_MODEL), lambda i: (i, 0)),
        out_shape=jax.ShapeDtypeStruct((N_TOK, D_MODEL), F32),
        compiler_params=_params(1),
        name="out_mlp",
    )(x, cat, mods_l, g2.reshape(1, D_MODEL), wo_bf, w1_bf, w2_bf)


def _final_norm_kernel(x_ref, g_ref, o_ref):
    o_ref[...] = _rmsnorm(x_ref[...], g_ref[...])


def _final_norm(x, g):
    return pl.pallas_call(
        _final_norm_kernel,
        grid=(N_ROW_TILES,),
        in_specs=[pl.BlockSpec((ROW_TILE, D_MODEL), lambda i: (i, 0)),
                  pl.BlockSpec((1, D_MODEL), lambda i: (0, 0))],
        out_specs=pl.BlockSpec((ROW_TILE, D_MODEL), lambda i: (i, 0)),
        out_shape=jax.ShapeDtypeStruct((N_TOK, D_MODEL), F32),
        compiler_params=_params(1),
        name="final_norm",
    )(x, g.reshape(1, D_MODEL))


def _dft_tables(n):
    j = np.arange(n)
    ang = 2.0 * np.pi * ((j[:, None] * j[None, :]) % n) / n
    return np.cos(ang) / np.sqrt(n), np.sin(ang) / np.sqrt(n)


def _block_diag4(m):
    out = np.zeros((256, 256), m.dtype)
    for g in range(4):
        out[64 * g:64 * g + 64, 64 * g:64 * g + 64] = m
    return out


def _rope_tables(n_axis_dims):
    half = n_axis_dims // 2
    inv = ROPE_BASE ** (-np.arange(half, dtype=np.float64) / half)
    t = np.arange(DEC_SEQ)
    lane = np.arange(128)
    w = lane % n_axis_dims
    is_col = (lane // n_axis_dims) % 2 == 1
    pos = np.where(is_col[None, :], (t % GRID_W)[:, None], (t // GRID_W)[:, None]).astype(np.float64)
    ang = pos * inv[w % half][None, :]
    sign = np.where(w < half, -1.0, 1.0)[None, :]
    return jnp.asarray(np.cos(ang), F32), jnp.asarray(np.sin(ang) * sign, F32)


def _na_bias_table(rpb):
    col = np.arange(GRID_W)
    cstart = np.clip(col - NA_KW // 2, 0, GRID_W - NA_KW)
    cmask = (col[None, :] >= cstart[:, None]) & (col[None, :] < cstart[:, None] + NA_KW)
    dc = np.clip(col[None, :] - col[:, None] + (NA_KW - 1), 0, 2 * NA_KW - 2)
    t = jnp.where(cmask[None, None], rpb[:, :, dc], NEG_INF)
    didx = np.arange(NA_KH)[:, None] + np.arange(NA_KH)[None, :]
    nb = t[:, didx]
    return nb.transpose(0, 1, 3, 2, 4).reshape(4, NA_KH, GRID_W, NA_KH * GRID_W)


def kernel(x_prompt, x_sample, cache_na_k, cache_na_v, cache_diff_k, cache_diff_v, cache_swa_k, cache_swa_v, c, c_ctx, w_ada, b_ada, norm1_g, norm2_g, w_in, na_rpb, diff_lq1, diff_lk1, diff_lq2, diff_lk2, diff_subln_g, w_fourier, swa_sink, w_out, w_mlp1, w_mlp2, final_g):
    x = jnp.concatenate([x_prompt.reshape(N_CTX_TOK, D_MODEL), x_sample.reshape(N_LAT_TOK, D_MODEL)], axis=0)
    cond = jnp.zeros((MOD_ROWS, D_MODEL), F32).at[0].set(c_ctx).at[1:1 + DEC_BATCH].set(c)
    mods = _ada(cond, w_ada, b_ada).reshape(DEPTH, MOD_ROWS, 1, 6 * D_MODEL)

    cl_p, sl_p = _dft_tables(SEQ)
    cl_s, sl_s = _dft_tables(DEC_SEQ)
    c64, s64 = _dft_tables(64)
    cl_p, sl_p, cl_s, sl_s, cbd, sbd = (
        jnp.asarray(a, F32).astype(BF) for a in (cl_p, sl_p, cl_s, sl_s, _block_diag4(c64), _block_diag4(s64)))
    cos_d, sin_d = _rope_tables(16)
    cos_s, sin_s = _rope_tables(32)

    ck_na = cache_na_k.reshape(DEC_BATCH, DEPTH, PAST_LEN, 256)
    cv_na = cache_na_v.reshape(DEC_BATCH, DEPTH, PAST_LEN, 256)
    ck_df = cache_diff_k.reshape(DEC_BATCH, DEPTH, PAST_LEN, 256)
    cv_df = cache_diff_v.reshape(DEC_BATCH, DEPTH, PAST_LEN, 256)
    ck_sw = cache_swa_k.reshape(DEC_BATCH, DEPTH, PAST_LEN, 128)
    cv_sw = cache_swa_v.reshape(DEC_BATCH, DEPTH, PAST_LEN, 128)

    new = [[] for _ in range(6)]
    for l in range(DEPTH):
        lam_init = 0.8 - 0.6 * math.exp(-0.3 * l)
        lamp = jnp.stack([diff_lq1[l], diff_lk1[l], diff_lq2[l], diff_lk2[l]], axis=0)
        subg = diff_subln_g[l].reshape(1, HEAD_DIM)
        wf_bf = w_fourier[l].astype(BF)

        z = _proj_in(x, mods[l], norm1_g[l], w_in[l].astype(BF))
        cat_p = _ctx_mix(z, swa_sink[l], lamp, subg, cl_p, sl_p, cbd, sbd, wf_bf, lam_init)
        cat_s = jnp.concatenate([
            _lat_na(z, ck_na, cv_na, _na_bias_table(na_rpb[l]), l),
            _lat_diff(z, ck_df, cv_df, cos_d, sin_d, lamp, subg, lam_init, l),
            _lat_fnet(z, cl_s, sl_s, cbd, sbd, wf_bf),
            _lat_swa(z, ck_sw, cv_sw, cos_s, sin_s, swa_sink[l], l),
        ], axis=1)
        cat = jnp.concatenate([cat_p, cat_s], axis=0)
        x = _out_mlp(x, cat, mods[l], norm2_g[l], w_out[l].astype(BF), w_mlp1[l].astype(BF),
                     w_mlp2[l].astype(BF))

        zc = z[:N_CTX_TOK]
        for i, (c0, wd) in enumerate(((C_NA_K, 256), (C_NA_V, 256), (C_DK, 256), (C_DV, 256), (C_SK, 128),
                                      (C_SV, 128))):
            new[i].append(zc[:, c0:c0 + wd].reshape(BATCH, SEQ, wd // 64, 64))

    y = _final_norm(x, final_g)
    y_prompt = y[:N_CTX_TOK].reshape(BATCH, SEQ, D_MODEL)
    y_sample = y[N_CTX_TOK:].reshape(DEC_BATCH, DEC_SEQ, D_MODEL)
    return (y_prompt, y_sample) + tuple(jnp.stack(t, axis=1) for t in new)
```

```python
import functools
import math

import numpy as np
import jax
import jax.numpy as jnp
from jax import lax
from jax.experimental import pallas as pl
from jax.experimental.pallas import tpu as pltpu

D_MODEL = 1024
BATCH = 16
SEQ = 256
DEPTH = 4
DEC_BATCH = 2
DEC_SEQ = 1024
PAST_LEN = 512
GRID_W = 64
GRID_ROWS = DEC_SEQ // GRID_W
HEAD_DIM = 64
NA_KH = 8
NA_KW = 16
DIFF_QK_DIM = 32
SWA_WINDOW = 128
D_FF = 4 * D_MODEL
ROPE_BASE = 10000.0
NORM_EPS = 1e-6
NEG_INF = -1e30
IN_WIDTH = 2304
MIX_WIDTH = 1024

N_CTX_TOK = BATCH * SEQ
N_LAT_TOK = DEC_BATCH * DEC_SEQ
N_TOK = N_CTX_TOK + N_LAT_TOK
ROW_TILE = 256
N_ROW_TILES = N_TOK // ROW_TILE
N_CTX_TILES = N_CTX_TOK // ROW_TILE
N_LAT_TILES = N_LAT_TOK // ROW_TILE
LAT_TILES_PER_REQ = DEC_SEQ // ROW_TILE
LAT_BLOCK0 = N_CTX_TOK // DEC_SEQ
MOD_ROWS = 8

C_NA_Q, C_NA_K, C_NA_V = 0, 256, 512
C_DQ, C_DK, C_DV = 768, 1024, 1280
C_FC = 1536
C_SQ, C_SK, C_SV = 1792, 2048, 2176
CACHE_COLS = ((C_NA_K, 256), (C_NA_V, 256), (C_DK, 256), (C_DV, 256), (C_SK, 128), (C_SV, 128))

NA_QROWS = 2
NA_WIN_ROWS = 10
NA_TABLE_OF_PAIR = (0, 1, 2, 2, 2, 2, 3, 4)
NA_TABLE_PAIRS = (0, 1, 2, 6, 7)

BF = jnp.bfloat16
F32 = jnp.float32
VMEM_LIMIT = 56 * 1024 * 1024


def _dot(a, b):
    return jnp.dot(a, b, preferred_element_type=F32)


def _dot_nt(a, b):
    return lax.dot_general(a, b, (((1,), (1,)), ((), ())), preferred_element_type=F32)


def _rmsnorm(x, g):
    ms = jnp.mean(x * x, axis=-1, keepdims=True)
    return x * lax.rsqrt(ms + NORM_EPS) * g


def _params(n_grid=1):
    return pltpu.CompilerParams(dimension_semantics=("arbitrary",) * n_grid, vmem_limit_bytes=VMEM_LIMIT)


def _rowmax(*parts):
    m = jnp.max(parts[0], axis=-1, keepdims=True)
    for p in parts[1:]:
        m = jnp.maximum(m, jnp.max(p, axis=-1, keepdims=True))
    return m


def _rowsum(*parts):
    s = jnp.sum(parts[0], axis=-1, keepdims=True)
    for p in parts[1:]:
        s = s + jnp.sum(p, axis=-1, keepdims=True)
    return s


def _ada_kernel(cond_ref, w_ref, b_ref, o_ref):
    cnd = cond_ref[...]
    s = cnd / (1.0 + jnp.exp(-cnd))
    o_ref[...] = _dot(s.astype(BF), w_ref[...].astype(BF)) + b_ref[...]


def _ada(cond, w_ada, b_ada):
    tn = 1024
    return pl.pallas_call(
        _ada_kernel,
        grid=(DEPTH, 6 * D_MODEL // tn),
        in_specs=[
            pl.BlockSpec((MOD_ROWS, D_MODEL), lambda l, j: (0, 0)),
            pl.BlockSpec((None, D_MODEL, tn), lambda l, j: (l, 0, j)),
            pl.BlockSpec((None, 1, tn), lambda l, j: (l, 0, j)),
        ],
        out_specs=pl.BlockSpec((None, MOD_ROWS, tn), lambda l, j: (l, 0, j)),
        out_shape=jax.ShapeDtypeStruct((DEPTH, MOD_ROWS, 6 * D_MODEL), F32),
        compiler_params=_params(2),
        name="ada",
    )(cond, w_ada, b_ada.reshape(DEPTH, 1, 6 * D_MODEL))


def _mod_row(t):
    return jnp.where(t < N_CTX_TILES, 0, 1 + (t - N_CTX_TILES) // LAT_TILES_PER_REQ)


def _ctx_tile(t):
    return jnp.minimum(t, N_CTX_TILES - 1)


def _lat_tile(t):
    return jnp.maximum(t - N_CTX_TILES, 0)


W_IN_CAST_STEPS = 4
W_IN_CAST_ROWS = D_MODEL // W_IN_CAST_STEPS


def _rope(x, cos, sin_signed, shift, first):
    partner = jnp.where(first, pltpu.roll(x, 128 - shift, 1), pltpu.roll(x, shift, 1))
    return x * cos + partner * sin_signed


def _proj_in_kernel(n_alias, xp_ref, xs_ref, m_ref, g_ref, w_ref, cosd_ref, sind_ref, coss_ref, sins_ref, *rest):
    rest = rest[n_alias:]
    zb_ref = rest[0]
    cache_refs = rest[1:7]
    wb = rest[7]
    s = pl.program_id(0)

    @pl.when(s < W_IN_CAST_STEPS)
    def _():
        r0 = pl.multiple_of(s * W_IN_CAST_ROWS, W_IN_CAST_ROWS)
        wb[pl.ds(r0, W_IN_CAST_ROWS), :] = w_ref[...].astype(BF)

    @pl.when(s >= W_IN_CAST_STEPS)
    def _():
        t = s - W_IN_CAST_STEPS
        is_ctx = t < N_CTX_TILES
        x = jnp.where(is_ctx, xp_ref[...], xs_ref[...])
        row = _mod_row(t)
        shift = m_ref[pl.ds(row, 1), 0:D_MODEL]
        scale = m_ref[pl.ds(row, 1), D_MODEL:2 * D_MODEL]
        h = _rmsnorm(x, g_ref[...]) * (1.0 + scale) + shift
        z = _dot(h.astype(BF), wb[...])
        dscale = DIFF_QK_DIM ** -0.5

        def put(c0, width, val):
            zb_ref[:, c0:c0 + width] = val.astype(BF)

        put(C_NA_Q, 256, z[:, C_NA_Q:C_NA_Q + 256] * 0.125)
        put(C_NA_K, 512, z[:, C_NA_K:C_NA_K + 512])
        put(C_DV, 512, z[:, C_DV:C_DV + 512])
        put(C_SV, 128, z[:, C_SV:C_SV + 128])

        @pl.when(is_ctx)
        def _():
            for ref, (c0, width) in zip(cache_refs, CACHE_COLS):
                ref[...] = z[:, c0:c0 + width]
            put(C_DQ, 256, z[:, C_DQ:C_DQ + 256] * dscale)
            put(C_DK, 256, z[:, C_DK:C_DK + 256])
            put(C_SQ, 256, z[:, C_SQ:C_SQ + 256] * 0.125)
            put(C_SK, 128, z[:, C_SK:C_SK + 128])

        @pl.when(jnp.logical_not(is_ctx))
        def _():
            p0 = pl.multiple_of(((t - N_CTX_TILES) % LAT_TILES_PER_REQ) * ROW_TILE, ROW_TILE)
            lane = lax.broadcasted_iota(jnp.int32, (ROW_TILE, 128), 1)
            first_d = (lane % 16) < 8
            first_s = (lane % 32) < 16
            cosd, sind = cosd_ref[pl.ds(p0, ROW_TILE), :], sind_ref[pl.ds(p0, ROW_TILE), :]
            coss, sins = coss_ref[pl.ds(p0, ROW_TILE), :], sins_ref[pl.ds(p0, ROW_TILE), :]
            for j in range(2):
                c = 128 * j
                put(C_DQ + c, 128, _rope(z[:, C_DQ + c:C_DQ + c + 128], cosd, sind, 8, first_d) * dscale)
                put(C_DK + c, 128, _rope(z[:, C_DK + c:C_DK + c + 128], cosd, sind, 8, first_d))
                put(C_SQ + c, 128, _rope(z[:, C_SQ + c:C_SQ + c + 128], coss, sins, 16, first_s) * 0.125)
            put(C_SK, 128, _rope(z[:, C_SK:C_SK + 128], coss, sins, 16, first_s))


def _proj_in(layer, xp, xs, mods, g1, w_in, ropes, caches):
    n_alias = 0 if caches is None else 6
    step = lambda f: (lambda s: f(jnp.maximum(s - W_IN_CAST_STEPS, 0)))
    const = lambda shape: pl.BlockSpec(shape, lambda s: (0,) * len(shape))
    in_specs = [
        pl.BlockSpec((ROW_TILE, D_MODEL), step(lambda t: (_ctx_tile(t), 0))),
        pl.BlockSpec((ROW_TILE, D_MODEL), step(lambda t: (_lat_tile(t), 0))),
        pl.BlockSpec((None, MOD_ROWS, 6 * D_MODEL), lambda s: (layer, 0, 0)),
        const((1, D_MODEL)),
        pl.BlockSpec((None, W_IN_CAST_ROWS, IN_WIDTH), lambda s: (layer, jnp.minimum(s, W_IN_CAST_STEPS - 1), 0)),
        const((DEC_SEQ, 128)), const((DEC_SEQ, 128)), const((DEC_SEQ, 128)), const((DEC_SEQ, 128)),
    ] + [pl.BlockSpec(memory_space=pl.ANY)] * n_alias
    out_specs = [pl.BlockSpec((ROW_TILE, IN_WIDTH), step(lambda t: (t, 0)))] + [
        pl.BlockSpec((None, None, SEQ, width), step(lambda t: (_ctx_tile(t), layer, 0, 0)))
        for _, width in CACHE_COLS]
    out_shape = [jax.ShapeDtypeStruct((N_TOK, IN_WIDTH), BF)] + [
        jax.ShapeDtypeStruct((BATCH, DEPTH, SEQ, width), F32) for _, width in CACHE_COLS]
    args = [xp, xs, mods, g1.reshape(1, D_MODEL), w_in, *ropes]
    aliases = {}
    if caches is not None:
        aliases = {len(args) + i: 1 + i for i in range(6)}
        args += list(caches)
    outs = pl.pallas_call(
        functools.partial(_proj_in_kernel, n_alias),
        grid=(W_IN_CAST_STEPS + N_ROW_TILES,),
        in_specs=in_specs,
        out_specs=out_specs,
        out_shape=out_shape,
        scratch_shapes=[pltpu.VMEM((D_MODEL, IN_WIDTH), BF)],
        input_output_aliases=aliases,
        compiler_params=_params(),
        name="proj_in",
    )(*args)
    return outs[0], tuple(outs[1:])


def _diff_lambda(lamp_ref, lam_init):
    a = jnp.sum(lamp_ref[0:1, :] * lamp_ref[1:2, :], axis=-1, keepdims=True)
    b = jnp.sum(lamp_ref[2:3, :] * lamp_ref[3:4, :], axis=-1, keepdims=True)
    return jnp.exp(a) - jnp.exp(b) + lam_init


def _subln(o, g, lam_init):
    return _rmsnorm(o, g) * (1.0 - lam_init)


def _fourier(x_bf, cl, sl, cbd, sbd, wf):
    xc = _dot(x_bf, cbd).astype(BF)
    xs = _dot(x_bf, sbd).astype(BF)
    y = _dot(cl, xc) - _dot(sl, xs)
    return _dot(y.astype(BF), wf)


def _ctx_mix_kernel(lam_init, sink_ref, z_ref, lamp_ref, subg_ref, cl_ref, sl_ref, cbd_ref, sbd_ref, wf_ref,
                    o_ref):
    for h in range(4):
        q = z_ref[:, C_NA_Q + 64 * h:C_NA_Q + 64 * h + 64]
        k = z_ref[:, C_NA_K + 64 * h:C_NA_K + 64 * h + 64]
        v = z_ref[:, C_NA_V + 64 * h:C_NA_V + 64 * h + 64]
        s = _dot_nt(q, k)
        e = jnp.exp(s - _rowmax(s))
        o = _dot(e.astype(BF), v) / _rowsum(e)
        o_ref[:, 64 * h:64 * h + 64] = o.astype(BF)

    lam = _diff_lambda(lamp_ref, lam_init)
    for h in range(4):
        c0 = 64 * h
        q1 = z_ref[:, C_DQ + c0:C_DQ + c0 + 32]
        q2 = z_ref[:, C_DQ + c0 + 32:C_DQ + c0 + 64]
        k1 = z_ref[:, C_DK + c0:C_DK + c0 + 32]
        k2 = z_ref[:, C_DK + c0 + 32:C_DK + c0 + 64]
        v = z_ref[:, C_DV + c0:C_DV + c0 + 64]
        s1 = _dot_nt(q1, k1)
        s2 = _dot_nt(q2, k2)
        e1 = jnp.exp(s1 - _rowmax(s1))
        e2 = jnp.exp(s2 - _rowmax(s2))
        r1 = 1.0 / _rowsum(e1)
        r2 = lam / _rowsum(e2)
        p = e1 * r1 - e2 * r2
        o = _subln(_dot(p.astype(BF), v), subg_ref[...], lam_init)
        o_ref[:, 256 + c0:256 + c0 + 64] = o.astype(BF)

    o_c = _fourier(z_ref[:, C_FC:C_FC + 256], cl_ref[...], sl_ref[...], cbd_ref[...], sbd_ref[...], wf_ref[...])
    o_ref[:, 512:768] = o_c.astype(BF)

    for h in range(4):
        g = h // 2
        q = z_ref[:, C_SQ + 64 * h:C_SQ + 64 * h + 64]
        k = z_ref[:, C_SK + 64 * g:C_SK + 64 * g + 64]
        v = z_ref[:, C_SV + 64 * g:C_SV + 64 * g + 64]
        sink = sink_ref[h]
        s = _dot_nt(q, k)
        m = jnp.maximum(_rowmax(s), sink)
        e = jnp.exp(s - m)
        o = _dot(e.astype(BF), v) / (_rowsum(e) + jnp.exp(sink - m))
        o_ref[:, 768 + 64 * h:768 + 64 * h + 64] = o.astype(BF)


def _ctx_mix(zb, sink, lamp, subg, cl, sl, cbd, sbd, wf, lam_init):
    full = lambda shape: pl.BlockSpec(shape, lambda b, s: (0,) * len(shape))
    grid_spec = pltpu.PrefetchScalarGridSpec(
        num_scalar_prefetch=1,
        grid=(BATCH,),
        in_specs=[
            pl.BlockSpec((SEQ, IN_WIDTH), lambda b, s: (b, 0)),
            full((4, DIFF_QK_DIM)),
            full((1, HEAD_DIM)),
            full((SEQ, SEQ)), full((SEQ, SEQ)),
            full((256, 256)), full((256, 256)), full((256, 256)),
        ],
        out_specs=pl.BlockSpec((SEQ, MIX_WIDTH), lambda b, s: (b, 0)),
    )
    return pl.pallas_call(
        functools.partial(_ctx_mix_kernel, lam_init),
        grid_spec=grid_spec,
        out_shape=jax.ShapeDtypeStruct((N_TOK, MIX_WIDTH), BF),
        compiler_params=_params(),
        name="ctx_mix",
    )(sink, zb, lamp, subg, cl, sl, cbd, sbd, wf)


def _lat_out_spec(col_block, n_prefetch=0):
    if n_prefetch:
        return pl.BlockSpec((DEC_SEQ, 256), lambda b, s: (LAT_BLOCK0 + b, col_block))
    return pl.BlockSpec((DEC_SEQ, 256), lambda b: (LAT_BLOCK0 + b, col_block))


def _lat_na_kernel(z_ref, ck_ref, cv_ref, nb_ref, cat_ref, o_ref, ckb, cvb):
    del cat_ref
    ckb[...] = ck_ref[...].astype(BF)
    cvb[...] = cv_ref[...].astype(BF)
    n_q = NA_QROWS * GRID_W
    n_loc = NA_WIN_ROWS * GRID_W

    def body(p, carry):
        w_row = jnp.clip(NA_QROWS * p - NA_KH // 2, 0, GRID_ROWS - NA_WIN_ROWS)
        tbl = jnp.minimum(p, 2) + jnp.maximum(p - 5, 0)
        q0 = pl.multiple_of(p * n_q, n_q)
        k0 = pl.multiple_of(w_row * GRID_W, NA_QROWS * GRID_W)
        for h in range(4):
            q = z_ref[pl.ds(q0, n_q), C_NA_Q + 64 * h:C_NA_Q + 64 * h + 64]
            kl = z_ref[pl.ds(k0, n_loc), C_NA_K + 64 * h:C_NA_K + 64 * h + 64]
            vl = z_ref[pl.ds(k0, n_loc), C_NA_V + 64 * h:C_NA_V + 64 * h + 64]
            s_loc = _dot_nt(q, kl) + nb_ref[h, tbl]
            s_ctx = _dot_nt(q, ckb[:, 64 * h:64 * h + 64])
            m = _rowmax(s_loc, s_ctx)
            e_loc = jnp.exp(s_loc - m)
            e_ctx = jnp.exp(s_ctx - m)
            o = _dot(e_loc.astype(BF), vl) + _dot(e_ctx.astype(BF), cvb[:, 64 * h:64 * h + 64])
            o_ref[pl.ds(q0, n_q), 64 * h:64 * h + 64] = (o / _rowsum(e_loc, e_ctx)).astype(BF)
        return carry

    lax.fori_loop(0, GRID_ROWS // NA_QROWS, body, 0)


def _lat_na(zb, cache_k, cache_v, nb, cat, layer):
    n_q = NA_QROWS * GRID_W
    n_loc = NA_WIN_ROWS * GRID_W
    return pl.pallas_call(
        _lat_na_kernel,
        grid=(DEC_BATCH,),
        in_specs=[
            pl.BlockSpec((DEC_SEQ, 768), lambda b: (LAT_BLOCK0 + b, 0)),
            pl.BlockSpec((None, None, PAST_LEN, 256), lambda b: (b, layer, 0, 0)),
            pl.BlockSpec((None, None, PAST_LEN, 256), lambda b: (b, layer, 0, 0)),
            pl.BlockSpec((None, 4, len(NA_TABLE_PAIRS), n_q, n_loc), lambda b: (layer, 0, 0, 0, 0)),
            pl.BlockSpec(memory_space=pl.ANY),
        ],
        out_specs=_lat_out_spec(0),
        out_shape=jax.ShapeDtypeStruct((N_TOK, MIX_WIDTH), BF),
        scratch_shapes=[pltpu.VMEM((PAST_LEN, 256), BF)] * 2,
        input_output_aliases={4: 0},
        compiler_params=_params(),
        name="lat_na",
    )(zb, cache_k, cache_v, nb, cat)


def _lat_diff_kernel(lam_init, z_ref, ck_ref, cv_ref, lamp_ref, subg_ref, cat_ref, o_ref, ckb, cvb):
    del cat_ref
    ckb[...] = ck_ref[...].astype(BF)
    cvb[...] = cv_ref[...].astype(BF)
    lam = _diff_lambda(lamp_ref, lam_init)
    qblk = 128

    def body(i, carry):
        q0 = pl.multiple_of(i * qblk, qblk)
        for h in range(4):
            c0 = 64 * h
            q1 = z_ref[pl.ds(q0, qblk), c0:c0 + 32]
            q2 = z_ref[pl.ds(q0, qblk), c0 + 32:c0 + 64]
            s1a = _dot_nt(q1, z_ref[:, 256 + c0:256 + c0 + 32])
            s1b = _dot_nt(q1, ckb[:, c0:c0 + 32])
            s2a = _dot_nt(q2, z_ref[:, 256 + c0 + 32:256 + c0 + 64])
            s2b = _dot_nt(q2, ckb[:, c0 + 32:c0 + 64])
            m1 = _rowmax(s1a, s1b)
            m2 = _rowmax(s2a, s2b)
            e1a = jnp.exp(s1a - m1)
            e1b = jnp.exp(s1b - m1)
            e2a = jnp.exp(s2a - m2)
            e2b = jnp.exp(s2b - m2)
            r1 = 1.0 / _rowsum(e1a, e1b)
            r2 = lam / _rowsum(e2a, e2b)
            pa = (e1a * r1 - e2a * r2).astype(BF)
            pb = (e1b * r1 - e2b * r2).astype(BF)
            o = _dot(pa, z_ref[:, 512 + c0:512 + c0 + 64]) + _dot(pb, cvb[:, c0:c0 + 64])
            o_ref[pl.ds(q0, qblk), c0:c0 + 64] = _subln(o, subg_ref[...], lam_init).astype(BF)
        return carry

    lax.fori_loop(0, DEC_SEQ // qblk, body, 0)


def _lat_diff(zb, cache_k, cache_v, lamp, subg, cat, lam_init, layer):
    full = lambda shape: pl.BlockSpec(shape, lambda b: (0,) * len(shape))
    return pl.pallas_call(
        functools.partial(_lat_diff_kernel, lam_init),
        grid=(DEC_BATCH,),
        in_specs=[
            pl.BlockSpec((DEC_SEQ, 768), lambda b: (LAT_BLOCK0 + b, 1)),
            pl.BlockSpec((None, None, PAST_LEN, 256), lambda b: (b, layer, 0, 0)),
            pl.BlockSpec((None, None, PAST_LEN, 256), lambda b: (b, layer, 0, 0)),
            full((4, DIFF_QK_DIM)), full((1, HEAD_DIM)),
            pl.BlockSpec(memory_space=pl.ANY),
        ],
        out_specs=_lat_out_spec(1),
        out_shape=jax.ShapeDtypeStruct((N_TOK, MIX_WIDTH), BF),
        scratch_shapes=[pltpu.VMEM((PAST_LEN, 256), BF)] * 2,
        input_output_aliases={5: 0},
        compiler_params=_params(),
        name="lat_diff",
    )(zb, cache_k, cache_v, lamp, subg, cat)


def _lat_fnet_kernel(z_ref, cl_ref, sl_ref, cbd_ref, sbd_ref, wf_ref, cat_ref, o_ref):
    del cat_ref
    o = _fourier(z_ref[...], cl_ref[...], sl_ref[...], cbd_ref[...], sbd_ref[...], wf_ref[...])
    o_ref[...] = o.astype(BF)


def _lat_fnet(zb, cl, sl, cbd, sbd, wf, cat):
    full = lambda shape: pl.BlockSpec(shape, lambda b: (0,) * len(shape))
    return pl.pallas_call(
        _lat_fnet_kernel,
        grid=(DEC_BATCH,),
        in_specs=[
            pl.BlockSpec((DEC_SEQ, 256), lambda b: (LAT_BLOCK0 + b, C_FC // 256)),
            full((DEC_SEQ, DEC_SEQ)), full((DEC_SEQ, DEC_SEQ)),
            full((256, 256)), full((256, 256)), full((256, 256)),
            pl.BlockSpec(memory_space=pl.ANY),
        ],
        out_specs=_lat_out_spec(2),
        out_shape=jax.ShapeDtypeStruct((N_TOK, MIX_WIDTH), BF),
        input_output_aliases={6: 0},
        compiler_params=_params(),
        name="lat_fnet",
    )(zb, cl, sl, cbd, sbd, wf, cat)


def _lat_swa_kernel(sink_ref, zq_ref, zkv_ref, ck_ref, cv_ref, cat_ref, o_ref, ckb, cvb):
    del cat_ref
    ckb[...] = ck_ref[...].astype(BF)
    cvb[...] = cv_ref[...].astype(BF)
    W = SWA_WINDOW
    n_win = 3 * W

    def body(n, carry):
        q0 = pl.multiple_of(n * W, W)
        w0 = pl.multiple_of(jnp.clip((n - 1) * W, 0, DEC_SEQ - n_win), W)
        qpos = q0 + lax.broadcasted_iota(jnp.int32, (W, n_win), 0)
        kpos = w0 + lax.broadcasted_iota(jnp.int32, (W, n_win), 1)
        valid = jnp.abs(qpos - kpos) <= W
        for h in range(4):
            g = h // 2
            q = zq_ref[pl.ds(q0, W), 64 * h:64 * h + 64]
            s_loc = _dot_nt(q, zkv_ref[pl.ds(w0, n_win), 64 * g:64 * g + 64])
            s_loc = jnp.where(valid, s_loc, NEG_INF)
            s_ctx = _dot_nt(q, ckb[:, 64 * g:64 * g + 64])
            sink = sink_ref[h]
            m = jnp.maximum(_rowmax(s_loc, s_ctx), sink)
            e_loc = jnp.exp(s_loc - m)
            e_ctx = jnp.exp(s_ctx - m)
            l = _rowsum(e_loc, e_ctx) + jnp.exp(sink - m)
            o = (_dot(e_loc.astype(BF), zkv_ref[pl.ds(w0, n_win), 128 + 64 * g:128 + 64 * g + 64])
                 + _dot(e_ctx.astype(BF), cvb[:, 64 * g:64 * g + 64]))
            o_ref[pl.ds(q0, W), 64 * h:64 * h + 64] = (o / l).astype(BF)
        return carry

    lax.fori_loop(0, DEC_SEQ // W, body, 0)


def _lat_swa(zb, cache_k, cache_v, sink, cat, layer):
    grid_spec = pltpu.PrefetchScalarGridSpec(
        num_scalar_prefetch=1,
        grid=(DEC_BATCH,),
        in_specs=[
            pl.BlockSpec((DEC_SEQ, 256), lambda b, s: (LAT_BLOCK0 + b, C_SQ // 256)),
            pl.BlockSpec((DEC_SEQ, 256), lambda b, s: (LAT_BLOCK0 + b, C_SK // 256)),
            pl.BlockSpec((None, None, PAST_LEN, 128), lambda b, s: (b, layer, 0, 0)),
            pl.BlockSpec((None, None, PAST_LEN, 128), lambda b, s: (b, layer, 0, 0)),
            pl.BlockSpec(memory_space=pl.ANY),
        ],
        out_specs=_lat_out_spec(3, n_prefetch=1),
        scratch_shapes=[pltpu.VMEM((PAST_LEN, 128), BF)] * 2,
    )
    return pl.pallas_call(
        _lat_swa_kernel,
        grid_spec=grid_spec,
        out_shape=jax.ShapeDtypeStruct((N_TOK, MIX_WIDTH), BF),
        input_output_aliases={5: 0},
        compiler_params=_params(),
        name="lat_swa",
    )(sink, zb, zb, cache_k, cache_v, cat)


FF_CHUNK = 1024
MLP_CAST_STEPS = 8
WO_CAST_ROWS = MIX_WIDTH // MLP_CAST_STEPS
W1_CAST_ROWS = D_MODEL // MLP_CAST_STEPS
W2_CAST_ROWS = D_FF // MLP_CAST_STEPS


def _out_mlp_kernel(last, xp_ref, xs_ref, cat_ref, m_ref, g_ref, fg_ref, wo_ref, w1_ref, w2_ref,
                    op_ref, os_ref, wob, w1b, w2b):
    s = pl.program_id(0)

    @pl.when(s < MLP_CAST_STEPS)
    def _():
        wob[pl.ds(pl.multiple_of(s * WO_CAST_ROWS, WO_CAST_ROWS), WO_CAST_ROWS), :] = wo_ref[...].astype(BF)
        w1b[pl.ds(pl.multiple_of(s * W1_CAST_ROWS, W1_CAST_ROWS), W1_CAST_ROWS), :] = w1_ref[...].astype(BF)
        w2b[pl.ds(pl.multiple_of(s * W2_CAST_ROWS, W2_CAST_ROWS), W2_CAST_ROWS), :] = w2_ref[...].astype(BF)

    @pl.when(s >= MLP_CAST_STEPS)
    def _():
        t = s - MLP_CAST_STEPS
        is_ctx = t < N_CTX_TILES
        row = _mod_row(t)
        gt1 = m_ref[pl.ds(row, 1), 2 * D_MODEL:3 * D_MODEL]
        sh2 = m_ref[pl.ds(row, 1), 3 * D_MODEL:4 * D_MODEL]
        sc2 = m_ref[pl.ds(row, 1), 4 * D_MODEL:5 * D_MODEL]
        gt2 = m_ref[pl.ds(row, 1), 5 * D_MODEL:6 * D_MODEL]
        x = jnp.where(is_ctx, xp_ref[...], xs_ref[...])
        x1 = x + gt1 * _dot(cat_ref[...], wob[...])
        hh = (_rmsnorm(x1, g_ref[...]) * (1.0 + sc2) + sh2).astype(BF)
        acc = jnp.zeros((ROW_TILE, D_MODEL), F32)
        for c in range(D_FF // FF_CHUNK):
            a = jnp.maximum(_dot(hh, w1b[:, c * FF_CHUNK:(c + 1) * FF_CHUNK]), 0.0)
            acc = acc + _dot((a * a).astype(BF), w2b[c * FF_CHUNK:(c + 1) * FF_CHUNK, :])
        out = x1 + gt2 * acc
        if last:
            out = _rmsnorm(out, fg_ref[...])

        @pl.when(is_ctx)
        def _():
            op_ref[...] = out

        @pl.when(jnp.logical_not(is_ctx))
        def _():
            os_ref[...] = out


def _out_mlp(layer, xp, xs, cat, mods, g2, final_g, wo, w1, w2):
    step = lambda f: (lambda s: f(jnp.maximum(s - MLP_CAST_STEPS, 0)))
    const = lambda shape: pl.BlockSpec(shape, lambda s: (0,) * len(shape))
    chunk = lambda rows, cols: pl.BlockSpec((None, rows, cols),
                                            lambda s: (layer, jnp.minimum(s, MLP_CAST_STEPS - 1), 0))
    return pl.pallas_call(
        functools.partial(_out_mlp_kernel, layer == DEPTH - 1),
        grid=(MLP_CAST_STEPS + N_ROW_TILES,),
        in_specs=[
            pl.BlockSpec((ROW_TILE, D_MODEL), step(lambda t: (_ctx_tile(t), 0))),
            pl.BlockSpec((ROW_TILE, D_MODEL), step(lambda t: (_lat_tile(t), 0))),
            pl.BlockSpec((ROW_TILE, MIX_WIDTH), step(lambda t: (t, 0))),
            pl.BlockSpec((None, MOD_ROWS, 6 * D_MODEL), lambda s: (layer, 0, 0)),
            const((1, D_MODEL)),
            const((1, D_MODEL)),
            chunk(WO_CAST_ROWS, D_MODEL),
            chunk(W1_CAST_ROWS, D_FF),
            chunk(W2_CAST_ROWS, D_MODEL),
        ],
        out_specs=[
            pl.BlockSpec((ROW_TILE, D_MODEL), step(lambda t: (_ctx_tile(t), 0))),
            pl.BlockSpec((ROW_TILE, D_MODEL), step(lambda t: (_lat_tile(t), 0))),
        ],
        out_shape=[jax.ShapeDtypeStruct((N_CTX_TOK, D_MODEL), F32),
                   jax.ShapeDtypeStruct((N_LAT_TOK, D_MODEL), F32)],
        scratch_shapes=[pltpu.VMEM((MIX_WIDTH, D_MODEL), BF), pltpu.VMEM((D_MODEL, D_FF), BF),
                        pltpu.VMEM((D_FF, D_MODEL), BF)],
        compiler_params=_params(),
        name="out_mlp",
    )(xp, xs, cat, mods, g2.reshape(1, D_MODEL), final_g.reshape(1, D_MODEL), wo, w1, w2)


def _dft_tables(n):
    j = np.arange(n)
    ang = 2.0 * np.pi * ((j[:, None] * j[None, :]) % n) / n
    return np.cos(ang) / np.sqrt(n), np.sin(ang) / np.sqrt(n)


def _block_diag4(m):
    out = np.zeros((256, 256), m.dtype)
    for g in range(4):
        out[64 * g:64 * g + 64, 64 * g:64 * g + 64] = m
    return out


def _rope_tables(n_axis_dims):
    half = n_axis_dims // 2
    inv = ROPE_BASE ** (-np.arange(half, dtype=np.float64) / half)
    t = np.arange(DEC_SEQ)
    lane = np.arange(128)
    w = lane % n_axis_dims
    is_col = (lane // n_axis_dims) % 2 == 1
    pos = np.where(is_col[None, :], (t % GRID_W)[:, None], (t // GRID_W)[:, None]).astype(np.float64)
    ang = pos * inv[w % half][None, :]
    sign = np.where(w < half, -1.0, 1.0)[None, :]
    return jnp.asarray(np.cos(ang), F32), jnp.asarray(np.sin(ang) * sign, F32)


def _na_bias_tables(rpb):
    lead = rpb.shape[:2]
    col = np.arange(GRID_W)
    cstart = np.clip(col - NA_KW // 2, 0, GRID_W - NA_KW)
    cmask = (col[None, :] >= cstart[:, None]) & (col[None, :] < cstart[:, None] + NA_KW)
    pad = GRID_W - NA_KW
    rp = jnp.pad(rpb, ((0, 0), (0, 0), (0, 0), (pad, pad)))
    t = jnp.stack([rp[..., pad + NA_KW - 1 - c:pad + NA_KW - 1 - c + GRID_W] for c in range(GRID_W)], axis=-2)
    t = jnp.where(cmask, t, NEG_INF)
    t = t.transpose(0, 1, 3, 2, 4).reshape(lead + (GRID_W, (2 * NA_KH - 1) * GRID_W))
    tables = []
    for p in NA_TABLE_PAIRS:
        w_row = int(np.clip(NA_QROWS * p - NA_KH // 2, 0, GRID_ROWS - NA_WIN_ROWS))
        rows = []
        for r in (NA_QROWS * p, NA_QROWS * p + 1):
            rs = int(np.clip(r - NA_KH // 2, 0, GRID_ROWS - NA_KH))
            j_lo, j_hi = rs - w_row, rs - w_row + NA_KH
            d_lo = rs - r + NA_KH - 1
            body = t[..., d_lo * GRID_W:(d_lo + NA_KH) * GRID_W]
            rows.append(jnp.pad(body, ((0, 0),) * 3 + ((j_lo * GRID_W, (NA_WIN_ROWS - j_hi) * GRID_W),),
                                constant_values=NEG_INF))
        tables.append(jnp.concatenate(rows, axis=-2))
    return jnp.stack(tables, axis=2)


def kernel(x_prompt, x_sample, cache_na_k, cache_na_v, cache_diff_k, cache_diff_v, cache_swa_k, cache_swa_v, c, c_ctx, w_ada, b_ada, norm1_g, norm2_g, w_in, na_rpb, diff_lq1, diff_lk1, diff_lq2, diff_lk2, diff_subln_g, w_fourier, swa_sink, w_out, w_mlp1, w_mlp2, final_g):
    xp = x_prompt.reshape(N_CTX_TOK, D_MODEL)
    xs = x_sample.reshape(N_LAT_TOK, D_MODEL)
    cond = jnp.zeros((MOD_ROWS, D_MODEL), F32).at[0].set(c_ctx).at[1:1 + DEC_BATCH].set(c)
    mods = _ada(cond, w_ada, b_ada)

    cl_p, sl_p = _dft_tables(SEQ)
    cl_s, sl_s = _dft_tables(DEC_SEQ)
    c64, s64 = _dft_tables(64)
    cl_p, sl_p, cl_s, sl_s, cbd, sbd = (
        jnp.asarray(a, F32).astype(BF) for a in (cl_p, sl_p, cl_s, sl_s, _block_diag4(c64), _block_diag4(s64)))
    ropes = _rope_tables(16) + _rope_tables(32)
    nb = _na_bias_tables(na_rpb)
    wf_bf = w_fourier.astype(BF)

    ck_na = cache_na_k.reshape(DEC_BATCH, DEPTH, PAST_LEN, 256)
    cv_na = cache_na_v.reshape(DEC_BATCH, DEPTH, PAST_LEN, 256)
    ck_df = cache_diff_k.reshape(DEC_BATCH, DEPTH, PAST_LEN, 256)
    cv_df = cache_diff_v.reshape(DEC_BATCH, DEPTH, PAST_LEN, 256)
    ck_sw = cache_swa_k.reshape(DEC_BATCH, DEPTH, PAST_LEN, 128)
    cv_sw = cache_swa_v.reshape(DEC_BATCH, DEPTH, PAST_LEN, 128)

    caches = None
    for l in range(DEPTH):
        lam_init = 0.8 - 0.6 * math.exp(-0.3 * l)
        lamp = jnp.stack([diff_lq1[l], diff_lk1[l], diff_lq2[l], diff_lk2[l]], axis=0)
        subg = diff_subln_g[l].reshape(1, HEAD_DIM)

        zb, caches = _proj_in(l, xp, xs, mods, norm1_g[l], w_in, ropes, caches)
        cat = _ctx_mix(zb, swa_sink[l], lamp, subg, cl_p, sl_p, cbd, sbd, wf_bf[l], lam_init)
        cat = _lat_na(zb, ck_na, cv_na, nb, cat, l)
        cat = _lat_diff(zb, ck_df, cv_df, lamp, subg, cat, lam_init, l)
        cat = _lat_fnet(zb, cl_s, sl_s, cbd, sbd, wf_bf[l], cat)
        cat = _lat_swa(zb, ck_sw, cv_sw, swa_sink[l], cat, l)
        xp, xs = _out_mlp(l, xp, xs, cat, mods, norm2_g[l], final_g, w_out, w_mlp1, w_mlp2)

    y_prompt = xp.reshape(BATCH, SEQ, D_MODEL)
    y_sample = xs.reshape(DEC_BATCH, DEC_SEQ, D_MODEL)
    new = [a.reshape(BATCH, DEPTH, SEQ, a.shape[-1] // HEAD_DIM, HEAD_DIM) for a in caches]
    return (y_prompt, y_sample) + tuple(new)
```

```python
import functools
import math

import numpy as np
import jax
import jax.numpy as jnp
from jax import lax
from jax.experimental import pallas as pl
from jax.experimental.pallas import tpu as pltpu

D_MODEL = 1024
BATCH = 16
SEQ = 256
DEPTH = 4
DEC_BATCH = 2
DEC_SEQ = 1024
PAST_LEN = 512
GRID_W = 64
GRID_ROWS = DEC_SEQ // GRID_W
HEAD_DIM = 64
NA_KH = 8
NA_KW = 16
DIFF_QK_DIM = 32
SWA_WINDOW = 128
D_FF = 4 * D_MODEL
ROPE_BASE = 10000.0
NORM_EPS = 1e-6
NEG_INF = -1e30
IN_WIDTH = 2304
MIX_WIDTH = 1024

N_CTX_TOK = BATCH * SEQ
N_LAT_TOK = DEC_BATCH * DEC_SEQ
N_TOK = N_CTX_TOK + N_LAT_TOK
ROW_TILE = 256
N_ROW_TILES = N_TOK // ROW_TILE
N_CTX_TILES = N_CTX_TOK // ROW_TILE
N_LAT_TILES = N_LAT_TOK // ROW_TILE
LAT_TILES_PER_REQ = DEC_SEQ // ROW_TILE
LAT_BLOCK0 = N_CTX_TOK // DEC_SEQ
MOD_ROWS = 8

C_NA_Q, C_NA_K, C_NA_V = 0, 256, 512
C_DQ, C_DK, C_DV = 768, 1024, 1280
C_FC = 1536
C_SQ, C_SK, C_SV = 1792, 2048, 2176
CACHE_COLS = ((C_NA_K, 256), (C_NA_V, 256), (C_DK, 256), (C_DV, 256), (C_SK, 128), (C_SV, 128))
ZB_SK, ZB_SV = 2048, 2304
ZB_WIDTH = 2560

NA_QROWS = 2
NA_WIN_ROWS = 10
NA_TABLE_OF_PAIR = (0, 1, 2, 2, 2, 2, 3, 4)
NA_TABLE_PAIRS = (0, 1, 2, 6, 7)

BF = jnp.bfloat16
F32 = jnp.float32
VMEM_LIMIT = 56 * 1024 * 1024


def _dot(a, b):
    return jnp.dot(a, b, preferred_element_type=F32)


def _dot_nt(a, b):
    return lax.dot_general(a, b, (((1,), (1,)), ((), ())), preferred_element_type=F32)


def _rmsnorm(x, g):
    ms = jnp.mean(x * x, axis=-1, keepdims=True)
    return x * lax.rsqrt(ms + NORM_EPS) * g


def _params(n_grid=1):
    return pltpu.CompilerParams(dimension_semantics=("arbitrary",) * n_grid, vmem_limit_bytes=VMEM_LIMIT)


def _rowmax(*parts):
    m = jnp.max(parts[0], axis=-1, keepdims=True)
    for p in parts[1:]:
        m = jnp.maximum(m, jnp.max(p, axis=-1, keepdims=True))
    return m


def _rowsum(*parts):
    s = jnp.sum(parts[0], axis=-1, keepdims=True)
    for p in parts[1:]:
        s = s + jnp.sum(p, axis=-1, keepdims=True)
    return s


def _ada_kernel(cond_ref, w_ref, b_ref, o_ref):
    cnd = cond_ref[...]
    s = cnd / (1.0 + jnp.exp(-cnd))
    o_ref[...] = _dot(s.astype(BF), w_ref[...].astype(BF)) + b_ref[...]


def _ada(cond, w_ada, b_ada):
    tn = 1024
    return pl.pallas_call(
        _ada_kernel,
        grid=(DEPTH, 6 * D_MODEL // tn),
        in_specs=[
            pl.BlockSpec((MOD_ROWS, D_MODEL), lambda l, j: (0, 0)),
            pl.BlockSpec((None, D_MODEL, tn), lambda l, j: (l, 0, j)),
            pl.BlockSpec((None, 1, tn), lambda l, j: (l, 0, j)),
        ],
        out_specs=pl.BlockSpec((None, MOD_ROWS, tn), lambda l, j: (l, 0, j)),
        out_shape=jax.ShapeDtypeStruct((DEPTH, MOD_ROWS, 6 * D_MODEL), F32),
        compiler_params=_params(2),
        name="ada",
    )(cond, w_ada, b_ada.reshape(DEPTH, 1, 6 * D_MODEL))


def _mod_row(t):
    return jnp.where(t < N_CTX_TILES, 0, 1 + (t - N_CTX_TILES) // LAT_TILES_PER_REQ)


def _ctx_tile(t):
    return jnp.minimum(t, N_CTX_TILES - 1)


def _lat_tile(t):
    return jnp.maximum(t - N_CTX_TILES, 0)


W_IN_CAST_STEPS = 4
W_IN_CAST_ROWS = D_MODEL // W_IN_CAST_STEPS


def _rope(x, cos, sin_signed, shift, first):
    partner = jnp.where(first, pltpu.roll(x, 128 - shift, 1), pltpu.roll(x, shift, 1))
    return x * cos + partner * sin_signed


def _proj_in_kernel(n_alias, xp_ref, xs_ref, m_ref, g_ref, w_ref, cosd_ref, sind_ref, coss_ref, sins_ref, *rest):
    rest = rest[n_alias:]
    zb_ref = rest[0]
    cache_refs = rest[1:7]
    wb = rest[7]
    s = pl.program_id(0)

    @pl.when(s < W_IN_CAST_STEPS)
    def _():
        r0 = pl.multiple_of(s * W_IN_CAST_ROWS, W_IN_CAST_ROWS)
        wb[pl.ds(r0, W_IN_CAST_ROWS), :] = w_ref[...].astype(BF)

    @pl.when(s >= W_IN_CAST_STEPS)
    def _():
        t = s - W_IN_CAST_STEPS
        is_ctx = t < N_CTX_TILES
        x = jnp.where(is_ctx, xp_ref[...], xs_ref[...])
        row = _mod_row(t)
        shift = m_ref[pl.ds(row, 1), 0:D_MODEL]
        scale = m_ref[pl.ds(row, 1), D_MODEL:2 * D_MODEL]
        h = _rmsnorm(x, g_ref[...]) * (1.0 + scale) + shift
        z = _dot(h.astype(BF), wb[...])
        dscale = DIFF_QK_DIM ** -0.5

        def put(c0, width, val):
            zb_ref[:, c0:c0 + width] = val.astype(BF)

        lo = lax.broadcasted_iota(jnp.int32, (ROW_TILE, 128), 1) < HEAD_DIM

        def put_dup(c0, val):
            r = pltpu.roll(val, HEAD_DIM, 1)
            put(c0, 128, jnp.where(lo, val, r))
            put(c0 + 128, 128, jnp.where(lo, r, val))

        put(C_NA_Q, 256, z[:, C_NA_Q:C_NA_Q + 256] * 0.125)
        put(C_NA_K, 512, z[:, C_NA_K:C_NA_K + 512])
        put(C_DV, 512, z[:, C_DV:C_DV + 512])
        put_dup(ZB_SV, z[:, C_SV:C_SV + 128])

        @pl.when(is_ctx)
        def _():
            for ref, (c0, width) in zip(cache_refs, CACHE_COLS):
                ref[...] = z[:, c0:c0 + width]
            put(C_DQ, 256, z[:, C_DQ:C_DQ + 256] * dscale)
            put(C_DK, 256, z[:, C_DK:C_DK + 256])
            put(C_SQ, 256, z[:, C_SQ:C_SQ + 256] * 0.125)
            put_dup(ZB_SK, z[:, C_SK:C_SK + 128])

        @pl.when(jnp.logical_not(is_ctx))
        def _():
            p0 = pl.multiple_of(((t - N_CTX_TILES) % LAT_TILES_PER_REQ) * ROW_TILE, ROW_TILE)
            lane = lax.broadcasted_iota(jnp.int32, (ROW_TILE, 128), 1)
            first_d = (lane % 16) < 8
            first_s = (lane % 32) < 16
            cosd, sind = cosd_ref[pl.ds(p0, ROW_TILE), :], sind_ref[pl.ds(p0, ROW_TILE), :]
            coss, sins = coss_ref[pl.ds(p0, ROW_TILE), :], sins_ref[pl.ds(p0, ROW_TILE), :]
            for j in range(2):
                c = 128 * j
                put(C_DQ + c, 128, _rope(z[:, C_DQ + c:C_DQ + c + 128], cosd, sind, 8, first_d) * dscale)
                put(C_DK + c, 128, _rope(z[:, C_DK + c:C_DK + c + 128], cosd, sind, 8, first_d))
                put(C_SQ + c, 128, _rope(z[:, C_SQ + c:C_SQ + c + 128], coss, sins, 16, first_s) * 0.125)
            put_dup(ZB_SK, _rope(z[:, C_SK:C_SK + 128], coss, sins, 16, first_s))


def _proj_in(layer, xp, xs, mods, g1, w_in, ropes, caches):
    n_alias = 0 if caches is None else 6
    step = lambda f: (lambda s: f(jnp.maximum(s - W_IN_CAST_STEPS, 0)))
    const = lambda shape: pl.BlockSpec(shape, lambda s: (0,) * len(shape))
    in_specs = [
        pl.BlockSpec((ROW_TILE, D_MODEL), step(lambda t: (_ctx_tile(t), 0))),
        pl.BlockSpec((ROW_TILE, D_MODEL), step(lambda t: (_lat_tile(t), 0))),
        pl.BlockSpec((None, MOD_ROWS, 6 * D_MODEL), lambda s: (layer, 0, 0)),
        const((1, D_MODEL)),
        pl.BlockSpec((None, W_IN_CAST_ROWS, IN_WIDTH), lambda s: (layer, jnp.minimum(s, W_IN_CAST_STEPS - 1), 0)),
        const((DEC_SEQ, 128)), const((DEC_SEQ, 128)), const((DEC_SEQ, 128)), const((DEC_SEQ, 128)),
    ] + [pl.BlockSpec(memory_space=pl.ANY)] * n_alias
    out_specs = [pl.BlockSpec((ROW_TILE, ZB_WIDTH), step(lambda t: (t, 0)))] + [
        pl.BlockSpec((None, None, SEQ, width), step(lambda t: (_ctx_tile(t), layer, 0, 0)))
        for _, width in CACHE_COLS]
    out_shape = [jax.ShapeDtypeStruct((N_TOK, ZB_WIDTH), BF)] + [
        jax.ShapeDtypeStruct((BATCH, DEPTH, SEQ, width), F32) for _, width in CACHE_COLS]
    args = [xp, xs, mods, g1.reshape(1, D_MODEL), w_in, *ropes]
    aliases = {}
    if caches is not None:
        aliases = {len(args) + i: 1 + i for i in range(6)}
        args += list(caches)
    outs = pl.pallas_call(
        functools.partial(_proj_in_kernel, n_alias),
        grid=(W_IN_CAST_STEPS + N_ROW_TILES,),
        in_specs=in_specs,
        out_specs=out_specs,
        out_shape=out_shape,
        scratch_shapes=[pltpu.VMEM((D_MODEL, IN_WIDTH), BF)],
        input_output_aliases=aliases,
        compiler_params=_params(),
        name="proj_in",
    )(*args)
    return outs[0], tuple(outs[1:])


def _diff_lambda(lamp_ref, lam_init):
    a = jnp.sum(lamp_ref[0:1, :] * lamp_ref[1:2, :], axis=-1, keepdims=True)
    b = jnp.sum(lamp_ref[2:3, :] * lamp_ref[3:4, :], axis=-1, keepdims=True)
    return jnp.exp(a) - jnp.exp(b) + lam_init


def _subln(o, g, lam_init):
    return _rmsnorm(o, g) * (1.0 - lam_init)


def _lane_lo(n):
    return lax.broadcasted_iota(jnp.int32, (n, 128), 1) < HEAD_DIM


def _lane_lo_wide(n, width):
    return lax.broadcasted_iota(jnp.int32, (n, width), 1) % 128 < HEAD_DIM


def _split_pair(x, lo):
    zero = jnp.zeros_like(x)
    return jnp.where(lo, x, zero), jnp.where(lo, zero, x)


def _v_ones_pair(v, lo):
    one = jnp.ones_like(v)
    return jnp.where(lo, v, one), jnp.where(lo, one, v)


def _merge_pair(a, b, lo):
    return jnp.where(lo, a, b), pltpu.roll(jnp.where(lo, b, a), HEAD_DIM, 1)


def _subln_pair(o, g2, lam_init, lo):
    sq = o * o
    ms_a = jnp.sum(jnp.where(lo, sq, 0.0), axis=-1, keepdims=True)
    ms_b = jnp.sum(jnp.where(lo, 0.0, sq), axis=-1, keepdims=True)
    ms = jnp.where(lo, ms_a, ms_b) * (1.0 / HEAD_DIM)
    return o * lax.rsqrt(ms + NORM_EPS) * g2 * (1.0 - lam_init)


def _diff_quarters(q, n):
    quarter = lax.broadcasted_iota(jnp.int32, (n, 128), 1) // DIFF_QK_DIM
    zero = jnp.zeros_like(q)
    return [jnp.where(quarter == i, q, zero) for i in range(4)]


def _fourier(x_bf, cl, sl, cbd, sbd, wf):
    xc = _dot(x_bf, cbd).astype(BF)
    xs = _dot(x_bf, sbd).astype(BF)
    y = _dot(cl, xc) - _dot(sl, xs)
    return _dot(y.astype(BF), wf)


def _ctx_mix_kernel(lam_init, sink_ref, z_ref, lamp_ref, subg_ref, cl_ref, sl_ref, cbd_ref, sbd_ref, wf_ref,
                    o_ref):
    lo = _lane_lo(SEQ)

    for j in range(2):
        c = 128 * j
        qs = _split_pair(z_ref[:, C_NA_Q + c:C_NA_Q + c + 128], lo)
        k = z_ref[:, C_NA_K + c:C_NA_K + c + 128]
        vs = _v_ones_pair(z_ref[:, C_NA_V + c:C_NA_V + c + 128], lo)
        res = []
        for q, v in zip(qs, vs):
            s = _dot_nt(q, k)
            e = jnp.exp(s - _rowmax(s))
            res.append(_dot(e.astype(BF), v))
        num, den = _merge_pair(res[0], res[1], lo)
        o_ref[:, c:c + 128] = (num / den).astype(BF)

    lam = _diff_lambda(lamp_ref, lam_init)
    for j in range(2):
        c = 128 * j
        q4 = _diff_quarters(z_ref[:, C_DQ + c:C_DQ + c + 128], SEQ)
        k = z_ref[:, C_DK + c:C_DK + c + 128]
        vs = _v_ones_pair(z_ref[:, C_DV + c:C_DV + c + 128], lo)
        res = []
        for i in range(4):
            s = _dot_nt(q4[i], k)
            e = jnp.exp(s - _rowmax(s))
            res.append(_dot(e.astype(BF), vs[i // 2]))
        n1, d1 = _merge_pair(res[0], res[2], lo)
        n2, d2 = _merge_pair(res[1], res[3], lo)
        o = n1 / d1 - lam * (n2 / d2)
        o_ref[:, 256 + c:256 + c + 128] = _subln_pair(o, subg_ref[...], lam_init, lo).astype(BF)

    o_c = _fourier(z_ref[:, C_FC:C_FC + 256], cl_ref[...], sl_ref[...], cbd_ref[...], sbd_ref[...], wf_ref[...])
    o_ref[:, 512:768] = o_c.astype(BF)

    for g in range(2):
        c = 128 * g
        qs = _split_pair(z_ref[:, C_SQ + c:C_SQ + c + 128], lo)
        k = z_ref[:, ZB_SK + c:ZB_SK + c + 128]
        vs = _v_ones_pair(z_ref[:, ZB_SV + c:ZB_SV + c + 128], lo)
        res, extra = [], []
        for i, (q, v) in enumerate(zip(qs, vs)):
            sink = sink_ref[2 * g + i]
            s = _dot_nt(q, k)
            m = jnp.maximum(_rowmax(s), sink)
            e = jnp.exp(s - m)
            res.append(_dot(e.astype(BF), v))
            extra.append(jnp.exp(sink - m))
        num, den = _merge_pair(res[0], res[1], lo)
        den = den + jnp.where(lo, extra[0], extra[1])
        o_ref[:, 768 + c:768 + c + 128] = (num / den).astype(BF)


def _ctx_mix(zb, sink, lamp, subg, cl, sl, cbd, sbd, wf, lam_init):
    full = lambda shape: pl.BlockSpec(shape, lambda b, s: (0,) * len(shape))
    grid_spec = pltpu.PrefetchScalarGridSpec(
        num_scalar_prefetch=1,
        grid=(BATCH,),
        in_specs=[
            pl.BlockSpec((SEQ, ZB_WIDTH), lambda b, s: (b, 0)),
            full((4, DIFF_QK_DIM)),
            full((1, 2 * HEAD_DIM)),
            full((SEQ, SEQ)), full((SEQ, SEQ)),
            full((256, 256)), full((256, 256)), full((256, 256)),
        ],
        out_specs=pl.BlockSpec((SEQ, MIX_WIDTH), lambda b, s: (b, 0)),
    )
    return pl.pallas_call(
        functools.partial(_ctx_mix_kernel, lam_init),
        grid_spec=grid_spec,
        out_shape=jax.ShapeDtypeStruct((N_TOK, MIX_WIDTH), BF),
        compiler_params=_params(),
        name="ctx_mix",
    )(sink, zb, lamp, subg, cl, sl, cbd, sbd, wf)


def _lat_out_spec(col_block, n_prefetch=0):
    if n_prefetch:
        return pl.BlockSpec((DEC_SEQ, 256), lambda b, s: (LAT_BLOCK0 + b, col_block))
    return pl.BlockSpec((DEC_SEQ, 256), lambda b: (LAT_BLOCK0 + b, col_block))


def _cache_v_ones(cv_ref, cva, cvb):
    lo = _lane_lo_wide(PAST_LEN, 256)
    cv = cv_ref[...]
    cva[...] = jnp.where(lo, cv, 1.0).astype(BF)
    cvb[...] = jnp.where(lo, 1.0, cv).astype(BF)


def _lat_na_kernel(z_ref, ck_ref, cv_ref, nb_ref, cat_ref, o_ref, ckb, cva, cvb):
    del cat_ref
    ckb[...] = ck_ref[...].astype(BF)
    _cache_v_ones(cv_ref, cva, cvb)
    n_q = NA_QROWS * GRID_W
    n_loc = NA_WIN_ROWS * GRID_W
    lo_q = _lane_lo(n_q)
    lo_w = _lane_lo(n_loc)

    def body(p, carry):
        w_row = jnp.clip(NA_QROWS * p - NA_KH // 2, 0, GRID_ROWS - NA_WIN_ROWS)
        tbl = jnp.minimum(p, 2) + jnp.maximum(p - 5, 0)
        q0 = pl.multiple_of(p * n_q, n_q)
        k0 = pl.multiple_of(w_row * GRID_W, NA_QROWS * GRID_W)
        for j in range(2):
            c = 128 * j
            qs = _split_pair(z_ref[pl.ds(q0, n_q), C_NA_Q + c:C_NA_Q + c + 128], lo_q)
            k_win = z_ref[pl.ds(k0, n_loc), C_NA_K + c:C_NA_K + c + 128]
            vs = _v_ones_pair(z_ref[pl.ds(k0, n_loc), C_NA_V + c:C_NA_V + c + 128], lo_w)
            cvs = (cva[:, c:c + 128], cvb[:, c:c + 128])
            res = []
            for i in range(2):
                s_loc = _dot_nt(qs[i], k_win) + nb_ref[2 * j + i, tbl]
                s_ctx = _dot_nt(qs[i], ckb[:, c:c + 128])
                m = _rowmax(s_loc, s_ctx)
                e_loc = jnp.exp(s_loc - m)
                e_ctx = jnp.exp(s_ctx - m)
                res.append(_dot(e_loc.astype(BF), vs[i]) + _dot(e_ctx.astype(BF), cvs[i]))
            num, den = _merge_pair(res[0], res[1], lo_q)
            o_ref[pl.ds(q0, n_q), c:c + 128] = (num / den).astype(BF)
        return carry

    lax.fori_loop(0, GRID_ROWS // NA_QROWS, body, 0)


def _lat_na(zb, cache_k, cache_v, nb, cat, layer):
    n_q = NA_QROWS * GRID_W
    n_loc = NA_WIN_ROWS * GRID_W
    return pl.pallas_call(
        _lat_na_kernel,
        grid=(DEC_BATCH,),
        in_specs=[
            pl.BlockSpec((DEC_SEQ, 768), lambda b: (LAT_BLOCK0 + b, 0)),
            pl.BlockSpec((None, None, PAST_LEN, 256), lambda b: (b, layer, 0, 0)),
            pl.BlockSpec((None, None, PAST_LEN, 256), lambda b: (b, layer, 0, 0)),
            pl.BlockSpec((None, 4, len(NA_TABLE_PAIRS), n_q, n_loc), lambda b: (layer, 0, 0, 0, 0)),
            pl.BlockSpec(memory_space=pl.ANY),
        ],
        out_specs=_lat_out_spec(0),
        out_shape=jax.ShapeDtypeStruct((N_TOK, MIX_WIDTH), BF),
        scratch_shapes=[pltpu.VMEM((PAST_LEN, 256), BF)] * 3,
        input_output_aliases={4: 0},
        compiler_params=_params(),
        name="lat_na",
    )(zb, cache_k, cache_v, nb, cat)


def _lat_diff_kernel(lam_init, z_ref, ck_ref, cv_ref, lamp_ref, subg_ref, cat_ref, o_ref,
                     ckb, cva, cvb, vla, vlb):
    del cat_ref
    ckb[...] = ck_ref[...].astype(BF)
    _cache_v_ones(cv_ref, cva, cvb)
    lo_v = _lane_lo_wide(DEC_SEQ, 256)
    v_loc = z_ref[:, 512:768]
    one = jnp.ones_like(v_loc)
    vla[...] = jnp.where(lo_v, v_loc, one)
    vlb[...] = jnp.where(lo_v, one, v_loc)
    lam = _diff_lambda(lamp_ref, lam_init)
    qblk = 128
    lo_q = _lane_lo(qblk)

    def body(i, carry):
        q0 = pl.multiple_of(i * qblk, qblk)
        for j in range(2):
            c = 128 * j
            q4 = _diff_quarters(z_ref[pl.ds(q0, qblk), c:c + 128], qblk)
            k_loc = z_ref[:, 256 + c:256 + c + 128]
            k_ctx = ckb[:, c:c + 128]
            v_locs = (vla[:, c:c + 128], vlb[:, c:c + 128])
            v_ctxs = (cva[:, c:c + 128], cvb[:, c:c + 128])
            res = []
            for t in range(4):
                s_loc = _dot_nt(q4[t], k_loc)
                s_ctx = _dot_nt(q4[t], k_ctx)
                m = _rowmax(s_loc, s_ctx)
                e_loc = jnp.exp(s_loc - m)
                e_ctx = jnp.exp(s_ctx - m)
                res.append(_dot(e_loc.astype(BF), v_locs[t // 2]) + _dot(e_ctx.astype(BF), v_ctxs[t // 2]))
            n1, d1 = _merge_pair(res[0], res[2], lo_q)
            n2, d2 = _merge_pair(res[1], res[3], lo_q)
            o = n1 / d1 - lam * (n2 / d2)
            o_ref[pl.ds(q0, qblk), c:c + 128] = _subln_pair(o, subg_ref[...], lam_init, lo_q).astype(BF)
        return carry

    lax.fori_loop(0, DEC_SEQ // qblk, body, 0)


def _lat_diff(zb, cache_k, cache_v, lamp, subg, cat, lam_init, layer):
    full = lambda shape: pl.BlockSpec(shape, lambda b: (0,) * len(shape))
    return pl.pallas_call(
        functools.partial(_lat_diff_kernel, lam_init),
        grid=(DEC_BATCH,),
        in_specs=[
            pl.BlockSpec((DEC_SEQ, 768), lambda b: (LAT_BLOCK0 + b, 1)),
            pl.BlockSpec((None, None, PAST_LEN, 256), lambda b: (b, layer, 0, 0)),
            pl.BlockSpec((None, None, PAST_LEN, 256), lambda b: (b, layer, 0, 0)),
            full((4, DIFF_QK_DIM)), full((1, 2 * HEAD_DIM)),
            pl.BlockSpec(memory_space=pl.ANY),
        ],
        out_specs=_lat_out_spec(1),
        out_shape=jax.ShapeDtypeStruct((N_TOK, MIX_WIDTH), BF),
        scratch_shapes=[pltpu.VMEM((PAST_LEN, 256), BF)] * 3 + [pltpu.VMEM((DEC_SEQ, 256), BF)] * 2,
        input_output_aliases={5: 0},
        compiler_params=_params(),
        name="lat_diff",
    )(zb, cache_k, cache_v, lamp, subg, cat)


def _lat_fnet_kernel(z_ref, cl_ref, sl_ref, cbd_ref, sbd_ref, wf_ref, cat_ref, o_ref):
    del cat_ref
    o = _fourier(z_ref[...], cl_ref[...], sl_ref[...], cbd_ref[...], sbd_ref[...], wf_ref[...])
    o_ref[...] = o.astype(BF)


def _lat_fnet(zb, cl, sl, cbd, sbd, wf, cat):
    full = lambda shape: pl.BlockSpec(shape, lambda b: (0,) * len(shape))
    return pl.pallas_call(
        _lat_fnet_kernel,
        grid=(DEC_BATCH,),
        in_specs=[
            pl.BlockSpec((DEC_SEQ, 256), lambda b: (LAT_BLOCK0 + b, C_FC // 256)),
            full((DEC_SEQ, DEC_SEQ)), full((DEC_SEQ, DEC_SEQ)),
            full((256, 256)), full((256, 256)), full((256, 256)),
            pl.BlockSpec(memory_space=pl.ANY),
        ],
        out_specs=_lat_out_spec(2),
        out_shape=jax.ShapeDtypeStruct((N_TOK, MIX_WIDTH), BF),
        input_output_aliases={6: 0},
        compiler_params=_params(),
        name="lat_fnet",
    )(zb, cl, sl, cbd, sbd, wf, cat)


def _lat_swa_kernel(sink_ref, zq_ref, zk_ref, zv_ref, ck_ref, cv_ref, cat_ref, o_ref, ckd, cva, cvb):
    del cat_ref
    W = SWA_WINDOW
    n_win = 3 * W
    lo_c = _lane_lo(PAST_LEN)
    ck = ck_ref[...]
    ck_r = pltpu.roll(ck, HEAD_DIM, 1)
    ckd[:, 0:128] = jnp.where(lo_c, ck, ck_r).astype(BF)
    ckd[:, 128:256] = jnp.where(lo_c, ck_r, ck).astype(BF)
    cv = cv_ref[...]
    cv_r = pltpu.roll(cv, HEAD_DIM, 1)
    cva[:, 0:128] = jnp.where(lo_c, cv, 1.0).astype(BF)
    cvb[:, 0:128] = jnp.where(lo_c, 1.0, cv_r).astype(BF)
    cva[:, 128:256] = jnp.where(lo_c, cv_r, 1.0).astype(BF)
    cvb[:, 128:256] = jnp.where(lo_c, 1.0, cv).astype(BF)
    lo_q = _lane_lo(W)
    lo_w = _lane_lo(n_win)

    def body(n, carry):
        q0 = pl.multiple_of(n * W, W)
        w0 = pl.multiple_of(jnp.clip((n - 1) * W, 0, DEC_SEQ - n_win), W)
        qpos = q0 + lax.broadcasted_iota(jnp.int32, (W, n_win), 0)
        kpos = w0 + lax.broadcasted_iota(jnp.int32, (W, n_win), 1)
        valid = jnp.abs(qpos - kpos) <= W
        for g in range(2):
            c = 128 * g
            qs = _split_pair(zq_ref[pl.ds(q0, W), c:c + 128], lo_q)
            k_win = zk_ref[pl.ds(w0, n_win), c:c + 128]
            vs = _v_ones_pair(zv_ref[pl.ds(w0, n_win), c:c + 128], lo_w)
            cvs = (cva[:, c:c + 128], cvb[:, c:c + 128])
            res, extra = [], []
            for i in range(2):
                sink = sink_ref[2 * g + i]
                s_loc = jnp.where(valid, _dot_nt(qs[i], k_win), NEG_INF)
                s_ctx = _dot_nt(qs[i], ckd[:, c:c + 128])
                m = jnp.maximum(_rowmax(s_loc, s_ctx), sink)
                e_loc = jnp.exp(s_loc - m)
                e_ctx = jnp.exp(s_ctx - m)
                res.append(_dot(e_loc.astype(BF), vs[i]) + _dot(e_ctx.astype(BF), cvs[i]))
                extra.append(jnp.exp(sink - m))
            num, den = _merge_pair(res[0], res[1], lo_q)
            den = den + jnp.where(lo_q, extra[0], extra[1])
            o_ref[pl.ds(q0, W), c:c + 128] = (num / den).astype(BF)
        return carry

    lax.fori_loop(0, DEC_SEQ // W, body, 0)


def _lat_swa(zb, cache_k, cache_v, sink, cat, layer):
    grid_spec = pltpu.PrefetchScalarGridSpec(
        num_scalar_prefetch=1,
        grid=(DEC_BATCH,),
        in_specs=[
            pl.BlockSpec((DEC_SEQ, 256), lambda b, s: (LAT_BLOCK0 + b, C_SQ // 256)),
            pl.BlockSpec((DEC_SEQ, 256), lambda b, s: (LAT_BLOCK0 + b, ZB_SK // 256)),
            pl.BlockSpec((DEC_SEQ, 256), lambda b, s: (LAT_BLOCK0 + b, ZB_SV // 256)),
            pl.BlockSpec((None, None, PAST_LEN, 128), lambda b, s: (b, layer, 0, 0)),
            pl.BlockSpec((None, None, PAST_LEN, 128), lambda b, s: (b, layer, 0, 0)),
            pl.BlockSpec(memory_space=pl.ANY),
        ],
        out_specs=_lat_out_spec(3, n_prefetch=1),
        scratch_shapes=[pltpu.VMEM((PAST_LEN, 256), BF)] * 3,
    )
    return pl.pallas_call(
        _lat_swa_kernel,
        grid_spec=grid_spec,
        out_shape=jax.ShapeDtypeStruct((N_TOK, MIX_WIDTH), BF),
        input_output_aliases={6: 0},
        compiler_params=_params(),
        name="lat_swa",
    )(sink, zb, zb, zb, cache_k, cache_v, cat)


FF_CHUNK = 1024
MLP_CAST_STEPS = 8
WO_CAST_ROWS = MIX_WIDTH // MLP_CAST_STEPS
W1_CAST_ROWS = D_MODEL // MLP_CAST_STEPS
W2_CAST_ROWS = D_FF // MLP_CAST_STEPS


def _out_mlp_kernel(last, xp_ref, xs_ref, cat_ref, m_ref, g_ref, fg_ref, wo_ref, w1_ref, w2_ref,
                    op_ref, os_ref, wob, w1b, w2b):
    s = pl.program_id(0)

    @pl.when(s < MLP_CAST_STEPS)
    def _():
        wob[pl.ds(pl.multiple_of(s * WO_CAST_ROWS, WO_CAST_ROWS), WO_CAST_ROWS), :] = wo_ref[...].astype(BF)
        w1b[pl.ds(pl.multiple_of(s * W1_CAST_ROWS, W1_CAST_ROWS), W1_CAST_ROWS), :] = w1_ref[...].astype(BF)
        w2b[pl.ds(pl.multiple_of(s * W2_CAST_ROWS, W2_CAST_ROWS), W2_CAST_ROWS), :] = w2_ref[...].astype(BF)

    @pl.when(s >= MLP_CAST_STEPS)
    def _():
        t = s - MLP_CAST_STEPS
        is_ctx = t < N_CTX_TILES
        row = _mod_row(t)
        gt1 = m_ref[pl.ds(row, 1), 2 * D_MODEL:3 * D_MODEL]
        sh2 = m_ref[pl.ds(row, 1), 3 * D_MODEL:4 * D_MODEL]
        sc2 = m_ref[pl.ds(row, 1), 4 * D_MODEL:5 * D_MODEL]
        gt2 = m_ref[pl.ds(row, 1), 5 * D_MODEL:6 * D_MODEL]
        x = jnp.where(is_ctx, xp_ref[...], xs_ref[...])
        x1 = x + gt1 * _dot(cat_ref[...], wob[...])
        hh = (_rmsnorm(x1, g_ref[...]) * (1.0 + sc2) + sh2).astype(BF)
        acc = jnp.zeros((ROW_TILE, D_MODEL), F32)
        for c in range(D_FF // FF_CHUNK):
            a = jnp.maximum(_dot(hh, w1b[:, c * FF_CHUNK:(c + 1) * FF_CHUNK]), 0.0)
            acc = acc + _dot((a * a).astype(BF), w2b[c * FF_CHUNK:(c + 1) * FF_CHUNK, :])
        out = x1 + gt2 * acc
        if last:
            out = _rmsnorm(out, fg_ref[...])

        @pl.when(is_ctx)
        def _():
            op_ref[...] = out

        @pl.when(jnp.logical_not(is_ctx))
        def _():
            os_ref[...] = out


def _out_mlp(layer, xp, xs, cat, mods, g2, final_g, wo, w1, w2):
    step = lambda f: (lambda s: f(jnp.maximum(s - MLP_CAST_STEPS, 0)))
    const = lambda shape: pl.BlockSpec(shape, lambda s: (0,) * len(shape))
    chunk = lambda rows, cols: pl.BlockSpec((None, rows, cols),
                                            lambda s: (layer, jnp.minimum(s, MLP_CAST_STEPS - 1), 0))
    return pl.pallas_call(
        functools.partial(_out_mlp_kernel, layer == DEPTH - 1),
        grid=(MLP_CAST_STEPS + N_ROW_TILES,),
        in_specs=[
            pl.BlockSpec((ROW_TILE, D_MODEL), step(lambda t: (_ctx_tile(t), 0))),
            pl.BlockSpec((ROW_TILE, D_MODEL), step(lambda t: (_lat_tile(t), 0))),
            pl.BlockSpec((ROW_TILE, MIX_WIDTH), step(lambda t: (t, 0))),
            pl.BlockSpec((None, MOD_ROWS, 6 * D_MODEL), lambda s: (layer, 0, 0)),
            const((1, D_MODEL)),
            const((1, D_MODEL)),
            chunk(WO_CAST_ROWS, D_MODEL),
            chunk(W1_CAST_ROWS, D_FF),
            chunk(W2_CAST_ROWS, D_MODEL),
        ],
        out_specs=[
            pl.BlockSpec((ROW_TILE, D_MODEL), step(lambda t: (_ctx_tile(t), 0))),
            pl.BlockSpec((ROW_TILE, D_MODEL), step(lambda t: (_lat_tile(t), 0))),
        ],
        out_shape=[jax.ShapeDtypeStruct((N_CTX_TOK, D_MODEL), F32),
                   jax.ShapeDtypeStruct((N_LAT_TOK, D_MODEL), F32)],
        scratch_shapes=[pltpu.VMEM((MIX_WIDTH, D_MODEL), BF), pltpu.VMEM((D_MODEL, D_FF), BF),
                        pltpu.VMEM((D_FF, D_MODEL), BF)],
        compiler_params=_params(),
        name="out_mlp",
    )(xp, xs, cat, mods, g2.reshape(1, D_MODEL), final_g.reshape(1, D_MODEL), wo, w1, w2)


def _dft_tables(n):
    j = np.arange(n)
    ang = 2.0 * np.pi * ((j[:, None] * j[None, :]) % n) / n
    return np.cos(ang) / np.sqrt(n), np.sin(ang) / np.sqrt(n)


def _block_diag4(m):
    out = np.zeros((256, 256), m.dtype)
    for g in range(4):
        out[64 * g:64 * g + 64, 64 * g:64 * g + 64] = m
    return out


def _rope_tables(n_axis_dims):
    half = n_axis_dims // 2
    inv = ROPE_BASE ** (-np.arange(half, dtype=np.float64) / half)
    t = np.arange(DEC_SEQ)
    lane = np.arange(128)
    w = lane % n_axis_dims
    is_col = (lane // n_axis_dims) % 2 == 1
    pos = np.where(is_col[None, :], (t % GRID_W)[:, None], (t // GRID_W)[:, None]).astype(np.float64)
    ang = pos * inv[w % half][None, :]
    sign = np.where(w < half, -1.0, 1.0)[None, :]
    return jnp.asarray(np.cos(ang), F32), jnp.asarray(np.sin(ang) * sign, F32)


def _na_bias_tables(rpb):
    lead = rpb.shape[:2]
    col = np.arange(GRID_W)
    cstart = np.clip(col - NA_KW // 2, 0, GRID_W - NA_KW)
    cmask = (col[None, :] >= cstart[:, None]) & (col[None, :] < cstart[:, None] + NA_KW)
    pad = GRID_W - NA_KW
    rp = jnp.pad(rpb, ((0, 0), (0, 0), (0, 0), (pad, pad)))
    t = jnp.stack([rp[..., pad + NA_KW - 1 - c:pad + NA_KW - 1 - c + GRID_W] for c in range(GRID_W)], axis=-2)
    t = jnp.where(cmask, t, NEG_INF)
    t = t.transpose(0, 1, 3, 2, 4).reshape(lead + (GRID_W, (2 * NA_KH - 1) * GRID_W))
    tables = []
    for p in NA_TABLE_PAIRS:
        w_row = int(np.clip(NA_QROWS * p - NA_KH // 2, 0, GRID_ROWS - NA_WIN_ROWS))
        rows = []
        for r in (NA_QROWS * p, NA_QROWS * p + 1):
            rs = int(np.clip(r - NA_KH // 2, 0, GRID_ROWS - NA_KH))
            j_lo, j_hi = rs - w_row, rs - w_row + NA_KH
            d_lo = rs - r + NA_KH - 1
            body = t[..., d_lo * GRID_W:(d_lo + NA_KH) * GRID_W]
            rows.append(jnp.pad(body, ((0, 0),) * 3 + ((j_lo * GRID_W, (NA_WIN_ROWS - j_hi) * GRID_W),),
                                constant_values=NEG_INF))
        tables.append(jnp.concatenate(rows, axis=-2))
    return jnp.stack(tables, axis=2)


def kernel(x_prompt, x_sample, cache_na_k, cache_na_v, cache_diff_k, cache_diff_v, cache_swa_k, cache_swa_v, c, c_ctx, w_ada, b_ada, norm1_g, norm2_g, w_in, na_rpb, diff_lq1, diff_lk1, diff_lq2, diff_lk2, diff_subln_g, w_fourier, swa_sink, w_out, w_mlp1, w_mlp2, final_g):
    xp = x_prompt.reshape(N_CTX_TOK, D_MODEL)
    xs = x_sample.reshape(N_LAT_TOK, D_MODEL)
    cond = jnp.zeros((MOD_ROWS, D_MODEL), F32).at[0].set(c_ctx).at[1:1 + DEC_BATCH].set(c)
    mods = _ada(cond, w_ada, b_ada)

    cl_p, sl_p = _dft_tables(SEQ)
    cl_s, sl_s = _dft_tables(DEC_SEQ)
    c64, s64 = _dft_tables(64)
    cl_p, sl_p, cl_s, sl_s, cbd, sbd = (
        jnp.asarray(a, F32).astype(BF) for a in (cl_p, sl_p, cl_s, sl_s, _block_diag4(c64), _block_diag4(s64)))
    ropes = _rope_tables(16) + _rope_tables(32)
    nb = _na_bias_tables(na_rpb)
    wf_bf = w_fourier.astype(BF)

    ck_na = cache_na_k.reshape(DEC_BATCH, DEPTH, PAST_LEN, 256)
    cv_na = cache_na_v.reshape(DEC_BATCH, DEPTH, PAST_LEN, 256)
    ck_df = cache_diff_k.reshape(DEC_BATCH, DEPTH, PAST_LEN, 256)
    cv_df = cache_diff_v.reshape(DEC_BATCH, DEPTH, PAST_LEN, 256)
    ck_sw = cache_swa_k.reshape(DEC_BATCH, DEPTH, PAST_LEN, 128)
    cv_sw = cache_swa_v.reshape(DEC_BATCH, DEPTH, PAST_LEN, 128)

    caches = None
    for l in range(DEPTH):
        lam_init = 0.8 - 0.6 * math.exp(-0.3 * l)
        lamp = jnp.stack([diff_lq1[l], diff_lk1[l], diff_lq2[l], diff_lk2[l]], axis=0)
        subg = jnp.tile(diff_subln_g[l].reshape(1, HEAD_DIM), (1, 2))

        zb, caches = _proj_in(l, xp, xs, mods, norm1_g[l], w_in, ropes, caches)
        cat = _ctx_mix(zb, swa_sink[l], lamp, subg, cl_p, sl_p, cbd, sbd, wf_bf[l], lam_init)
        cat = _lat_na(zb, ck_na, cv_na, nb, cat, l)
        cat = _lat_diff(zb, ck_df, cv_df, lamp, subg, cat, lam_init, l)
        cat = _lat_fnet(zb, cl_s, sl_s, cbd, sbd, wf_bf[l], cat)
        cat = _lat_swa(zb, ck_sw, cv_sw, swa_sink[l], cat, l)
        xp, xs = _out_mlp(l, xp, xs, cat, mods, norm2_g[l], final_g, w_out, w_mlp1, w_mlp2)

    y_prompt = xp.reshape(BATCH, SEQ, D_MODEL)
    y_sample = xs.reshape(DEC_BATCH, DEC_SEQ, D_MODEL)
    new = [a.reshape(BATCH, DEPTH, SEQ, a.shape[-1] // HEAD_DIM, HEAD_DIM) for a in caches]
    return (y_prompt, y_sample) + tuple(new)
```

```python
import functools
import math

import numpy as np
import jax
import jax.numpy as jnp
from jax import lax
from jax.experimental import pallas as pl
from jax.experimental.pallas import tpu as pltpu

D_MODEL = 1024
BATCH = 16
SEQ = 256
DEPTH = 4
DEC_BATCH = 2
DEC_SEQ = 1024
PAST_LEN = 512
GRID_W = 64
GRID_ROWS = DEC_SEQ // GRID_W
HEAD_DIM = 64
NA_KH = 8
NA_KW = 16
DIFF_QK_DIM = 32
SWA_WINDOW = 128
D_FF = 4 * D_MODEL
ROPE_BASE = 10000.0
NORM_EPS = 1e-6
NEG_INF = -1e30
IN_WIDTH = 2304
MIX_WIDTH = 1024

N_CTX_TOK = BATCH * SEQ
N_LAT_TOK = DEC_BATCH * DEC_SEQ
N_TOK = N_CTX_TOK + N_LAT_TOK
ROW_TILE = 256
N_ROW_TILES = N_TOK // ROW_TILE
N_CTX_TILES = N_CTX_TOK // ROW_TILE
N_LAT_TILES = N_LAT_TOK // ROW_TILE
LAT_TILES_PER_REQ = DEC_SEQ // ROW_TILE
LAT_BLOCK0 = N_CTX_TOK // DEC_SEQ
MOD_ROWS = 8

C_NA_Q, C_NA_K, C_NA_V = 0, 256, 512
C_DQ, C_DK, C_DV = 768, 1024, 1280
C_FC = 1536
C_SQ, C_SK, C_SV = 1792, 2048, 2176
CACHE_COLS = ((C_NA_K, 256), (C_NA_V, 256), (C_DK, 256), (C_DV, 256), (C_SK, 128), (C_SV, 128))
ZB_SK, ZB_SV = 2048, 2304
ZB_WIDTH = 2560

NA_QROWS = 2
NA_WIN_ROWS = 10
NA_TABLE_OF_PAIR = (0, 1, 2, 2, 2, 2, 3, 4)
NA_TABLE_PAIRS = (0, 1, 2, 6, 7)

BF = jnp.bfloat16
F32 = jnp.float32
VMEM_LIMIT = 56 * 1024 * 1024


def _dot(a, b):
    return jnp.dot(a, b, preferred_element_type=F32)


def _dot_nt(a, b):
    return lax.dot_general(a, b, (((1,), (1,)), ((), ())), preferred_element_type=F32)


def _rmsnorm(x, g):
    ms = jnp.mean(x * x, axis=-1, keepdims=True)
    return x * lax.rsqrt(ms + NORM_EPS) * g


def _params(n_grid=1):
    return pltpu.CompilerParams(dimension_semantics=("arbitrary",) * n_grid, vmem_limit_bytes=VMEM_LIMIT)


def _rowmax(*parts):
    m = jnp.max(parts[0], axis=-1, keepdims=True)
    for p in parts[1:]:
        m = jnp.maximum(m, jnp.max(p, axis=-1, keepdims=True))
    return m


def _rowsum(*parts):
    s = jnp.sum(parts[0], axis=-1, keepdims=True)
    for p in parts[1:]:
        s = s + jnp.sum(p, axis=-1, keepdims=True)
    return s


def _ada_kernel(cond_ref, w_ref, b_ref, o_ref):
    cnd = cond_ref[...]
    s = cnd / (1.0 + jnp.exp(-cnd))
    o_ref[...] = _dot(s.astype(BF), w_ref[...].astype(BF)) + b_ref[...]


def _ada(cond, w_ada, b_ada):
    tn = 1024
    return pl.pallas_call(
        _ada_kernel,
        grid=(DEPTH, 6 * D_MODEL // tn),
        in_specs=[
            pl.BlockSpec((MOD_ROWS, D_MODEL), lambda l, j: (0, 0)),
            pl.BlockSpec((None, D_MODEL, tn), lambda l, j: (l, 0, j)),
            pl.BlockSpec((None, 1, tn), lambda l, j: (l, 0, j)),
        ],
        out_specs=pl.BlockSpec((None, MOD_ROWS, tn), lambda l, j: (l, 0, j)),
        out_shape=jax.ShapeDtypeStruct((DEPTH, MOD_ROWS, 6 * D_MODEL), F32),
        compiler_params=_params(2),
        name="ada",
    )(cond, w_ada, b_ada.reshape(DEPTH, 1, 6 * D_MODEL))


def _mod_row(t):
    return jnp.where(t < N_CTX_TILES, 0, 1 + (t - N_CTX_TILES) // LAT_TILES_PER_REQ)


def _ctx_tile(t):
    return jnp.minimum(t, N_CTX_TILES - 1)


def _lat_tile(t):
    return jnp.maximum(t - N_CTX_TILES, 0)


TRUNK_CAST_STEPS = 16
FF_CHUNK = 1024
TRUNK_VMEM_LIMIT = 60 * 1024 * 1024


def _rope(x, cos, sin_signed, shift, first):
    partner = jnp.where(first, pltpu.roll(x, 128 - shift, 1), pltpu.roll(x, shift, 1))
    return x * cos + partner * sin_signed


def _write_projection(z, t, is_ctx, zb_ref, cache_refs, rope_refs):
    cosd_ref, sind_ref, coss_ref, sins_ref = rope_refs
    dscale = DIFF_QK_DIM ** -0.5

    def put(c0, width, val):
        zb_ref[:, c0:c0 + width] = val.astype(BF)

    lo = lax.broadcasted_iota(jnp.int32, (ROW_TILE, 128), 1) < HEAD_DIM

    def put_dup(c0, val):
        r = pltpu.roll(val, HEAD_DIM, 1)
        put(c0, 128, jnp.where(lo, val, r))
        put(c0 + 128, 128, jnp.where(lo, r, val))

    put(C_NA_Q, 256, z[:, C_NA_Q:C_NA_Q + 256] * 0.125)
    put(C_NA_K, 512, z[:, C_NA_K:C_NA_K + 512])
    put(C_DV, 512, z[:, C_DV:C_DV + 512])
    put_dup(ZB_SV, z[:, C_SV:C_SV + 128])

    @pl.when(is_ctx)
    def _():
        for ref, (c0, width) in zip(cache_refs, CACHE_COLS):
            ref[...] = z[:, c0:c0 + width]
        put(C_DQ, 256, z[:, C_DQ:C_DQ + 256] * dscale)
        put(C_DK, 256, z[:, C_DK:C_DK + 256])
        put(C_SQ, 256, z[:, C_SQ:C_SQ + 256] * 0.125)
        put_dup(ZB_SK, z[:, C_SK:C_SK + 128])

    @pl.when(jnp.logical_not(is_ctx))
    def _():
        p0 = pl.multiple_of(((t - N_CTX_TILES) % LAT_TILES_PER_REQ) * ROW_TILE, ROW_TILE)
        lane = lax.broadcasted_iota(jnp.int32, (ROW_TILE, 128), 1)
        first_d = (lane % 16) < 8
        first_s = (lane % 32) < 16
        cosd, sind = cosd_ref[pl.ds(p0, ROW_TILE), :], sind_ref[pl.ds(p0, ROW_TILE), :]
        coss, sins = coss_ref[pl.ds(p0, ROW_TILE), :], sins_ref[pl.ds(p0, ROW_TILE), :]
        for j in range(2):
            c = 128 * j
            put(C_DQ + c, 128, _rope(z[:, C_DQ + c:C_DQ + c + 128], cosd, sind, 8, first_d) * dscale)
            put(C_DK + c, 128, _rope(z[:, C_DK + c:C_DK + c + 128], cosd, sind, 8, first_d))
            put(C_SQ + c, 128, _rope(z[:, C_SQ + c:C_SQ + c + 128], coss, sins, 16, first_s) * 0.125)
        put_dup(ZB_SK, _rope(z[:, C_SK:C_SK + 128], coss, sins, 16, first_s))


def _trunk_kernel(do_mlp, do_proj, last, n_alias, *refs):
    xp_ref, xs_ref = refs[:2]
    pos = 2
    if do_mlp:
        cat_ref, mm_ref, g2_ref, fg_ref, wo_ref, w1_ref, w2_ref = refs[pos:pos + 7]
        pos += 7
    if do_proj:
        pm_ref, g1_ref, win_ref = refs[pos:pos + 3]
        rope_refs = refs[pos + 3:pos + 7]
        pos += 7 + n_alias
    if do_mlp:
        op_ref, os_ref = refs[pos:pos + 2]
        pos += 2
    if do_proj:
        zb_ref = refs[pos]
        cache_refs = refs[pos + 1:pos + 7]
        pos += 7
    if do_mlp:
        wob, w1b, w2b = refs[pos:pos + 3]
        pos += 3
    if do_proj:
        wb = refs[pos]
    s = pl.program_id(0)

    @pl.when(s < TRUNK_CAST_STEPS)
    def _():
        def cast(dst, src):
            rows = src.shape[0]
            dst[pl.ds(pl.multiple_of(s * rows, rows), rows), :] = src[...].astype(BF)

        if do_mlp:
            cast(wob, wo_ref)
            cast(w1b, w1_ref)
            cast(w2b, w2_ref)
        if do_proj:
            cast(wb, win_ref)

    @pl.when(s >= TRUNK_CAST_STEPS)
    def _():
        t = s - TRUNK_CAST_STEPS
        is_ctx = t < N_CTX_TILES
        row = _mod_row(t)
        x = jnp.where(is_ctx, xp_ref[...], xs_ref[...])
        if do_mlp:
            gt1 = mm_ref[pl.ds(row, 1), 2 * D_MODEL:3 * D_MODEL]
            sh2 = mm_ref[pl.ds(row, 1), 3 * D_MODEL:4 * D_MODEL]
            sc2 = mm_ref[pl.ds(row, 1), 4 * D_MODEL:5 * D_MODEL]
            gt2 = mm_ref[pl.ds(row, 1), 5 * D_MODEL:6 * D_MODEL]
            x = x + gt1 * _dot(cat_ref[...], wob[...])
            hh = (_rmsnorm(x, g2_ref[...]) * (1.0 + sc2) + sh2).astype(BF)
            acc = jnp.zeros((ROW_TILE, D_MODEL), F32)
            for c in range(D_FF // FF_CHUNK):
                a = jnp.maximum(_dot(hh, w1b[:, c * FF_CHUNK:(c + 1) * FF_CHUNK]), 0.0)
                acc = acc + _dot((a * a).astype(BF), w2b[c * FF_CHUNK:(c + 1) * FF_CHUNK, :])
            x = x + gt2 * acc
            out = _rmsnorm(x, fg_ref[...]) if last else x

            @pl.when(is_ctx)
            def _():
                op_ref[...] = out

            @pl.when(jnp.logical_not(is_ctx))
            def _():
                os_ref[...] = out

        if do_proj:
            shift = pm_ref[pl.ds(row, 1), 0:D_MODEL]
            scale = pm_ref[pl.ds(row, 1), D_MODEL:2 * D_MODEL]
            h = _rmsnorm(x, g1_ref[...]) * (1.0 + scale) + shift
            z = _dot(h.astype(BF), wb[...])
            _write_projection(z, t, is_ctx, zb_ref, cache_refs, rope_refs)


def _trunk(xp, xs, mods, ropes, mlp=None, proj=None):
    do_mlp, do_proj = mlp is not None, proj is not None
    last = do_mlp and not do_proj
    step = lambda f: (lambda s: f(jnp.maximum(s - TRUNK_CAST_STEPS, 0)))
    const = lambda shape: pl.BlockSpec(shape, lambda s: (0,) * len(shape), pipeline_mode=pl.Buffered(1))
    chunk = lambda layer, rows, cols: pl.BlockSpec(
        (None, rows // TRUNK_CAST_STEPS, cols), lambda s: (layer, jnp.minimum(s, TRUNK_CAST_STEPS - 1), 0))
    mod_spec = lambda layer: pl.BlockSpec((None, MOD_ROWS, 6 * D_MODEL), lambda s: (layer, 0, 0),
                                          pipeline_mode=pl.Buffered(1))
    x_specs = [pl.BlockSpec((ROW_TILE, D_MODEL), step(lambda t: (_ctx_tile(t), 0))),
               pl.BlockSpec((ROW_TILE, D_MODEL), step(lambda t: (_lat_tile(t), 0)))]
    args, in_specs = [xp, xs], list(x_specs)
    out_specs, out_shape, scratch, aliases = [], [], [], {}
    n_alias = 0
    if do_mlp:
        layer, cat, g2, final_g, wo, w1, w2 = mlp
        args += [cat, mods, g2.reshape(1, D_MODEL), final_g.reshape(1, D_MODEL), wo, w1, w2]
        in_specs += [pl.BlockSpec((ROW_TILE, MIX_WIDTH), step(lambda t: (t, 0))), mod_spec(layer),
                     const((1, D_MODEL)), const((1, D_MODEL)),
                     chunk(layer, MIX_WIDTH, D_MODEL), chunk(layer, D_MODEL, D_FF), chunk(layer, D_FF, D_MODEL)]
    if do_proj:
        p_layer, g1, w_in, caches = proj
        args += [mods, g1.reshape(1, D_MODEL), w_in, *ropes]
        in_specs += [mod_spec(p_layer), const((1, D_MODEL)), chunk(p_layer, D_MODEL, IN_WIDTH)]
        in_specs += [const((DEC_SEQ, 128))] * 4
        if caches is not None:
            n_alias = 6
            n_out_before = 2 if do_mlp else 0
            aliases = {len(args) + i: n_out_before + 1 + i for i in range(6)}
            args += list(caches)
            in_specs += [pl.BlockSpec(memory_space=pl.ANY)] * 6
    if do_mlp:
        out_specs += x_specs
        out_shape += [jax.ShapeDtypeStruct((N_CTX_TOK, D_MODEL), F32),
                      jax.ShapeDtypeStruct((N_LAT_TOK, D_MODEL), F32)]
        scratch += [pltpu.VMEM((MIX_WIDTH, D_MODEL), BF), pltpu.VMEM((D_MODEL, D_FF), BF),
                    pltpu.VMEM((D_FF, D_MODEL), BF)]
    if do_proj:
        out_specs += [pl.BlockSpec((ROW_TILE, ZB_WIDTH), step(lambda t: (t, 0)))] + [
            pl.BlockSpec((None, None, SEQ, width), step(lambda t: (_ctx_tile(t), p_layer, 0, 0)))
            for _, width in CACHE_COLS]
        out_shape += [jax.ShapeDtypeStruct((N_TOK, ZB_WIDTH), BF)] + [
            jax.ShapeDtypeStruct((BATCH, DEPTH, SEQ, width), F32) for _, width in CACHE_COLS]
        scratch += [pltpu.VMEM((D_MODEL, IN_WIDTH), BF)]
    outs = pl.pallas_call(
        functools.partial(_trunk_kernel, do_mlp, do_proj, last, n_alias),
        grid=(TRUNK_CAST_STEPS + N_ROW_TILES,),
        in_specs=in_specs,
        out_specs=out_specs,
        out_shape=out_shape,
        scratch_shapes=scratch,
        input_output_aliases=aliases,
        compiler_params=pltpu.CompilerParams(dimension_semantics=("arbitrary",),
                                             vmem_limit_bytes=TRUNK_VMEM_LIMIT),
        name="trunk_mlp_proj" if (do_mlp and do_proj) else ("trunk_mlp" if do_mlp else "trunk_proj"),
    )(*args)
    outs = list(outs)
    if do_mlp:
        xp, xs = outs[:2]
        outs = outs[2:]
    zb, new_caches = (outs[0], tuple(outs[1:7])) if do_proj else (None, None)
    return xp, xs, zb, new_caches


def _diff_lambda(lamp_ref, lam_init):
    a = jnp.sum(lamp_ref[0:1, :] * lamp_ref[1:2, :], axis=-1, keepdims=True)
    b = jnp.sum(lamp_ref[2:3, :] * lamp_ref[3:4, :], axis=-1, keepdims=True)
    return jnp.exp(a) - jnp.exp(b) + lam_init


def _subln(o, g, lam_init):
    return _rmsnorm(o, g) * (1.0 - lam_init)


def _lane_lo(n):
    return lax.broadcasted_iota(jnp.int32, (n, 128), 1) < HEAD_DIM


def _lane_lo_wide(n, width):
    return lax.broadcasted_iota(jnp.int32, (n, width), 1) % 128 < HEAD_DIM


def _split_pair(x, lo):
    zero = jnp.zeros_like(x)
    return jnp.where(lo, x, zero), jnp.where(lo, zero, x)


def _v_ones_pair(v, lo):
    one = jnp.ones_like(v)
    return jnp.where(lo, v, one), jnp.where(lo, one, v)


def _merge_pair(a, b, lo):
    return jnp.where(lo, a, b), pltpu.roll(jnp.where(lo, b, a), HEAD_DIM, 1)


def _subln_pair(o, g2, lam_init, lo):
    sq = o * o
    ms_a = jnp.sum(jnp.where(lo, sq, 0.0), axis=-1, keepdims=True)
    ms_b = jnp.sum(jnp.where(lo, 0.0, sq), axis=-1, keepdims=True)
    ms = jnp.where(lo, ms_a, ms_b) * (1.0 / HEAD_DIM)
    return o * lax.rsqrt(ms + NORM_EPS) * g2 * (1.0 - lam_init)


def _diff_quarters(q, n):
    quarter = lax.broadcasted_iota(jnp.int32, (n, 128), 1) // DIFF_QK_DIM
    zero = jnp.zeros_like(q)
    return [jnp.where(quarter == i, q, zero) for i in range(4)]


def _fourier(x_bf, cl, sl, cbd, sbd, wf):
    xc = _dot(x_bf, cbd).astype(BF)
    xs = _dot(x_bf, sbd).astype(BF)
    y = _dot(cl, xc) - _dot(sl, xs)
    return _dot(y.astype(BF), wf)


CTX_UNIT_GROUP = 4


def _ctx_mix_kernel(lam_init, sink_ref, z_ref, lamp_ref, subg_ref, cl_ref, sl_ref, cbd_ref, sbd_ref, wf_ref,
                    o_ref):
    lo = _lane_lo(SEQ)

    units = []
    for j in range(2):
        c = 128 * j
        qs = _split_pair(z_ref[:, C_NA_Q + c:C_NA_Q + c + 128], lo)
        vs = _v_ones_pair(z_ref[:, C_NA_V + c:C_NA_V + c + 128], lo)
        units += [(qs[i], (C_NA_K + c), vs[i], None) for i in range(2)]
    for j in range(2):
        c = 128 * j
        q4 = _diff_quarters(z_ref[:, C_DQ + c:C_DQ + c + 128], SEQ)
        vs = _v_ones_pair(z_ref[:, C_DV + c:C_DV + c + 128], lo)
        units += [(q4[i], (C_DK + c), vs[i // 2], None) for i in (0, 2, 1, 3)]
    for g in range(2):
        c = 128 * g
        qs = _split_pair(z_ref[:, C_SQ + c:C_SQ + c + 128], lo)
        vs = _v_ones_pair(z_ref[:, ZB_SV + c:ZB_SV + c + 128], lo)
        units += [(qs[i], (ZB_SK + c), vs[i], sink_ref[2 * g + i]) for i in range(2)]

    res, extras = [], []
    for u0 in range(0, len(units), CTX_UNIT_GROUP):
        group = units[u0:u0 + CTX_UNIT_GROUP]
        scores = [_dot_nt(q, z_ref[:, kc:kc + 128]) for q, kc, _, _ in group]
        weights = []
        for s, (_, _, _, sink) in zip(scores, group):
            m = _rowmax(s)
            if sink is not None:
                m = jnp.maximum(m, sink)
                extras.append(jnp.exp(sink - m))
            weights.append(jnp.exp(s - m).astype(BF))
        res += [_dot(e, v) for e, (_, _, v, _) in zip(weights, group)]

    for j in range(2):
        num, den = _merge_pair(res[2 * j], res[2 * j + 1], lo)
        o_ref[:, 128 * j:128 * j + 128] = (num / den).astype(BF)

    lam = _diff_lambda(lamp_ref, lam_init)
    for j in range(2):
        n1, d1 = _merge_pair(res[4 + 4 * j], res[5 + 4 * j], lo)
        n2, d2 = _merge_pair(res[6 + 4 * j], res[7 + 4 * j], lo)
        o = n1 / d1 - lam * (n2 / d2)
        o_ref[:, 256 + 128 * j:384 + 128 * j] = _subln_pair(o, subg_ref[...], lam_init, lo).astype(BF)

    o_c = _fourier(z_ref[:, C_FC:C_FC + 256], cl_ref[...], sl_ref[...], cbd_ref[...], sbd_ref[...], wf_ref[...])
    o_ref[:, 512:768] = o_c.astype(BF)

    for g in range(2):
        num, den = _merge_pair(res[12 + 2 * g], res[13 + 2 * g], lo)
        den = den + jnp.where(lo, extras[2 * g], extras[2 * g + 1])
        o_ref[:, 768 + 128 * g:896 + 128 * g] = (num / den).astype(BF)


def _ctx_mix(zb, sink, lamp, subg, cl, sl, cbd, sbd, wf, lam_init):
    full = lambda shape: pl.BlockSpec(shape, lambda b, s: (0,) * len(shape))
    grid_spec = pltpu.PrefetchScalarGridSpec(
        num_scalar_prefetch=1,
        grid=(BATCH,),
        in_specs=[
            pl.BlockSpec((SEQ, ZB_WIDTH), lambda b, s: (b, 0)),
            full((4, DIFF_QK_DIM)),
            full((1, 2 * HEAD_DIM)),
            full((SEQ, SEQ)), full((SEQ, SEQ)),
            full((256, 256)), full((256, 256)), full((256, 256)),
        ],
        out_specs=pl.BlockSpec((SEQ, MIX_WIDTH), lambda b, s: (b, 0)),
    )
    return pl.pallas_call(
        functools.partial(_ctx_mix_kernel, lam_init),
        grid_spec=grid_spec,
        out_shape=jax.ShapeDtypeStruct((N_TOK, MIX_WIDTH), BF),
        compiler_params=_params(),
        name="ctx_mix",
    )(sink, zb, lamp, subg, cl, sl, cbd, sbd, wf)


def _lat_out_spec(col_block, n_prefetch=0):
    if n_prefetch:
        return pl.BlockSpec((DEC_SEQ, 256), lambda b, s: (LAT_BLOCK0 + b, col_block))
    return pl.BlockSpec((DEC_SEQ, 256), lambda b: (LAT_BLOCK0 + b, col_block))


def _cache_v_ones(cv_ref, cva, cvb):
    lo = _lane_lo_wide(PAST_LEN, 256)
    cv = cv_ref[...]
    cva[...] = jnp.where(lo, cv, 1.0).astype(BF)
    cvb[...] = jnp.where(lo, 1.0, cv).astype(BF)


def _lat_na_kernel(z_ref, ck_ref, cv_ref, nb_ref, cat_ref, o_ref, ckb, cva, cvb):
    del cat_ref
    ckb[...] = ck_ref[...].astype(BF)
    _cache_v_ones(cv_ref, cva, cvb)
    n_q = NA_QROWS * GRID_W
    n_loc = NA_WIN_ROWS * GRID_W
    lo_q = _lane_lo(n_q)
    lo_w = _lane_lo(n_loc)

    def body(p, carry):
        w_row = jnp.clip(NA_QROWS * p - NA_KH // 2, 0, GRID_ROWS - NA_WIN_ROWS)
        tbl = jnp.minimum(p, 2) + jnp.maximum(p - 5, 0)
        q0 = pl.multiple_of(p * n_q, n_q)
        k0 = pl.multiple_of(w_row * GRID_W, NA_QROWS * GRID_W)
        scores = []
        for h in range(4):
            c = 128 * (h // 2)
            q = _split_pair(z_ref[pl.ds(q0, n_q), C_NA_Q + c:C_NA_Q + c + 128], lo_q)[h % 2]
            s_loc = _dot_nt(q, z_ref[pl.ds(k0, n_loc), C_NA_K + c:C_NA_K + c + 128]) + nb_ref[h, tbl]
            s_ctx = _dot_nt(q, ckb[:, c:c + 128])
            scores.append((s_loc, s_ctx))
        weights = []
        for s_loc, s_ctx in scores:
            m = _rowmax(s_loc, s_ctx)
            weights.append((jnp.exp(s_loc - m).astype(BF), jnp.exp(s_ctx - m).astype(BF)))
        res = []
        for h, (e_loc, e_ctx) in enumerate(weights):
            c = 128 * (h // 2)
            v = _v_ones_pair(z_ref[pl.ds(k0, n_loc), C_NA_V + c:C_NA_V + c + 128], lo_w)[h % 2]
            cv = (cva, cvb)[h % 2][:, c:c + 128]
            res.append(_dot(e_loc, v) + _dot(e_ctx, cv))
        for j in range(2):
            num, den = _merge_pair(res[2 * j], res[2 * j + 1], lo_q)
            o_ref[pl.ds(q0, n_q), 128 * j:128 * j + 128] = (num / den).astype(BF)
        return carry

    lax.fori_loop(0, GRID_ROWS // NA_QROWS, body, 0)


def _lat_na(zb, cache_k, cache_v, nb, cat, layer):
    n_q = NA_QROWS * GRID_W
    n_loc = NA_WIN_ROWS * GRID_W
    return pl.pallas_call(
        _lat_na_kernel,
        grid=(DEC_BATCH,),
        in_specs=[
            pl.BlockSpec((DEC_SEQ, 768), lambda b: (LAT_BLOCK0 + b, 0)),
            pl.BlockSpec((None, None, PAST_LEN, 256), lambda b: (b, layer, 0, 0)),
            pl.BlockSpec((None, None, PAST_LEN, 256), lambda b: (b, layer, 0, 0)),
            pl.BlockSpec((None, 4, len(NA_TABLE_PAIRS), n_q, n_loc), lambda b: (layer, 0, 0, 0, 0)),
            pl.BlockSpec(memory_space=pl.ANY),
        ],
        out_specs=_lat_out_spec(0),
        out_shape=jax.ShapeDtypeStruct((N_TOK, MIX_WIDTH), BF),
        scratch_shapes=[pltpu.VMEM((PAST_LEN, 256), BF)] * 3,
        input_output_aliases={4: 0},
        compiler_params=_params(),
        name="lat_na",
    )(zb, cache_k, cache_v, nb, cat)


DIFF_QBLK = 512


def _lat_diff_kernel(lam_init, z_ref, ck_ref, cv_ref, lamp_ref, subg_ref, cat_ref, o_ref,
                     ckb, cva, cvb, vla, vlb):
    del cat_ref
    ckb[...] = ck_ref[...].astype(BF)
    _cache_v_ones(cv_ref, cva, cvb)
    lo_v = _lane_lo_wide(DEC_SEQ, 256)
    v_loc = z_ref[:, 512:768]
    one = jnp.ones_like(v_loc)
    vla[...] = jnp.where(lo_v, v_loc, one)
    vlb[...] = jnp.where(lo_v, one, v_loc)
    lam = _diff_lambda(lamp_ref, lam_init)
    qblk = DIFF_QBLK
    lo_q = _lane_lo(qblk)

    def body(i, carry):
        q0 = pl.multiple_of(i * qblk, qblk)
        for j in range(2):
            c = 128 * j
            q4 = _diff_quarters(z_ref[pl.ds(q0, qblk), c:c + 128], qblk)
            scores = [(_dot_nt(q, z_ref[:, 256 + c:256 + c + 128]), _dot_nt(q, ckb[:, c:c + 128])) for q in q4]
            weights = []
            for s_loc, s_ctx in scores:
                m = _rowmax(s_loc, s_ctx)
                weights.append((jnp.exp(s_loc - m).astype(BF), jnp.exp(s_ctx - m).astype(BF)))
            res = []
            for t, (e_loc, e_ctx) in enumerate(weights):
                v_loc, v_ctx = ((vla, cva), (vlb, cvb))[t // 2]
                res.append(_dot(e_loc, v_loc[:, c:c + 128]) + _dot(e_ctx, v_ctx[:, c:c + 128]))
            n1, d1 = _merge_pair(res[0], res[2], lo_q)
            n2, d2 = _merge_pair(res[1], res[3], lo_q)
            o = n1 / d1 - lam * (n2 / d2)
            o_ref[pl.ds(q0, qblk), c:c + 128] = _subln_pair(o, subg_ref[...], lam_init, lo_q).astype(BF)
        return carry

    lax.fori_loop(0, DEC_SEQ // qblk, body, 0)


def _lat_diff(zb, cache_k, cache_v, lamp, subg, cat, lam_init, layer):
    full = lambda shape: pl.BlockSpec(shape, lambda b: (0,) * len(shape))
    return pl.pallas_call(
        functools.partial(_lat_diff_kernel, lam_init),
        grid=(DEC_BATCH,),
        in_specs=[
            pl.BlockSpec((DEC_SEQ, 768), lambda b: (LAT_BLOCK0 + b, 1)),
            pl.BlockSpec((None, None, PAST_LEN, 256), lambda b: (b, layer, 0, 0)),
            pl.BlockSpec((None, None, PAST_LEN, 256), lambda b: (b, layer, 0, 0)),
            full((4, DIFF_QK_DIM)), full((1, 2 * HEAD_DIM)),
            pl.BlockSpec(memory_space=pl.ANY),
        ],
        out_specs=_lat_out_spec(1),
        out_shape=jax.ShapeDtypeStruct((N_TOK, MIX_WIDTH), BF),
        scratch_shapes=[pltpu.VMEM((PAST_LEN, 256), BF)] * 3 + [pltpu.VMEM((DEC_SEQ, 256), BF)] * 2,
        input_output_aliases={5: 0},
        compiler_params=_params(),
        name="lat_diff",
    )(zb, cache_k, cache_v, lamp, subg, cat)


def _lat_fnet_kernel(z_ref, cl_ref, sl_ref, cbd_ref, sbd_ref, wf_ref, cat_ref, o_ref):
    del cat_ref
    o = _fourier(z_ref[...], cl_ref[...], sl_ref[...], cbd_ref[...], sbd_ref[...], wf_ref[...])
    o_ref[...] = o.astype(BF)


def _lat_fnet(zb, cl, sl, cbd, sbd, wf, cat):
    full = lambda shape: pl.BlockSpec(shape, lambda b: (0,) * len(shape))
    return pl.pallas_call(
        _lat_fnet_kernel,
        grid=(DEC_BATCH,),
        in_specs=[
            pl.BlockSpec((DEC_SEQ, 256), lambda b: (LAT_BLOCK0 + b, C_FC // 256)),
            full((DEC_SEQ, DEC_SEQ)), full((DEC_SEQ, DEC_SEQ)),
            full((256, 256)), full((256, 256)), full((256, 256)),
            pl.BlockSpec(memory_space=pl.ANY),
        ],
        out_specs=_lat_out_spec(2),
        out_shape=jax.ShapeDtypeStruct((N_TOK, MIX_WIDTH), BF),
        input_output_aliases={6: 0},
        compiler_params=_params(),
        name="lat_fnet",
    )(zb, cl, sl, cbd, sbd, wf, cat)


def _lat_swa_kernel(sink_ref, zq_ref, zk_ref, zv_ref, ck_ref, cv_ref, cat_ref, o_ref, ckd, cva, cvb):
    del cat_ref
    W = SWA_WINDOW
    n_win = 3 * W
    lo_c = _lane_lo(PAST_LEN)
    ck = ck_ref[...]
    ck_r = pltpu.roll(ck, HEAD_DIM, 1)
    ckd[:, 0:128] = jnp.where(lo_c, ck, ck_r).astype(BF)
    ckd[:, 128:256] = jnp.where(lo_c, ck_r, ck).astype(BF)
    cv = cv_ref[...]
    cv_r = pltpu.roll(cv, HEAD_DIM, 1)
    cva[:, 0:128] = jnp.where(lo_c, cv, 1.0).astype(BF)
    cvb[:, 0:128] = jnp.where(lo_c, 1.0, cv_r).astype(BF)
    cva[:, 128:256] = jnp.where(lo_c, cv_r, 1.0).astype(BF)
    cvb[:, 128:256] = jnp.where(lo_c, 1.0, cv).astype(BF)
    lo_q = _lane_lo(W)
    lo_w = _lane_lo(n_win)

    def body(n, carry):
        q0 = pl.multiple_of(n * W, W)
        w0 = pl.multiple_of(jnp.clip((n - 1) * W, 0, DEC_SEQ - n_win), W)
        qpos = q0 + lax.broadcasted_iota(jnp.int32, (W, n_win), 0)
        kpos = w0 + lax.broadcasted_iota(jnp.int32, (W, n_win), 1)
        valid = jnp.abs(qpos - kpos) <= W
        scores = []
        for h in range(4):
            c = 128 * (h // 2)
            q = _split_pair(zq_ref[pl.ds(q0, W), c:c + 128], lo_q)[h % 2]
            s_loc = jnp.where(valid, _dot_nt(q, zk_ref[pl.ds(w0, n_win), c:c + 128]), NEG_INF)
            scores.append((s_loc, _dot_nt(q, ckd[:, c:c + 128])))
        weights, extras = [], []
        for h, (s_loc, s_ctx) in enumerate(scores):
            sink = sink_ref[h]
            m = jnp.maximum(_rowmax(s_loc, s_ctx), sink)
            weights.append((jnp.exp(s_loc - m).astype(BF), jnp.exp(s_ctx - m).astype(BF)))
            extras.append(jnp.exp(sink - m))
        res = []
        for h, (e_loc, e_ctx) in enumerate(weights):
            c = 128 * (h // 2)
            v = _v_ones_pair(zv_ref[pl.ds(w0, n_win), c:c + 128], lo_w)[h % 2]
            res.append(_dot(e_loc, v) + _dot(e_ctx, (cva, cvb)[h % 2][:, c:c + 128]))
        for g in range(2):
            num, den = _merge_pair(res[2 * g], res[2 * g + 1], lo_q)
            den = den + jnp.where(lo_q, extras[2 * g], extras[2 * g + 1])
            o_ref[pl.ds(q0, W), 128 * g:128 * g + 128] = (num / den).astype(BF)
        return carry

    lax.fori_loop(0, DEC_SEQ // W, body, 0)


def _lat_swa(zb, cache_k, cache_v, sink, cat, layer):
    grid_spec = pltpu.PrefetchScalarGridSpec(
        num_scalar_prefetch=1,
        grid=(DEC_BATCH,),
        in_specs=[
            pl.BlockSpec((DEC_SEQ, 256), lambda b, s: (LAT_BLOCK0 + b, C_SQ // 256)),
            pl.BlockSpec((DEC_SEQ, 256), lambda b, s: (LAT_BLOCK0 + b, ZB_SK // 256)),
            pl.BlockSpec((DEC_SEQ, 256), lambda b, s: (LAT_BLOCK0 + b, ZB_SV // 256)),
            pl.BlockSpec((None, None, PAST_LEN, 128), lambda b, s: (b, layer, 0, 0)),
            pl.BlockSpec((None, None, PAST_LEN, 128), lambda b, s: (b, layer, 0, 0)),
            pl.BlockSpec(memory_space=pl.ANY),
        ],
        out_specs=_lat_out_spec(3, n_prefetch=1),
        scratch_shapes=[pltpu.VMEM((PAST_LEN, 256), BF)] * 3,
    )
    return pl.pallas_call(
        _lat_swa_kernel,
        grid_spec=grid_spec,
        out_shape=jax.ShapeDtypeStruct((N_TOK, MIX_WIDTH), BF),
        input_output_aliases={6: 0},
        compiler_params=_params(),
        name="lat_swa",
    )(sink, zb, zb, zb, cache_k, cache_v, cat)


def _dft_tables(n):
    j = np.arange(n)
    ang = 2.0 * np.pi * ((j[:, None] * j[None, :]) % n) / n
    return np.cos(ang) / np.sqrt(n), np.sin(ang) / np.sqrt(n)


def _block_diag4(m):
    out = np.zeros((256, 256), m.dtype)
    for g in range(4):
        out[64 * g:64 * g + 64, 64 * g:64 * g + 64] = m
    return out


def _rope_tables(n_axis_dims):
    half = n_axis_dims // 2
    inv = ROPE_BASE ** (-np.arange(half, dtype=np.float64) / half)
    t = np.arange(DEC_SEQ)
    lane = np.arange(128)
    w = lane % n_axis_dims
    is_col = (lane // n_axis_dims) % 2 == 1
    pos = np.where(is_col[None, :], (t % GRID_W)[:, None], (t // GRID_W)[:, None]).astype(np.float64)
    ang = pos * inv[w % half][None, :]
    sign = np.where(w < half, -1.0, 1.0)[None, :]
    return jnp.asarray(np.cos(ang), F32), jnp.asarray(np.sin(ang) * sign, F32)


def _na_bias_tables(rpb):
    lead = rpb.shape[:2]
    col = np.arange(GRID_W)
    cstart = np.clip(col - NA_KW // 2, 0, GRID_W - NA_KW)
    cmask = (col[None, :] >= cstart[:, None]) & (col[None, :] < cstart[:, None] + NA_KW)
    pad = GRID_W - NA_KW
    rp = jnp.pad(rpb, ((0, 0), (0, 0), (0, 0), (pad, pad)))
    t = jnp.stack([rp[..., pad + NA_KW - 1 - c:pad + NA_KW - 1 - c + GRID_W] for c in range(GRID_W)], axis=-2)
    t = jnp.where(cmask, t, NEG_INF)
    t = t.transpose(0, 1, 3, 2, 4).reshape(lead + (GRID_W, (2 * NA_KH - 1) * GRID_W))
    tables = []
    for p in NA_TABLE_PAIRS:
        w_row = int(np.clip(NA_QROWS * p - NA_KH // 2, 0, GRID_ROWS - NA_WIN_ROWS))
        rows = []
        for r in (NA_QROWS * p, NA_QROWS * p + 1):
            rs = int(np.clip(r - NA_KH // 2, 0, GRID_ROWS - NA_KH))
            j_lo, j_hi = rs - w_row, rs - w_row + NA_KH
            d_lo = rs - r + NA_KH - 1
            body = t[..., d_lo * GRID_W:(d_lo + NA_KH) * GRID_W]
            rows.append(jnp.pad(body, ((0, 0),) * 3 + ((j_lo * GRID_W, (NA_WIN_ROWS - j_hi) * GRID_W),),
                                constant_values=NEG_INF))
        tables.append(jnp.concatenate(rows, axis=-2))
    return jnp.stack(tables, axis=2)


def kernel(x_prompt, x_sample, cache_na_k, cache_na_v, cache_diff_k, cache_diff_v, cache_swa_k, cache_swa_v, c, c_ctx, w_ada, b_ada, norm1_g, norm2_g, w_in, na_rpb, diff_lq1, diff_lk1, diff_lq2, diff_lk2, diff_subln_g, w_fourier, swa_sink, w_out, w_mlp1, w_mlp2, final_g):
    xp = x_prompt.reshape(N_CTX_TOK, D_MODEL)
    xs = x_sample.reshape(N_LAT_TOK, D_MODEL)
    cond = jnp.zeros((MOD_ROWS, D_MODEL), F32).at[0].set(c_ctx).at[1:1 + DEC_BATCH].set(c)
    mods = _ada(cond, w_ada, b_ada)

    cl_p, sl_p = _dft_tables(SEQ)
    cl_s, sl_s = _dft_tables(DEC_SEQ)
    c64, s64 = _dft_tables(64)
    cl_p, sl_p, cl_s, sl_s, cbd, sbd = (
        jnp.asarray(a, F32).astype(BF) for a in (cl_p, sl_p, cl_s, sl_s, _block_diag4(c64), _block_diag4(s64)))
    ropes = _rope_tables(16) + _rope_tables(32)
    nb = _na_bias_tables(na_rpb)
    wf_bf = w_fourier.astype(BF)

    ck_na = cache_na_k.reshape(DEC_BATCH, DEPTH, PAST_LEN, 256)
    cv_na = cache_na_v.reshape(DEC_BATCH, DEPTH, PAST_LEN, 256)
    ck_df = cache_diff_k.reshape(DEC_BATCH, DEPTH, PAST_LEN, 256)
    cv_df = cache_diff_v.reshape(DEC_BATCH, DEPTH, PAST_LEN, 256)
    ck_sw = cache_swa_k.reshape(DEC_BATCH, DEPTH, PAST_LEN, 128)
    cv_sw = cache_swa_v.reshape(DEC_BATCH, DEPTH, PAST_LEN, 128)

    _, _, zb, caches = _trunk(xp, xs, mods, ropes, proj=(0, norm1_g[0], w_in, None))
    for l in range(DEPTH):
        lam_init = 0.8 - 0.6 * math.exp(-0.3 * l)
        lamp = jnp.stack([diff_lq1[l], diff_lk1[l], diff_lq2[l], diff_lk2[l]], axis=0)
        subg = jnp.tile(diff_subln_g[l].reshape(1, HEAD_DIM), (1, 2))

        cat = _ctx_mix(zb, swa_sink[l], lamp, subg, cl_p, sl_p, cbd, sbd, wf_bf[l], lam_init)
        cat = _lat_na(zb, ck_na, cv_na, nb, cat, l)
        cat = _lat_diff(zb, ck_df, cv_df, lamp, subg, cat, lam_init, l)
        cat = _lat_fnet(zb, cl_s, sl_s, cbd, sbd, wf_bf[l], cat)
        cat = _lat_swa(zb, ck_sw, cv_sw, swa_sink[l], cat, l)
        nxt = (l + 1, norm1_g[l + 1], w_in, caches) if l + 1 < DEPTH else None
        xp, xs, zb, new_caches = _trunk(xp, xs, mods, ropes,
                                        mlp=(l, cat, norm2_g[l], final_g, w_out, w_mlp1, w_mlp2), proj=nxt)
        caches = new_caches if nxt is not None else caches

    y_prompt = xp.reshape(BATCH, SEQ, D_MODEL)
    y_sample = xs.reshape(DEC_BATCH, DEC_SEQ, D_MODEL)
    new = [a.reshape(BATCH, DEPTH, SEQ, a.shape[-1] // HEAD_DIM, HEAD_DIM) for a in caches]
    return (y_prompt, y_sample) + tuple(new)
```

```python
import functools
import math

import numpy as np
import jax
import jax.numpy as jnp
from jax import lax
from jax.experimental import pallas as pl
from jax.experimental.pallas import tpu as pltpu

D_MODEL = 1024
BATCH = 16
SEQ = 256
DEPTH = 4
DEC_BATCH = 2
DEC_SEQ = 1024
PAST_LEN = 512
GRID_W = 64
GRID_ROWS = DEC_SEQ // GRID_W
HEAD_DIM = 64
NA_KH = 8
NA_KW = 16
DIFF_QK_DIM = 32
SWA_WINDOW = 128
D_FF = 4 * D_MODEL
ROPE_BASE = 10000.0
NORM_EPS = 1e-6
NEG_INF = -1e30
IN_WIDTH = 2304
MIX_WIDTH = 1024

N_CTX_TOK = BATCH * SEQ
N_LAT_TOK = DEC_BATCH * DEC_SEQ
N_TOK = N_CTX_TOK + N_LAT_TOK
ROW_TILE = 256
N_ROW_TILES = N_TOK // ROW_TILE
N_CTX_TILES = N_CTX_TOK // ROW_TILE
N_LAT_TILES = N_LAT_TOK // ROW_TILE
LAT_TILES_PER_REQ = DEC_SEQ // ROW_TILE
LAT_BLOCK0 = N_CTX_TOK // DEC_SEQ
MOD_ROWS = 8

C_NA_Q, C_NA_K, C_NA_V = 0, 256, 512
C_DQ, C_DK, C_DV = 768, 1024, 1280
C_FC = 1536
C_SQ, C_SK, C_SV = 1792, 2048, 2176
CACHE_COLS = ((C_NA_K, 256), (C_NA_V, 256), (C_DK, 256), (C_DV, 256), (C_SK, 128), (C_SV, 128))
ZB_SK, ZB_SV = 2048, 2304
ZB_WIDTH = 2560

NA_QROWS = 2
NA_WIN_ROWS = 10
NA_TABLE_OF_PAIR = (0, 1, 2, 2, 2, 2, 3, 4)
NA_TABLE_PAIRS = (0, 1, 2, 6, 7)

BF = jnp.bfloat16
F32 = jnp.float32
VMEM_LIMIT = 56 * 1024 * 1024


def _dot(a, b):
    return jnp.dot(a, b, preferred_element_type=F32)


def _dot_nt(a, b):
    return lax.dot_general(a, b, (((1,), (1,)), ((), ())), preferred_element_type=F32)


def _rmsnorm(x, g):
    ms = jnp.mean(x * x, axis=-1, keepdims=True)
    return x * lax.rsqrt(ms + NORM_EPS) * g


def _params(n_grid=1):
    return pltpu.CompilerParams(dimension_semantics=("arbitrary",) * n_grid, vmem_limit_bytes=VMEM_LIMIT)


def _rowmax(*parts):
    m = jnp.max(parts[0], axis=-1, keepdims=True)
    for p in parts[1:]:
        m = jnp.maximum(m, jnp.max(p, axis=-1, keepdims=True))
    return m


def _rowsum(*parts):
    s = jnp.sum(parts[0], axis=-1, keepdims=True)
    for p in parts[1:]:
        s = s + jnp.sum(p, axis=-1, keepdims=True)
    return s


def _ada_kernel(cond_ref, w_ref, b_ref, o_ref):
    cnd = cond_ref[...]
    s = cnd / (1.0 + jnp.exp(-cnd))
    o_ref[...] = _dot(s.astype(BF), w_ref[...].astype(BF)) + b_ref[...]


def _ada(cond, w_ada, b_ada):
    tn = 1024
    return pl.pallas_call(
        _ada_kernel,
        grid=(DEPTH, 6 * D_MODEL // tn),
        in_specs=[
            pl.BlockSpec((MOD_ROWS, D_MODEL), lambda l, j: (0, 0)),
            pl.BlockSpec((None, D_MODEL, tn), lambda l, j: (l, 0, j)),
            pl.BlockSpec((None, 1, tn), lambda l, j: (l, 0, j)),
        ],
        out_specs=pl.BlockSpec((None, MOD_ROWS, tn), lambda l, j: (l, 0, j)),
        out_shape=jax.ShapeDtypeStruct((DEPTH, MOD_ROWS, 6 * D_MODEL), F32),
        compiler_params=_params(2),
        name="ada",
    )(cond, w_ada, b_ada.reshape(DEPTH, 1, 6 * D_MODEL))


def _mod_row(t):
    return jnp.where(t < N_CTX_TILES, 0, 1 + (t - N_CTX_TILES) // LAT_TILES_PER_REQ)


def _ctx_tile(t):
    return jnp.minimum(t, N_CTX_TILES - 1)


def _lat_tile(t):
    return jnp.maximum(t - N_CTX_TILES, 0)


TRUNK_CAST_STEPS = 16
FF_CHUNK = 1024
TRUNK_VMEM_LIMIT = 60 * 1024 * 1024


def _rope(x, cos, sin_signed, shift, first):
    partner = jnp.where(first, pltpu.roll(x, 128 - shift, 1), pltpu.roll(x, shift, 1))
    return x * cos + partner * sin_signed


def _write_projection(hb, wb, t, is_ctx, zb_ref, cache_refs, rope_refs):
    cosd_ref, sind_ref, coss_ref, sins_ref = rope_refs
    dscale = DIFF_QK_DIM ** -0.5
    p0 = pl.multiple_of(
        jnp.where(is_ctx, DEC_SEQ, ((t - N_CTX_TILES) % LAT_TILES_PER_REQ) * ROW_TILE), ROW_TILE)
    lane = lax.broadcasted_iota(jnp.int32, (ROW_TILE, 128), 1)
    first_d = (lane % 16) < 8
    first_s = (lane % 32) < 16
    lo = lane < HEAD_DIM
    cosd, sind = cosd_ref[pl.ds(p0, ROW_TILE), :], sind_ref[pl.ds(p0, ROW_TILE), :]
    coss, sins = coss_ref[pl.ds(p0, ROW_TILE), :], sins_ref[pl.ds(p0, ROW_TILE), :]

    def cols(c0, width):
        return _dot(hb, wb[:, c0:c0 + width])

    def put(c0, width, val):
        zb_ref[:, c0:c0 + width] = val.astype(BF)

    def put_dup(c0, val):
        r = pltpu.roll(val, HEAD_DIM, 1)
        put(c0, 128, jnp.where(lo, val, r))
        put(c0 + 128, 128, jnp.where(lo, r, val))

    put(C_NA_Q, 256, cols(C_NA_Q, 256) * 0.125)
    z_nakv = cols(C_NA_K, 512)
    put(C_NA_K, 512, z_nakv)
    z_dq = cols(C_DQ, 256)
    for j in range(2):
        put(C_DQ + 128 * j, 128, _rope(z_dq[:, 128 * j:128 * j + 128], cosd, sind, 8, first_d) * dscale)
    z_dk = cols(C_DK, 256)
    for j in range(2):
        put(C_DK + 128 * j, 128, _rope(z_dk[:, 128 * j:128 * j + 128], cosd, sind, 8, first_d))
    z_dvfc = cols(C_DV, 512)
    put(C_DV, 512, z_dvfc)
    z_sq = cols(C_SQ, 256)
    for j in range(2):
        put(C_SQ + 128 * j, 128, _rope(z_sq[:, 128 * j:128 * j + 128], coss, sins, 16, first_s) * 0.125)
    z_skv = cols(C_SK, 256)
    put_dup(ZB_SK, _rope(z_skv[:, 0:128], coss, sins, 16, first_s))
    put_dup(ZB_SV, z_skv[:, 128:256])

    @pl.when(is_ctx)
    def _():
        new = (z_nakv[:, 0:256], z_nakv[:, 256:512], z_dk, z_dvfc[:, 0:256], z_skv[:, 0:128], z_skv[:, 128:256])
        for ref, val in zip(cache_refs, new):
            ref[...] = val


def _trunk_kernel(do_mlp, do_proj, last, n_alias, *refs):
    xp_ref, xs_ref = refs[:2]
    pos = 2
    if do_mlp:
        catp_ref, cats_ref, mm_ref, g2_ref, fg_ref, wo_ref, w1_ref, w2_ref = refs[pos:pos + 8]
        pos += 8
    if do_proj:
        pm_ref, g1_ref, win_ref = refs[pos:pos + 3]
        rope_refs = refs[pos + 3:pos + 7]
        pos += 7 + n_alias
    if do_mlp:
        op_ref, os_ref = refs[pos:pos + 2]
        pos += 2
    if do_proj:
        zb_ref = refs[pos]
        cache_refs = refs[pos + 1:pos + 7]
        pos += 7
    if do_mlp:
        wob, w1b, w2b = refs[pos:pos + 3]
        pos += 3
    if do_proj:
        wb = refs[pos]
    s = pl.program_id(0)

    @pl.when(s < TRUNK_CAST_STEPS)
    def _():
        def cast(dst, src):
            rows = src.shape[0]
            dst[pl.ds(pl.multiple_of(s * rows, rows), rows), :] = src[...].astype(BF)

        if do_mlp:
            cast(wob, wo_ref)
            cast(w1b, w1_ref)
            cast(w2b, w2_ref)
        if do_proj:
            cast(wb, win_ref)

    @pl.when(s >= TRUNK_CAST_STEPS)
    def _():
        t = s - TRUNK_CAST_STEPS
        is_ctx = t < N_CTX_TILES
        row = _mod_row(t)
        x = jnp.where(is_ctx, xp_ref[...], xs_ref[...])
        if do_mlp:
            gt1 = mm_ref[pl.ds(row, 1), 2 * D_MODEL:3 * D_MODEL]
            sh2 = mm_ref[pl.ds(row, 1), 3 * D_MODEL:4 * D_MODEL]
            sc2 = mm_ref[pl.ds(row, 1), 4 * D_MODEL:5 * D_MODEL]
            gt2 = mm_ref[pl.ds(row, 1), 5 * D_MODEL:6 * D_MODEL]
            cat = jnp.where(is_ctx, catp_ref[...], cats_ref[...])
            x = x + gt1 * _dot(cat, wob[...])
            hh = (_rmsnorm(x, g2_ref[...]) * (1.0 + sc2) + sh2).astype(BF)
            acc = jnp.zeros((ROW_TILE, D_MODEL), F32)
            for c in range(D_FF // FF_CHUNK):
                a = jnp.maximum(_dot(hh, w1b[:, c * FF_CHUNK:(c + 1) * FF_CHUNK]), 0.0)
                acc = acc + _dot((a * a).astype(BF), w2b[c * FF_CHUNK:(c + 1) * FF_CHUNK, :])
            x = x + gt2 * acc
            out = _rmsnorm(x, fg_ref[...]) if last else x

            @pl.when(is_ctx)
            def _():
                op_ref[...] = out

            @pl.when(jnp.logical_not(is_ctx))
            def _():
                os_ref[...] = out

        if do_proj:
            shift = pm_ref[pl.ds(row, 1), 0:D_MODEL]
            scale = pm_ref[pl.ds(row, 1), D_MODEL:2 * D_MODEL]
            h = _rmsnorm(x, g1_ref[...]) * (1.0 + scale) + shift
            _write_projection(h.astype(BF), wb, t, is_ctx, zb_ref, cache_refs, rope_refs)


def _trunk(xp, xs, mods, ropes, mlp=None, proj=None):
    do_mlp, do_proj = mlp is not None, proj is not None
    last = do_mlp and not do_proj
    step = lambda f: (lambda s: f(jnp.maximum(s - TRUNK_CAST_STEPS, 0)))
    const = lambda shape: pl.BlockSpec(shape, lambda s: (0,) * len(shape), pipeline_mode=pl.Buffered(1))
    chunk = lambda layer, rows, cols: pl.BlockSpec(
        (None, rows // TRUNK_CAST_STEPS, cols), lambda s: (layer, jnp.minimum(s, TRUNK_CAST_STEPS - 1), 0))
    mod_spec = lambda layer: pl.BlockSpec((None, MOD_ROWS, 6 * D_MODEL), lambda s: (layer, 0, 0),
                                          pipeline_mode=pl.Buffered(1))
    x_specs = [pl.BlockSpec((ROW_TILE, D_MODEL), step(lambda t: (_ctx_tile(t), 0))),
               pl.BlockSpec((ROW_TILE, D_MODEL), step(lambda t: (_lat_tile(t), 0)))]
    args, in_specs = [xp, xs], list(x_specs)
    out_specs, out_shape, scratch, aliases = [], [], [], {}
    n_alias = 0
    if do_mlp:
        layer, cat_p, cat_s, g2, final_g, wo, w1, w2 = mlp
        args += [cat_p, cat_s, mods, g2.reshape(1, D_MODEL), final_g.reshape(1, D_MODEL), wo, w1, w2]
        in_specs += [pl.BlockSpec((ROW_TILE, MIX_WIDTH), step(lambda t: (_ctx_tile(t), 0))),
                     pl.BlockSpec((ROW_TILE, MIX_WIDTH), step(lambda t: (_lat_tile(t), 0))), mod_spec(layer),
                     const((1, D_MODEL)), const((1, D_MODEL)),
                     chunk(layer, MIX_WIDTH, D_MODEL), chunk(layer, D_MODEL, D_FF), chunk(layer, D_FF, D_MODEL)]
    if do_proj:
        p_layer, g1, w_in, caches = proj
        args += [mods, g1.reshape(1, D_MODEL), w_in, *ropes]
        in_specs += [mod_spec(p_layer), const((1, D_MODEL)), chunk(p_layer, D_MODEL, IN_WIDTH)]
        in_specs += [const((DEC_SEQ + ROW_TILE, 128))] * 4
        if caches is not None:
            n_alias = 6
            n_out_before = 2 if do_mlp else 0
            aliases = {len(args) + i: n_out_before + 1 + i for i in range(6)}
            args += list(caches)
            in_specs += [pl.BlockSpec(memory_space=pl.ANY)] * 6
    if do_mlp:
        out_specs += x_specs
        out_shape += [jax.ShapeDtypeStruct((N_CTX_TOK, D_MODEL), F32),
                      jax.ShapeDtypeStruct((N_LAT_TOK, D_MODEL), F32)]
        scratch += [pltpu.VMEM((MIX_WIDTH, D_MODEL), BF), pltpu.VMEM((D_MODEL, D_FF), BF),
                    pltpu.VMEM((D_FF, D_MODEL), BF)]
    if do_proj:
        out_specs += [pl.BlockSpec((ROW_TILE, ZB_WIDTH), step(lambda t: (t, 0)))] + [
            pl.BlockSpec((None, None, SEQ, width), step(lambda t: (_ctx_tile(t), p_layer, 0, 0)))
            for _, width in CACHE_COLS]
        out_shape += [jax.ShapeDtypeStruct((N_TOK, ZB_WIDTH), BF)] + [
            jax.ShapeDtypeStruct((BATCH, DEPTH, SEQ, width), F32) for _, width in CACHE_COLS]
        scratch += [pltpu.VMEM((D_MODEL, IN_WIDTH), BF)]
    outs = pl.pallas_call(
        functools.partial(_trunk_kernel, do_mlp, do_proj, last, n_alias),
        grid=(TRUNK_CAST_STEPS + N_ROW_TILES,),
        in_specs=in_specs,
        out_specs=out_specs,
        out_shape=out_shape,
        scratch_shapes=scratch,
        input_output_aliases=aliases,
        compiler_params=pltpu.CompilerParams(dimension_semantics=("arbitrary",),
                                             vmem_limit_bytes=TRUNK_VMEM_LIMIT),
        name="trunk_mlp_proj" if (do_mlp and do_proj) else ("trunk_mlp" if do_mlp else "trunk_proj"),
    )(*args)
    outs = list(outs)
    if do_mlp:
        xp, xs = outs[:2]
        outs = outs[2:]
    zb, new_caches = (outs[0], tuple(outs[1:7])) if do_proj else (None, None)
    return xp, xs, zb, new_caches


def _diff_lambda(lamp_ref, lam_init):
    a = jnp.sum(lamp_ref[0:1, :] * lamp_ref[1:2, :], axis=-1, keepdims=True)
    b = jnp.sum(lamp_ref[2:3, :] * lamp_ref[3:4, :], axis=-1, keepdims=True)
    return jnp.exp(a) - jnp.exp(b) + lam_init


def _subln(o, g, lam_init):
    return _rmsnorm(o, g) * (1.0 - lam_init)


def _lane_lo(n):
    return lax.broadcasted_iota(jnp.int32, (n, 128), 1) < HEAD_DIM


def _lane_lo_wide(n, width):
    return lax.broadcasted_iota(jnp.int32, (n, width), 1) % 128 < HEAD_DIM


def _split_pair(x, lo):
    zero = jnp.zeros_like(x)
    return jnp.where(lo, x, zero), jnp.where(lo, zero, x)


def _v_ones_pair(v, lo):
    one = jnp.ones_like(v)
    return jnp.where(lo, v, one), jnp.where(lo, one, v)


def _merge_pair(a, b, lo):
    return jnp.where(lo, a, b), pltpu.roll(jnp.where(lo, b, a), HEAD_DIM, 1)


def _subln_pair(o, g2, lam_init, lo):
    sq = o * o
    ms_a = jnp.sum(jnp.where(lo, sq, 0.0), axis=-1, keepdims=True)
    ms_b = jnp.sum(jnp.where(lo, 0.0, sq), axis=-1, keepdims=True)
    ms = jnp.where(lo, ms_a, ms_b) * (1.0 / HEAD_DIM)
    return o * lax.rsqrt(ms + NORM_EPS) * g2 * (1.0 - lam_init)


def _diff_quarters(q, n):
    quarter = lax.broadcasted_iota(jnp.int32, (n, 128), 1) // DIFF_QK_DIM
    zero = jnp.zeros_like(q)
    return [jnp.where(quarter == i, q, zero) for i in range(4)]


def _fourier(x_bf, cl, sl, cbd, sbd, wf):
    xc = _dot(x_bf, cbd).astype(BF)
    xs = _dot(x_bf, sbd).astype(BF)
    y = _dot(cl, xc) - _dot(sl, xs)
    return _dot(y.astype(BF), wf)


CTX_UNIT_GROUP = 4


def _ctx_mix_kernel(lam_init, sink_ref, z_ref, lamp_ref, subg_ref, cl_ref, sl_ref, cbd_ref, sbd_ref, wf_ref,
                    o_ref):
    lo = _lane_lo(SEQ)

    units = []
    for j in range(2):
        c = 128 * j
        qs = _split_pair(z_ref[:, C_NA_Q + c:C_NA_Q + c + 128], lo)
        vs = _v_ones_pair(z_ref[:, C_NA_V + c:C_NA_V + c + 128], lo)
        units += [(qs[i], (C_NA_K + c), vs[i], None) for i in range(2)]
    for j in range(2):
        c = 128 * j
        q4 = _diff_quarters(z_ref[:, C_DQ + c:C_DQ + c + 128], SEQ)
        vs = _v_ones_pair(z_ref[:, C_DV + c:C_DV + c + 128], lo)
        units += [(q4[i], (C_DK + c), vs[i // 2], None) for i in (0, 2, 1, 3)]
    for g in range(2):
        c = 128 * g
        qs = _split_pair(z_ref[:, C_SQ + c:C_SQ + c + 128], lo)
        vs = _v_ones_pair(z_ref[:, ZB_SV + c:ZB_SV + c + 128], lo)
        units += [(qs[i], (ZB_SK + c), vs[i], sink_ref[2 * g + i]) for i in range(2)]

    res, extras = [], []
    for u0 in range(0, len(units), CTX_UNIT_GROUP):
        group = units[u0:u0 + CTX_UNIT_GROUP]
        scores = [_dot_nt(q, z_ref[:, kc:kc + 128]) for q, kc, _, _ in group]
        weights = []
        for s, (_, _, _, sink) in zip(scores, group):
            m = _rowmax(s)
            if sink is not None:
                m = jnp.maximum(m, sink)
                extras.append(jnp.exp(sink - m))
            weights.append(jnp.exp(s - m).astype(BF))
        res += [_dot(e, v) for e, (_, _, v, _) in zip(weights, group)]

    for j in range(2):
        num, den = _merge_pair(res[2 * j], res[2 * j + 1], lo)
        o_ref[:, 128 * j:128 * j + 128] = (num / den).astype(BF)

    lam = _diff_lambda(lamp_ref, lam_init)
    for j in range(2):
        n1, d1 = _merge_pair(res[4 + 4 * j], res[5 + 4 * j], lo)
        n2, d2 = _merge_pair(res[6 + 4 * j], res[7 + 4 * j], lo)
        o = n1 / d1 - lam * (n2 / d2)
        o_ref[:, 256 + 128 * j:384 + 128 * j] = _subln_pair(o, subg_ref[...], lam_init, lo).astype(BF)

    o_c = _fourier(z_ref[:, C_FC:C_FC + 256], cl_ref[...], sl_ref[...], cbd_ref[...], sbd_ref[...], wf_ref[...])
    o_ref[:, 512:768] = o_c.astype(BF)

    for g in range(2):
        num, den = _merge_pair(res[12 + 2 * g], res[13 + 2 * g], lo)
        den = den + jnp.where(lo, extras[2 * g], extras[2 * g + 1])
        o_ref[:, 768 + 128 * g:896 + 128 * g] = (num / den).astype(BF)


def _ctx_mix(zb, sink, lamp, subg, cl, sl, cbd, sbd, wf, lam_init):
    full = lambda shape: pl.BlockSpec(shape, lambda b, s: (0,) * len(shape))
    grid_spec = pltpu.PrefetchScalarGridSpec(
        num_scalar_prefetch=1,
        grid=(BATCH,),
        in_specs=[
            pl.BlockSpec((SEQ, ZB_WIDTH), lambda b, s: (b, 0)),
            full((4, DIFF_QK_DIM)),
            full((1, 2 * HEAD_DIM)),
            full((SEQ, SEQ)), full((SEQ, SEQ)),
            full((256, 256)), full((256, 256)), full((256, 256)),
        ],
        out_specs=pl.BlockSpec((SEQ, MIX_WIDTH), lambda b, s: (b, 0)),
    )
    return pl.pallas_call(
        functools.partial(_ctx_mix_kernel, lam_init),
        grid_spec=grid_spec,
        out_shape=jax.ShapeDtypeStruct((N_CTX_TOK, MIX_WIDTH), BF),
        compiler_params=_params(),
        name="ctx_mix",
    )(sink, zb, lamp, subg, cl, sl, cbd, sbd, wf)


def _cache_v_ones(cv_ref, cva, cvb):
    lo = _lane_lo_wide(PAST_LEN, 256)
    cv = cv_ref[...]
    cva[...] = jnp.where(lo, cv, 1.0).astype(BF)
    cvb[...] = jnp.where(lo, 1.0, cv).astype(BF)


def _lat_na_part(z_ref, ck_ref, cv_ref, nb_ref, o_ref, ckb, cva, cvb):
    ckb[...] = ck_ref[...].astype(BF)
    _cache_v_ones(cv_ref, cva, cvb)
    n_q = NA_QROWS * GRID_W
    n_loc = NA_WIN_ROWS * GRID_W
    lo_q = _lane_lo(n_q)
    lo_w = _lane_lo(n_loc)

    def body(p, carry):
        w_row = jnp.clip(NA_QROWS * p - NA_KH // 2, 0, GRID_ROWS - NA_WIN_ROWS)
        tbl = jnp.minimum(p, 2) + jnp.maximum(p - 5, 0)
        q0 = pl.multiple_of(p * n_q, n_q)
        k0 = pl.multiple_of(w_row * GRID_W, NA_QROWS * GRID_W)
        scores = []
        for h in range(4):
            c = 128 * (h // 2)
            q = _split_pair(z_ref[pl.ds(q0, n_q), C_NA_Q + c:C_NA_Q + c + 128], lo_q)[h % 2]
            s_loc = _dot_nt(q, z_ref[pl.ds(k0, n_loc), C_NA_K + c:C_NA_K + c + 128]) + nb_ref[h, tbl]
            s_ctx = _dot_nt(q, ckb[:, c:c + 128])
            scores.append((s_loc, s_ctx))
        weights = []
        for s_loc, s_ctx in scores:
            m = _rowmax(s_loc, s_ctx)
            weights.append((jnp.exp(s_loc - m).astype(BF), jnp.exp(s_ctx - m).astype(BF)))
        res = []
        for h, (e_loc, e_ctx) in enumerate(weights):
            c = 128 * (h // 2)
            v = _v_ones_pair(z_ref[pl.ds(k0, n_loc), C_NA_V + c:C_NA_V + c + 128], lo_w)[h % 2]
            cv = (cva, cvb)[h % 2][:, c:c + 128]
            res.append(_dot(e_loc, v) + _dot(e_ctx, cv))
        for j in range(2):
            num, den = _merge_pair(res[2 * j], res[2 * j + 1], lo_q)
            o_ref[pl.ds(q0, n_q), 128 * j:128 * j + 128] = (num / den).astype(BF)
        return carry

    lax.fori_loop(0, GRID_ROWS // NA_QROWS, body, 0)


DIFF_QBLK = 512


def _lat_diff_part(lam_init, z_ref, ck_ref, cv_ref, lamp_ref, subg_ref, o_ref, ckb, cva, cvb, vla, vlb):
    ckb[...] = ck_ref[...].astype(BF)
    _cache_v_ones(cv_ref, cva, cvb)
    lo_v = _lane_lo_wide(DEC_SEQ, 256)
    v_loc = z_ref[:, C_DV:C_DV + 256]
    one = jnp.ones_like(v_loc)
    vla[...] = jnp.where(lo_v, v_loc, one)
    vlb[...] = jnp.where(lo_v, one, v_loc)
    lam = _diff_lambda(lamp_ref, lam_init)
    qblk = DIFF_QBLK
    lo_q = _lane_lo(qblk)

    def body(i, carry):
        q0 = pl.multiple_of(i * qblk, qblk)
        for j in range(2):
            c = 128 * j
            q4 = _diff_quarters(z_ref[pl.ds(q0, qblk), C_DQ + c:C_DQ + c + 128], qblk)
            scores = [(_dot_nt(q, z_ref[:, C_DK + c:C_DK + c + 128]), _dot_nt(q, ckb[:, c:c + 128])) for q in q4]
            weights = []
            for s_loc, s_ctx in scores:
                m = _rowmax(s_loc, s_ctx)
                weights.append((jnp.exp(s_loc - m).astype(BF), jnp.exp(s_ctx - m).astype(BF)))
            res = []
            for t, (e_loc, e_ctx) in enumerate(weights):
                v_loc, v_ctx = ((vla, cva), (vlb, cvb))[t // 2]
                res.append(_dot(e_loc, v_loc[:, c:c + 128]) + _dot(e_ctx, v_ctx[:, c:c + 128]))
            n1, d1 = _merge_pair(res[0], res[2], lo_q)
            n2, d2 = _merge_pair(res[1], res[3], lo_q)
            o = n1 / d1 - lam * (n2 / d2)
            o_ref[pl.ds(q0, qblk), 256 + c:256 + c + 128] = (
                _subln_pair(o, subg_ref[...], lam_init, lo_q).astype(BF))
        return carry

    lax.fori_loop(0, DEC_SEQ // qblk, body, 0)


def _lat_swa_part(sink_ref, z_ref, ck_ref, cv_ref, o_ref, ckd, cva, cvb):
    W = SWA_WINDOW
    n_win = 3 * W
    lo_c = _lane_lo(PAST_LEN)
    ck = ck_ref[...]
    ck_r = pltpu.roll(ck, HEAD_DIM, 1)
    ckd[:, 0:128] = jnp.where(lo_c, ck, ck_r).astype(BF)
    ckd[:, 128:256] = jnp.where(lo_c, ck_r, ck).astype(BF)
    cv = cv_ref[...]
    cv_r = pltpu.roll(cv, HEAD_DIM, 1)
    cva[:, 0:128] = jnp.where(lo_c, cv, 1.0).astype(BF)
    cvb[:, 0:128] = jnp.where(lo_c, 1.0, cv_r).astype(BF)
    cva[:, 128:256] = jnp.where(lo_c, cv_r, 1.0).astype(BF)
    cvb[:, 128:256] = jnp.where(lo_c, 1.0, cv).astype(BF)
    lo_q = _lane_lo(W)
    lo_w = _lane_lo(n_win)

    def body(n, carry):
        q0 = pl.multiple_of(n * W, W)
        w0 = pl.multiple_of(jnp.clip((n - 1) * W, 0, DEC_SEQ - n_win), W)
        qpos = q0 + lax.broadcasted_iota(jnp.int32, (W, n_win), 0)
        kpos = w0 + lax.broadcasted_iota(jnp.int32, (W, n_win), 1)
        valid = jnp.abs(qpos - kpos) <= W
        scores = []
        for h in range(4):
            c = 128 * (h // 2)
            q = _split_pair(z_ref[pl.ds(q0, W), C_SQ + c:C_SQ + c + 128], lo_q)[h % 2]
            s_loc = jnp.where(valid, _dot_nt(q, z_ref[pl.ds(w0, n_win), ZB_SK + c:ZB_SK + c + 128]), NEG_INF)
            scores.append((s_loc, _dot_nt(q, ckd[:, c:c + 128])))
        weights, extras = [], []
        for h, (s_loc, s_ctx) in enumerate(scores):
            sink = sink_ref[h]
            m = jnp.maximum(_rowmax(s_loc, s_ctx), sink)
            weights.append((jnp.exp(s_loc - m).astype(BF), jnp.exp(s_ctx - m).astype(BF)))
            extras.append(jnp.exp(sink - m))
        res = []
        for h, (e_loc, e_ctx) in enumerate(weights):
            c = 128 * (h // 2)
            v = _v_ones_pair(z_ref[pl.ds(w0, n_win), ZB_SV + c:ZB_SV + c + 128], lo_w)[h % 2]
            res.append(_dot(e_loc, v) + _dot(e_ctx, (cva, cvb)[h % 2][:, c:c + 128]))
        for g in range(2):
            num, den = _merge_pair(res[2 * g], res[2 * g + 1], lo_q)
            den = den + jnp.where(lo_q, extras[2 * g], extras[2 * g + 1])
            o_ref[pl.ds(q0, W), 768 + 128 * g:768 + 128 * g + 128] = (num / den).astype(BF)
        return carry

    lax.fori_loop(0, DEC_SEQ // W, body, 0)


def _lat_mix_kernel(lam_init, sink_ref, z_ref, nak_ref, nav_ref, dfk_ref, dfv_ref, swk_ref, swv_ref, nb_ref,
                    lamp_ref, subg_ref, cl_ref, sl_ref, cbd_ref, sbd_ref, wf_ref, o_ref, ck, cva, cvb, vla, vlb):
    _lat_na_part(z_ref, nak_ref, nav_ref, nb_ref, o_ref, ck, cva, cvb)
    _lat_diff_part(lam_init, z_ref, dfk_ref, dfv_ref, lamp_ref, subg_ref, o_ref, ck, cva, cvb, vla, vlb)
    o_c = _fourier(z_ref[:, C_FC:C_FC + 256], cl_ref[...], sl_ref[...], cbd_ref[...], sbd_ref[...], wf_ref[...])
    o_ref[:, 512:768] = o_c.astype(BF)
    _lat_swa_part(sink_ref, z_ref, swk_ref, swv_ref, o_ref, ck, cva, cvb)


def _lat_mix(zb, caches_in, nb, sink, lamp, subg, cl, sl, cbd, sbd, wf, lam_init, layer):
    full = lambda shape: pl.BlockSpec(shape, lambda b, s: (0,) * len(shape), pipeline_mode=pl.Buffered(1))
    cache = lambda width: pl.BlockSpec((None, None, PAST_LEN, width), lambda b, s: (b, layer, 0, 0))
    n_q = NA_QROWS * GRID_W
    n_loc = NA_WIN_ROWS * GRID_W
    grid_spec = pltpu.PrefetchScalarGridSpec(
        num_scalar_prefetch=1,
        grid=(DEC_BATCH,),
        in_specs=[
            pl.BlockSpec((DEC_SEQ, ZB_WIDTH), lambda b, s: (LAT_BLOCK0 + b, 0)),
            cache(256), cache(256), cache(256), cache(256), cache(128), cache(128),
            pl.BlockSpec((None, 4, len(NA_TABLE_PAIRS), n_q, n_loc), lambda b, s: (layer, 0, 0, 0, 0),
                         pipeline_mode=pl.Buffered(1)),
            full((4, DIFF_QK_DIM)), full((1, 2 * HEAD_DIM)),
            full((DEC_SEQ, DEC_SEQ)), full((DEC_SEQ, DEC_SEQ)),
            full((256, 256)), full((256, 256)), full((256, 256)),
        ],
        out_specs=pl.BlockSpec((DEC_SEQ, MIX_WIDTH), lambda b, s: (b, 0)),
        scratch_shapes=[pltpu.VMEM((PAST_LEN, 256), BF)] * 3 + [pltpu.VMEM((DEC_SEQ, 256), BF)] * 2,
    )
    return pl.pallas_call(
        functools.partial(_lat_mix_kernel, lam_init),
        grid_spec=grid_spec,
        out_shape=jax.ShapeDtypeStruct((N_LAT_TOK, MIX_WIDTH), BF),
        compiler_params=_params(),
        name="lat_mix",
    )(sink, zb, *caches_in, nb, lamp, subg, cl, sl, cbd, sbd, wf)


def _dft_tables(n):
    j = np.arange(n)
    ang = 2.0 * np.pi * ((j[:, None] * j[None, :]) % n) / n
    return np.cos(ang) / np.sqrt(n), np.sin(ang) / np.sqrt(n)


def _block_diag4(m):
    out = np.zeros((256, 256), m.dtype)
    for g in range(4):
        out[64 * g:64 * g + 64, 64 * g:64 * g + 64] = m
    return out


def _rope_tables(n_axis_dims):
    half = n_axis_dims // 2
    inv = ROPE_BASE ** (-np.arange(half, dtype=np.float64) / half)
    t = np.arange(DEC_SEQ)
    lane = np.arange(128)
    w = lane % n_axis_dims
    is_col = (lane // n_axis_dims) % 2 == 1
    pos = np.where(is_col[None, :], (t % GRID_W)[:, None], (t // GRID_W)[:, None]).astype(np.float64)
    ang = pos * inv[w % half][None, :]
    sign = np.where(w < half, -1.0, 1.0)[None, :]
    cos = np.concatenate([np.cos(ang), np.ones((ROW_TILE, 128))], axis=0)
    sin = np.concatenate([np.sin(ang) * sign, np.zeros((ROW_TILE, 128))], axis=0)
    return jnp.asarray(cos, F32), jnp.asarray(sin, F32)


def _na_bias_tables(rpb):
    lead = rpb.shape[:2]
    col = np.arange(GRID_W)
    cstart = np.clip(col - NA_KW // 2, 0, GRID_W - NA_KW)
    cmask = (col[None, :] >= cstart[:, None]) & (col[None, :] < cstart[:, None] + NA_KW)
    pad = GRID_W - NA_KW
    rp = jnp.pad(rpb, ((0, 0), (0, 0), (0, 0), (pad, pad)))
    t = jnp.stack([rp[..., pad + NA_KW - 1 - c:pad + NA_KW - 1 - c + GRID_W] for c in range(GRID_W)], axis=-2)
    t = jnp.where(cmask, t, NEG_INF)
    t = t.transpose(0, 1, 3, 2, 4).reshape(lead + (GRID_W, (2 * NA_KH - 1) * GRID_W))
    tables = []
    for p in NA_TABLE_PAIRS:
        w_row = int(np.clip(NA_QROWS * p - NA_KH // 2, 0, GRID_ROWS - NA_WIN_ROWS))
        rows = []
        for r in (NA_QROWS * p, NA_QROWS * p + 1):
            rs = int(np.clip(r - NA_KH // 2, 0, GRID_ROWS - NA_KH))
            j_lo, j_hi = rs - w_row, rs - w_row + NA_KH
            d_lo = rs - r + NA_KH - 1
            body = t[..., d_lo * GRID_W:(d_lo + NA_KH) * GRID_W]
            rows.append(jnp.pad(body, ((0, 0),) * 3 + ((j_lo * GRID_W, (NA_WIN_ROWS - j_hi) * GRID_W),),
                                constant_values=NEG_INF))
        tables.append(jnp.concatenate(rows, axis=-2))
    return jnp.stack(tables, axis=2)


def kernel(x_prompt, x_sample, cache_na_k, cache_na_v, cache_diff_k, cache_diff_v, cache_swa_k, cache_swa_v, c, c_ctx, w_ada, b_ada, norm1_g, norm2_g, w_in, na_rpb, diff_lq1, diff_lk1, diff_lq2, diff_lk2, diff_subln_g, w_fourier, swa_sink, w_out, w_mlp1, w_mlp2, final_g):
    xp = x_prompt.reshape(N_CTX_TOK, D_MODEL)
    xs = x_sample.reshape(N_LAT_TOK, D_MODEL)
    cond = jnp.zeros((MOD_ROWS, D_MODEL), F32).at[0].set(c_ctx).at[1:1 + DEC_BATCH].set(c)
    mods = _ada(cond, w_ada, b_ada)

    cl_p, sl_p = _dft_tables(SEQ)
    cl_s, sl_s = _dft_tables(DEC_SEQ)
    c64, s64 = _dft_tables(64)
    cl_p, sl_p, cl_s, sl_s, cbd, sbd = (
        jnp.asarray(a, F32).astype(BF) for a in (cl_p, sl_p, cl_s, sl_s, _block_diag4(c64), _block_diag4(s64)))
    ropes = _rope_tables(16) + _rope_tables(32)
    nb = _na_bias_tables(na_rpb)
    wf_bf = w_fourier.astype(BF)

    ck_na = cache_na_k.reshape(DEC_BATCH, DEPTH, PAST_LEN, 256)
    cv_na = cache_na_v.reshape(DEC_BATCH, DEPTH, PAST_LEN, 256)
    ck_df = cache_diff_k.reshape(DEC_BATCH, DEPTH, PAST_LEN, 256)
    cv_df = cache_diff_v.reshape(DEC_BATCH, DEPTH, PAST_LEN, 256)
    ck_sw = cache_swa_k.reshape(DEC_BATCH, DEPTH, PAST_LEN, 128)
    cv_sw = cache_swa_v.reshape(DEC_BATCH, DEPTH, PAST_LEN, 128)

    _, _, zb, caches = _trunk(xp, xs, mods, ropes, proj=(0, norm1_g[0], w_in, None))
    for l in range(DEPTH):
        lam_init = 0.8 - 0.6 * math.exp(-0.3 * l)
        lamp = jnp.stack([diff_lq1[l], diff_lk1[l], diff_lq2[l], diff_lk2[l]], axis=0)
        subg = jnp.tile(diff_subln_g[l].reshape(1, HEAD_DIM), (1, 2))

        cat_p = _ctx_mix(zb, swa_sink[l], lamp, subg, cl_p, sl_p, cbd, sbd, wf_bf[l], lam_init)
        cat_s = _lat_mix(zb, (ck_na, cv_na, ck_df, cv_df, ck_sw, cv_sw), nb, swa_sink[l], lamp, subg,
                         cl_s, sl_s, cbd, sbd, wf_bf[l], lam_init, l)
        nxt = (l + 1, norm1_g[l + 1], w_in, caches) if l + 1 < DEPTH else None
        xp, xs, zb, new_caches = _trunk(
            xp, xs, mods, ropes, mlp=(l, cat_p, cat_s, norm2_g[l], final_g, w_out, w_mlp1, w_mlp2), proj=nxt)
        caches = new_caches if nxt is not None else caches

    y_prompt = xp.reshape(BATCH, SEQ, D_MODEL)
    y_sample = xs.reshape(DEC_BATCH, DEC_SEQ, D_MODEL)
    new = [a.reshape(BATCH, DEPTH, SEQ, a.shape[-1] // HEAD_DIM, HEAD_DIM) for a in caches]
    return (y_prompt, y_sample) + tuple(new)
```

```python
import functools
import math

import numpy as np
import jax
import jax.numpy as jnp
from jax import lax
from jax.experimental import pallas as pl
from jax.experimental.pallas import tpu as pltpu

D_MODEL = 1024
BATCH = 16
SEQ = 256
DEPTH = 4
DEC_BATCH = 2
DEC_SEQ = 1024
PAST_LEN = 512
GRID_W = 64
GRID_ROWS = DEC_SEQ // GRID_W
HEAD_DIM = 64
NA_KH = 8
NA_KW = 16
DIFF_QK_DIM = 32
SWA_WINDOW = 128
D_FF = 4 * D_MODEL
ROPE_BASE = 10000.0
NORM_EPS = 1e-6
NEG_INF = -1e30
IN_WIDTH = 2304
MIX_WIDTH = 1024

N_CTX_TOK = BATCH * SEQ
N_LAT_TOK = DEC_BATCH * DEC_SEQ
N_TOK = N_CTX_TOK + N_LAT_TOK
ROW_TILE = 256
N_ROW_TILES = N_TOK // ROW_TILE
N_CTX_TILES = N_CTX_TOK // ROW_TILE
N_LAT_TILES = N_LAT_TOK // ROW_TILE
LAT_TILES_PER_REQ = DEC_SEQ // ROW_TILE
LAT_BLOCK0 = N_CTX_TOK // DEC_SEQ
MOD_ROWS = 8

C_NA_Q, C_NA_K, C_NA_V = 0, 256, 512
C_DQ, C_DK, C_DV = 768, 1024, 1280
C_FC = 1536
C_SQ, C_SK, C_SV = 1792, 2048, 2176
CACHE_COLS = ((C_NA_K, 256), (C_NA_V, 256), (C_DK, 256), (C_DV, 256), (C_SK, 128), (C_SV, 128))
ZB_SK, ZB_SV = 2048, 2304
ZB_WIDTH = 2560

NA_QROWS = 2
NA_WIN_ROWS = 10
NA_TABLE_OF_PAIR = (0, 1, 2, 2, 2, 2, 3, 4)
NA_TABLE_PAIRS = (0, 1, 2, 6, 7)

BF = jnp.bfloat16
F32 = jnp.float32
VMEM_LIMIT = 56 * 1024 * 1024


def _dot(a, b):
    return jnp.dot(a, b, preferred_element_type=F32)


def _dot_nt(a, b):
    return lax.dot_general(a, b, (((1,), (1,)), ((), ())), preferred_element_type=F32)


def _rmsnorm(x, g):
    ms = jnp.mean(x * x, axis=-1, keepdims=True)
    return x * lax.rsqrt(ms + NORM_EPS) * g


def _params(n_grid=1):
    return pltpu.CompilerParams(dimension_semantics=("arbitrary",) * n_grid, vmem_limit_bytes=VMEM_LIMIT)


def _rowmax(*parts):
    m = jnp.max(parts[0], axis=-1, keepdims=True)
    for p in parts[1:]:
        m = jnp.maximum(m, jnp.max(p, axis=-1, keepdims=True))
    return m


def _rowsum(*parts):
    s = jnp.sum(parts[0], axis=-1, keepdims=True)
    for p in parts[1:]:
        s = s + jnp.sum(p, axis=-1, keepdims=True)
    return s


def _ada_kernel(cond_ref, w_ref, b_ref, o_ref):
    cnd = cond_ref[...]
    s = cnd / (1.0 + jnp.exp(-cnd))
    o_ref[...] = _dot(s.astype(BF), w_ref[...].astype(BF)) + b_ref[...]


def _ada(cond, w_ada, b_ada):
    tn = 1024
    return pl.pallas_call(
        _ada_kernel,
        grid=(DEPTH, 6 * D_MODEL // tn),
        in_specs=[
            pl.BlockSpec((MOD_ROWS, D_MODEL), lambda l, j: (0, 0)),
            pl.BlockSpec((None, D_MODEL, tn), lambda l, j: (l, 0, j)),
            pl.BlockSpec((None, 1, tn), lambda l, j: (l, 0, j)),
        ],
        out_specs=pl.BlockSpec((None, MOD_ROWS, tn), lambda l, j: (l, 0, j)),
        out_shape=jax.ShapeDtypeStruct((DEPTH, MOD_ROWS, 6 * D_MODEL), F32),
        compiler_params=_params(2),
        name="ada",
    )(cond, w_ada, b_ada.reshape(DEPTH, 1, 6 * D_MODEL))


def _mod_row(t):
    return jnp.where(t < N_CTX_TILES, 0, 1 + (t - N_CTX_TILES) // LAT_TILES_PER_REQ)


def _ctx_tile(t):
    return jnp.minimum(t, N_CTX_TILES - 1)


def _lat_tile(t):
    return jnp.maximum(t - N_CTX_TILES, 0)


TRUNK_CAST_STEPS = 16
FF_CHUNK = 1024
TRUNK_VMEM_LIMIT = 60 * 1024 * 1024


def _rope(x, cos, sin_signed, shift, first):
    partner = jnp.where(first, pltpu.roll(x, 128 - shift, 1), pltpu.roll(x, shift, 1))
    return x * cos + partner * sin_signed


def _projection_stages(hb, wb, t, is_ctx, zb_ref, cache_refs, rope_refs):
    cosd_ref, sind_ref, coss_ref, sins_ref = rope_refs
    dscale = DIFF_QK_DIM ** -0.5
    p0 = pl.multiple_of(
        jnp.where(is_ctx, DEC_SEQ, ((t - N_CTX_TILES) % LAT_TILES_PER_REQ) * ROW_TILE), ROW_TILE)
    lane = lax.broadcasted_iota(jnp.int32, (ROW_TILE, 128), 1)
    first_d = (lane % 16) < 8
    first_s = (lane % 32) < 16
    lo = lane < HEAD_DIM
    cosd, sind = cosd_ref[pl.ds(p0, ROW_TILE), :], sind_ref[pl.ds(p0, ROW_TILE), :]
    coss, sins = coss_ref[pl.ds(p0, ROW_TILE), :], sins_ref[pl.ds(p0, ROW_TILE), :]

    def cols(c0, width):
        return _dot(hb(), wb[:, c0:c0 + width])

    def put(c0, width, val):
        zb_ref[:, c0:c0 + width] = val.astype(BF)

    def put_dup(c0, val):
        r = pltpu.roll(val, HEAD_DIM, 1)
        put(c0, 128, jnp.where(lo, val, r))
        put(c0 + 128, 128, jnp.where(lo, r, val))

    kept = {}

    def na_q():
        put(C_NA_Q, 256, cols(C_NA_Q, 256) * 0.125)

    def na_kv():
        kept["nakv"] = cols(C_NA_K, 512)
        put(C_NA_K, 512, kept["nakv"])

    def diff_q():
        z = cols(C_DQ, 256)
        for j in range(2):
            put(C_DQ + 128 * j, 128, _rope(z[:, 128 * j:128 * j + 128], cosd, sind, 8, first_d) * dscale)

    def diff_k():
        kept["dk"] = cols(C_DK, 256)
        for j in range(2):
            put(C_DK + 128 * j, 128, _rope(kept["dk"][:, 128 * j:128 * j + 128], cosd, sind, 8, first_d))

    def diff_v_fourier():
        kept["dvfc"] = cols(C_DV, 512)
        put(C_DV, 512, kept["dvfc"])

    def swa_q():
        z = cols(C_SQ, 256)
        for j in range(2):
            put(C_SQ + 128 * j, 128, _rope(z[:, 128 * j:128 * j + 128], coss, sins, 16, first_s) * 0.125)

    def swa_kv():
        kept["skv"] = cols(C_SK, 256)
        put_dup(ZB_SK, _rope(kept["skv"][:, 0:128], coss, sins, 16, first_s))
        put_dup(ZB_SV, kept["skv"][:, 128:256])

    def finish():
        new = (kept["nakv"][:, 0:256], kept["nakv"][:, 256:512], kept["dk"], kept["dvfc"][:, 0:256],
               kept["skv"][:, 0:128], kept["skv"][:, 128:256])
        for ref, val in zip(cache_refs, new):
            ref[...] = jnp.where(is_ctx, val, ref[...])

    return [na_q, na_kv, diff_q, diff_k, diff_v_fourier, swa_q, swa_kv], finish


def _trunk_kernel(do_mlp, do_proj, last, n_alias, *refs):
    xp_ref, xs_ref = refs[:2]
    pos = 2
    if do_mlp:
        catp_ref, cats_ref, mm_ref, g2_ref, fg_ref, wo_ref, w1_ref, w2_ref = refs[pos:pos + 8]
        pos += 8
    if do_proj:
        pm_ref, g1_ref, win_ref = refs[pos:pos + 3]
        rope_refs = refs[pos + 3:pos + 7]
        pos += 7 + n_alias
    if do_mlp:
        op_ref, os_ref = refs[pos:pos + 2]
        pos += 2
    if do_proj:
        zb_ref = refs[pos]
        cache_refs = refs[pos + 1:pos + 7]
        pos += 7
    if do_mlp:
        wob, w1b, w2b = refs[pos:pos + 3]
        pos += 3
    if do_proj:
        wb = refs[pos]
        pos += 1
    skew = do_mlp and do_proj
    if skew:
        hprev = refs[pos]
    s = pl.program_id(0)

    @pl.when(s < TRUNK_CAST_STEPS)
    def _():
        def cast(dst, src):
            rows = src.shape[0]
            dst[pl.ds(pl.multiple_of(s * rows, rows), rows), :] = src[...].astype(BF)

        if do_mlp:
            cast(wob, wo_ref)
            cast(w1b, w1_ref)
            cast(w2b, w2_ref)
        if do_proj:
            cast(wb, win_ref)
        if skew:
            @pl.when(s == 0)
            def _():
                hprev[...] = jnp.zeros_like(hprev)

    def tile_step(t, proj_stages, proj_finish):
        pending = list(proj_stages)

        def issue(n):
            for _ in range(min(n, len(pending))):
                pending.pop(0)()

        is_ctx = t < N_CTX_TILES
        row = _mod_row(t)
        x = jnp.where(is_ctx, xp_ref[...], xs_ref[...])
        if do_mlp:
            issue(1)
            gt1 = mm_ref[pl.ds(row, 1), 2 * D_MODEL:3 * D_MODEL]
            sh2 = mm_ref[pl.ds(row, 1), 3 * D_MODEL:4 * D_MODEL]
            sc2 = mm_ref[pl.ds(row, 1), 4 * D_MODEL:5 * D_MODEL]
            gt2 = mm_ref[pl.ds(row, 1), 5 * D_MODEL:6 * D_MODEL]
            cat = jnp.where(is_ctx, catp_ref[...], cats_ref[...])
            x = x + gt1 * _dot(cat, wob[...])
            issue(1)
            hh = (_rmsnorm(x, g2_ref[...]) * (1.0 + sc2) + sh2).astype(BF)
            acc = jnp.zeros((ROW_TILE, D_MODEL), F32)
            for c in range(D_FF // FF_CHUNK):
                a = jnp.maximum(_dot(hh, w1b[:, c * FF_CHUNK:(c + 1) * FF_CHUNK]), 0.0)
                acc = acc + _dot((a * a).astype(BF), w2b[c * FF_CHUNK:(c + 1) * FF_CHUNK, :])
                if c < 2:
                    issue(1)
            issue(1)
            x = x + gt2 * acc
            out = _rmsnorm(x, fg_ref[...]) if last else x
        hb = None
        if do_proj:
            issue(1)
            shift = pm_ref[pl.ds(row, 1), 0:D_MODEL]
            scale = pm_ref[pl.ds(row, 1), D_MODEL:2 * D_MODEL]
            hb = (_rmsnorm(x, g1_ref[...]) * (1.0 + scale) + shift).astype(BF)
        issue(len(pending))
        if proj_finish is not None:
            proj_finish()
        if do_mlp:
            op_ref[...] = jnp.where(is_ctx, out, op_ref[...])
            os_ref[...] = jnp.where(is_ctx, os_ref[...], out)
        return hb

    def projection(hb, t):
        return _projection_stages(hb, wb, t, t < N_CTX_TILES, zb_ref, cache_refs, rope_refs)

    @pl.when(s >= TRUNK_CAST_STEPS)
    def _():
        t = s - TRUNK_CAST_STEPS
        if not skew:
            hb = tile_step(t, [], None)
            if do_proj:
                stages, finish = projection(lambda: hb, t)
                for stage in stages:
                    stage()
                finish()
        else:
            u = jnp.maximum(t - 1, 0)

            @pl.when(t < N_ROW_TILES)
            def _():
                stages, finish = projection(lambda: hprev[...], u)
                hprev[...] = tile_step(t, stages, finish)

            @pl.when(t >= N_ROW_TILES)
            def _():
                stages, finish = projection(lambda: hprev[...], u)
                for stage in stages:
                    stage()
                finish()


def _trunk(xp, xs, mods, ropes, mlp=None, proj=None):
    do_mlp, do_proj = mlp is not None, proj is not None
    last = do_mlp and not do_proj
    skew = do_mlp and do_proj
    clamp_tile = lambda s: jnp.clip(s - TRUNK_CAST_STEPS, 0, N_ROW_TILES - 1)
    step = lambda f: (lambda s: f(clamp_tile(s)))
    step_proj = (lambda f: (lambda s: f(clamp_tile(s - 1)))) if skew else step
    const = lambda shape: pl.BlockSpec(shape, lambda s: (0,) * len(shape), pipeline_mode=pl.Buffered(1))
    chunk = lambda layer, rows, cols: pl.BlockSpec(
        (None, rows // TRUNK_CAST_STEPS, cols), lambda s: (layer, jnp.minimum(s, TRUNK_CAST_STEPS - 1), 0))
    mod_spec = lambda layer: pl.BlockSpec((None, MOD_ROWS, 6 * D_MODEL), lambda s: (layer, 0, 0),
                                          pipeline_mode=pl.Buffered(1))
    x_specs = [pl.BlockSpec((ROW_TILE, D_MODEL), step(lambda t: (_ctx_tile(t), 0))),
               pl.BlockSpec((ROW_TILE, D_MODEL), step(lambda t: (_lat_tile(t), 0)))]
    args, in_specs = [xp, xs], list(x_specs)
    out_specs, out_shape, scratch, aliases = [], [], [], {}
    n_alias = 0
    if do_mlp:
        layer, cat_p, cat_s, g2, final_g, wo, w1, w2 = mlp
        args += [cat_p, cat_s, mods, g2.reshape(1, D_MODEL), final_g.reshape(1, D_MODEL), wo, w1, w2]
        in_specs += [pl.BlockSpec((ROW_TILE, MIX_WIDTH), step(lambda t: (_ctx_tile(t), 0))),
                     pl.BlockSpec((ROW_TILE, MIX_WIDTH), step(lambda t: (_lat_tile(t), 0))), mod_spec(layer),
                     const((1, D_MODEL)), const((1, D_MODEL)),
                     chunk(layer, MIX_WIDTH, D_MODEL), chunk(layer, D_MODEL, D_FF), chunk(layer, D_FF, D_MODEL)]
    if do_proj:
        p_layer, g1, w_in, caches = proj
        args += [mods, g1.reshape(1, D_MODEL), w_in, *ropes]
        in_specs += [mod_spec(p_layer), const((1, D_MODEL)), chunk(p_layer, D_MODEL, IN_WIDTH)]
        in_specs += [const((DEC_SEQ + ROW_TILE, 128))] * 4
        if caches is not None:
            n_alias = 6
            n_out_before = 2 if do_mlp else 0
            aliases = {len(args) + i: n_out_before + 1 + i for i in range(6)}
            args += list(caches)
            in_specs += [pl.BlockSpec(memory_space=pl.ANY)] * 6
    if do_mlp:
        out_specs += x_specs
        out_shape += [jax.ShapeDtypeStruct((N_CTX_TOK, D_MODEL), F32),
                      jax.ShapeDtypeStruct((N_LAT_TOK, D_MODEL), F32)]
        scratch += [pltpu.VMEM((MIX_WIDTH, D_MODEL), BF), pltpu.VMEM((D_MODEL, D_FF), BF),
                    pltpu.VMEM((D_FF, D_MODEL), BF)]
    if do_proj:
        out_specs += [pl.BlockSpec((ROW_TILE, ZB_WIDTH), step_proj(lambda t: (t, 0)))] + [
            pl.BlockSpec((None, None, SEQ, width), step_proj(lambda t: (_ctx_tile(t), p_layer, 0, 0)))
            for _, width in CACHE_COLS]
        out_shape += [jax.ShapeDtypeStruct((N_TOK, ZB_WIDTH), BF)] + [
            jax.ShapeDtypeStruct((BATCH, DEPTH, SEQ, width), F32) for _, width in CACHE_COLS]
        scratch += [pltpu.VMEM((D_MODEL, IN_WIDTH), BF)]
    if skew:
        scratch += [pltpu.VMEM((ROW_TILE, D_MODEL), BF)]
    outs = pl.pallas_call(
        functools.partial(_trunk_kernel, do_mlp, do_proj, last, n_alias),
        grid=(TRUNK_CAST_STEPS + N_ROW_TILES + (1 if skew else 0),),
        in_specs=in_specs,
        out_specs=out_specs,
        out_shape=out_shape,
        scratch_shapes=scratch,
        input_output_aliases=aliases,
        compiler_params=pltpu.CompilerParams(dimension_semantics=("arbitrary",),
                                             vmem_limit_bytes=TRUNK_VMEM_LIMIT),
        name="trunk_mlp_proj" if (do_mlp and do_proj) else ("trunk_mlp" if do_mlp else "trunk_proj"),
    )(*args)
    outs = list(outs)
    if do_mlp:
        xp, xs = outs[:2]
        outs = outs[2:]
    zb, new_caches = (outs[0], tuple(outs[1:7])) if do_proj else (None, None)
    return xp, xs, zb, new_caches


def _diff_lambda(lamp_ref, lam_init):
    a = jnp.sum(lamp_ref[0:1, :] * lamp_ref[1:2, :], axis=-1, keepdims=True)
    b = jnp.sum(lamp_ref[2:3, :] * lamp_ref[3:4, :], axis=-1, keepdims=True)
    return jnp.exp(a) - jnp.exp(b) + lam_init


def _subln(o, g, lam_init):
    return _rmsnorm(o, g) * (1.0 - lam_init)


def _lane_lo(n):
    return lax.broadcasted_iota(jnp.int32, (n, 128), 1) < HEAD_DIM


def _lane_lo_wide(n, width):
    return lax.broadcasted_iota(jnp.int32, (n, width), 1) % 128 < HEAD_DIM


def _split_pair(x, lo):
    zero = jnp.zeros_like(x)
    return jnp.where(lo, x, zero), jnp.where(lo, zero, x)


def _v_ones_pair(v, lo):
    one = jnp.ones_like(v)
    return jnp.where(lo, v, one), jnp.where(lo, one, v)


def _merge_pair(a, b, lo):
    return jnp.where(lo, a, b), pltpu.roll(jnp.where(lo, b, a), HEAD_DIM, 1)


def _subln_pair(o, g2, lam_init, lo):
    sq = o * o
    ms_a = jnp.sum(jnp.where(lo, sq, 0.0), axis=-1, keepdims=True)
    ms_b = jnp.sum(jnp.where(lo, 0.0, sq), axis=-1, keepdims=True)
    ms = jnp.where(lo, ms_a, ms_b) * (1.0 / HEAD_DIM)
    return o * lax.rsqrt(ms + NORM_EPS) * g2 * (1.0 - lam_init)


def _diff_quarters(q, n):
    quarter = lax.broadcasted_iota(jnp.int32, (n, 128), 1) // DIFF_QK_DIM
    zero = jnp.zeros_like(q)
    return [jnp.where(quarter == i, q, zero) for i in range(4)]


def _fourier(x_bf, cl, sl, cbd, sbd, wf):
    xc = _dot(x_bf, cbd).astype(BF)
    xs = _dot(x_bf, sbd).astype(BF)
    y = _dot(cl, xc) - _dot(sl, xs)
    return _dot(y.astype(BF), wf)


CTX_UNIT_GROUP = 4
CTX_REQ_PER_STEP = 2


def _ctx_mix_kernel(lam_init, sink_ref, z_ref, lamp_ref, subg_ref, cl_ref, sl_ref, cbd_ref, sbd_ref, wf_ref,
                    o_ref):
    for r in range(CTX_REQ_PER_STEP):
        rows = pl.ds(r * SEQ, SEQ)
        _ctx_request(lam_init, sink_ref, z_ref.at[rows], lamp_ref, subg_ref, cl_ref, sl_ref, cbd_ref, sbd_ref,
                     wf_ref, o_ref.at[rows])


def _ctx_request(lam_init, sink_ref, z_ref, lamp_ref, subg_ref, cl_ref, sl_ref, cbd_ref, sbd_ref, wf_ref, o_ref):
    lo = _lane_lo(SEQ)

    units = []
    for j in range(2):
        c = 128 * j
        qs = _split_pair(z_ref[:, C_NA_Q + c:C_NA_Q + c + 128], lo)
        vs = _v_ones_pair(z_ref[:, C_NA_V + c:C_NA_V + c + 128], lo)
        units += [(qs[i], (C_NA_K + c), vs[i], None) for i in range(2)]
    for j in range(2):
        c = 128 * j
        q4 = _diff_quarters(z_ref[:, C_DQ + c:C_DQ + c + 128], SEQ)
        vs = _v_ones_pair(z_ref[:, C_DV + c:C_DV + c + 128], lo)
        units += [(q4[i], (C_DK + c), vs[i // 2], None) for i in (0, 2, 1, 3)]
    for g in range(2):
        c = 128 * g
        qs = _split_pair(z_ref[:, C_SQ + c:C_SQ + c + 128], lo)
        vs = _v_ones_pair(z_ref[:, ZB_SV + c:ZB_SV + c + 128], lo)
        units += [(qs[i], (ZB_SK + c), vs[i], sink_ref[2 * g + i]) for i in range(2)]

    res, extras = [], []
    for u0 in range(0, len(units), CTX_UNIT_GROUP):
        group = units[u0:u0 + CTX_UNIT_GROUP]
        scores = [_dot_nt(q, z_ref[:, kc:kc + 128]) for q, kc, _, _ in group]
        weights = []
        for s, (_, _, _, sink) in zip(scores, group):
            m = _rowmax(s)
            if sink is not None:
                m = jnp.maximum(m, sink)
                extras.append(jnp.exp(sink - m))
            weights.append(jnp.exp(s - m).astype(BF))
        res += [_dot(e, v) for e, (_, _, v, _) in zip(weights, group)]

    for j in range(2):
        num, den = _merge_pair(res[2 * j], res[2 * j + 1], lo)
        o_ref[:, 128 * j:128 * j + 128] = (num / den).astype(BF)

    lam = _diff_lambda(lamp_ref, lam_init)
    for j in range(2):
        n1, d1 = _merge_pair(res[4 + 4 * j], res[5 + 4 * j], lo)
        n2, d2 = _merge_pair(res[6 + 4 * j], res[7 + 4 * j], lo)
        o = n1 / d1 - lam * (n2 / d2)
        o_ref[:, 256 + 128 * j:384 + 128 * j] = _subln_pair(o, subg_ref[...], lam_init, lo).astype(BF)

    o_c = _fourier(z_ref[:, C_FC:C_FC + 256], cl_ref[...], sl_ref[...], cbd_ref[...], sbd_ref[...], wf_ref[...])
    o_ref[:, 512:768] = o_c.astype(BF)

    for g in range(2):
        num, den = _merge_pair(res[12 + 2 * g], res[13 + 2 * g], lo)
        den = den + jnp.where(lo, extras[2 * g], extras[2 * g + 1])
        o_ref[:, 768 + 128 * g:896 + 128 * g] = (num / den).astype(BF)


def _ctx_mix(zb, sink, lamp, subg, cl, sl, cbd, sbd, wf, lam_init):
    full = lambda shape: pl.BlockSpec(shape, lambda b, s: (0,) * len(shape))
    grid_spec = pltpu.PrefetchScalarGridSpec(
        num_scalar_prefetch=1,
        grid=(BATCH // CTX_REQ_PER_STEP,),
        in_specs=[
            pl.BlockSpec((CTX_REQ_PER_STEP * SEQ, ZB_WIDTH), lambda b, s: (b, 0)),
            full((4, DIFF_QK_DIM)),
            full((1, 2 * HEAD_DIM)),
            full((SEQ, SEQ)), full((SEQ, SEQ)),
            full((256, 256)), full((256, 256)), full((256, 256)),
        ],
        out_specs=pl.BlockSpec((CTX_REQ_PER_STEP * SEQ, MIX_WIDTH), lambda b, s: (b, 0)),
    )
    return pl.pallas_call(
        functools.partial(_ctx_mix_kernel, lam_init),
        grid_spec=grid_spec,
        out_shape=jax.ShapeDtypeStruct((N_CTX_TOK, MIX_WIDTH), BF),
        compiler_params=_params(),
        name="ctx_mix",
    )(sink, zb, lamp, subg, cl, sl, cbd, sbd, wf)


def _cache_v_ones(cv_ref, cva, cvb):
    lo = _lane_lo_wide(PAST_LEN, 256)
    cv = cv_ref[...]
    cva[...] = jnp.where(lo, cv, 1.0).astype(BF)
    cvb[...] = jnp.where(lo, 1.0, cv).astype(BF)


def _lat_na_part(z_ref, ck_ref, cv_ref, nb_ref, o_ref, ckb, cva, cvb):
    ckb[...] = ck_ref[...].astype(BF)
    _cache_v_ones(cv_ref, cva, cvb)
    n_q = NA_QROWS * GRID_W
    n_loc = NA_WIN_ROWS * GRID_W
    lo_q = _lane_lo(n_q)
    lo_w = _lane_lo(n_loc)

    def body(p, carry):
        w_row = jnp.clip(NA_QROWS * p - NA_KH // 2, 0, GRID_ROWS - NA_WIN_ROWS)
        tbl = jnp.minimum(p, 2) + jnp.maximum(p - 5, 0)
        q0 = pl.multiple_of(p * n_q, n_q)
        k0 = pl.multiple_of(w_row * GRID_W, NA_QROWS * GRID_W)
        scores = []
        for h in range(4):
            c = 128 * (h // 2)
            q = _split_pair(z_ref[pl.ds(q0, n_q), C_NA_Q + c:C_NA_Q + c + 128], lo_q)[h % 2]
            s_loc = _dot_nt(q, z_ref[pl.ds(k0, n_loc), C_NA_K + c:C_NA_K + c + 128]) + nb_ref[h, tbl]
            s_ctx = _dot_nt(q, ckb[:, c:c + 128])
            scores.append((s_loc, s_ctx))
        weights = []
        for s_loc, s_ctx in scores:
            m = _rowmax(s_loc, s_ctx)
            weights.append((jnp.exp(s_loc - m).astype(BF), jnp.exp(s_ctx - m).astype(BF)))
        res = []
        for h, (e_loc, e_ctx) in enumerate(weights):
            c = 128 * (h // 2)
            v = _v_ones_pair(z_ref[pl.ds(k0, n_loc), C_NA_V + c:C_NA_V + c + 128], lo_w)[h % 2]
            cv = (cva, cvb)[h % 2][:, c:c + 128]
            res.append(_dot(e_loc, v) + _dot(e_ctx, cv))
        for j in range(2):
            num, den = _merge_pair(res[2 * j], res[2 * j + 1], lo_q)
            o_ref[pl.ds(q0, n_q), 128 * j:128 * j + 128] = (num / den).astype(BF)
        return carry

    lax.fori_loop(0, GRID_ROWS // NA_QROWS, body, 0)


DIFF_QBLK = 512


def _lat_diff_part(lam_init, z_ref, ck_ref, cv_ref, lamp_ref, subg_ref, o_ref, ckb, cva, cvb, vla, vlb):
    ckb[...] = ck_ref[...].astype(BF)
    _cache_v_ones(cv_ref, cva, cvb)
    lo_v = _lane_lo_wide(DEC_SEQ, 256)
    v_loc = z_ref[:, C_DV:C_DV + 256]
    one = jnp.ones_like(v_loc)
    vla[...] = jnp.where(lo_v, v_loc, one)
    vlb[...] = jnp.where(lo_v, one, v_loc)
    lam = _diff_lambda(lamp_ref, lam_init)
    qblk = DIFF_QBLK
    lo_q = _lane_lo(qblk)

    def body(i, carry):
        q0 = pl.multiple_of(i * qblk, qblk)
        for j in range(2):
            c = 128 * j
            q4 = _diff_quarters(z_ref[pl.ds(q0, qblk), C_DQ + c:C_DQ + c + 128], qblk)
            scores = [(_dot_nt(q, z_ref[:, C_DK + c:C_DK + c + 128]), _dot_nt(q, ckb[:, c:c + 128])) for q in q4]
            weights = []
            for s_loc, s_ctx in scores:
                m = _rowmax(s_loc, s_ctx)
                weights.append((jnp.exp(s_loc - m).astype(BF), jnp.exp(s_ctx - m).astype(BF)))
            res = []
            for t, (e_loc, e_ctx) in enumerate(weights):
                v_loc, v_ctx = ((vla, cva), (vlb, cvb))[t // 2]
                res.append(_dot(e_loc, v_loc[:, c:c + 128]) + _dot(e_ctx, v_ctx[:, c:c + 128]))
            n1, d1 = _merge_pair(res[0], res[2], lo_q)
            n2, d2 = _merge_pair(res[1], res[3], lo_q)
            o = n1 / d1 - lam * (n2 / d2)
            o_ref[pl.ds(q0, qblk), 256 + c:256 + c + 128] = (
                _subln_pair(o, subg_ref[...], lam_init, lo_q).astype(BF))
        return carry

    lax.fori_loop(0, DEC_SEQ // qblk, body, 0)


def _lat_swa_part(sink_ref, z_ref, ck_ref, cv_ref, o_ref, ckd, cva, cvb):
    W = SWA_WINDOW
    n_win = 3 * W
    lo_c = _lane_lo(PAST_LEN)
    ck = ck_ref[...]
    ck_r = pltpu.roll(ck, HEAD_DIM, 1)
    ckd[:, 0:128] = jnp.where(lo_c, ck, ck_r).astype(BF)
    ckd[:, 128:256] = jnp.where(lo_c, ck_r, ck).astype(BF)
    cv = cv_ref[...]
    cv_r = pltpu.roll(cv, HEAD_DIM, 1)
    cva[:, 0:128] = jnp.where(lo_c, cv, 1.0).astype(BF)
    cvb[:, 0:128] = jnp.where(lo_c, 1.0, cv_r).astype(BF)
    cva[:, 128:256] = jnp.where(lo_c, cv_r, 1.0).astype(BF)
    cvb[:, 128:256] = jnp.where(lo_c, 1.0, cv).astype(BF)
    lo_q = _lane_lo(W)
    lo_w = _lane_lo(n_win)

    def body(n, carry):
        q0 = pl.multiple_of(n * W, W)
        w0 = pl.multiple_of(jnp.clip((n - 1) * W, 0, DEC_SEQ - n_win), W)
        qpos = q0 + lax.broadcasted_iota(jnp.int32, (W, n_win), 0)
        kpos = w0 + lax.broadcasted_iota(jnp.int32, (W, n_win), 1)
        valid = jnp.abs(qpos - kpos) <= W
        scores = []
        for h in range(4):
            c = 128 * (h // 2)
            q = _split_pair(z_ref[pl.ds(q0, W), C_SQ + c:C_SQ + c + 128], lo_q)[h % 2]
            s_loc = jnp.where(valid, _dot_nt(q, z_ref[pl.ds(w0, n_win), ZB_SK + c:ZB_SK + c + 128]), NEG_INF)
            scores.append((s_loc, _dot_nt(q, ckd[:, c:c + 128])))
        weights, extras = [], []
        for h, (s_loc, s_ctx) in enumerate(scores):
            sink = sink_ref[h]
            m = jnp.maximum(_rowmax(s_loc, s_ctx), sink)
            weights.append((jnp.exp(s_loc - m).astype(BF), jnp.exp(s_ctx - m).astype(BF)))
            extras.append(jnp.exp(sink - m))
        res = []
        for h, (e_loc, e_ctx) in enumerate(weights):
            c = 128 * (h // 2)
            v = _v_ones_pair(z_ref[pl.ds(w0, n_win), ZB_SV + c:ZB_SV + c + 128], lo_w)[h % 2]
            res.append(_dot(e_loc, v) + _dot(e_ctx, (cva, cvb)[h % 2][:, c:c + 128]))
        for g in range(2):
            num, den = _merge_pair(res[2 * g], res[2 * g + 1], lo_q)
            den = den + jnp.where(lo_q, extras[2 * g], extras[2 * g + 1])
            o_ref[pl.ds(q0, W), 768 + 128 * g:768 + 128 * g + 128] = (num / den).astype(BF)
        return carry

    lax.fori_loop(0, DEC_SEQ // W, body, 0)


def _lat_mix_kernel(lam_init, sink_ref, z_ref, nak_ref, nav_ref, dfk_ref, dfv_ref, swk_ref, swv_ref, nb_ref,
                    lamp_ref, subg_ref, cl_ref, sl_ref, cbd_ref, sbd_ref, wf_ref, o_ref, ck, cva, cvb, vla, vlb):
    _lat_na_part(z_ref, nak_ref, nav_ref, nb_ref, o_ref, ck, cva, cvb)
    _lat_diff_part(lam_init, z_ref, dfk_ref, dfv_ref, lamp_ref, subg_ref, o_ref, ck, cva, cvb, vla, vlb)
    o_c = _fourier(z_ref[:, C_FC:C_FC + 256], cl_ref[...], sl_ref[...], cbd_ref[...], sbd_ref[...], wf_ref[...])
    o_ref[:, 512:768] = o_c.astype(BF)
    _lat_swa_part(sink_ref, z_ref, swk_ref, swv_ref, o_ref, ck, cva, cvb)


def _lat_mix(zb, caches_in, nb, sink, lamp, subg, cl, sl, cbd, sbd, wf, lam_init, layer):
    full = lambda shape: pl.BlockSpec(shape, lambda b, s: (0,) * len(shape), pipeline_mode=pl.Buffered(1))
    cache = lambda width: pl.BlockSpec((None, None, PAST_LEN, width), lambda b, s: (b, layer, 0, 0))
    n_q = NA_QROWS * GRID_W
    n_loc = NA_WIN_ROWS * GRID_W
    grid_spec = pltpu.PrefetchScalarGridSpec(
        num_scalar_prefetch=1,
        grid=(DEC_BATCH,),
        in_specs=[
            pl.BlockSpec((DEC_SEQ, ZB_WIDTH), lambda b, s: (LAT_BLOCK0 + b, 0)),
            cache(256), cache(256), cache(256), cache(256), cache(128), cache(128),
            pl.BlockSpec((None, 4, len(NA_TABLE_PAIRS), n_q, n_loc), lambda b, s: (layer, 0, 0, 0, 0),
                         pipeline_mode=pl.Buffered(1)),
            full((4, DIFF_QK_DIM)), full((1, 2 * HEAD_DIM)),
            full((DEC_SEQ, DEC_SEQ)), full((DEC_SEQ, DEC_SEQ)),
            full((256, 256)), full((256, 256)), full((256, 256)),
        ],
        out_specs=pl.BlockSpec((DEC_SEQ, MIX_WIDTH), lambda b, s: (b, 0)),
        scratch_shapes=[pltpu.VMEM((PAST_LEN, 256), BF)] * 3 + [pltpu.VMEM((DEC_SEQ, 256), BF)] * 2,
    )
    return pl.pallas_call(
        functools.partial(_lat_mix_kernel, lam_init),
        grid_spec=grid_spec,
        out_shape=jax.ShapeDtypeStruct((N_LAT_TOK, MIX_WIDTH), BF),
        compiler_params=_params(),
        name="lat_mix",
    )(sink, zb, *caches_in, nb, lamp, subg, cl, sl, cbd, sbd, wf)


def _dft_tables(n):
    j = np.arange(n)
    ang = 2.0 * np.pi * ((j[:, None] * j[None, :]) % n) / n
    return np.cos(ang) / np.sqrt(n), np.sin(ang) / np.sqrt(n)


def _block_diag4(m):
    out = np.zeros((256, 256), m.dtype)
    for g in range(4):
        out[64 * g:64 * g + 64, 64 * g:64 * g + 64] = m
    return out


def _rope_tables(n_axis_dims):
    half = n_axis_dims // 2
    inv = ROPE_BASE ** (-np.arange(half, dtype=np.float64) / half)
    t = np.arange(DEC_SEQ)
    lane = np.arange(128)
    w = lane % n_axis_dims
    is_col = (lane // n_axis_dims) % 2 == 1
    pos = np.where(is_col[None, :], (t % GRID_W)[:, None], (t // GRID_W)[:, None]).astype(np.float64)
    ang = pos * inv[w % half][None, :]
    sign = np.where(w < half, -1.0, 1.0)[None, :]
    cos = np.concatenate([np.cos(ang), np.ones((ROW_TILE, 128))], axis=0)
    sin = np.concatenate([np.sin(ang) * sign, np.zeros((ROW_TILE, 128))], axis=0)
    return jnp.asarray(cos, F32), jnp.asarray(sin, F32)


def _na_bias_tables(rpb):
    lead = rpb.shape[:2]
    col = np.arange(GRID_W)
    cstart = np.clip(col - NA_KW // 2, 0, GRID_W - NA_KW)
    cmask = (col[None, :] >= cstart[:, None]) & (col[None, :] < cstart[:, None] + NA_KW)
    pad = GRID_W - NA_KW
    rp = jnp.pad(rpb, ((0, 0), (0, 0), (0, 0), (pad, pad)))
    t = jnp.stack([rp[..., pad + NA_KW - 1 - c:pad + NA_KW - 1 - c + GRID_W] for c in range(GRID_W)], axis=-2)
    t = jnp.where(cmask, t, NEG_INF)
    t = t.transpose(0, 1, 3, 2, 4).reshape(lead + (GRID_W, (2 * NA_KH - 1) * GRID_W))
    tables = []
    for p in NA_TABLE_PAIRS:
        w_row = int(np.clip(NA_QROWS * p - NA_KH // 2, 0, GRID_ROWS - NA_WIN_ROWS))
        rows = []
        for r in (NA_QROWS * p, NA_QROWS * p + 1):
            rs = int(np.clip(r - NA_KH // 2, 0, GRID_ROWS - NA_KH))
            j_lo, j_hi = rs - w_row, rs - w_row + NA_KH
            d_lo = rs - r + NA_KH - 1
            body = t[..., d_lo * GRID_W:(d_lo + NA_KH) * GRID_W]
            rows.append(jnp.pad(body, ((0, 0),) * 3 + ((j_lo * GRID_W, (NA_WIN_ROWS - j_hi) * GRID_W),),
                                constant_values=NEG_INF))
        tables.append(jnp.concatenate(rows, axis=-2))
    return jnp.stack(tables, axis=2)


def kernel(x_prompt, x_sample, cache_na_k, cache_na_v, cache_diff_k, cache_diff_v, cache_swa_k, cache_swa_v, c, c_ctx, w_ada, b_ada, norm1_g, norm2_g, w_in, na_rpb, diff_lq1, diff_lk1, diff_lq2, diff_lk2, diff_subln_g, w_fourier, swa_sink, w_out, w_mlp1, w_mlp2, final_g):
    xp = x_prompt.reshape(N_CTX_TOK, D_MODEL)
    xs = x_sample.reshape(N_LAT_TOK, D_MODEL)
    cond = jnp.zeros((MOD_ROWS, D_MODEL), F32).at[0].set(c_ctx).at[1:1 + DEC_BATCH].set(c)
    mods = _ada(cond, w_ada, b_ada)

    cl_p, sl_p = _dft_tables(SEQ)
    cl_s, sl_s = _dft_tables(DEC_SEQ)
    c64, s64 = _dft_tables(64)
    cl_p, sl_p, cl_s, sl_s, cbd, sbd = (
        jnp.asarray(a, F32).astype(BF) for a in (cl_p, sl_p, cl_s, sl_s, _block_diag4(c64), _block_diag4(s64)))
    ropes = _rope_tables(16) + _rope_tables(32)
    nb = _na_bias_tables(na_rpb)
    wf_bf = w_fourier.astype(BF)

    ck_na = cache_na_k.reshape(DEC_BATCH, DEPTH, PAST_LEN, 256)
    cv_na = cache_na_v.reshape(DEC_BATCH, DEPTH, PAST_LEN, 256)
    ck_df = cache_diff_k.reshape(DEC_BATCH, DEPTH, PAST_LEN, 256)
    cv_df = cache_diff_v.reshape(DEC_BATCH, DEPTH, PAST_LEN, 256)
    ck_sw = cache_swa_k.reshape(DEC_BATCH, DEPTH, PAST_LEN, 128)
    cv_sw = cache_swa_v.reshape(DEC_BATCH, DEPTH, PAST_LEN, 128)

    _, _, zb, caches = _trunk(xp, xs, mods, ropes, proj=(0, norm1_g[0], w_in, None))
    for l in range(DEPTH):
        lam_init = 0.8 - 0.6 * math.exp(-0.3 * l)
        lamp = jnp.stack([diff_lq1[l], diff_lk1[l], diff_lq2[l], diff_lk2[l]], axis=0)
        subg = jnp.tile(diff_subln_g[l].reshape(1, HEAD_DIM), (1, 2))

        cat_p = _ctx_mix(zb, swa_sink[l], lamp, subg, cl_p, sl_p, cbd, sbd, wf_bf[l], lam_init)
        cat_s = _lat_mix(zb, (ck_na, cv_na, ck_df, cv_df, ck_sw, cv_sw), nb, swa_sink[l], lamp, subg,
                         cl_s, sl_s, cbd, sbd, wf_bf[l], lam_init, l)
        nxt = (l + 1, norm1_g[l + 1], w_in, caches) if l + 1 < DEPTH else None
        xp, xs, zb, new_caches = _trunk(
            xp, xs, mods, ropes, mlp=(l, cat_p, cat_s, norm2_g[l], final_g, w_out, w_mlp1, w_mlp2), proj=nxt)
        caches = new_caches if nxt is not None else caches

    y_prompt = xp.reshape(BATCH, SEQ, D_MODEL)
    y_sample = xs.reshape(DEC_BATCH, DEC_SEQ, D_MODEL)
    new = [a.reshape(BATCH, DEPTH, SEQ, a.shape[-1] // HEAD_DIM, HEAD_DIM) for a in caches]
    return (y_prompt, y_sample) + tuple(new)
```

```python
import functools
import math

import numpy as np
import jax
import jax.numpy as jnp
from jax import lax
from jax.experimental import pallas as pl
from jax.experimental.pallas import tpu as pltpu

D_MODEL = 1024
BATCH = 16
SEQ = 256
DEPTH = 4
DEC_BATCH = 2
DEC_SEQ = 1024
PAST_LEN = 512
GRID_W = 64
GRID_ROWS = DEC_SEQ // GRID_W
HEAD_DIM = 64
NA_KH = 8
NA_KW = 16
DIFF_QK_DIM = 32
SWA_WINDOW = 128
D_FF = 4 * D_MODEL
ROPE_BASE = 10000.0
NORM_EPS = 1e-6
NEG_INF = -1e30
IN_WIDTH = 2304
MIX_WIDTH = 1024

N_CTX_TOK = BATCH * SEQ
N_LAT_TOK = DEC_BATCH * DEC_SEQ
N_TOK = N_CTX_TOK + N_LAT_TOK
ROW_TILE = 256
N_ROW_TILES = N_TOK // ROW_TILE
N_CTX_TILES = N_CTX_TOK // ROW_TILE
N_LAT_TILES = N_LAT_TOK // ROW_TILE
LAT_TILES_PER_REQ = DEC_SEQ // ROW_TILE
LAT_BLOCK0 = N_CTX_TOK // DEC_SEQ
MOD_ROWS = 8

C_NA_Q, C_NA_K, C_NA_V = 0, 256, 512
C_DQ, C_DK, C_DV = 768, 1024, 1280
C_FC = 1536
C_SQ, C_SK, C_SV = 1792, 2048, 2176
CACHE_COLS = ((C_NA_K, 256), (C_NA_V, 256), (C_DK, 256), (C_DV, 256), (C_SK, 128), (C_SV, 128))
ZB_SK, ZB_SV = 2048, 2304
ZB_WIDTH = 2560

NA_QROWS = 2
NA_WIN_ROWS = 10
NA_TABLE_OF_PAIR = (0, 1, 2, 2, 2, 2, 3, 4)
NA_TABLE_PAIRS = (0, 1, 2, 6, 7)

BF = jnp.bfloat16
F32 = jnp.float32
VMEM_LIMIT = 56 * 1024 * 1024


def _dot(a, b):
    return jnp.dot(a, b, preferred_element_type=F32)


def _dot_nt(a, b):
    return lax.dot_general(a, b, (((1,), (1,)), ((), ())), preferred_element_type=F32)


def _rmsnorm(x, g):
    ms = jnp.mean(x * x, axis=-1, keepdims=True)
    return x * lax.rsqrt(ms + NORM_EPS) * g


def _params(n_grid=1):
    return pltpu.CompilerParams(dimension_semantics=("arbitrary",) * n_grid, vmem_limit_bytes=VMEM_LIMIT)


def _rowmax(*parts):
    m = jnp.max(parts[0], axis=-1, keepdims=True)
    for p in parts[1:]:
        m = jnp.maximum(m, jnp.max(p, axis=-1, keepdims=True))
    return m


def _rowsum(*parts):
    s = jnp.sum(parts[0], axis=-1, keepdims=True)
    for p in parts[1:]:
        s = s + jnp.sum(p, axis=-1, keepdims=True)
    return s


def _ada_kernel(cond_ref, w_ref, b_ref, o_ref):
    cnd = cond_ref[...]
    s = cnd / (1.0 + jnp.exp(-cnd))
    o_ref[...] = _dot(s.astype(BF), w_ref[...].astype(BF)) + b_ref[...]


def _ada_specs(layer, n_steps, index_of_step):
    tn = 6 * D_MODEL // n_steps
    col = lambda *g: index_of_step(*g)
    in_specs = [
        pl.BlockSpec((MOD_ROWS, D_MODEL), lambda *g: (0, 0)),
        pl.BlockSpec((None, D_MODEL, tn), lambda *g: (layer, 0, col(*g))),
        pl.BlockSpec((None, 1, tn), lambda *g: (layer, 0, col(*g))),
    ]
    return in_specs, pl.BlockSpec((MOD_ROWS, tn), lambda *g: (0, col(*g)))


def _ada(cond, w_ada, b_ada3, layer):
    in_specs, out_spec = _ada_specs(layer, 6, lambda j: j)
    return pl.pallas_call(
        _ada_kernel,
        grid=(6,),
        in_specs=in_specs,
        out_specs=out_spec,
        out_shape=jax.ShapeDtypeStruct((MOD_ROWS, 6 * D_MODEL), F32),
        compiler_params=_params(),
        name="ada",
    )(cond, w_ada, b_ada3)


def _mod_row(t):
    return jnp.where(t < N_CTX_TILES, 0, 1 + (t - N_CTX_TILES) // LAT_TILES_PER_REQ)


def _ctx_tile(t):
    return jnp.minimum(t, N_CTX_TILES - 1)


def _lat_tile(t):
    return jnp.maximum(t - N_CTX_TILES, 0)


TRUNK_CAST_STEPS = 16
FF_CHUNK = 1024
TRUNK_VMEM_LIMIT = 60 * 1024 * 1024


def _rope(x, cos, sin_signed, shift, first):
    partner = jnp.where(first, pltpu.roll(x, 128 - shift, 1), pltpu.roll(x, shift, 1))
    return x * cos + partner * sin_signed


def _projection_stages(hb, wb, t, is_ctx, zb_ref, cache_refs, rope_refs):
    cosd_ref, sind_ref, coss_ref, sins_ref = rope_refs
    dscale = DIFF_QK_DIM ** -0.5
    p0 = pl.multiple_of(
        jnp.where(is_ctx, DEC_SEQ, ((t - N_CTX_TILES) % LAT_TILES_PER_REQ) * ROW_TILE), ROW_TILE)
    lane = lax.broadcasted_iota(jnp.int32, (ROW_TILE, 128), 1)
    first_d = (lane % 16) < 8
    first_s = (lane % 32) < 16
    lo = lane < HEAD_DIM
    cosd, sind = cosd_ref[pl.ds(p0, ROW_TILE), :], sind_ref[pl.ds(p0, ROW_TILE), :]
    coss, sins = coss_ref[pl.ds(p0, ROW_TILE), :], sins_ref[pl.ds(p0, ROW_TILE), :]

    def cols(c0, width):
        return _dot(hb(), wb[:, c0:c0 + width])

    def put(c0, width, val):
        zb_ref[:, c0:c0 + width] = val.astype(BF)

    def put_dup(c0, val):
        r = pltpu.roll(val, HEAD_DIM, 1)
        put(c0, 128, jnp.where(lo, val, r))
        put(c0 + 128, 128, jnp.where(lo, r, val))

    kept = {}

    def na_q():
        put(C_NA_Q, 256, cols(C_NA_Q, 256) * 0.125)

    def na_kv():
        kept["nakv"] = cols(C_NA_K, 512)
        put(C_NA_K, 512, kept["nakv"])

    def diff_q():
        z = cols(C_DQ, 256)
        for j in range(2):
            put(C_DQ + 128 * j, 128, _rope(z[:, 128 * j:128 * j + 128], cosd, sind, 8, first_d) * dscale)

    def diff_k():
        kept["dk"] = cols(C_DK, 256)
        for j in range(2):
            put(C_DK + 128 * j, 128, _rope(kept["dk"][:, 128 * j:128 * j + 128], cosd, sind, 8, first_d))

    def diff_v_fourier():
        kept["dvfc"] = cols(C_DV, 512)
        put(C_DV, 512, kept["dvfc"])

    def swa_q():
        z = cols(C_SQ, 256)
        for j in range(2):
            put(C_SQ + 128 * j, 128, _rope(z[:, 128 * j:128 * j + 128], coss, sins, 16, first_s) * 0.125)

    def swa_kv():
        kept["skv"] = cols(C_SK, 256)
        put_dup(ZB_SK, _rope(kept["skv"][:, 0:128], coss, sins, 16, first_s))
        put_dup(ZB_SV, kept["skv"][:, 128:256])

    def finish():
        new = (kept["nakv"][:, 0:256], kept["nakv"][:, 256:512], kept["dk"], kept["dvfc"][:, 0:256],
               kept["skv"][:, 0:128], kept["skv"][:, 128:256])
        for ref, val in zip(cache_refs, new):
            ref[...] = jnp.where(is_ctx, val, ref[...])

    return [na_q, na_kv, diff_q, diff_k, diff_v_fourier, swa_q, swa_kv], finish


def _trunk_kernel(do_mlp, do_proj, last, n_alias, *refs):
    xp_ref, xs_ref = refs[:2]
    pos = 2
    if do_mlp:
        catp_ref, cats_ref, mm_ref, g2_ref, fg_ref, wo_ref, w1_ref, w2_ref = refs[pos:pos + 8]
        pos += 8
    if do_proj:
        pm_ref, g1_ref, win_ref = refs[pos:pos + 3]
        rope_refs = refs[pos + 3:pos + 7]
        pos += 7 + n_alias
    if do_mlp:
        op_ref, os_ref = refs[pos:pos + 2]
        pos += 2
    if do_proj:
        zb_ref = refs[pos]
        cache_refs = refs[pos + 1:pos + 7]
        pos += 7
    if do_mlp:
        wob, w1b, w2b = refs[pos:pos + 3]
        pos += 3
    if do_proj:
        wb = refs[pos]
        pos += 1
    skew = do_mlp and do_proj
    if skew:
        hprev = refs[pos]
    s = pl.program_id(0)

    @pl.when(s < TRUNK_CAST_STEPS)
    def _():
        def cast(dst, src):
            rows = src.shape[0]
            dst[pl.ds(pl.multiple_of(s * rows, rows), rows), :] = src[...].astype(BF)

        if do_mlp:
            cast(wob, wo_ref)
            cast(w1b, w1_ref)
            cast(w2b, w2_ref)
        if do_proj:
            cast(wb, win_ref)
        if skew:
            @pl.when(s == 0)
            def _():
                hprev[...] = jnp.zeros_like(hprev)

    def tile_step(t, proj_stages, proj_finish):
        pending = list(proj_stages)

        def issue(n):
            for _ in range(min(n, len(pending))):
                pending.pop(0)()

        is_ctx = t < N_CTX_TILES
        row = _mod_row(t)
        x = jnp.where(is_ctx, xp_ref[...], xs_ref[...])
        if do_mlp:
            issue(1)
            gt1 = mm_ref[pl.ds(row, 1), 2 * D_MODEL:3 * D_MODEL]
            sh2 = mm_ref[pl.ds(row, 1), 3 * D_MODEL:4 * D_MODEL]
            sc2 = mm_ref[pl.ds(row, 1), 4 * D_MODEL:5 * D_MODEL]
            gt2 = mm_ref[pl.ds(row, 1), 5 * D_MODEL:6 * D_MODEL]
            cat = jnp.where(is_ctx, catp_ref[...], cats_ref[...])
            x = x + gt1 * _dot(cat, wob[...])
            issue(1)
            hh = (_rmsnorm(x, g2_ref[...]) * (1.0 + sc2) + sh2).astype(BF)
            acc = jnp.zeros((ROW_TILE, D_MODEL), F32)
            for c in range(D_FF // FF_CHUNK):
                a = jnp.maximum(_dot(hh, w1b[:, c * FF_CHUNK:(c + 1) * FF_CHUNK]), 0.0)
                acc = acc + _dot((a * a).astype(BF), w2b[c * FF_CHUNK:(c + 1) * FF_CHUNK, :])
                if c < 2:
                    issue(1)
            issue(1)
            x = x + gt2 * acc
            out = _rmsnorm(x, fg_ref[...]) if last else x
        hb = None
        if do_proj:
            issue(1)
            shift = pm_ref[pl.ds(row, 1), 0:D_MODEL]
            scale = pm_ref[pl.ds(row, 1), D_MODEL:2 * D_MODEL]
            hb = (_rmsnorm(x, g1_ref[...]) * (1.0 + scale) + shift).astype(BF)
        issue(len(pending))
        if proj_finish is not None:
            proj_finish()
        if do_mlp:
            op_ref[...] = jnp.where(is_ctx, out, op_ref[...])
            os_ref[...] = jnp.where(is_ctx, os_ref[...], out)
        return hb

    def projection(hb, t):
        return _projection_stages(hb, wb, t, t < N_CTX_TILES, zb_ref, cache_refs, rope_refs)

    @pl.when(s >= TRUNK_CAST_STEPS)
    def _():
        t = s - TRUNK_CAST_STEPS
        if not skew:
            hb = tile_step(t, [], None)
            if do_proj:
                stages, finish = projection(lambda: hb, t)
                for stage in stages:
                    stage()
                finish()
        else:
            u = jnp.maximum(t - 1, 0)

            @pl.when(t < N_ROW_TILES)
            def _():
                stages, finish = projection(lambda: hprev[...], u)
                hprev[...] = tile_step(t, stages, finish)

            @pl.when(t >= N_ROW_TILES)
            def _():
                stages, finish = projection(lambda: hprev[...], u)
                for stage in stages:
                    stage()
                finish()


def _trunk(xp, xs, ropes, mlp=None, proj=None):
    do_mlp, do_proj = mlp is not None, proj is not None
    last = do_mlp and not do_proj
    skew = do_mlp and do_proj
    clamp_tile = lambda s: jnp.clip(s - TRUNK_CAST_STEPS, 0, N_ROW_TILES - 1)
    step = lambda f: (lambda s: f(clamp_tile(s)))
    step_proj = (lambda f: (lambda s: f(clamp_tile(s - 1)))) if skew else step
    const = lambda shape: pl.BlockSpec(shape, lambda s: (0,) * len(shape), pipeline_mode=pl.Buffered(1))
    chunk = lambda layer, rows, cols: pl.BlockSpec(
        (None, rows // TRUNK_CAST_STEPS, cols), lambda s: (layer, jnp.minimum(s, TRUNK_CAST_STEPS - 1), 0))
    mod_spec = const((MOD_ROWS, 6 * D_MODEL))
    x_specs = [pl.BlockSpec((ROW_TILE, D_MODEL), step(lambda t: (_ctx_tile(t), 0))),
               pl.BlockSpec((ROW_TILE, D_MODEL), step(lambda t: (_lat_tile(t), 0)))]
    args, in_specs = [xp, xs], list(x_specs)
    out_specs, out_shape, scratch, aliases = [], [], [], {}
    n_alias = 0
    if do_mlp:
        layer, mods_l, cat_p, cat_s, g2, final_g, wo, w1, w2 = mlp
        args += [cat_p, cat_s, mods_l, g2.reshape(1, D_MODEL), final_g.reshape(1, D_MODEL), wo, w1, w2]
        in_specs += [pl.BlockSpec((ROW_TILE, MIX_WIDTH), step(lambda t: (_ctx_tile(t), 0))),
                     pl.BlockSpec((ROW_TILE, MIX_WIDTH), step(lambda t: (_lat_tile(t), 0))), mod_spec,
                     const((1, D_MODEL)), const((1, D_MODEL)),
                     chunk(layer, MIX_WIDTH, D_MODEL), chunk(layer, D_MODEL, D_FF), chunk(layer, D_FF, D_MODEL)]
    if do_proj:
        p_layer, mods_p, g1, w_in, caches = proj
        args += [mods_p, g1.reshape(1, D_MODEL), w_in, *ropes]
        in_specs += [mod_spec, const((1, D_MODEL)), chunk(p_layer, D_MODEL, IN_WIDTH)]
        in_specs += [const((DEC_SEQ + ROW_TILE, 128))] * 4
        if caches is not None:
            n_alias = 6
            n_out_before = 2 if do_mlp else 0
            aliases = {len(args) + i: n_out_before + 1 + i for i in range(6)}
            args += list(caches)
            in_specs += [pl.BlockSpec(memory_space=pl.ANY)] * 6
    if do_mlp:
        out_specs += x_specs
        out_shape += [jax.ShapeDtypeStruct((N_CTX_TOK, D_MODEL), F32),
                      jax.ShapeDtypeStruct((N_LAT_TOK, D_MODEL), F32)]
        scratch += [pltpu.VMEM((MIX_WIDTH, D_MODEL), BF), pltpu.VMEM((D_MODEL, D_FF), BF),
                    pltpu.VMEM((D_FF, D_MODEL), BF)]
    if do_proj:
        out_specs += [pl.BlockSpec((ROW_TILE, ZB_WIDTH), step_proj(lambda t: (t, 0)))] + [
            pl.BlockSpec((None, None, SEQ, width), step_proj(lambda t: (_ctx_tile(t), p_layer, 0, 0)))
            for _, width in CACHE_COLS]
        out_shape += [jax.ShapeDtypeStruct((N_TOK, ZB_WIDTH), BF)] + [
            jax.ShapeDtypeStruct((BATCH, DEPTH, SEQ, width), F32) for _, width in CACHE_COLS]
        scratch += [pltpu.VMEM((D_MODEL, IN_WIDTH), BF)]
    if skew:
        scratch += [pltpu.VMEM((ROW_TILE, D_MODEL), BF)]
    outs = pl.pallas_call(
        functools.partial(_trunk_kernel, do_mlp, do_proj, last, n_alias),
        grid=(TRUNK_CAST_STEPS + N_ROW_TILES + (1 if skew else 0),),
        in_specs=in_specs,
        out_specs=out_specs,
        out_shape=out_shape,
        scratch_shapes=scratch,
        input_output_aliases=aliases,
        compiler_params=pltpu.CompilerParams(dimension_semantics=("arbitrary",),
                                             vmem_limit_bytes=TRUNK_VMEM_LIMIT),
        name="trunk_mlp_proj" if (do_mlp and do_proj) else ("trunk_mlp" if do_mlp else "trunk_proj"),
    )(*args)
    outs = list(outs)
    if do_mlp:
        xp, xs = outs[:2]
        outs = outs[2:]
    zb, new_caches = (outs[0], tuple(outs[1:7])) if do_proj else (None, None)
    return xp, xs, zb, new_caches


def _diff_lambda(lamp_ref, lam_init):
    a = jnp.sum(lamp_ref[0:1, :] * lamp_ref[1:2, :], axis=-1, keepdims=True)
    b = jnp.sum(lamp_ref[2:3, :] * lamp_ref[3:4, :], axis=-1, keepdims=True)
    return jnp.exp(a) - jnp.exp(b) + lam_init


def _subln(o, g, lam_init):
    return _rmsnorm(o, g) * (1.0 - lam_init)


def _lane_lo(n):
    return lax.broadcasted_iota(jnp.int32, (n, 128), 1) < HEAD_DIM


def _lane_lo_wide(n, width):
    return lax.broadcasted_iota(jnp.int32, (n, width), 1) % 128 < HEAD_DIM


def _split_pair(x, lo):
    zero = jnp.zeros_like(x)
    return jnp.where(lo, x, zero), jnp.where(lo, zero, x)


def _v_ones_pair(v, lo):
    one = jnp.ones_like(v)
    return jnp.where(lo, v, one), jnp.where(lo, one, v)


def _merge_pair(a, b, lo):
    return jnp.where(lo, a, b), pltpu.roll(jnp.where(lo, b, a), HEAD_DIM, 1)


def _subln_pair(o, g2, lam_init, lo):
    sq = o * o
    ms_a = jnp.sum(jnp.where(lo, sq, 0.0), axis=-1, keepdims=True)
    ms_b = jnp.sum(jnp.where(lo, 0.0, sq), axis=-1, keepdims=True)
    ms = jnp.where(lo, ms_a, ms_b) * (1.0 / HEAD_DIM)
    return o * lax.rsqrt(ms + NORM_EPS) * g2 * (1.0 - lam_init)


def _diff_quarters(q, n):
    quarter = lax.broadcasted_iota(jnp.int32, (n, 128), 1) // DIFF_QK_DIM
    zero = jnp.zeros_like(q)
    return [jnp.where(quarter == i, q, zero) for i in range(4)]


def _fourier(x_bf, cl, sl, cbd, sbd, wf):
    xc = _dot(x_bf, cbd).astype(BF)
    xs = _dot(x_bf, sbd).astype(BF)
    y = _dot(cl, xc) - _dot(sl, xs)
    return _dot(y.astype(BF), wf)


CTX_UNIT_GROUP = 4
CTX_REQ_PER_STEP = 2


def _ctx_mix_kernel(lam_init, with_ada, sink_ref, z_ref, lamp_ref, subg_ref, cl_ref, sl_ref, cbd_ref, sbd_ref,
                    wf_ref, *rest):
    if with_ada:
        cond_ref, wa_ref, ba_ref, o_ref, mods_ref = rest
        _ada_kernel(cond_ref, wa_ref, ba_ref, mods_ref)
    else:
        o_ref, = rest
    for r in range(CTX_REQ_PER_STEP):
        rows = pl.ds(r * SEQ, SEQ)
        _ctx_request(lam_init, sink_ref, z_ref.at[rows], lamp_ref, subg_ref, cl_ref, sl_ref, cbd_ref, sbd_ref,
                     wf_ref, o_ref.at[rows])


def _ctx_request(lam_init, sink_ref, z_ref, lamp_ref, subg_ref, cl_ref, sl_ref, cbd_ref, sbd_ref, wf_ref, o_ref):
    lo = _lane_lo(SEQ)

    units = []
    for j in range(2):
        c = 128 * j
        qs = _split_pair(z_ref[:, C_NA_Q + c:C_NA_Q + c + 128], lo)
        vs = _v_ones_pair(z_ref[:, C_NA_V + c:C_NA_V + c + 128], lo)
        units += [(qs[i], (C_NA_K + c), vs[i], None) for i in range(2)]
    for j in range(2):
        c = 128 * j
        q4 = _diff_quarters(z_ref[:, C_DQ + c:C_DQ + c + 128], SEQ)
        vs = _v_ones_pair(z_ref[:, C_DV + c:C_DV + c + 128], lo)
        units += [(q4[i], (C_DK + c), vs[i // 2], None) for i in (0, 2, 1, 3)]
    for g in range(2):
        c = 128 * g
        qs = _split_pair(z_ref[:, C_SQ + c:C_SQ + c + 128], lo)
        vs = _v_ones_pair(z_ref[:, ZB_SV + c:ZB_SV + c + 128], lo)
        units += [(qs[i], (ZB_SK + c), vs[i], sink_ref[2 * g + i]) for i in range(2)]

    res, extras = [], []
    for u0 in range(0, len(units), CTX_UNIT_GROUP):
        group = units[u0:u0 + CTX_UNIT_GROUP]
        scores = [_dot_nt(q, z_ref[:, kc:kc + 128]) for q, kc, _, _ in group]
        weights = []
        for s, (_, _, _, sink) in zip(scores, group):
            m = _rowmax(s)
            if sink is not None:
                m = jnp.maximum(m, sink)
                extras.append(jnp.exp(sink - m))
            weights.append(jnp.exp(s - m).astype(BF))
        res += [_dot(e, v) for e, (_, _, v, _) in zip(weights, group)]

    for j in range(2):
        num, den = _merge_pair(res[2 * j], res[2 * j + 1], lo)
        o_ref[:, 128 * j:128 * j + 128] = (num / den).astype(BF)

    lam = _diff_lambda(lamp_ref, lam_init)
    for j in range(2):
        n1, d1 = _merge_pair(res[4 + 4 * j], res[5 + 4 * j], lo)
        n2, d2 = _merge_pair(res[6 + 4 * j], res[7 + 4 * j], lo)
        o = n1 / d1 - lam * (n2 / d2)
        o_ref[:, 256 + 128 * j:384 + 128 * j] = _subln_pair(o, subg_ref[...], lam_init, lo).astype(BF)

    o_c = _fourier(z_ref[:, C_FC:C_FC + 256], cl_ref[...], sl_ref[...], cbd_ref[...], sbd_ref[...], wf_ref[...])
    o_ref[:, 512:768] = o_c.astype(BF)

    for g in range(2):
        num, den = _merge_pair(res[12 + 2 * g], res[13 + 2 * g], lo)
        den = den + jnp.where(lo, extras[2 * g], extras[2 * g + 1])
        o_ref[:, 768 + 128 * g:896 + 128 * g] = (num / den).astype(BF)


def _ctx_mix(zb, sink, lamp, subg, cl, sl, cbd, sbd, wf, lam_init, ada=None):
    full = lambda shape: pl.BlockSpec(shape, lambda b, s: (0,) * len(shape))
    n_steps = BATCH // CTX_REQ_PER_STEP
    in_specs = [
        pl.BlockSpec((CTX_REQ_PER_STEP * SEQ, ZB_WIDTH), lambda b, s: (b, 0)),
        full((4, DIFF_QK_DIM)),
        full((1, 2 * HEAD_DIM)),
        full((SEQ, SEQ)), full((SEQ, SEQ)),
        full((256, 256)), full((256, 256)), full((256, 256)),
    ]
    out_specs = [pl.BlockSpec((CTX_REQ_PER_STEP * SEQ, MIX_WIDTH), lambda b, s: (b, 0))]
    out_shape = [jax.ShapeDtypeStruct((N_CTX_TOK, MIX_WIDTH), BF)]
    args = [sink, zb, lamp, subg, cl, sl, cbd, sbd, wf]
    if ada is not None:
        cond, w_ada, b_ada3, layer = ada
        ada_in, ada_out = _ada_specs(layer, n_steps, lambda b, s: b)
        in_specs += ada_in
        out_specs.append(ada_out)
        out_shape.append(jax.ShapeDtypeStruct((MOD_ROWS, 6 * D_MODEL), F32))
        args += [cond, w_ada, b_ada3]
    grid_spec = pltpu.PrefetchScalarGridSpec(
        num_scalar_prefetch=1, grid=(n_steps,), in_specs=in_specs, out_specs=out_specs)
    outs = pl.pallas_call(
        functools.partial(_ctx_mix_kernel, lam_init, ada is not None),
        grid_spec=grid_spec,
        out_shape=out_shape,
        compiler_params=_params(),
        name="ctx_mix",
    )(*args)
    return outs[0], (outs[1] if ada is not None else None)


def _cache_v_ones(cv_ref, cva, cvb):
    lo = _lane_lo_wide(PAST_LEN, 256)
    cv = cv_ref[...]
    cva[...] = jnp.where(lo, cv, 1.0).astype(BF)
    cvb[...] = jnp.where(lo, 1.0, cv).astype(BF)


def _lat_na_part(z_ref, ck_ref, cv_ref, nb_ref, o_ref, ckb, cva, cvb):
    ckb[...] = ck_ref[...].astype(BF)
    _cache_v_ones(cv_ref, cva, cvb)
    n_q = NA_QROWS * GRID_W
    n_loc = NA_WIN_ROWS * GRID_W
    lo_q = _lane_lo(n_q)
    lo_w = _lane_lo(n_loc)

    def body(p, carry):
        w_row = jnp.clip(NA_QROWS * p - NA_KH // 2, 0, GRID_ROWS - NA_WIN_ROWS)
        tbl = jnp.minimum(p, 2) + jnp.maximum(p - 5, 0)
        q0 = pl.multiple_of(p * n_q, n_q)
        k0 = pl.multiple_of(w_row * GRID_W, NA_QROWS * GRID_W)
        scores = []
        for h in range(4):
            c = 128 * (h // 2)
            q = _split_pair(z_ref[pl.ds(q0, n_q), C_NA_Q + c:C_NA_Q + c + 128], lo_q)[h % 2]
            s_loc = _dot_nt(q, z_ref[pl.ds(k0, n_loc), C_NA_K + c:C_NA_K + c + 128]) + nb_ref[h, tbl]
            s_ctx = _dot_nt(q, ckb[:, c:c + 128])
            scores.append((s_loc, s_ctx))
        weights = []
        for s_loc, s_ctx in scores:
            m = _rowmax(s_loc, s_ctx)
            weights.append((jnp.exp(s_loc - m).astype(BF), jnp.exp(s_ctx - m).astype(BF)))
        res = []
        for h, (e_loc, e_ctx) in enumerate(weights):
            c = 128 * (h // 2)
            v = _v_ones_pair(z_ref[pl.ds(k0, n_loc), C_NA_V + c:C_NA_V + c + 128], lo_w)[h % 2]
            cv = (cva, cvb)[h % 2][:, c:c + 128]
            res.append(_dot(e_loc, v) + _dot(e_ctx, cv))
        for j in range(2):
            num, den = _merge_pair(res[2 * j], res[2 * j + 1], lo_q)
            o_ref[pl.ds(q0, n_q), 128 * j:128 * j + 128] = (num / den).astype(BF)
        return carry

    lax.fori_loop(0, GRID_ROWS // NA_QROWS, body, 0)


DIFF_QBLK = 512


def _lat_diff_part(lam_init, z_ref, ck_ref, cv_ref, lamp_ref, subg_ref, o_ref, ckb, cva, cvb, vla, vlb):
    ckb[...] = ck_ref[...].astype(BF)
    _cache_v_ones(cv_ref, cva, cvb)
    lo_v = _lane_lo_wide(DEC_SEQ, 256)
    v_loc = z_ref[:, C_DV:C_DV + 256]
    one = jnp.ones_like(v_loc)
    vla[...] = jnp.where(lo_v, v_loc, one)
    vlb[...] = jnp.where(lo_v, one, v_loc)
    lam = _diff_lambda(lamp_ref, lam_init)
    qblk = DIFF_QBLK
    lo_q = _lane_lo(qblk)

    def body(i, carry):
        q0 = pl.multiple_of(i * qblk, qblk)
        for j in range(2):
            c = 128 * j
            q4 = _diff_quarters(z_ref[pl.ds(q0, qblk), C_DQ + c:C_DQ + c + 128], qblk)
            scores = [(_dot_nt(q, z_ref[:, C_DK + c:C_DK + c + 128]), _dot_nt(q, ckb[:, c:c + 128])) for q in q4]
            weights = []
            for s_loc, s_ctx in scores:
                m = _rowmax(s_loc, s_ctx)
                weights.append((jnp.exp(s_loc - m).astype(BF), jnp.exp(s_ctx - m).astype(BF)))
            res = []
            for t, (e_loc, e_ctx) in enumerate(weights):
                v_loc, v_ctx = ((vla, cva), (vlb, cvb))[t // 2]
                res.append(_dot(e_loc, v_loc[:, c:c + 128]) + _dot(e_ctx, v_ctx[:, c:c + 128]))
            n1, d1 = _merge_pair(res[0], res[2], lo_q)
            n2, d2 = _merge_pair(res[1], res[3], lo_q)
            o = n1 / d1 - lam * (n2 / d2)
            o_ref[pl.ds(q0, qblk), 256 + c:256 + c + 128] = (
                _subln_pair(o, subg_ref[...], lam_init, lo_q).astype(BF))
        return carry

    lax.fori_loop(0, DEC_SEQ // qblk, body, 0)


def _lat_swa_part(sink_ref, z_ref, ck_ref, cv_ref, o_ref, ckd, cva, cvb):
    W = SWA_WINDOW
    n_win = 3 * W
    lo_c = _lane_lo(PAST_LEN)
    ck = ck_ref[...]
    ck_r = pltpu.roll(ck, HEAD_DIM, 1)
    ckd[:, 0:128] = jnp.where(lo_c, ck, ck_r).astype(BF)
    ckd[:, 128:256] = jnp.where(lo_c, ck_r, ck).astype(BF)
    cv = cv_ref[...]
    cv_r = pltpu.roll(cv, HEAD_DIM, 1)
    cva[:, 0:128] = jnp.where(lo_c, cv, 1.0).astype(BF)
    cvb[:, 0:128] = jnp.where(lo_c, 1.0, cv_r).astype(BF)
    cva[:, 128:256] = jnp.where(lo_c, cv_r, 1.0).astype(BF)
    cvb[:, 128:256] = jnp.where(lo_c, 1.0, cv).astype(BF)
    lo_q = _lane_lo(W)
    lo_w = _lane_lo(n_win)

    def body(n, carry):
        q0 = pl.multiple_of(n * W, W)
        w0 = pl.multiple_of(jnp.clip((n - 1) * W, 0, DEC_SEQ - n_win), W)
        qpos = q0 + lax.broadcasted_iota(jnp.int32, (W, n_win), 0)
        kpos = w0 + lax.broadcasted_iota(jnp.int32, (W, n_win), 1)
        valid = jnp.abs(qpos - kpos) <= W
        scores = []
        for h in range(4):
            c = 128 * (h // 2)
            q = _split_pair(z_ref[pl.ds(q0, W), C_SQ + c:C_SQ + c + 128], lo_q)[h % 2]
            s_loc = jnp.where(valid, _dot_nt(q, z_ref[pl.ds(w0, n_win), ZB_SK + c:ZB_SK + c + 128]), NEG_INF)
            scores.append((s_loc, _dot_nt(q, ckd[:, c:c + 128])))
        weights, extras = [], []
        for h, (s_loc, s_ctx) in enumerate(scores):
            sink = sink_ref[h]
            m = jnp.maximum(_rowmax(s_loc, s_ctx), sink)
            weights.append((jnp.exp(s_loc - m).astype(BF), jnp.exp(s_ctx - m).astype(BF)))
            extras.append(jnp.exp(sink - m))
        res = []
        for h, (e_loc, e_ctx) in enumerate(weights):
            c = 128 * (h // 2)
            v = _v_ones_pair(z_ref[pl.ds(w0, n_win), ZB_SV + c:ZB_SV + c + 128], lo_w)[h % 2]
            res.append(_dot(e_loc, v) + _dot(e_ctx, (cva, cvb)[h % 2][:, c:c + 128]))
        for g in range(2):
            num, den = _merge_pair(res[2 * g], res[2 * g + 1], lo_q)
            den = den + jnp.where(lo_q, extras[2 * g], extras[2 * g + 1])
            o_ref[pl.ds(q0, W), 768 + 128 * g:768 + 128 * g + 128] = (num / den).astype(BF)
        return carry

    lax.fori_loop(0, DEC_SEQ // W, body, 0)


def _lat_mix_kernel(lam_init, sink_ref, z_ref, nak_ref, nav_ref, dfk_ref, dfv_ref, swk_ref, swv_ref, nb_ref,
                    lamp_ref, subg_ref, cl_ref, sl_ref, cbd_ref, sbd_ref, wf_ref, o_ref, ck, cva, cvb, vla, vlb):
    _lat_na_part(z_ref, nak_ref, nav_ref, nb_ref, o_ref, ck, cva, cvb)
    _lat_diff_part(lam_init, z_ref, dfk_ref, dfv_ref, lamp_ref, subg_ref, o_ref, ck, cva, cvb, vla, vlb)
    o_c = _fourier(z_ref[:, C_FC:C_FC + 256], cl_ref[...], sl_ref[...], cbd_ref[...], sbd_ref[...], wf_ref[...])
    o_ref[:, 512:768] = o_c.astype(BF)
    _lat_swa_part(sink_ref, z_ref, swk_ref, swv_ref, o_ref, ck, cva, cvb)


def _lat_mix(zb, caches_in, nb, sink, lamp, subg, cl, sl, cbd, sbd, wf, lam_init, layer):
    full = lambda shape: pl.BlockSpec(shape, lambda b, s: (0,) * len(shape), pipeline_mode=pl.Buffered(1))
    cache = lambda width: pl.BlockSpec((None, None, PAST_LEN, width), lambda b, s: (b, layer, 0, 0))
    n_q = NA_QROWS * GRID_W
    n_loc = NA_WIN_ROWS * GRID_W
    grid_spec = pltpu.PrefetchScalarGridSpec(
        num_scalar_prefetch=1,
        grid=(DEC_BATCH,),
        in_specs=[
            pl.BlockSpec((DEC_SEQ, ZB_WIDTH), lambda b, s: (LAT_BLOCK0 + b, 0)),
            cache(256), cache(256), cache(256), cache(256), cache(128), cache(128),
            pl.BlockSpec((None, 4, len(NA_TABLE_PAIRS), n_q, n_loc), lambda b, s: (layer, 0, 0, 0, 0),
                         pipeline_mode=pl.Buffered(1)),
            full((4, DIFF_QK_DIM)), full((1, 2 * HEAD_DIM)),
            full((DEC_SEQ, DEC_SEQ)), full((DEC_SEQ, DEC_SEQ)),
            full((256, 256)), full((256, 256)), full((256, 256)),
        ],
        out_specs=pl.BlockSpec((DEC_SEQ, MIX_WIDTH), lambda b, s: (b, 0)),
        scratch_shapes=[pltpu.VMEM((PAST_LEN, 256), BF)] * 3 + [pltpu.VMEM((DEC_SEQ, 256), BF)] * 2,
    )
    return pl.pallas_call(
        functools.partial(_lat_mix_kernel, lam_init),
        grid_spec=grid_spec,
        out_shape=jax.ShapeDtypeStruct((N_LAT_TOK, MIX_WIDTH), BF),
        compiler_params=_params(),
        name="lat_mix",
    )(sink, zb, *caches_in, nb, lamp, subg, cl, sl, cbd, sbd, wf)


def _dft_tables(n):
    j = np.arange(n)
    ang = 2.0 * np.pi * ((j[:, None] * j[None, :]) % n) / n
    return np.cos(ang) / np.sqrt(n), np.sin(ang) / np.sqrt(n)


def _block_diag4(m):
    out = np.zeros((256, 256), m.dtype)
    for g in range(4):
        out[64 * g:64 * g + 64, 64 * g:64 * g + 64] = m
    return out


def _rope_tables(n_axis_dims):
    half = n_axis_dims // 2
    inv = ROPE_BASE ** (-np.arange(half, dtype=np.float64) / half)
    t = np.arange(DEC_SEQ)
    lane = np.arange(128)
    w = lane % n_axis_dims
    is_col = (lane // n_axis_dims) % 2 == 1
    pos = np.where(is_col[None, :], (t % GRID_W)[:, None], (t // GRID_W)[:, None]).astype(np.float64)
    ang = pos * inv[w % half][None, :]
    sign = np.where(w < half, -1.0, 1.0)[None, :]
    cos = np.concatenate([np.cos(ang), np.ones((ROW_TILE, 128))], axis=0)
    sin = np.concatenate([np.sin(ang) * sign, np.zeros((ROW_TILE, 128))], axis=0)
    return jnp.asarray(cos, F32), jnp.asarray(sin, F32)


def _na_bias_tables(rpb):
    lead = rpb.shape[:2]
    col = np.arange(GRID_W)
    cstart = np.clip(col - NA_KW // 2, 0, GRID_W - NA_KW)
    cmask = (col[None, :] >= cstart[:, None]) & (col[None, :] < cstart[:, None] + NA_KW)
    pad = GRID_W - NA_KW
    rp = jnp.pad(rpb, ((0, 0), (0, 0), (0, 0), (pad, pad)))
    t = jnp.stack([rp[..., pad + NA_KW - 1 - c:pad + NA_KW - 1 - c + GRID_W] for c in range(GRID_W)], axis=-2)
    t = jnp.where(cmask, t, NEG_INF)
    t = t.transpose(0, 1, 3, 2, 4).reshape(lead + (GRID_W, (2 * NA_KH - 1) * GRID_W))
    tables = []
    for p in NA_TABLE_PAIRS:
        w_row = int(np.clip(NA_QROWS * p - NA_KH // 2, 0, GRID_ROWS - NA_WIN_ROWS))
        rows = []
        for r in (NA_QROWS * p, NA_QROWS * p + 1):
            rs = int(np.clip(r - NA_KH // 2, 0, GRID_ROWS - NA_KH))
            j_lo, j_hi = rs - w_row, rs - w_row + NA_KH
            d_lo = rs - r + NA_KH - 1
            body = t[..., d_lo * GRID_W:(d_lo + NA_KH) * GRID_W]
            rows.append(jnp.pad(body, ((0, 0),) * 3 + ((j_lo * GRID_W, (NA_WIN_ROWS - j_hi) * GRID_W),),
                                constant_values=NEG_INF))
        tables.append(jnp.concatenate(rows, axis=-2))
    return jnp.stack(tables, axis=2)


def kernel(x_prompt, x_sample, cache_na_k, cache_na_v, cache_diff_k, cache_diff_v, cache_swa_k, cache_swa_v, c, c_ctx, w_ada, b_ada, norm1_g, norm2_g, w_in, na_rpb, diff_lq1, diff_lk1, diff_lq2, diff_lk2, diff_subln_g, w_fourier, swa_sink, w_out, w_mlp1, w_mlp2, final_g):
    xp = x_prompt.reshape(N_CTX_TOK, D_MODEL)
    xs = x_sample.reshape(N_LAT_TOK, D_MODEL)
    cond = jnp.zeros((MOD_ROWS, D_MODEL), F32).at[0].set(c_ctx).at[1:1 + DEC_BATCH].set(c)
    b_ada3 = b_ada.reshape(DEPTH, 1, 6 * D_MODEL)
    mods = _ada(cond, w_ada, b_ada3, 0)

    cl_p, sl_p = _dft_tables(SEQ)
    cl_s, sl_s = _dft_tables(DEC_SEQ)
    c64, s64 = _dft_tables(64)
    cl_p, sl_p, cl_s, sl_s, cbd, sbd = (
        jnp.asarray(a, F32).astype(BF) for a in (cl_p, sl_p, cl_s, sl_s, _block_diag4(c64), _block_diag4(s64)))
    ropes = _rope_tables(16) + _rope_tables(32)
    nb = _na_bias_tables(na_rpb)
    wf_bf = w_fourier.astype(BF)

    ck_na = cache_na_k.reshape(DEC_BATCH, DEPTH, PAST_LEN, 256)
    cv_na = cache_na_v.reshape(DEC_BATCH, DEPTH, PAST_LEN, 256)
    ck_df = cache_diff_k.reshape(DEC_BATCH, DEPTH, PAST_LEN, 256)
    cv_df = cache_diff_v.reshape(DEC_BATCH, DEPTH, PAST_LEN, 256)
    ck_sw = cache_swa_k.reshape(DEC_BATCH, DEPTH, PAST_LEN, 128)
    cv_sw = cache_swa_v.reshape(DEC_BATCH, DEPTH, PAST_LEN, 128)

    _, _, zb, caches = _trunk(xp, xs, ropes, proj=(0, mods, norm1_g[0], w_in, None))
    for l in range(DEPTH):
        lam_init = 0.8 - 0.6 * math.exp(-0.3 * l)
        lamp = jnp.stack([diff_lq1[l], diff_lk1[l], diff_lq2[l], diff_lk2[l]], axis=0)
        subg = jnp.tile(diff_subln_g[l].reshape(1, HEAD_DIM), (1, 2))
        has_next = l + 1 < DEPTH

        cat_p, mods_next = _ctx_mix(zb, swa_sink[l], lamp, subg, cl_p, sl_p, cbd, sbd, wf_bf[l], lam_init,
                                    ada=(cond, w_ada, b_ada3, l + 1) if has_next else None)
        cat_s = _lat_mix(zb, (ck_na, cv_na, ck_df, cv_df, ck_sw, cv_sw), nb, swa_sink[l], lamp, subg,
                         cl_s, sl_s, cbd, sbd, wf_bf[l], lam_init, l)
        nxt = (l + 1, mods_next, norm1_g[l + 1], w_in, caches) if has_next else None
        xp, xs, zb, new_caches = _trunk(
            xp, xs, ropes, mlp=(l, mods, cat_p, cat_s, norm2_g[l], final_g, w_out, w_mlp1, w_mlp2), proj=nxt)
        caches = new_caches if has_next else caches
        mods = mods_next

    y_prompt = xp.reshape(BATCH, SEQ, D_MODEL)
    y_sample = xs.reshape(DEC_BATCH, DEC_SEQ, D_MODEL)
    new = [a.reshape(BATCH, DEPTH, SEQ, a.shape[-1] // HEAD_DIM, HEAD_DIM) for a in caches]
    return (y_prompt, y_sample) + tuple(new)
```

```python
import functools
import math

import numpy as np
import jax
import jax.numpy as jnp
from jax import lax
from jax.experimental import pallas as pl
from jax.experimental.pallas import tpu as pltpu

D_MODEL = 1024
BATCH = 16
SEQ = 256
DEPTH = 4
DEC_BATCH = 2
DEC_SEQ = 1024
PAST_LEN = 512
GRID_W = 64
GRID_ROWS = DEC_SEQ // GRID_W
HEAD_DIM = 64
NA_KH = 8
NA_KW = 16
DIFF_QK_DIM = 32
SWA_WINDOW = 128
D_FF = 4 * D_MODEL
ROPE_BASE = 10000.0
NORM_EPS = 1e-6
NEG_INF = -1e30
IN_WIDTH = 2304
MIX_WIDTH = 1024

N_CTX_TOK = BATCH * SEQ
N_LAT_TOK = DEC_BATCH * DEC_SEQ
N_TOK = N_CTX_TOK + N_LAT_TOK
ROW_TILE = 256
N_ROW_TILES = N_TOK // ROW_TILE
N_CTX_TILES = N_CTX_TOK // ROW_TILE
N_LAT_TILES = N_LAT_TOK // ROW_TILE
LAT_TILES_PER_REQ = DEC_SEQ // ROW_TILE
LAT_BLOCK0 = N_CTX_TOK // DEC_SEQ
MOD_ROWS = 8

C_NA_Q, C_NA_K, C_NA_V = 0, 256, 512
C_DQ, C_DK, C_DV = 768, 1024, 1280
C_FC = 1536
C_SQ, C_SK, C_SV = 1792, 2048, 2176
CACHE_COLS = ((C_NA_K, 256), (C_NA_V, 256), (C_DK, 256), (C_DV, 256), (C_SK, 128), (C_SV, 128))
ZB_SK, ZB_SV = 2048, 2304
ZB_WIDTH = 2560

NA_QROWS = 2
NA_WIN_ROWS = 10
NA_TABLE_OF_PAIR = (0, 1, 2, 2, 2, 2, 3, 4)
NA_TABLE_PAIRS = (0, 1, 2, 6, 7)

BF = jnp.bfloat16
F32 = jnp.float32
VMEM_LIMIT = 56 * 1024 * 1024


def _dot(a, b):
    return jnp.dot(a, b, preferred_element_type=F32)


def _dot_nt(a, b):
    return lax.dot_general(a, b, (((1,), (1,)), ((), ())), preferred_element_type=F32)


def _rmsnorm(x, g):
    ms = jnp.mean(x * x, axis=-1, keepdims=True)
    return x * lax.rsqrt(ms + NORM_EPS) * g


def _params(n_grid=1):
    return pltpu.CompilerParams(dimension_semantics=("arbitrary",) * n_grid, vmem_limit_bytes=VMEM_LIMIT)


def _rowmax(*parts):
    m = jnp.max(parts[0], axis=-1, keepdims=True)
    for p in parts[1:]:
        m = jnp.maximum(m, jnp.max(p, axis=-1, keepdims=True))
    return m


def _rowsum(*parts):
    s = jnp.sum(parts[0], axis=-1, keepdims=True)
    for p in parts[1:]:
        s = s + jnp.sum(p, axis=-1, keepdims=True)
    return s


def _ada_kernel(cond_ref, w_ref, b_ref, o_ref):
    cnd = cond_ref[...]
    s = cnd / (1.0 + jnp.exp(-cnd))
    o_ref[...] = _dot(s.astype(BF), w_ref[...].astype(BF)) + b_ref[...]


def _ada_specs(layer, n_steps, index_of_step):
    tn = 6 * D_MODEL // n_steps
    col = lambda *g: index_of_step(*g)
    in_specs = [
        pl.BlockSpec((MOD_ROWS, D_MODEL), lambda *g: (0, 0)),
        pl.BlockSpec((None, D_MODEL, tn), lambda *g: (layer, 0, col(*g))),
        pl.BlockSpec((None, 1, tn), lambda *g: (layer, 0, col(*g))),
    ]
    return in_specs, pl.BlockSpec((MOD_ROWS, tn), lambda *g: (0, col(*g)))


def _ada(cond, w_ada, b_ada3, layer):
    in_specs, out_spec = _ada_specs(layer, 6, lambda j: j)
    return pl.pallas_call(
        _ada_kernel,
        grid=(6,),
        in_specs=in_specs,
        out_specs=out_spec,
        out_shape=jax.ShapeDtypeStruct((MOD_ROWS, 6 * D_MODEL), F32),
        compiler_params=_params(),
        name="ada",
    )(cond, w_ada, b_ada3)


def _mod_row(t):
    return jnp.where(t < N_CTX_TILES, 0, 1 + (t - N_CTX_TILES) // LAT_TILES_PER_REQ)


def _ctx_tile(t):
    return jnp.minimum(t, N_CTX_TILES - 1)


def _lat_tile(t):
    return jnp.maximum(t - N_CTX_TILES, 0)


TRUNK_CAST_STEPS = 8
FF_CHUNK = 1024
TRUNK_VMEM_LIMIT = 60 * 1024 * 1024


def _rope(x, cos, sin_signed, shift, first):
    partner = jnp.where(first, pltpu.roll(x, 128 - shift, 1), pltpu.roll(x, shift, 1))
    return x * cos + partner * sin_signed


def _projection_stages(hb, wb, t, is_ctx, zb_ref, cache_refs, rope_refs):
    cosd_ref, sind_ref, coss_ref, sins_ref = rope_refs
    dscale = DIFF_QK_DIM ** -0.5
    p0 = pl.multiple_of(
        jnp.where(is_ctx, DEC_SEQ, ((t - N_CTX_TILES) % LAT_TILES_PER_REQ) * ROW_TILE), ROW_TILE)
    lane = lax.broadcasted_iota(jnp.int32, (ROW_TILE, 128), 1)
    first_d = (lane % 16) < 8
    first_s = (lane % 32) < 16
    lo = lane < HEAD_DIM
    cosd, sind = cosd_ref[pl.ds(p0, ROW_TILE), :], sind_ref[pl.ds(p0, ROW_TILE), :]
    coss, sins = coss_ref[pl.ds(p0, ROW_TILE), :], sins_ref[pl.ds(p0, ROW_TILE), :]

    def cols(c0, width):
        return _dot(hb(), wb[:, c0:c0 + width])

    def put(c0, width, val):
        zb_ref[:, c0:c0 + width] = val.astype(BF)

    def put_dup(c0, val):
        r = pltpu.roll(val, HEAD_DIM, 1)
        put(c0, 128, jnp.where(lo, val, r))
        put(c0 + 128, 128, jnp.where(lo, r, val))

    kept = {}

    def na_q():
        put(C_NA_Q, 256, cols(C_NA_Q, 256) * 0.125)

    def na_kv():
        kept["nakv"] = cols(C_NA_K, 512)
        put(C_NA_K, 512, kept["nakv"])

    def diff_q():
        z = cols(C_DQ, 256)
        for j in range(2):
            put(C_DQ + 128 * j, 128, _rope(z[:, 128 * j:128 * j + 128], cosd, sind, 8, first_d) * dscale)

    def diff_k():
        kept["dk"] = cols(C_DK, 256)
        for j in range(2):
            put(C_DK + 128 * j, 128, _rope(kept["dk"][:, 128 * j:128 * j + 128], cosd, sind, 8, first_d))

    def diff_v_fourier():
        kept["dvfc"] = cols(C_DV, 512)
        put(C_DV, 512, kept["dvfc"])

    def swa_q():
        z = cols(C_SQ, 256)
        for j in range(2):
            put(C_SQ + 128 * j, 128, _rope(z[:, 128 * j:128 * j + 128], coss, sins, 16, first_s) * 0.125)

    def swa_kv():
        kept["skv"] = cols(C_SK, 256)
        put_dup(ZB_SK, _rope(kept["skv"][:, 0:128], coss, sins, 16, first_s))
        put_dup(ZB_SV, kept["skv"][:, 128:256])

    def finish():
        new = (kept["nakv"][:, 0:256], kept["nakv"][:, 256:512], kept["dk"], kept["dvfc"][:, 0:256],
               kept["skv"][:, 0:128], kept["skv"][:, 128:256])
        for ref, val in zip(cache_refs, new):
            ref[...] = jnp.where(is_ctx, val, ref[...])

    return [na_q, na_kv, diff_q, diff_k, diff_v_fourier, swa_q, swa_kv], finish


def _trunk_kernel(do_mlp, do_proj, last, n_alias, *refs):
    xp_ref, xs_ref = refs[:2]
    pos = 2
    if do_mlp:
        catp_ref, cats_ref, mm_ref, g2_ref, fg_ref, wo_ref, w1_ref, w2_ref = refs[pos:pos + 8]
        pos += 8
    if do_proj:
        pm_ref, g1_ref, win_ref = refs[pos:pos + 3]
        rope_refs = refs[pos + 3:pos + 7]
        pos += 7 + n_alias
    if do_mlp:
        op_ref, os_ref = refs[pos:pos + 2]
        pos += 2
    if do_proj:
        zb_ref = refs[pos]
        cache_refs = refs[pos + 1:pos + 7]
        pos += 7
    if do_mlp:
        wob, w1b, w2b = refs[pos:pos + 3]
        pos += 3
    if do_proj:
        wb = refs[pos]
        pos += 1
    skew = do_mlp and do_proj
    if skew:
        hprev = refs[pos]
    s = pl.program_id(0)

    @pl.when(s < TRUNK_CAST_STEPS)
    def _():
        def cast(dst, src):
            rows = src.shape[0]
            dst[pl.ds(pl.multiple_of(s * rows, rows), rows), :] = src[...].astype(BF)

        if do_mlp:
            cast(wob, wo_ref)
            cast(w1b, w1_ref)
            cast(w2b, w2_ref)
        if do_proj:
            cast(wb, win_ref)
        if skew:
            @pl.when(s == 0)
            def _():
                hprev[...] = jnp.zeros_like(hprev)

    def tile_step(t, proj_stages, proj_finish):
        pending = list(proj_stages)

        def issue(n):
            for _ in range(min(n, len(pending))):
                pending.pop(0)()

        is_ctx = t < N_CTX_TILES
        row = _mod_row(t)
        x = jnp.where(is_ctx, xp_ref[...], xs_ref[...])
        if do_mlp:
            issue(1)
            gt1 = mm_ref[pl.ds(row, 1), 2 * D_MODEL:3 * D_MODEL]
            sh2 = mm_ref[pl.ds(row, 1), 3 * D_MODEL:4 * D_MODEL]
            sc2 = mm_ref[pl.ds(row, 1), 4 * D_MODEL:5 * D_MODEL]
            gt2 = mm_ref[pl.ds(row, 1), 5 * D_MODEL:6 * D_MODEL]
            cat = jnp.where(is_ctx, catp_ref[...], cats_ref[...])
            x = x + gt1 * _dot(cat, wob[...])
            issue(1)
            hh = (_rmsnorm(x, g2_ref[...]) * (1.0 + sc2) + sh2).astype(BF)
            acc = jnp.zeros((ROW_TILE, D_MODEL), F32)
            for c in range(D_FF // FF_CHUNK):
                a = jnp.maximum(_dot(hh, w1b[:, c * FF_CHUNK:(c + 1) * FF_CHUNK]), 0.0)
                acc = acc + _dot((a * a).astype(BF), w2b[c * FF_CHUNK:(c + 1) * FF_CHUNK, :])
                if c < 2:
                    issue(1)
            issue(1)
            x = x + gt2 * acc
            out = _rmsnorm(x, fg_ref[...]) if last else x
        hb = None
        if do_proj:
            issue(1)
            shift = pm_ref[pl.ds(row, 1), 0:D_MODEL]
            scale = pm_ref[pl.ds(row, 1), D_MODEL:2 * D_MODEL]
            hb = (_rmsnorm(x, g1_ref[...]) * (1.0 + scale) + shift).astype(BF)
        issue(len(pending))
        if proj_finish is not None:
            proj_finish()
        if do_mlp:
            op_ref[...] = jnp.where(is_ctx, out, op_ref[...])
            os_ref[...] = jnp.where(is_ctx, os_ref[...], out)
        return hb

    def projection(hb, t):
        return _projection_stages(hb, wb, t, t < N_CTX_TILES, zb_ref, cache_refs, rope_refs)

    @pl.when(s >= TRUNK_CAST_STEPS)
    def _():
        t = s - TRUNK_CAST_STEPS
        if not skew:
            hb = tile_step(t, [], None)
            if do_proj:
                stages, finish = projection(lambda: hb, t)
                for stage in stages:
                    stage()
                finish()
        else:
            u = jnp.maximum(t - 1, 0)

            @pl.when(t < N_ROW_TILES)
            def _():
                stages, finish = projection(lambda: hprev[...], u)
                hprev[...] = tile_step(t, stages, finish)

            @pl.when(t >= N_ROW_TILES)
            def _():
                stages, finish = projection(lambda: hprev[...], u)
                for stage in stages:
                    stage()
                finish()


def _trunk(xp, xs, ropes, mlp=None, proj=None):
    do_mlp, do_proj = mlp is not None, proj is not None
    last = do_mlp and not do_proj
    skew = do_mlp and do_proj
    clamp_tile = lambda s: jnp.clip(s - TRUNK_CAST_STEPS, 0, N_ROW_TILES - 1)
    step = lambda f: (lambda s: f(clamp_tile(s)))
    step_proj = (lambda f: (lambda s: f(clamp_tile(s - 1)))) if skew else step
    const = lambda shape: pl.BlockSpec(shape, lambda s: (0,) * len(shape), pipeline_mode=pl.Buffered(1))
    chunk = lambda layer, rows, cols: pl.BlockSpec(
        (None, rows // TRUNK_CAST_STEPS, cols), lambda s: (layer, jnp.minimum(s, TRUNK_CAST_STEPS - 1), 0))
    mod_spec = const((MOD_ROWS, 6 * D_MODEL))
    x_specs = [pl.BlockSpec((ROW_TILE, D_MODEL), step(lambda t: (_ctx_tile(t), 0))),
               pl.BlockSpec((ROW_TILE, D_MODEL), step(lambda t: (_lat_tile(t), 0)))]
    args, in_specs = [xp, xs], list(x_specs)
    out_specs, out_shape, scratch, aliases = [], [], [], {}
    n_alias = 0
    if do_mlp:
        layer, mods_l, cat_p, cat_s, g2, final_g, wo, w1, w2 = mlp
        args += [cat_p, cat_s, mods_l, g2.reshape(1, D_MODEL), final_g.reshape(1, D_MODEL), wo, w1, w2]
        in_specs += [pl.BlockSpec((ROW_TILE, MIX_WIDTH), step(lambda t: (_ctx_tile(t), 0))),
                     pl.BlockSpec((ROW_TILE, MIX_WIDTH), step(lambda t: (_lat_tile(t), 0))), mod_spec,
                     const((1, D_MODEL)), const((1, D_MODEL)),
                     chunk(layer, MIX_WIDTH, D_MODEL), chunk(layer, D_MODEL, D_FF), chunk(layer, D_FF, D_MODEL)]
    if do_proj:
        p_layer, mods_p, g1, w_in, caches = proj
        args += [mods_p, g1.reshape(1, D_MODEL), w_in, *ropes]
        in_specs += [mod_spec, const((1, D_MODEL)), chunk(p_layer, D_MODEL, IN_WIDTH)]
        in_specs += [const((DEC_SEQ + ROW_TILE, 128))] * 4
        if caches is not None:
            n_alias = 6
            n_out_before = 2 if do_mlp else 0
            aliases = {len(args) + i: n_out_before + 1 + i for i in range(6)}
            args += list(caches)
            in_specs += [pl.BlockSpec(memory_space=pl.ANY)] * 6
    if do_mlp:
        out_specs += x_specs
        out_shape += [jax.ShapeDtypeStruct((N_CTX_TOK, D_MODEL), F32),
                      jax.ShapeDtypeStruct((N_LAT_TOK, D_MODEL), F32)]
        scratch += [pltpu.VMEM((MIX_WIDTH, D_MODEL), BF), pltpu.VMEM((D_MODEL, D_FF), BF),
                    pltpu.VMEM((D_FF, D_MODEL), BF)]
    if do_proj:
        out_specs += [pl.BlockSpec((ROW_TILE, ZB_WIDTH), step_proj(lambda t: (t, 0)))] + [
            pl.BlockSpec((None, None, SEQ, width), step_proj(lambda t: (_ctx_tile(t), p_layer, 0, 0)))
            for _, width in CACHE_COLS]
        out_shape += [jax.ShapeDtypeStruct((N_TOK, ZB_WIDTH), BF)] + [
            jax.ShapeDtypeStruct((BATCH, DEPTH, SEQ, width), F32) for _, width in CACHE_COLS]
        scratch += [pltpu.VMEM((D_MODEL, IN_WIDTH), BF)]
    if skew:
        scratch += [pltpu.VMEM((ROW_TILE, D_MODEL), BF)]
    outs = pl.pallas_call(
        functools.partial(_trunk_kernel, do_mlp, do_proj, last, n_alias),
        grid=(TRUNK_CAST_STEPS + N_ROW_TILES + (1 if skew else 0),),
        in_specs=in_specs,
        out_specs=out_specs,
        out_shape=out_shape,
        scratch_shapes=scratch,
        input_output_aliases=aliases,
        compiler_params=pltpu.CompilerParams(dimension_semantics=("arbitrary",),
                                             vmem_limit_bytes=TRUNK_VMEM_LIMIT),
        name="trunk_mlp_proj" if (do_mlp and do_proj) else ("trunk_mlp" if do_mlp else "trunk_proj"),
    )(*args)
    outs = list(outs)
    if do_mlp:
        xp, xs = outs[:2]
        outs = outs[2:]
    zb, new_caches = (outs[0], tuple(outs[1:7])) if do_proj else (None, None)
    return xp, xs, zb, new_caches


def _diff_lambda(lamp_ref, lam_init):
    a = jnp.sum(lamp_ref[0:1, :] * lamp_ref[1:2, :], axis=-1, keepdims=True)
    b = jnp.sum(lamp_ref[2:3, :] * lamp_ref[3:4, :], axis=-1, keepdims=True)
    return jnp.exp(a) - jnp.exp(b) + lam_init


def _subln(o, g, lam_init):
    return _rmsnorm(o, g) * (1.0 - lam_init)


def _lane_lo(n):
    return lax.broadcasted_iota(jnp.int32, (n, 128), 1) < HEAD_DIM


def _lane_lo_wide(n, width):
    return lax.broadcasted_iota(jnp.int32, (n, width), 1) % 128 < HEAD_DIM


def _split_pair(x, lo):
    zero = jnp.zeros_like(x)
    return jnp.where(lo, x, zero), jnp.where(lo, zero, x)


def _v_ones_pair(v, lo):
    one = jnp.ones_like(v)
    return jnp.where(lo, v, one), jnp.where(lo, one, v)


def _merge_pair(a, b, lo):
    return jnp.where(lo, a, b), pltpu.roll(jnp.where(lo, b, a), HEAD_DIM, 1)


def _subln_pair(o, g2, lam_init, lo):
    sq = o * o
    ms_a = jnp.sum(jnp.where(lo, sq, 0.0), axis=-1, keepdims=True)
    ms_b = jnp.sum(jnp.where(lo, 0.0, sq), axis=-1, keepdims=True)
    ms = jnp.where(lo, ms_a, ms_b) * (1.0 / HEAD_DIM)
    return o * lax.rsqrt(ms + NORM_EPS) * g2 * (1.0 - lam_init)


def _diff_quarters(q, n):
    quarter = lax.broadcasted_iota(jnp.int32, (n, 128), 1) // DIFF_QK_DIM
    zero = jnp.zeros_like(q)
    return [jnp.where(quarter == i, q, zero) for i in range(4)]


def _fourier(x_bf, cl, sl, cbd, sbd, wf):
    xc = _dot(x_bf, cbd).astype(BF)
    xs = _dot(x_bf, sbd).astype(BF)
    y = _dot(cl, xc) - _dot(sl, xs)
    return _dot(y.astype(BF), wf)


CTX_UNIT_GROUP = 4
CTX_REQ_PER_STEP = 2


def _ctx_mix_kernel(lam_init, with_ada, sink_ref, z_ref, lamp_ref, subg_ref, cl_ref, sl_ref, cbd_ref, sbd_ref,
                    wf_ref, *rest):
    if with_ada:
        cond_ref, wa_ref, ba_ref, o_ref, mods_ref = rest
        _ada_kernel(cond_ref, wa_ref, ba_ref, mods_ref)
    else:
        o_ref, = rest
    for r in range(CTX_REQ_PER_STEP):
        rows = pl.ds(r * SEQ, SEQ)
        _ctx_request(lam_init, sink_ref, z_ref.at[rows], lamp_ref, subg_ref, cl_ref, sl_ref, cbd_ref, sbd_ref,
                     wf_ref, o_ref.at[rows])


def _ctx_request(lam_init, sink_ref, z_ref, lamp_ref, subg_ref, cl_ref, sl_ref, cbd_ref, sbd_ref, wf_ref, o_ref):
    lo = _lane_lo(SEQ)

    units = []
    for j in range(2):
        c = 128 * j
        qs = _split_pair(z_ref[:, C_NA_Q + c:C_NA_Q + c + 128], lo)
        vs = _v_ones_pair(z_ref[:, C_NA_V + c:C_NA_V + c + 128], lo)
        units += [(qs[i], (C_NA_K + c), vs[i], None) for i in range(2)]
    for j in range(2):
        c = 128 * j
        q4 = _diff_quarters(z_ref[:, C_DQ + c:C_DQ + c + 128], SEQ)
        vs = _v_ones_pair(z_ref[:, C_DV + c:C_DV + c + 128], lo)
        units += [(q4[i], (C_DK + c), vs[i // 2], None) for i in (0, 2, 1, 3)]
    for g in range(2):
        c = 128 * g
        qs = _split_pair(z_ref[:, C_SQ + c:C_SQ + c + 128], lo)
        vs = _v_ones_pair(z_ref[:, ZB_SV + c:ZB_SV + c + 128], lo)
        units += [(qs[i], (ZB_SK + c), vs[i], sink_ref[2 * g + i]) for i in range(2)]

    res, extras = [], []
    for u0 in range(0, len(units), CTX_UNIT_GROUP):
        group = units[u0:u0 + CTX_UNIT_GROUP]
        scores = [_dot_nt(q, z_ref[:, kc:kc + 128]) for q, kc, _, _ in group]
        weights = []
        for s, (_, _, _, sink) in zip(scores, group):
            m = _rowmax(s)
            if sink is not None:
                m = jnp.maximum(m, sink)
                extras.append(jnp.exp(sink - m))
            weights.append(jnp.exp(s - m).astype(BF))
        res += [_dot(e, v) for e, (_, _, v, _) in zip(weights, group)]

    for j in range(2):
        num, den = _merge_pair(res[2 * j], res[2 * j + 1], lo)
        o_ref[:, 128 * j:128 * j + 128] = (num / den).astype(BF)

    lam = _diff_lambda(lamp_ref, lam_init)
    for j in range(2):
        n1, d1 = _merge_pair(res[4 + 4 * j], res[5 + 4 * j], lo)
        n2, d2 = _merge_pair(res[6 + 4 * j], res[7 + 4 * j], lo)
        o = n1 / d1 - lam * (n2 / d2)
        o_ref[:, 256 + 128 * j:384 + 128 * j] = _subln_pair(o, subg_ref[...], lam_init, lo).astype(BF)

    o_c = _fourier(z_ref[:, C_FC:C_FC + 256], cl_ref[...], sl_ref[...], cbd_ref[...], sbd_ref[...], wf_ref[...])
    o_ref[:, 512:768] = o_c.astype(BF)

    for g in range(2):
        num, den = _merge_pair(res[12 + 2 * g], res[13 + 2 * g], lo)
        den = den + jnp.where(lo, extras[2 * g], extras[2 * g + 1])
        o_ref[:, 768 + 128 * g:896 + 128 * g] = (num / den).astype(BF)


def _ctx_mix(zb, sink, lamp, subg, cl, sl, cbd, sbd, wf, lam_init, ada=None):
    full = lambda shape: pl.BlockSpec(shape, lambda b, s: (0,) * len(shape))
    n_steps = BATCH // CTX_REQ_PER_STEP
    in_specs = [
        pl.BlockSpec((CTX_REQ_PER_STEP * SEQ, ZB_WIDTH), lambda b, s: (b, 0)),
        full((4, DIFF_QK_DIM)),
        full((1, 2 * HEAD_DIM)),
        full((SEQ, SEQ)), full((SEQ, SEQ)),
        full((256, 256)), full((256, 256)), full((256, 256)),
    ]
    out_specs = [pl.BlockSpec((CTX_REQ_PER_STEP * SEQ, MIX_WIDTH), lambda b, s: (b, 0))]
    out_shape = [jax.ShapeDtypeStruct((N_CTX_TOK, MIX_WIDTH), BF)]
    args = [sink, zb, lamp, subg, cl, sl, cbd, sbd, wf]
    if ada is not None:
        cond, w_ada, b_ada3, layer = ada
        ada_in, ada_out = _ada_specs(layer, n_steps, lambda b, s: b)
        in_specs += ada_in
        out_specs.append(ada_out)
        out_shape.append(jax.ShapeDtypeStruct((MOD_ROWS, 6 * D_MODEL), F32))
        args += [cond, w_ada, b_ada3]
    grid_spec = pltpu.PrefetchScalarGridSpec(
        num_scalar_prefetch=1, grid=(n_steps,), in_specs=in_specs, out_specs=out_specs)
    outs = pl.pallas_call(
        functools.partial(_ctx_mix_kernel, lam_init, ada is not None),
        grid_spec=grid_spec,
        out_shape=out_shape,
        compiler_params=_params(),
        name="ctx_mix",
    )(*args)
    return outs[0], (outs[1] if ada is not None else None)


def _cache_v_ones(cv_ref, cva, cvb):
    first = lax.broadcasted_iota(jnp.int32, (256, PAST_LEN), 0) % 128 < HEAD_DIM
    cv = cv_ref[...]
    cva[...] = jnp.where(first, cv, 1.0).astype(BF)
    cvb[...] = jnp.where(first, 1.0, cv).astype(BF)


def _lat_na_part(z_ref, ck_ref, cv_ref, nb_ref, o_ref, ckb, cva, cvb):
    ckb[...] = ck_ref[...].astype(BF)
    _cache_v_ones(cv_ref, cva, cvb)
    n_q = NA_QROWS * GRID_W
    n_loc = NA_WIN_ROWS * GRID_W
    lo_q = _lane_lo(n_q)
    lo_w = _lane_lo(n_loc)

    def body(p, carry):
        w_row = jnp.clip(NA_QROWS * p - NA_KH // 2, 0, GRID_ROWS - NA_WIN_ROWS)
        tbl = jnp.minimum(p, 2) + jnp.maximum(p - 5, 0)
        q0 = pl.multiple_of(p * n_q, n_q)
        k0 = pl.multiple_of(w_row * GRID_W, NA_QROWS * GRID_W)
        scores = []
        for h in range(4):
            c = 128 * (h // 2)
            q = _split_pair(z_ref[pl.ds(q0, n_q), C_NA_Q + c:C_NA_Q + c + 128], lo_q)[h % 2]
            s_loc = _dot_nt(q, z_ref[pl.ds(k0, n_loc), C_NA_K + c:C_NA_K + c + 128]) + nb_ref[h, tbl]
            s_ctx = _dot(q, ckb[c:c + 128, :])
            scores.append((s_loc, s_ctx))
        weights = []
        for s_loc, s_ctx in scores:
            m = _rowmax(s_loc, s_ctx)
            weights.append((jnp.exp(s_loc - m).astype(BF), jnp.exp(s_ctx - m).astype(BF)))
        res = []
        for h, (e_loc, e_ctx) in enumerate(weights):
            c = 128 * (h // 2)
            v = _v_ones_pair(z_ref[pl.ds(k0, n_loc), C_NA_V + c:C_NA_V + c + 128], lo_w)[h % 2]
            cv = (cva, cvb)[h % 2][c:c + 128, :]
            res.append(_dot(e_loc, v) + _dot_nt(e_ctx, cv))
        for j in range(2):
            num, den = _merge_pair(res[2 * j], res[2 * j + 1], lo_q)
            o_ref[pl.ds(q0, n_q), 128 * j:128 * j + 128] = (num / den).astype(BF)
        return carry

    lax.fori_loop(0, GRID_ROWS // NA_QROWS, body, 0)


DIFF_QBLK = 512


def _lat_diff_part(lam_init, z_ref, ck_ref, cv_ref, lamp_ref, subg_ref, o_ref, ckb, cva, cvb, vla, vlb):
    ckb[...] = ck_ref[...].astype(BF)
    _cache_v_ones(cv_ref, cva, cvb)
    lo_v = _lane_lo_wide(DEC_SEQ, 256)
    v_loc = z_ref[:, C_DV:C_DV + 256]
    one = jnp.ones_like(v_loc)
    vla[...] = jnp.where(lo_v, v_loc, one)
    vlb[...] = jnp.where(lo_v, one, v_loc)
    lam = _diff_lambda(lamp_ref, lam_init)
    qblk = DIFF_QBLK
    lo_q = _lane_lo(qblk)

    def body(i, carry):
        q0 = pl.multiple_of(i * qblk, qblk)
        for j in range(2):
            c = 128 * j
            q4 = _diff_quarters(z_ref[pl.ds(q0, qblk), C_DQ + c:C_DQ + c + 128], qblk)
            scores = [(_dot_nt(q, z_ref[:, C_DK + c:C_DK + c + 128]), _dot(q, ckb[c:c + 128, :])) for q in q4]
            weights = []
            for s_loc, s_ctx in scores:
                m = _rowmax(s_loc, s_ctx)
                weights.append((jnp.exp(s_loc - m).astype(BF), jnp.exp(s_ctx - m).astype(BF)))
            res = []
            for t, (e_loc, e_ctx) in enumerate(weights):
                v_loc, v_ctx = ((vla, cva), (vlb, cvb))[t // 2]
                res.append(_dot(e_loc, v_loc[:, c:c + 128]) + _dot_nt(e_ctx, v_ctx[c:c + 128, :]))
            n1, d1 = _merge_pair(res[0], res[2], lo_q)
            n2, d2 = _merge_pair(res[1], res[3], lo_q)
            o = n1 / d1 - lam * (n2 / d2)
            o_ref[pl.ds(q0, qblk), 256 + c:256 + c + 128] = (
                _subln_pair(o, subg_ref[...], lam_init, lo_q).astype(BF))
        return carry

    lax.fori_loop(0, DEC_SEQ // qblk, body, 0)


def _lat_swa_part(sink_ref, z_ref, ck_ref, cv_ref, o_ref, ckd, cva, cvb):
    W = SWA_WINDOW
    n_win = 3 * W
    ones = jnp.ones((HEAD_DIM, PAST_LEN), BF)
    for g in range(2):
        k_g = ck_ref[HEAD_DIM * g:HEAD_DIM * (g + 1), :].astype(BF)
        v_g = cv_ref[HEAD_DIM * g:HEAD_DIM * (g + 1), :].astype(BF)
        r0, r1, r2 = 128 * g, 128 * g + HEAD_DIM, 128 * (g + 1)
        ckd[r0:r1, :] = k_g
        ckd[r1:r2, :] = k_g
        cva[r0:r1, :] = v_g
        cva[r1:r2, :] = ones
        cvb[r0:r1, :] = ones
        cvb[r1:r2, :] = v_g
    lo_q = _lane_lo(W)
    lo_w = _lane_lo(n_win)

    def body(n, carry):
        q0 = pl.multiple_of(n * W, W)
        w0 = pl.multiple_of(jnp.clip((n - 1) * W, 0, DEC_SEQ - n_win), W)
        qpos = q0 + lax.broadcasted_iota(jnp.int32, (W, n_win), 0)
        kpos = w0 + lax.broadcasted_iota(jnp.int32, (W, n_win), 1)
        valid = jnp.abs(qpos - kpos) <= W
        scores = []
        for h in range(4):
            c = 128 * (h // 2)
            q = _split_pair(z_ref[pl.ds(q0, W), C_SQ + c:C_SQ + c + 128], lo_q)[h % 2]
            s_loc = jnp.where(valid, _dot_nt(q, z_ref[pl.ds(w0, n_win), ZB_SK + c:ZB_SK + c + 128]), NEG_INF)
            scores.append((s_loc, _dot(q, ckd[c:c + 128, :])))
        weights, extras = [], []
        for h, (s_loc, s_ctx) in enumerate(scores):
            sink = sink_ref[h]
            m = jnp.maximum(_rowmax(s_loc, s_ctx), sink)
            weights.append((jnp.exp(s_loc - m).astype(BF), jnp.exp(s_ctx - m).astype(BF)))
            extras.append(jnp.exp(sink - m))
        res = []
        for h, (e_loc, e_ctx) in enumerate(weights):
            c = 128 * (h // 2)
            v = _v_ones_pair(z_ref[pl.ds(w0, n_win), ZB_SV + c:ZB_SV + c + 128], lo_w)[h % 2]
            res.append(_dot(e_loc, v) + _dot_nt(e_ctx, (cva, cvb)[h % 2][c:c + 128, :]))
        for g in range(2):
            num, den = _merge_pair(res[2 * g], res[2 * g + 1], lo_q)
            den = den + jnp.where(lo_q, extras[2 * g], extras[2 * g + 1])
            o_ref[pl.ds(q0, W), 768 + 128 * g:768 + 128 * g + 128] = (num / den).astype(BF)
        return carry

    lax.fori_loop(0, DEC_SEQ // W, body, 0)


def _lat_mix_kernel(lam_init, sink_ref, z_ref, nak_ref, nav_ref, dfk_ref, dfv_ref, swk_ref, swv_ref, nb_ref,
                    lamp_ref, subg_ref, cl_ref, sl_ref, cbd_ref, sbd_ref, wf_ref, o_ref, ck, cva, cvb, vla, vlb):
    _lat_na_part(z_ref, nak_ref, nav_ref, nb_ref, o_ref, ck, cva, cvb)
    _lat_diff_part(lam_init, z_ref, dfk_ref, dfv_ref, lamp_ref, subg_ref, o_ref, ck, cva, cvb, vla, vlb)
    o_c = _fourier(z_ref[:, C_FC:C_FC + 256], cl_ref[...], sl_ref[...], cbd_ref[...], sbd_ref[...], wf_ref[...])
    o_ref[:, 512:768] = o_c.astype(BF)
    _lat_swa_part(sink_ref, z_ref, swk_ref, swv_ref, o_ref, ck, cva, cvb)


def _lat_mix(zb, caches_in, nb, sink, lamp, subg, cl, sl, cbd, sbd, wf, lam_init, layer):
    full = lambda shape: pl.BlockSpec(shape, lambda b, s: (0,) * len(shape), pipeline_mode=pl.Buffered(1))
    cache = lambda width: pl.BlockSpec((None, None, width, PAST_LEN), lambda b, s: (b, layer, 0, 0))
    n_q = NA_QROWS * GRID_W
    n_loc = NA_WIN_ROWS * GRID_W
    grid_spec = pltpu.PrefetchScalarGridSpec(
        num_scalar_prefetch=1,
        grid=(DEC_BATCH,),
        in_specs=[
            pl.BlockSpec((DEC_SEQ, ZB_WIDTH), lambda b, s: (LAT_BLOCK0 + b, 0)),
            cache(256), cache(256), cache(256), cache(256), cache(128), cache(128),
            pl.BlockSpec((None, 4, len(NA_TABLE_PAIRS), n_q, n_loc), lambda b, s: (layer, 0, 0, 0, 0),
                         pipeline_mode=pl.Buffered(1)),
            full((4, DIFF_QK_DIM)), full((1, 2 * HEAD_DIM)),
            full((DEC_SEQ, DEC_SEQ)), full((DEC_SEQ, DEC_SEQ)),
            full((256, 256)), full((256, 256)), full((256, 256)),
        ],
        out_specs=pl.BlockSpec((DEC_SEQ, MIX_WIDTH), lambda b, s: (b, 0)),
        scratch_shapes=[pltpu.VMEM((256, PAST_LEN), BF)] * 3 + [pltpu.VMEM((DEC_SEQ, 256), BF)] * 2,
    )
    return pl.pallas_call(
        functools.partial(_lat_mix_kernel, lam_init),
        grid_spec=grid_spec,
        out_shape=jax.ShapeDtypeStruct((N_LAT_TOK, MIX_WIDTH), BF),
        compiler_params=_params(),
        name="lat_mix",
    )(sink, zb, *caches_in, nb, lamp, subg, cl, sl, cbd, sbd, wf)


def _dft_tables(n):
    j = np.arange(n)
    ang = 2.0 * np.pi * ((j[:, None] * j[None, :]) % n) / n
    return np.cos(ang) / np.sqrt(n), np.sin(ang) / np.sqrt(n)


def _block_diag4(m):
    out = np.zeros((256, 256), m.dtype)
    for g in range(4):
        out[64 * g:64 * g + 64, 64 * g:64 * g + 64] = m
    return out


def _rope_tables(n_axis_dims):
    half = n_axis_dims // 2
    inv = ROPE_BASE ** (-np.arange(half, dtype=np.float64) / half)
    t = np.arange(DEC_SEQ)
    lane = np.arange(128)
    w = lane % n_axis_dims
    is_col = (lane // n_axis_dims) % 2 == 1
    pos = np.where(is_col[None, :], (t % GRID_W)[:, None], (t // GRID_W)[:, None]).astype(np.float64)
    ang = pos * inv[w % half][None, :]
    sign = np.where(w < half, -1.0, 1.0)[None, :]
    cos = np.concatenate([np.cos(ang), np.ones((ROW_TILE, 128))], axis=0)
    sin = np.concatenate([np.sin(ang) * sign, np.zeros((ROW_TILE, 128))], axis=0)
    return jnp.asarray(cos, F32), jnp.asarray(sin, F32)


def _na_bias_tables(rpb):
    lead = rpb.shape[:2]
    col = np.arange(GRID_W)
    cstart = np.clip(col - NA_KW // 2, 0, GRID_W - NA_KW)
    cmask = (col[None, :] >= cstart[:, None]) & (col[None, :] < cstart[:, None] + NA_KW)
    pad = GRID_W - NA_KW
    rp = jnp.pad(rpb, ((0, 0), (0, 0), (0, 0), (pad, pad)))
    t = jnp.stack([rp[..., pad + NA_KW - 1 - c:pad + NA_KW - 1 - c + GRID_W] for c in range(GRID_W)], axis=-2)
    t = jnp.where(cmask, t, NEG_INF)
    t = t.transpose(0, 1, 3, 2, 4).reshape(lead + (GRID_W, (2 * NA_KH - 1) * GRID_W))
    tables = []
    for p in NA_TABLE_PAIRS:
        w_row = int(np.clip(NA_QROWS * p - NA_KH // 2, 0, GRID_ROWS - NA_WIN_ROWS))
        rows = []
        for r in (NA_QROWS * p, NA_QROWS * p + 1):
            rs = int(np.clip(r - NA_KH // 2, 0, GRID_ROWS - NA_KH))
            j_lo, j_hi = rs - w_row, rs - w_row + NA_KH
            d_lo = rs - r + NA_KH - 1
            body = t[..., d_lo * GRID_W:(d_lo + NA_KH) * GRID_W]
            rows.append(jnp.pad(body, ((0, 0),) * 3 + ((j_lo * GRID_W, (NA_WIN_ROWS - j_hi) * GRID_W),),
                                constant_values=NEG_INF))
        tables.append(jnp.concatenate(rows, axis=-2))
    return jnp.stack(tables, axis=2)


def kernel(x_prompt, x_sample, cache_na_k, cache_na_v, cache_diff_k, cache_diff_v, cache_swa_k, cache_swa_v, c, c_ctx, w_ada, b_ada, norm1_g, norm2_g, w_in, na_rpb, diff_lq1, diff_lk1, diff_lq2, diff_lk2, diff_subln_g, w_fourier, swa_sink, w_out, w_mlp1, w_mlp2, final_g):
    xp = x_prompt.reshape(N_CTX_TOK, D_MODEL)
    xs = x_sample.reshape(N_LAT_TOK, D_MODEL)
    cond = jnp.zeros((MOD_ROWS, D_MODEL), F32).at[0].set(c_ctx).at[1:1 + DEC_BATCH].set(c)
    b_ada3 = b_ada.reshape(DEPTH, 1, 6 * D_MODEL)
    mods = _ada(cond, w_ada, b_ada3, 0)

    cl_p, sl_p = _dft_tables(SEQ)
    cl_s, sl_s = _dft_tables(DEC_SEQ)
    c64, s64 = _dft_tables(64)
    cl_p, sl_p, cl_s, sl_s, cbd, sbd = (
        jnp.asarray(a, F32).astype(BF) for a in (cl_p, sl_p, cl_s, sl_s, _block_diag4(c64), _block_diag4(s64)))
    ropes = _rope_tables(16) + _rope_tables(32)
    nb = _na_bias_tables(na_rpb)
    wf_bf = w_fourier.astype(BF)

    ck_na, cv_na, ck_df, cv_df, ck_sw, cv_sw = (
        a.transpose(0, 1, 3, 4, 2).reshape(DEC_BATCH, DEPTH, -1, PAST_LEN)
        for a in (cache_na_k, cache_na_v, cache_diff_k, cache_diff_v, cache_swa_k, cache_swa_v))

    _, _, zb, caches = _trunk(xp, xs, ropes, proj=(0, mods, norm1_g[0], w_in, None))
    for l in range(DEPTH):
        lam_init = 0.8 - 0.6 * math.exp(-0.3 * l)
        lamp = jnp.stack([diff_lq1[l], diff_lk1[l], diff_lq2[l], diff_lk2[l]], axis=0)
        subg = jnp.tile(diff_subln_g[l].reshape(1, HEAD_DIM), (1, 2))
        has_next = l + 1 < DEPTH

        cat_p, mods_next = _ctx_mix(zb, swa_sink[l], lamp, subg, cl_p, sl_p, cbd, sbd, wf_bf[l], lam_init,
                                    ada=(cond, w_ada, b_ada3, l + 1) if has_next else None)
        cat_s = _lat_mix(zb, (ck_na, cv_na, ck_df, cv_df, ck_sw, cv_sw), nb, swa_sink[l], lamp, subg,
                         cl_s, sl_s, cbd, sbd, wf_bf[l], lam_init, l)
        nxt = (l + 1, mods_next, norm1_g[l + 1], w_in, caches) if has_next else None
        xp, xs, zb, new_caches = _trunk(
            xp, xs, ropes, mlp=(l, mods, cat_p, cat_s, norm2_g[l], final_g, w_out, w_mlp1, w_mlp2), proj=nxt)
        caches = new_caches if has_next else caches
        mods = mods_next

    y_prompt = xp.reshape(BATCH, SEQ, D_MODEL)
    y_sample = xs.reshape(DEC_BATCH, DEC_SEQ, D_MODEL)
    new = [a.reshape(BATCH, DEPTH, SEQ, a.shape[-1] // HEAD_DIM, HEAD_DIM) for a in caches]
    return (y_prompt, y_sample) + tuple(new)
```

```python
import functools
import math

import numpy as np
import jax
import jax.numpy as jnp
from jax import lax
from jax.experimental import pallas as pl
from jax.experimental.pallas import tpu as pltpu

D_MODEL = 1024
BATCH = 16
SEQ = 256
DEPTH = 4
DEC_BATCH = 2
DEC_SEQ = 1024
PAST_LEN = 512
GRID_W = 64
GRID_ROWS = DEC_SEQ // GRID_W
HEAD_DIM = 64
NA_KH = 8
NA_KW = 16
DIFF_QK_DIM = 32
SWA_WINDOW = 128
D_FF = 4 * D_MODEL
ROPE_BASE = 10000.0
NORM_EPS = 1e-6
NEG_INF = -1e30
IN_WIDTH = 2304
MIX_WIDTH = 1024

N_CTX_TOK = BATCH * SEQ
N_LAT_TOK = DEC_BATCH * DEC_SEQ
N_TOK = N_CTX_TOK + N_LAT_TOK
ROW_TILE = 256
N_ROW_TILES = N_TOK // ROW_TILE
N_CTX_TILES = N_CTX_TOK // ROW_TILE
N_LAT_TILES = N_LAT_TOK // ROW_TILE
LAT_TILES_PER_REQ = DEC_SEQ // ROW_TILE
LAT_BLOCK0 = N_CTX_TOK // DEC_SEQ
MOD_ROWS = 8

C_NA_Q, C_NA_K, C_NA_V = 0, 256, 512
C_DQ, C_DK, C_DV = 768, 1024, 1280
C_FC = 1536
C_SQ, C_SK, C_SV = 1792, 2048, 2176
CACHE_COLS = ((C_NA_K, 256), (C_NA_V, 256), (C_DK, 256), (C_DV, 256), (C_SK, 128), (C_SV, 128))
ZB_SK, ZB_SV = 2048, 2304
ZB_WIDTH = 2560

NA_QROWS = 2
NA_WIN_ROWS = 10

BF = jnp.bfloat16
F32 = jnp.float32
VMEM_LIMIT = 56 * 1024 * 1024


def _dot(a, b):
    return jnp.dot(a, b, preferred_element_type=F32)


def _dot_nt(a, b):
    return lax.dot_general(a, b, (((1,), (1,)), ((), ())), preferred_element_type=F32)


def _rmsnorm(x, g):
    ms = jnp.mean(x * x, axis=-1, keepdims=True)
    return x * lax.rsqrt(ms + NORM_EPS) * g


def _params(n_grid=1):
    return pltpu.CompilerParams(dimension_semantics=("arbitrary",) * n_grid, vmem_limit_bytes=VMEM_LIMIT)


def _rowmax(*parts):
    m = jnp.max(parts[0], axis=-1, keepdims=True)
    for p in parts[1:]:
        m = jnp.maximum(m, jnp.max(p, axis=-1, keepdims=True))
    return m


def _rowsum(*parts):
    s = jnp.sum(parts[0], axis=-1, keepdims=True)
    for p in parts[1:]:
        s = s + jnp.sum(p, axis=-1, keepdims=True)
    return s


def _ada_kernel(cond_ref, w_ref, b_ref, o_ref):
    cnd = cond_ref[...]
    s = cnd / (1.0 + jnp.exp(-cnd))
    o_ref[...] = _dot(s.astype(BF), w_ref[...].astype(BF)) + b_ref[...]


def _ada_specs(layer, n_steps, index_of_step):
    tn = 6 * D_MODEL // n_steps
    col = lambda *g: index_of_step(*g)
    in_specs = [
        pl.BlockSpec((MOD_ROWS, D_MODEL), lambda *g: (0, 0)),
        pl.BlockSpec((None, D_MODEL, tn), lambda *g: (layer, 0, col(*g))),
        pl.BlockSpec((None, 1, tn), lambda *g: (layer, 0, col(*g))),
    ]
    return in_specs, pl.BlockSpec((MOD_ROWS, tn), lambda *g: (0, col(*g)))


def _ada(cond, w_ada, b_ada3, layer):
    in_specs, out_spec = _ada_specs(layer, 6, lambda j: j)
    return pl.pallas_call(
        _ada_kernel,
        grid=(6,),
        in_specs=in_specs,
        out_specs=out_spec,
        out_shape=jax.ShapeDtypeStruct((MOD_ROWS, 6 * D_MODEL), F32),
        compiler_params=_params(),
        name="ada",
    )(cond, w_ada, b_ada3)


def _mod_row(t):
    return jnp.where(t < N_CTX_TILES, 0, 1 + (t - N_CTX_TILES) // LAT_TILES_PER_REQ)


def _ctx_tile(t):
    return jnp.minimum(t, N_CTX_TILES - 1)


def _lat_tile(t):
    return jnp.maximum(t - N_CTX_TILES, 0)


TRUNK_CAST_STEPS = 8
FF_CHUNK = 1024
TRUNK_VMEM_LIMIT = 60 * 1024 * 1024


def _rope(x, cos, sin_signed, shift, first):
    partner = jnp.where(first, pltpu.roll(x, 128 - shift, 1), pltpu.roll(x, shift, 1))
    return x * cos + partner * sin_signed


def _projection_stages(hb, wb, t, is_ctx, zb_ref, cache_refs, rope_refs):
    cosd_ref, sind_ref, coss_ref, sins_ref = rope_refs
    dscale = DIFF_QK_DIM ** -0.5
    p0 = pl.multiple_of(
        jnp.where(is_ctx, DEC_SEQ, ((t - N_CTX_TILES) % LAT_TILES_PER_REQ) * ROW_TILE), ROW_TILE)
    lane = lax.broadcasted_iota(jnp.int32, (ROW_TILE, 128), 1)
    first_d = (lane % 16) < 8
    first_s = (lane % 32) < 16
    lo = lane < HEAD_DIM
    cosd, sind = cosd_ref[pl.ds(p0, ROW_TILE), :], sind_ref[pl.ds(p0, ROW_TILE), :]
    coss, sins = coss_ref[pl.ds(p0, ROW_TILE), :], sins_ref[pl.ds(p0, ROW_TILE), :]

    def cols(c0, width):
        return _dot(hb(), wb[:, c0:c0 + width])

    def put(c0, width, val):
        zb_ref[:, c0:c0 + width] = val.astype(BF)

    def put_dup(c0, val):
        r = pltpu.roll(val, HEAD_DIM, 1)
        put(c0, 128, jnp.where(lo, val, r))
        put(c0 + 128, 128, jnp.where(lo, r, val))

    kept = {}

    def na_q():
        put(C_NA_Q, 256, cols(C_NA_Q, 256) * 0.125)

    def na_kv():
        kept["nakv"] = cols(C_NA_K, 512)
        put(C_NA_K, 512, kept["nakv"])

    def diff_q():
        z = cols(C_DQ, 256)
        for j in range(2):
            put(C_DQ + 128 * j, 128, _rope(z[:, 128 * j:128 * j + 128], cosd, sind, 8, first_d) * dscale)

    def diff_k():
        kept["dk"] = cols(C_DK, 256)
        for j in range(2):
            put(C_DK + 128 * j, 128, _rope(kept["dk"][:, 128 * j:128 * j + 128], cosd, sind, 8, first_d))

    def diff_v_fourier():
        kept["dvfc"] = cols(C_DV, 512)
        put(C_DV, 512, kept["dvfc"])

    def swa_q():
        z = cols(C_SQ, 256)
        for j in range(2):
            put(C_SQ + 128 * j, 128, _rope(z[:, 128 * j:128 * j + 128], coss, sins, 16, first_s) * 0.125)

    def swa_kv():
        kept["skv"] = cols(C_SK, 256)
        put_dup(ZB_SK, _rope(kept["skv"][:, 0:128], coss, sins, 16, first_s))
        put_dup(ZB_SV, kept["skv"][:, 128:256])

    def finish():
        new = (kept["nakv"][:, 0:256], kept["nakv"][:, 256:512], kept["dk"], kept["dvfc"][:, 0:256],
               kept["skv"][:, 0:128], kept["skv"][:, 128:256])
        for ref, val in zip(cache_refs, new):
            ref[...] = jnp.where(is_ctx, val, ref[...])

    return [na_q, na_kv, diff_q, diff_k, diff_v_fourier, swa_q, swa_kv], finish


def _trunk_kernel(do_mlp, do_proj, last, n_alias, *refs):
    xp_ref, xs_ref = refs[:2]
    pos = 2
    if do_mlp:
        catp_ref, cats_ref, mm_ref, g2_ref, fg_ref, wo_ref, w1_ref, w2_ref = refs[pos:pos + 8]
        pos += 8
    if do_proj:
        pm_ref, g1_ref, win_ref = refs[pos:pos + 3]
        rope_refs = refs[pos + 3:pos + 7]
        pos += 7 + n_alias
    if do_mlp:
        op_ref, os_ref = refs[pos:pos + 2]
        pos += 2
    if do_proj:
        zb_ref = refs[pos]
        cache_refs = refs[pos + 1:pos + 7]
        pos += 7
    if do_mlp:
        wob, w1b, w2b = refs[pos:pos + 3]
        pos += 3
    if do_proj:
        wb = refs[pos]
        pos += 1
    skew = do_mlp and do_proj
    if skew:
        hprev = refs[pos]
    s = pl.program_id(0)

    @pl.when(s < TRUNK_CAST_STEPS)
    def _():
        def cast(dst, src):
            rows = src.shape[0]
            dst[pl.ds(pl.multiple_of(s * rows, rows), rows), :] = src[...].astype(BF)

        if do_mlp:
            cast(wob, wo_ref)
            cast(w1b, w1_ref)
            cast(w2b, w2_ref)
        if do_proj:
            cast(wb, win_ref)
        if skew:
            @pl.when(s == 0)
            def _():
                hprev[...] = jnp.zeros_like(hprev)

    def tile_step(t, proj_stages, proj_finish):
        pending = list(proj_stages)

        def issue(n):
            for _ in range(min(n, len(pending))):
                pending.pop(0)()

        is_ctx = t < N_CTX_TILES
        row = _mod_row(t)
        x = jnp.where(is_ctx, xp_ref[...], xs_ref[...])
        if do_mlp:
            issue(1)
            gt1 = mm_ref[pl.ds(row, 1), 2 * D_MODEL:3 * D_MODEL]
            sh2 = mm_ref[pl.ds(row, 1), 3 * D_MODEL:4 * D_MODEL]
            sc2 = mm_ref[pl.ds(row, 1), 4 * D_MODEL:5 * D_MODEL]
            gt2 = mm_ref[pl.ds(row, 1), 5 * D_MODEL:6 * D_MODEL]
            cat = jnp.where(is_ctx, catp_ref[...], cats_ref[...])
            x = x + gt1 * _dot(cat, wob[...])
            issue(1)
            hh = (_rmsnorm(x, g2_ref[...]) * (1.0 + sc2) + sh2).astype(BF)
            acc = jnp.zeros((ROW_TILE, D_MODEL), F32)
            for c in range(D_FF // FF_CHUNK):
                a = jnp.maximum(_dot(hh, w1b[:, c * FF_CHUNK:(c + 1) * FF_CHUNK]), 0.0)
                acc = acc + _dot((a * a).astype(BF), w2b[c * FF_CHUNK:(c + 1) * FF_CHUNK, :])
                if c < 2:
                    issue(1)
            issue(1)
            x = x + gt2 * acc
            out = _rmsnorm(x, fg_ref[...]) if last else x
        hb = None
        if do_proj:
            issue(1)
            shift = pm_ref[pl.ds(row, 1), 0:D_MODEL]
            scale = pm_ref[pl.ds(row, 1), D_MODEL:2 * D_MODEL]
            hb = (_rmsnorm(x, g1_ref[...]) * (1.0 + scale) + shift).astype(BF)
        issue(len(pending))
        if proj_finish is not None:
            proj_finish()
        if do_mlp:
            op_ref[...] = jnp.where(is_ctx, out, op_ref[...])
            os_ref[...] = jnp.where(is_ctx, os_ref[...], out)
        return hb

    def projection(hb, t):
        return _projection_stages(hb, wb, t, t < N_CTX_TILES, zb_ref, cache_refs, rope_refs)

    @pl.when(s >= TRUNK_CAST_STEPS)
    def _():
        t = s - TRUNK_CAST_STEPS
        if not skew:
            hb = tile_step(t, [], None)
            if do_proj:
                stages, finish = projection(lambda: hb, t)
                for stage in stages:
                    stage()
                finish()
        else:
            u = jnp.maximum(t - 1, 0)

            @pl.when(t < N_ROW_TILES)
            def _():
                stages, finish = projection(lambda: hprev[...], u)
                hprev[...] = tile_step(t, stages, finish)

            @pl.when(t >= N_ROW_TILES)
            def _():
                stages, finish = projection(lambda: hprev[...], u)
                for stage in stages:
                    stage()
                finish()


def _trunk(xp, xs, ropes, mlp=None, proj=None):
    do_mlp, do_proj = mlp is not None, proj is not None
    last = do_mlp and not do_proj
    skew = do_mlp and do_proj
    clamp_tile = lambda s: jnp.clip(s - TRUNK_CAST_STEPS, 0, N_ROW_TILES - 1)
    step = lambda f: (lambda s: f(clamp_tile(s)))
    step_proj = (lambda f: (lambda s: f(clamp_tile(s - 1)))) if skew else step
    const = lambda shape: pl.BlockSpec(shape, lambda s: (0,) * len(shape), pipeline_mode=pl.Buffered(1))
    chunk = lambda layer, rows, cols: pl.BlockSpec(
        (None, rows // TRUNK_CAST_STEPS, cols), lambda s: (layer, jnp.minimum(s, TRUNK_CAST_STEPS - 1), 0))
    mod_spec = const((MOD_ROWS, 6 * D_MODEL))
    x_specs = [pl.BlockSpec((ROW_TILE, D_MODEL), step(lambda t: (_ctx_tile(t), 0))),
               pl.BlockSpec((ROW_TILE, D_MODEL), step(lambda t: (_lat_tile(t), 0)))]
    args, in_specs = [xp, xs], list(x_specs)
    out_specs, out_shape, scratch, aliases = [], [], [], {}
    n_alias = 0
    if do_mlp:
        layer, mods_l, cat_p, cat_s, g2, final_g, wo, w1, w2 = mlp
        args += [cat_p, cat_s, mods_l, g2.reshape(1, D_MODEL), final_g.reshape(1, D_MODEL), wo, w1, w2]
        in_specs += [pl.BlockSpec((ROW_TILE, MIX_WIDTH), step(lambda t: (_ctx_tile(t), 0))),
                     pl.BlockSpec((ROW_TILE, MIX_WIDTH), step(lambda t: (_lat_tile(t), 0))), mod_spec,
                     const((1, D_MODEL)), const((1, D_MODEL)),
                     chunk(layer, MIX_WIDTH, D_MODEL), chunk(layer, D_MODEL, D_FF), chunk(layer, D_FF, D_MODEL)]
    if do_proj:
        p_layer, mods_p, g1, w_in, caches = proj
        args += [mods_p, g1.reshape(1, D_MODEL), w_in, *ropes]
        in_specs += [mod_spec, const((1, D_MODEL)), chunk(p_layer, D_MODEL, IN_WIDTH)]
        in_specs += [const((DEC_SEQ + ROW_TILE, 128))] * 4
        if caches is not None:
            n_alias = 6
            n_out_before = 2 if do_mlp else 0
            aliases = {len(args) + i: n_out_before + 1 + i for i in range(6)}
            args += list(caches)
            in_specs += [pl.BlockSpec(memory_space=pl.ANY)] * 6
    if do_mlp:
        out_specs += x_specs
        out_shape += [jax.ShapeDtypeStruct((N_CTX_TOK, D_MODEL), F32),
                      jax.ShapeDtypeStruct((N_LAT_TOK, D_MODEL), F32)]
        scratch += [pltpu.VMEM((MIX_WIDTH, D_MODEL), BF), pltpu.VMEM((D_MODEL, D_FF), BF),
                    pltpu.VMEM((D_FF, D_MODEL), BF)]
    if do_proj:
        out_specs += [pl.BlockSpec((ROW_TILE, ZB_WIDTH), step_proj(lambda t: (t, 0)))] + [
            pl.BlockSpec((None, None, SEQ, width), step_proj(lambda t: (_ctx_tile(t), p_layer, 0, 0)))
            for _, width in CACHE_COLS]
        out_shape += [jax.ShapeDtypeStruct((N_TOK, ZB_WIDTH), BF)] + [
            jax.ShapeDtypeStruct((BATCH, DEPTH, SEQ, width), F32) for _, width in CACHE_COLS]
        scratch += [pltpu.VMEM((D_MODEL, IN_WIDTH), BF)]
    if skew:
        scratch += [pltpu.VMEM((ROW_TILE, D_MODEL), BF)]
    outs = pl.pallas_call(
        functools.partial(_trunk_kernel, do_mlp, do_proj, last, n_alias),
        grid=(TRUNK_CAST_STEPS + N_ROW_TILES + (1 if skew else 0),),
        in_specs=in_specs,
        out_specs=out_specs,
        out_shape=out_shape,
        scratch_shapes=scratch,
        input_output_aliases=aliases,
        compiler_params=pltpu.CompilerParams(dimension_semantics=("arbitrary",),
                                             vmem_limit_bytes=TRUNK_VMEM_LIMIT),
        name="trunk_mlp_proj" if (do_mlp and do_proj) else ("trunk_mlp" if do_mlp else "trunk_proj"),
    )(*args)
    outs = list(outs)
    if do_mlp:
        xp, xs = outs[:2]
        outs = outs[2:]
    zb, new_caches = (outs[0], tuple(outs[1:7])) if do_proj else (None, None)
    return xp, xs, zb, new_caches


def _diff_lambda(lamp_ref, lam_init):
    a = jnp.sum(lamp_ref[0:1, :] * lamp_ref[1:2, :], axis=-1, keepdims=True)
    b = jnp.sum(lamp_ref[2:3, :] * lamp_ref[3:4, :], axis=-1, keepdims=True)
    return jnp.exp(a) - jnp.exp(b) + lam_init


def _subln(o, g, lam_init):
    return _rmsnorm(o, g) * (1.0 - lam_init)


def _lane_lo(n):
    return lax.broadcasted_iota(jnp.int32, (n, 128), 1) < HEAD_DIM


def _lane_lo_wide(n, width):
    return lax.broadcasted_iota(jnp.int32, (n, width), 1) % 128 < HEAD_DIM


def _split_pair(x, lo):
    zero = jnp.zeros_like(x)
    return jnp.where(lo, x, zero), jnp.where(lo, zero, x)


def _v_ones_pair(v, lo):
    one = jnp.ones_like(v)
    return jnp.where(lo, v, one), jnp.where(lo, one, v)


def _merge_pair(a, b, lo):
    return jnp.where(lo, a, b), pltpu.roll(jnp.where(lo, b, a), HEAD_DIM, 1)


def _subln_pair(o, g2, lam_init, lo):
    sq = o * o
    ms_a = jnp.sum(jnp.where(lo, sq, 0.0), axis=-1, keepdims=True)
    ms_b = jnp.sum(jnp.where(lo, 0.0, sq), axis=-1, keepdims=True)
    ms = jnp.where(lo, ms_a, ms_b) * (1.0 / HEAD_DIM)
    return o * lax.rsqrt(ms + NORM_EPS) * g2 * (1.0 - lam_init)


def _diff_quarters(q, n):
    quarter = lax.broadcasted_iota(jnp.int32, (n, 128), 1) // DIFF_QK_DIM
    zero = jnp.zeros_like(q)
    return [jnp.where(quarter == i, q, zero) for i in range(4)]


def _fourier(x_bf, cl, sl, cbd, sbd, wf):
    xc = _dot(x_bf, cbd).astype(BF)
    xs = _dot(x_bf, sbd).astype(BF)
    y = _dot(cl, xc) - _dot(sl, xs)
    return _dot(y.astype(BF), wf)


CTX_UNIT_GROUP = 4
CTX_REQ_PER_STEP = 2


def _ctx_mix_kernel(lam_init, with_ada, sink_ref, z_ref, lamp_ref, subg_ref, cl_ref, sl_ref, cbd_ref, sbd_ref,
                    wf_ref, *rest):
    if with_ada:
        cond_ref, wa_ref, ba_ref, o_ref, mods_ref = rest
        _ada_kernel(cond_ref, wa_ref, ba_ref, mods_ref)
    else:
        o_ref, = rest
    for r in range(CTX_REQ_PER_STEP):
        rows = pl.ds(r * SEQ, SEQ)
        _ctx_request(lam_init, sink_ref, z_ref.at[rows], lamp_ref, subg_ref, cl_ref, sl_ref, cbd_ref, sbd_ref,
                     wf_ref, o_ref.at[rows])


def _ctx_request(lam_init, sink_ref, z_ref, lamp_ref, subg_ref, cl_ref, sl_ref, cbd_ref, sbd_ref, wf_ref, o_ref):
    lo = _lane_lo(SEQ)

    units = []
    for j in range(2):
        c = 128 * j
        qs = _split_pair(z_ref[:, C_NA_Q + c:C_NA_Q + c + 128], lo)
        vs = _v_ones_pair(z_ref[:, C_NA_V + c:C_NA_V + c + 128], lo)
        units += [(qs[i], (C_NA_K + c), vs[i], None) for i in range(2)]
    for j in range(2):
        c = 128 * j
        q4 = _diff_quarters(z_ref[:, C_DQ + c:C_DQ + c + 128], SEQ)
        vs = _v_ones_pair(z_ref[:, C_DV + c:C_DV + c + 128], lo)
        units += [(q4[i], (C_DK + c), vs[i // 2], None) for i in (0, 2, 1, 3)]
    for g in range(2):
        c = 128 * g
        qs = _split_pair(z_ref[:, C_SQ + c:C_SQ + c + 128], lo)
        vs = _v_ones_pair(z_ref[:, ZB_SV + c:ZB_SV + c + 128], lo)
        units += [(qs[i], (ZB_SK + c), vs[i], sink_ref[2 * g + i]) for i in range(2)]

    res, extras = [], []
    for u0 in range(0, len(units), CTX_UNIT_GROUP):
        group = units[u0:u0 + CTX_UNIT_GROUP]
        scores = [_dot_nt(q, z_ref[:, kc:kc + 128]) for q, kc, _, _ in group]
        weights = []
        for s, (_, _, _, sink) in zip(scores, group):
            m = _rowmax(s)
            if sink is not None:
                m = jnp.maximum(m, sink)
                extras.append(jnp.exp(sink - m))
            weights.append(jnp.exp(s - m).astype(BF))
        res += [_dot(e, v) for e, (_, _, v, _) in zip(weights, group)]

    for j in range(2):
        num, den = _merge_pair(res[2 * j], res[2 * j + 1], lo)
        o_ref[:, 128 * j:128 * j + 128] = (num / den).astype(BF)

    lam = _diff_lambda(lamp_ref, lam_init)
    for j in range(2):
        n1, d1 = _merge_pair(res[4 + 4 * j], res[5 + 4 * j], lo)
        n2, d2 = _merge_pair(res[6 + 4 * j], res[7 + 4 * j], lo)
        o = n1 / d1 - lam * (n2 / d2)
        o_ref[:, 256 + 128 * j:384 + 128 * j] = _subln_pair(o, subg_ref[...], lam_init, lo).astype(BF)

    o_c = _fourier(z_ref[:, C_FC:C_FC + 256], cl_ref[...], sl_ref[...], cbd_ref[...], sbd_ref[...], wf_ref[...])
    o_ref[:, 512:768] = o_c.astype(BF)

    for g in range(2):
        num, den = _merge_pair(res[12 + 2 * g], res[13 + 2 * g], lo)
        den = den + jnp.where(lo, extras[2 * g], extras[2 * g + 1])
        o_ref[:, 768 + 128 * g:896 + 128 * g] = (num / den).astype(BF)


def _ctx_mix(zb, sink, lamp, subg, cl, sl, cbd, sbd, wf, lam_init, ada=None):
    full = lambda shape: pl.BlockSpec(shape, lambda b, s: (0,) * len(shape))
    n_steps = BATCH // CTX_REQ_PER_STEP
    in_specs = [
        pl.BlockSpec((CTX_REQ_PER_STEP * SEQ, ZB_WIDTH), lambda b, s: (b, 0)),
        full((4, DIFF_QK_DIM)),
        full((1, 2 * HEAD_DIM)),
        full((SEQ, SEQ)), full((SEQ, SEQ)),
        full((256, 256)), full((256, 256)), full((256, 256)),
    ]
    out_specs = [pl.BlockSpec((CTX_REQ_PER_STEP * SEQ, MIX_WIDTH), lambda b, s: (b, 0))]
    out_shape = [jax.ShapeDtypeStruct((N_CTX_TOK, MIX_WIDTH), BF)]
    args = [sink, zb, lamp, subg, cl, sl, cbd, sbd, wf]
    if ada is not None:
        cond, w_ada, b_ada3, layer = ada
        ada_in, ada_out = _ada_specs(layer, n_steps, lambda b, s: b)
        in_specs += ada_in
        out_specs.append(ada_out)
        out_shape.append(jax.ShapeDtypeStruct((MOD_ROWS, 6 * D_MODEL), F32))
        args += [cond, w_ada, b_ada3]
    grid_spec = pltpu.PrefetchScalarGridSpec(
        num_scalar_prefetch=1, grid=(n_steps,), in_specs=in_specs, out_specs=out_specs)
    outs = pl.pallas_call(
        functools.partial(_ctx_mix_kernel, lam_init, ada is not None),
        grid_spec=grid_spec,
        out_shape=out_shape,
        compiler_params=_params(),
        name="ctx_mix",
    )(*args)
    return outs[0], (outs[1] if ada is not None else None)


def _cache_v_ones(cv_ref, cva, cvb):
    first = lax.broadcasted_iota(jnp.int32, (256, PAST_LEN), 0) % 128 < HEAD_DIM
    cv = cv_ref[...]
    cva[...] = jnp.where(first, cv, 1.0).astype(BF)
    cvb[...] = jnp.where(first, 1.0, cv).astype(BF)


def _na_bias_tiles(rpp_ref, t2):
    c = lax.broadcasted_iota(jnp.int32, (GRID_W, 128), 0)
    kc = lax.broadcasted_iota(jnp.int32, (GRID_W, 128), 1) % GRID_W
    c_start = jnp.clip(c - NA_KW // 2, 0, GRID_W - NA_KW)
    inside = jnp.logical_and(kc >= c_start, kc < c_start + NA_KW)
    for h in range(4):
        for dr in range(2 * NA_KH - 1):
            row = jnp.broadcast_to(rpp_ref[h, dr:dr + 1, :], (GRID_W, 128))
            toeplitz = pltpu.roll(row, 128 - (NA_KW - 1), 1, stride=1, stride_axis=0)
            t2[h, dr] = jnp.where(inside, toeplitz, NEG_INF)
        t2[h, 2 * NA_KH - 1] = jnp.full((GRID_W, 128), NEG_INF, F32)


def _lat_na_part(z_ref, ck_ref, cv_ref, t2, o_ref, ckb, cva, cvb):
    ckb[...] = ck_ref[...].astype(BF)
    _cache_v_ones(cv_ref, cva, cvb)
    n_q = NA_QROWS * GRID_W
    n_loc = NA_WIN_ROWS * GRID_W
    lo_q = _lane_lo(n_q)
    lo_w = _lane_lo(n_loc)

    lo_t = _lane_lo(GRID_W)

    def body(p, carry):
        w_row = jnp.clip(NA_QROWS * p - NA_KH // 2, 0, GRID_ROWS - NA_WIN_ROWS)
        q0 = pl.multiple_of(p * n_q, n_q)
        k0 = pl.multiple_of(w_row * GRID_W, NA_QROWS * GRID_W)

        def tile_index(r, j):
            r_start = jnp.clip(r - NA_KH // 2, 0, GRID_ROWS - NA_KH)
            kr = w_row + j
            inside = jnp.logical_and(kr >= r_start, kr < r_start + NA_KH)
            return jnp.where(inside, kr - r + NA_KH - 1, 2 * NA_KH - 1)

        def bias(h):
            rows = []
            for rl in range(NA_QROWS):
                r = NA_QROWS * p + rl
                pieces = [jnp.where(lo_t, t2[h, tile_index(r, 2 * jj)], t2[h, tile_index(r, 2 * jj + 1)])
                          for jj in range(NA_WIN_ROWS // 2)]
                rows.append(jnp.concatenate(pieces, axis=1))
            return jnp.concatenate(rows, axis=0)

        scores = []
        for h in range(4):
            c = 128 * (h // 2)
            q = _split_pair(z_ref[pl.ds(q0, n_q), C_NA_Q + c:C_NA_Q + c + 128], lo_q)[h % 2]
            s_loc = _dot_nt(q, z_ref[pl.ds(k0, n_loc), C_NA_K + c:C_NA_K + c + 128]) + bias(h)
            s_ctx = _dot(q, ckb[c:c + 128, :])
            scores.append((s_loc, s_ctx))
        weights = []
        for s_loc, s_ctx in scores:
            m = _rowmax(s_loc, s_ctx)
            weights.append((jnp.exp(s_loc - m).astype(BF), jnp.exp(s_ctx - m).astype(BF)))
        res = []
        for h, (e_loc, e_ctx) in enumerate(weights):
            c = 128 * (h // 2)
            v = _v_ones_pair(z_ref[pl.ds(k0, n_loc), C_NA_V + c:C_NA_V + c + 128], lo_w)[h % 2]
            cv = (cva, cvb)[h % 2][c:c + 128, :]
            res.append(_dot(e_loc, v) + _dot_nt(e_ctx, cv))
        for j in range(2):
            num, den = _merge_pair(res[2 * j], res[2 * j + 1], lo_q)
            o_ref[pl.ds(q0, n_q), 128 * j:128 * j + 128] = (num / den).astype(BF)
        return carry

    lax.fori_loop(0, GRID_ROWS // NA_QROWS, body, 0)


DIFF_QBLK = 512


def _lat_diff_part(lam_init, z_ref, ck_ref, cv_ref, lamp_ref, subg_ref, o_ref, ckb, cva, cvb, vla, vlb):
    ckb[...] = ck_ref[...].astype(BF)
    _cache_v_ones(cv_ref, cva, cvb)
    lo_v = _lane_lo_wide(DEC_SEQ, 256)
    v_loc = z_ref[:, C_DV:C_DV + 256]
    one = jnp.ones_like(v_loc)
    vla[...] = jnp.where(lo_v, v_loc, one)
    vlb[...] = jnp.where(lo_v, one, v_loc)
    lam = _diff_lambda(lamp_ref, lam_init)
    qblk = DIFF_QBLK
    lo_q = _lane_lo(qblk)

    def body(i, carry):
        q0 = pl.multiple_of(i * qblk, qblk)
        for j in range(2):
            c = 128 * j
            q4 = _diff_quarters(z_ref[pl.ds(q0, qblk), C_DQ + c:C_DQ + c + 128], qblk)
            scores = [(_dot_nt(q, z_ref[:, C_DK + c:C_DK + c + 128]), _dot(q, ckb[c:c + 128, :])) for q in q4]
            weights = []
            for s_loc, s_ctx in scores:
                m = _rowmax(s_loc, s_ctx)
                weights.append((jnp.exp(s_loc - m).astype(BF), jnp.exp(s_ctx - m).astype(BF)))
            res = []
            for t, (e_loc, e_ctx) in enumerate(weights):
                v_loc, v_ctx = ((vla, cva), (vlb, cvb))[t // 2]
                res.append(_dot(e_loc, v_loc[:, c:c + 128]) + _dot_nt(e_ctx, v_ctx[c:c + 128, :]))
            n1, d1 = _merge_pair(res[0], res[2], lo_q)
            n2, d2 = _merge_pair(res[1], res[3], lo_q)
            o = n1 / d1 - lam * (n2 / d2)
            o_ref[pl.ds(q0, qblk), 256 + c:256 + c + 128] = (
                _subln_pair(o, subg_ref[...], lam_init, lo_q).astype(BF))
        return carry

    lax.fori_loop(0, DEC_SEQ // qblk, body, 0)


def _lat_swa_part(sink_ref, z_ref, ck_ref, cv_ref, o_ref, ckd, cva, cvb):
    W = SWA_WINDOW
    n_win = 3 * W
    ones = jnp.ones((HEAD_DIM, PAST_LEN), BF)
    for g in range(2):
        k_g = ck_ref[HEAD_DIM * g:HEAD_DIM * (g + 1), :].astype(BF)
        v_g = cv_ref[HEAD_DIM * g:HEAD_DIM * (g + 1), :].astype(BF)
        r0, r1, r2 = 128 * g, 128 * g + HEAD_DIM, 128 * (g + 1)
        ckd[r0:r1, :] = k_g
        ckd[r1:r2, :] = k_g
        cva[r0:r1, :] = v_g
        cva[r1:r2, :] = ones
        cvb[r0:r1, :] = ones
        cvb[r1:r2, :] = v_g
    lo_q = _lane_lo(W)
    lo_w = _lane_lo(n_win)

    def body(n, carry):
        q0 = pl.multiple_of(n * W, W)
        w0 = pl.multiple_of(jnp.clip((n - 1) * W, 0, DEC_SEQ - n_win), W)
        qpos = q0 + lax.broadcasted_iota(jnp.int32, (W, n_win), 0)
        kpos = w0 + lax.broadcasted_iota(jnp.int32, (W, n_win), 1)
        valid = jnp.abs(qpos - kpos) <= W
        scores = []
        for h in range(4):
            c = 128 * (h // 2)
            q = _split_pair(z_ref[pl.ds(q0, W), C_SQ + c:C_SQ + c + 128], lo_q)[h % 2]
            s_loc = jnp.where(valid, _dot_nt(q, z_ref[pl.ds(w0, n_win), ZB_SK + c:ZB_SK + c + 128]), NEG_INF)
            scores.append((s_loc, _dot(q, ckd[c:c + 128, :])))
        weights, extras = [], []
        for h, (s_loc, s_ctx) in enumerate(scores):
            sink = sink_ref[h]
            m = jnp.maximum(_rowmax(s_loc, s_ctx), sink)
            weights.append((jnp.exp(s_loc - m).astype(BF), jnp.exp(s_ctx - m).astype(BF)))
            extras.append(jnp.exp(sink - m))
        res = []
        for h, (e_loc, e_ctx) in enumerate(weights):
            c = 128 * (h // 2)
            v = _v_ones_pair(z_ref[pl.ds(w0, n_win), ZB_SV + c:ZB_SV + c + 128], lo_w)[h % 2]
            res.append(_dot(e_loc, v) + _dot_nt(e_ctx, (cva, cvb)[h % 2][c:c + 128, :]))
        for g in range(2):
            num, den = _merge_pair(res[2 * g], res[2 * g + 1], lo_q)
            den = den + jnp.where(lo_q, extras[2 * g], extras[2 * g + 1])
            o_ref[pl.ds(q0, W), 768 + 128 * g:768 + 128 * g + 128] = (num / den).astype(BF)
        return carry

    lax.fori_loop(0, DEC_SEQ // W, body, 0)


def _lat_mix_kernel(lam_init, sink_ref, z_ref, nak_ref, nav_ref, dfk_ref, dfv_ref, swk_ref, swv_ref, rpp_ref,
                    lamp_ref, subg_ref, cl_ref, sl_ref, cbd_ref, sbd_ref, wf_ref, o_ref,
                    ck, cva, cvb, vla, vlb, t2):
    @pl.when(pl.program_id(0) == 0)
    def _():
        _na_bias_tiles(rpp_ref, t2)

    _lat_na_part(z_ref, nak_ref, nav_ref, t2, o_ref, ck, cva, cvb)
    _lat_diff_part(lam_init, z_ref, dfk_ref, dfv_ref, lamp_ref, subg_ref, o_ref, ck, cva, cvb, vla, vlb)
    o_c = _fourier(z_ref[:, C_FC:C_FC + 256], cl_ref[...], sl_ref[...], cbd_ref[...], sbd_ref[...], wf_ref[...])
    o_ref[:, 512:768] = o_c.astype(BF)
    _lat_swa_part(sink_ref, z_ref, swk_ref, swv_ref, o_ref, ck, cva, cvb)


def _lat_mix(zb, caches_in, rpp, sink, lamp, subg, cl, sl, cbd, sbd, wf, lam_init, layer):
    full = lambda shape: pl.BlockSpec(shape, lambda b, s: (0,) * len(shape), pipeline_mode=pl.Buffered(1))
    cache = lambda width: pl.BlockSpec((None, None, width, PAST_LEN), lambda b, s: (b, layer, 0, 0))
    grid_spec = pltpu.PrefetchScalarGridSpec(
        num_scalar_prefetch=1,
        grid=(DEC_BATCH,),
        in_specs=[
            pl.BlockSpec((DEC_SEQ, ZB_WIDTH), lambda b, s: (LAT_BLOCK0 + b, 0)),
            cache(256), cache(256), cache(256), cache(256), cache(128), cache(128),
            pl.BlockSpec((None, 4, 2 * NA_KH, 128), lambda b, s: (layer, 0, 0, 0), pipeline_mode=pl.Buffered(1)),
            full((4, DIFF_QK_DIM)), full((1, 2 * HEAD_DIM)),
            full((DEC_SEQ, DEC_SEQ)), full((DEC_SEQ, DEC_SEQ)),
            full((256, 256)), full((256, 256)), full((256, 256)),
        ],
        out_specs=pl.BlockSpec((DEC_SEQ, MIX_WIDTH), lambda b, s: (b, 0)),
        scratch_shapes=[pltpu.VMEM((256, PAST_LEN), BF)] * 3 + [pltpu.VMEM((DEC_SEQ, 256), BF)] * 2
        + [pltpu.VMEM((4, 2 * NA_KH, GRID_W, 128), F32)],
    )
    return pl.pallas_call(
        functools.partial(_lat_mix_kernel, lam_init),
        grid_spec=grid_spec,
        out_shape=jax.ShapeDtypeStruct((N_LAT_TOK, MIX_WIDTH), BF),
        compiler_params=_params(),
        name="lat_mix",
    )(sink, zb, *caches_in, rpp, lamp, subg, cl, sl, cbd, sbd, wf)


def _dft_tables(n):
    j = np.arange(n)
    ang = 2.0 * np.pi * ((j[:, None] * j[None, :]) % n) / n
    return np.cos(ang) / np.sqrt(n), np.sin(ang) / np.sqrt(n)


def _block_diag4(m):
    out = np.zeros((256, 256), m.dtype)
    for g in range(4):
        out[64 * g:64 * g + 64, 64 * g:64 * g + 64] = m
    return out


def _rope_tables(n_axis_dims):
    half = n_axis_dims // 2
    inv = ROPE_BASE ** (-np.arange(half, dtype=np.float64) / half)
    t = np.arange(DEC_SEQ)
    lane = np.arange(128)
    w = lane % n_axis_dims
    is_col = (lane // n_axis_dims) % 2 == 1
    pos = np.where(is_col[None, :], (t % GRID_W)[:, None], (t // GRID_W)[:, None]).astype(np.float64)
    ang = pos * inv[w % half][None, :]
    sign = np.where(w < half, -1.0, 1.0)[None, :]
    cos = np.concatenate([np.cos(ang), np.ones((ROW_TILE, 128))], axis=0)
    sin = np.concatenate([np.sin(ang) * sign, np.zeros((ROW_TILE, 128))], axis=0)
    return jnp.asarray(cos, F32), jnp.asarray(sin, F32)


def _pad_rpb_rows(rpb):
    n_dc = rpb.shape[-1]
    half = jnp.pad(rpb, ((0, 0), (0, 0), (0, 1), (0, GRID_W - n_dc)))
    return jnp.concatenate([half, half], axis=-1)


def kernel(x_prompt, x_sample, cache_na_k, cache_na_v, cache_diff_k, cache_diff_v, cache_swa_k, cache_swa_v, c, c_ctx, w_ada, b_ada, norm1_g, norm2_g, w_in, na_rpb, diff_lq1, diff_lk1, diff_lq2, diff_lk2, diff_subln_g, w_fourier, swa_sink, w_out, w_mlp1, w_mlp2, final_g):
    xp = x_prompt.reshape(N_CTX_TOK, D_MODEL)
    xs = x_sample.reshape(N_LAT_TOK, D_MODEL)
    cond = jnp.zeros((MOD_ROWS, D_MODEL), F32).at[0].set(c_ctx).at[1:1 + DEC_BATCH].set(c)
    b_ada3 = b_ada.reshape(DEPTH, 1, 6 * D_MODEL)
    mods = _ada(cond, w_ada, b_ada3, 0)

    cl_p, sl_p = _dft_tables(SEQ)
    cl_s, sl_s = _dft_tables(DEC_SEQ)
    c64, s64 = _dft_tables(64)
    cl_p, sl_p, cl_s, sl_s, cbd, sbd = (
        jnp.asarray(a, F32).astype(BF) for a in (cl_p, sl_p, cl_s, sl_s, _block_diag4(c64), _block_diag4(s64)))
    ropes = _rope_tables(16) + _rope_tables(32)
    rpp = _pad_rpb_rows(na_rpb)
    wf_bf = w_fourier.astype(BF)

    ck_na, cv_na, ck_df, cv_df, ck_sw, cv_sw = (
        a.transpose(0, 1, 3, 4, 2).reshape(DEC_BATCH, DEPTH, -1, PAST_LEN)
        for a in (cache_na_k, cache_na_v, cache_diff_k, cache_diff_v, cache_swa_k, cache_swa_v))

    _, _, zb, caches = _trunk(xp, xs, ropes, proj=(0, mods, norm1_g[0], w_in, None))
    for l in range(DEPTH):
        lam_init = 0.8 - 0.6 * math.exp(-0.3 * l)
        lamp = jnp.stack([diff_lq1[l], diff_lk1[l], diff_lq2[l], diff_lk2[l]], axis=0)
        subg = jnp.tile(diff_subln_g[l].reshape(1, HEAD_DIM), (1, 2))
        has_next = l + 1 < DEPTH

        cat_p, mods_next = _ctx_mix(zb, swa_sink[l], lamp, subg, cl_p, sl_p, cbd, sbd, wf_bf[l], lam_init,
                                    ada=(cond, w_ada, b_ada3, l + 1) if has_next else None)
        cat_s = _lat_mix(zb, (ck_na, cv_na, ck_df, cv_df, ck_sw, cv_sw), rpp, swa_sink[l], lamp, subg,
                         cl_s, sl_s, cbd, sbd, wf_bf[l], lam_init, l)
        nxt = (l + 1, mods_next, norm1_g[l + 1], w_in, caches) if has_next else None
        xp, xs, zb, new_caches = _trunk(
            xp, xs, ropes, mlp=(l, mods, cat_p, cat_s, norm2_g[l], final_g, w_out, w_mlp1, w_mlp2), proj=nxt)
        caches = new_caches if has_next else caches
        mods = mods_next

    y_prompt = xp.reshape(BATCH, SEQ, D_MODEL)
    y_sample = xs.reshape(DEC_BATCH, DEC_SEQ, D_MODEL)
    new = [a.reshape(BATCH, DEPTH, SEQ, a.shape[-1] // HEAD_DIM, HEAD_DIM) for a in caches]
    return (y_prompt, y_sample) + tuple(new)
```

```python
import functools
import math

import numpy as np
import jax
import jax.numpy as jnp
from jax import lax
from jax.experimental import pallas as pl
from jax.experimental.pallas import tpu as pltpu

D_MODEL = 1024
BATCH = 16
SEQ = 256
DEPTH = 4
DEC_BATCH = 2
DEC_SEQ = 1024
PAST_LEN = 512
GRID_W = 64
GRID_ROWS = DEC_SEQ // GRID_W
HEAD_DIM = 64
NA_KH = 8
NA_KW = 16
DIFF_QK_DIM = 32
SWA_WINDOW = 128
D_FF = 4 * D_MODEL
ROPE_BASE = 10000.0
NORM_EPS = 1e-6
NEG_INF = -1e30
IN_WIDTH = 2304
MIX_WIDTH = 1024

N_CTX_TOK = BATCH * SEQ
N_LAT_TOK = DEC_BATCH * DEC_SEQ
N_TOK = N_CTX_TOK + N_LAT_TOK
ROW_TILE = 256
N_ROW_TILES = N_TOK // ROW_TILE
N_CTX_TILES = N_CTX_TOK // ROW_TILE
LAT_TILES_PER_REQ = DEC_SEQ // ROW_TILE
LAT_BLOCK0 = N_CTX_TOK // DEC_SEQ
MOD_ROWS = 8

C_NA_Q, C_NA_K, C_NA_V = 0, 256, 512
C_DQ, C_DK, C_DV = 768, 1024, 1280
C_FC = 1536
C_SQ, C_SK, C_SV = 1792, 2048, 2176
CACHE_COLS = ((C_NA_K, 256), (C_NA_V, 256), (C_DK, 256), (C_DV, 256), (C_SK, 128), (C_SV, 128))
ZB_SK, ZB_SV = 2048, 2304
ZB_WIDTH = 2560

NA_QROWS = 2
NA_WIN_ROWS = 10

BF = jnp.bfloat16
F32 = jnp.float32
VMEM_LIMIT = 56 * 1024 * 1024


def _dot(a, b):
    return jnp.dot(a, b, preferred_element_type=F32)


def _dot_nt(a, b):
    return lax.dot_general(a, b, (((1,), (1,)), ((), ())), preferred_element_type=F32)


def _rmsnorm(x, g):
    ms = jnp.mean(x * x, axis=-1, keepdims=True)
    return x * lax.rsqrt(ms + NORM_EPS) * g


def _params(n_grid=1):
    return pltpu.CompilerParams(dimension_semantics=("arbitrary",) * n_grid, vmem_limit_bytes=VMEM_LIMIT)


def _rowmax(*parts):
    m = jnp.max(parts[0], axis=-1, keepdims=True)
    for p in parts[1:]:
        m = jnp.maximum(m, jnp.max(p, axis=-1, keepdims=True))
    return m


def _ada_kernel(cond_ref, w_ref, b_ref, o_ref):
    cnd = cond_ref[...]
    s = cnd / (1.0 + jnp.exp(-cnd))
    o_ref[...] = _dot(s.astype(BF), w_ref[...].astype(BF)) + b_ref[...]


def _ada_specs(layer, n_steps, index_of_step):
    tn = 6 * D_MODEL // n_steps
    col = lambda *g: index_of_step(*g)
    in_specs = [
        pl.BlockSpec((MOD_ROWS, D_MODEL), lambda *g: (0, 0)),
        pl.BlockSpec((None, D_MODEL, tn), lambda *g: (layer, 0, col(*g))),
        pl.BlockSpec((None, 1, tn), lambda *g: (layer, 0, col(*g))),
    ]
    return in_specs, pl.BlockSpec((MOD_ROWS, tn), lambda *g: (0, col(*g)))


def _ada(cond, w_ada, b_ada3, layer):
    in_specs, out_spec = _ada_specs(layer, 6, lambda j: j)
    return pl.pallas_call(
        _ada_kernel,
        grid=(6,),
        in_specs=in_specs,
        out_specs=out_spec,
        out_shape=jax.ShapeDtypeStruct((MOD_ROWS, 6 * D_MODEL), F32),
        compiler_params=_params(),
        name="ada",
    )(cond, w_ada, b_ada3)


def _mod_row(t):
    return jnp.where(t < N_CTX_TILES, 0, 1 + (t - N_CTX_TILES) // LAT_TILES_PER_REQ)


def _ctx_tile(t):
    return jnp.minimum(t, N_CTX_TILES - 1)


def _lat_tile(t):
    return jnp.maximum(t - N_CTX_TILES, 0)


TRUNK_CAST_STEPS = 8
FF_CHUNK = 1024
TRUNK_VMEM_LIMIT = 60 * 1024 * 1024


def _rope(x, cos, sin_signed, shift, first):
    partner = jnp.where(first, pltpu.roll(x, 128 - shift, 1), pltpu.roll(x, shift, 1))
    return x * cos + partner * sin_signed


def _projection_stages(hb, wb, t, is_ctx, zb_ref, cache_refs, rope_refs):
    cosd_ref, sind_ref, coss_ref, sins_ref = rope_refs
    dscale = DIFF_QK_DIM ** -0.5
    p0 = pl.multiple_of(
        jnp.where(is_ctx, DEC_SEQ, ((t - N_CTX_TILES) % LAT_TILES_PER_REQ) * ROW_TILE), ROW_TILE)
    lane = lax.broadcasted_iota(jnp.int32, (ROW_TILE, 128), 1)
    first_d = (lane % 16) < 8
    first_s = (lane % 32) < 16
    lo = lane < HEAD_DIM
    cosd, sind = cosd_ref[pl.ds(p0, ROW_TILE), :], sind_ref[pl.ds(p0, ROW_TILE), :]
    coss, sins = coss_ref[pl.ds(p0, ROW_TILE), :], sins_ref[pl.ds(p0, ROW_TILE), :]

    def cols(c0, width):
        return _dot(hb(), wb[:, c0:c0 + width])

    def put(c0, width, val):
        zb_ref[:, c0:c0 + width] = val.astype(BF)

    def put_dup(c0, val):
        r = pltpu.roll(val, HEAD_DIM, 1)
        put(c0, 128, jnp.where(lo, val, r))
        put(c0 + 128, 128, jnp.where(lo, r, val))

    kept = {}

    def na_q():
        put(C_NA_Q, 256, cols(C_NA_Q, 256) * 0.125)

    def na_kv():
        kept["nakv"] = cols(C_NA_K, 512)
        put(C_NA_K, 512, kept["nakv"])

    def diff_q():
        z = cols(C_DQ, 256)
        for j in range(2):
            put(C_DQ + 128 * j, 128, _rope(z[:, 128 * j:128 * j + 128], cosd, sind, 8, first_d) * dscale)

    def diff_k():
        kept["dk"] = cols(C_DK, 256)
        for j in range(2):
            put(C_DK + 128 * j, 128, _rope(kept["dk"][:, 128 * j:128 * j + 128], cosd, sind, 8, first_d))

    def diff_v_fourier():
        kept["dvfc"] = cols(C_DV, 512)
        put(C_DV, 512, kept["dvfc"])

    def swa_q():
        z = cols(C_SQ, 256)
        for j in range(2):
            put(C_SQ + 128 * j, 128, _rope(z[:, 128 * j:128 * j + 128], coss, sins, 16, first_s) * 0.125)

    def swa_kv():
        kept["skv"] = cols(C_SK, 256)
        put_dup(ZB_SK, _rope(kept["skv"][:, 0:128], coss, sins, 16, first_s))
        put_dup(ZB_SV, kept["skv"][:, 128:256])

    def finish():
        new = (kept["nakv"][:, 0:256], kept["nakv"][:, 256:512], kept["dk"], kept["dvfc"][:, 0:256],
               kept["skv"][:, 0:128], kept["skv"][:, 128:256])
        for ref, val in zip(cache_refs, new):
            ref[...] = jnp.where(is_ctx, val, ref[...])

    return [na_q, na_kv, diff_q, diff_k, diff_v_fourier, swa_q, swa_kv], finish


def _trunk_kernel(do_mlp, do_proj, last, n_alias, *refs):
    xp_ref, xs_ref = refs[:2]
    pos = 2
    if do_mlp:
        catp_ref, cats_ref, mm_ref, g2_ref, fg_ref, wo_ref, w1_ref, w2_ref = refs[pos:pos + 8]
        pos += 8
    if do_proj:
        pm_ref, g1_ref, win_ref = refs[pos:pos + 3]
        rope_refs = refs[pos + 3:pos + 7]
        pos += 7 + n_alias
    if do_mlp:
        op_ref, os_ref = refs[pos:pos + 2]
        pos += 2
    if do_proj:
        zb_ref = refs[pos]
        cache_refs = refs[pos + 1:pos + 7]
        pos += 7
    if do_mlp:
        wob, w1b, w2b = refs[pos:pos + 3]
        pos += 3
    if do_proj:
        wb = refs[pos]
        pos += 1
    skew = do_mlp and do_proj
    if skew:
        hprev = refs[pos]
    s = pl.program_id(0)

    @pl.when(s < TRUNK_CAST_STEPS)
    def _():
        def cast(dst, src):
            rows = src.shape[0]
            dst[pl.ds(pl.multiple_of(s * rows, rows), rows), :] = src[...].astype(BF)

        if do_mlp:
            cast(wob, wo_ref)
            cast(w1b, w1_ref)
            cast(w2b, w2_ref)
        if do_proj:
            cast(wb, win_ref)
        if skew:
            @pl.when(s == 0)
            def _():
                hprev[...] = jnp.zeros_like(hprev)

    def tile_step(t, proj_stages, proj_finish):
        pending = list(proj_stages)

        def issue(n):
            for _ in range(min(n, len(pending))):
                pending.pop(0)()

        is_ctx = t < N_CTX_TILES
        row = _mod_row(t)
        x = jnp.where(is_ctx, xp_ref[...], xs_ref[...])
        if do_mlp:
            issue(1)
            gt1 = mm_ref[pl.ds(row, 1), 2 * D_MODEL:3 * D_MODEL]
            sh2 = mm_ref[pl.ds(row, 1), 3 * D_MODEL:4 * D_MODEL]
            sc2 = mm_ref[pl.ds(row, 1), 4 * D_MODEL:5 * D_MODEL]
            gt2 = mm_ref[pl.ds(row, 1), 5 * D_MODEL:6 * D_MODEL]
            cat = jnp.where(is_ctx, catp_ref[...], cats_ref[...])
            x = x + gt1 * _dot(cat, wob[...])
            issue(1)
            hh = (_rmsnorm(x, g2_ref[...]) * (1.0 + sc2) + sh2).astype(BF)
            acc = jnp.zeros((ROW_TILE, D_MODEL), F32)
            for c in range(D_FF // FF_CHUNK):
                a = jnp.maximum(_dot(hh, w1b[:, c * FF_CHUNK:(c + 1) * FF_CHUNK]), 0.0)
                acc = acc + _dot((a * a).astype(BF), w2b[c * FF_CHUNK:(c + 1) * FF_CHUNK, :])
                if c < 2:
                    issue(1)
            issue(1)
            x = x + gt2 * acc
            out = _rmsnorm(x, fg_ref[...]) if last else x
        hb = None
        if do_proj:
            issue(1)
            shift = pm_ref[pl.ds(row, 1), 0:D_MODEL]
            scale = pm_ref[pl.ds(row, 1), D_MODEL:2 * D_MODEL]
            hb = (_rmsnorm(x, g1_ref[...]) * (1.0 + scale) + shift).astype(BF)
        issue(len(pending))
        if proj_finish is not None:
            proj_finish()
        if do_mlp:
            op_ref[...] = jnp.where(is_ctx, out, op_ref[...])
            os_ref[...] = jnp.where(is_ctx, os_ref[...], out)
        return hb

    def projection(hb, t):
        return _projection_stages(hb, wb, t, t < N_CTX_TILES, zb_ref, cache_refs, rope_refs)

    @pl.when(s >= TRUNK_CAST_STEPS)
    def _():
        t = s - TRUNK_CAST_STEPS
        if not skew:
            hb = tile_step(t, [], None)
            if do_proj:
                stages, finish = projection(lambda: hb, t)
                for stage in stages:
                    stage()
                finish()
        else:
            u = jnp.maximum(t - 1, 0)

            @pl.when(t < N_ROW_TILES)
            def _():
                stages, finish = projection(lambda: hprev[...], u)
                hprev[...] = tile_step(t, stages, finish)

            @pl.when(t >= N_ROW_TILES)
            def _():
                stages, finish = projection(lambda: hprev[...], u)
                for stage in stages:
                    stage()
                finish()


def _trunk(xp, xs, ropes, mlp=None, proj=None):
    do_mlp, do_proj = mlp is not None, proj is not None
    last = do_mlp and not do_proj
    skew = do_mlp and do_proj
    clamp_tile = lambda s: jnp.clip(s - TRUNK_CAST_STEPS, 0, N_ROW_TILES - 1)
    step = lambda f: (lambda s: f(clamp_tile(s)))
    step_proj = (lambda f: (lambda s: f(clamp_tile(s - 1)))) if skew else step
    const = lambda shape: pl.BlockSpec(shape, lambda s: (0,) * len(shape), pipeline_mode=pl.Buffered(1))
    chunk = lambda layer, rows, cols: pl.BlockSpec(
        (None, rows // TRUNK_CAST_STEPS, cols), lambda s: (layer, jnp.minimum(s, TRUNK_CAST_STEPS - 1), 0))
    mod_spec = const((MOD_ROWS, 6 * D_MODEL))
    x_specs = [pl.BlockSpec((ROW_TILE, D_MODEL), step(lambda t: (_ctx_tile(t), 0))),
               pl.BlockSpec((ROW_TILE, D_MODEL), step(lambda t: (_lat_tile(t), 0)))]
    args, in_specs = [xp, xs], list(x_specs)
    out_specs, out_shape, scratch, aliases = [], [], [], {}
    n_alias = 0
    if do_mlp:
        layer, mods_l, cat_p, cat_s, g2, final_g, wo, w1, w2 = mlp
        args += [cat_p, cat_s, mods_l, g2.reshape(1, D_MODEL), final_g.reshape(1, D_MODEL), wo, w1, w2]
        in_specs += [pl.BlockSpec((ROW_TILE, MIX_WIDTH), step(lambda t: (_ctx_tile(t), 0))),
                     pl.BlockSpec((ROW_TILE, MIX_WIDTH), step(lambda t: (_lat_tile(t), 0))), mod_spec,
                     const((1, D_MODEL)), const((1, D_MODEL)),
                     chunk(layer, MIX_WIDTH, D_MODEL), chunk(layer, D_MODEL, D_FF), chunk(layer, D_FF, D_MODEL)]
    if do_proj:
        p_layer, mods_p, g1, w_in, caches = proj
        args += [mods_p, g1.reshape(1, D_MODEL), w_in, *ropes]
        in_specs += [mod_spec, const((1, D_MODEL)), chunk(p_layer, D_MODEL, IN_WIDTH)]
        in_specs += [const((DEC_SEQ + ROW_TILE, 128))] * 4
        if caches is not None:
            n_alias = 6
            n_out_before = 2 if do_mlp else 0
            aliases = {len(args) + i: n_out_before + 1 + i for i in range(6)}
            args += list(caches)
            in_specs += [pl.BlockSpec(memory_space=pl.ANY)] * 6
    if do_mlp:
        out_specs += x_specs
        out_shape += [jax.ShapeDtypeStruct((N_CTX_TOK, D_MODEL), F32),
                      jax.ShapeDtypeStruct((N_LAT_TOK, D_MODEL), F32)]
        scratch += [pltpu.VMEM((MIX_WIDTH, D_MODEL), BF), pltpu.VMEM((D_MODEL, D_FF), BF),
                    pltpu.VMEM((D_FF, D_MODEL), BF)]
    if do_proj:
        out_specs += [pl.BlockSpec((ROW_TILE, ZB_WIDTH), step_proj(lambda t: (t, 0)))] + [
            pl.BlockSpec((None, None, SEQ, width), step_proj(lambda t: (_ctx_tile(t), p_layer, 0, 0)))
            for _, width in CACHE_COLS]
        out_shape += [jax.ShapeDtypeStruct((N_TOK, ZB_WIDTH), BF)] + [
            jax.ShapeDtypeStruct((BATCH, DEPTH, SEQ, width), F32) for _, width in CACHE_COLS]
        scratch += [pltpu.VMEM((D_MODEL, IN_WIDTH), BF)]
    if skew:
        scratch += [pltpu.VMEM((ROW_TILE, D_MODEL), BF)]
    outs = pl.pallas_call(
        functools.partial(_trunk_kernel, do_mlp, do_proj, last, n_alias),
        grid=(TRUNK_CAST_STEPS + N_ROW_TILES + (1 if skew else 0),),
        in_specs=in_specs,
        out_specs=out_specs,
        out_shape=out_shape,
        scratch_shapes=scratch,
        input_output_aliases=aliases,
        compiler_params=pltpu.CompilerParams(dimension_semantics=("arbitrary",),
                                             vmem_limit_bytes=TRUNK_VMEM_LIMIT),
        name="trunk_mlp_proj" if (do_mlp and do_proj) else ("trunk_mlp" if do_mlp else "trunk_proj"),
    )(*args)
    outs = list(outs)
    if do_mlp:
        xp, xs = outs[:2]
        outs = outs[2:]
    zb, new_caches = (outs[0], tuple(outs[1:7])) if do_proj else (None, None)
    return xp, xs, zb, new_caches


def _diff_lambda(lamp_ref, lam_init):
    a = jnp.sum(lamp_ref[0:1, :] * lamp_ref[1:2, :], axis=-1, keepdims=True)
    b = jnp.sum(lamp_ref[2:3, :] * lamp_ref[3:4, :], axis=-1, keepdims=True)
    return jnp.exp(a) - jnp.exp(b) + lam_init


def _lane_lo(n):
    return lax.broadcasted_iota(jnp.int32, (n, 128), 1) < HEAD_DIM


def _lane_lo_wide(n, width):
    return lax.broadcasted_iota(jnp.int32, (n, width), 1) % 128 < HEAD_DIM


def _split_pair(x, lo):
    zero = jnp.zeros_like(x)
    return jnp.where(lo, x, zero), jnp.where(lo, zero, x)


def _v_ones_pair(v, lo):
    one = jnp.ones_like(v)
    return jnp.where(lo, v, one), jnp.where(lo, one, v)


def _v_stack(v, lo):
    zero = jnp.zeros_like(v)
    ones_a = jnp.where(lo, 1.0, 0.0).astype(v.dtype)
    ones_b = jnp.where(lo, 0.0, 1.0).astype(v.dtype)
    top = jnp.concatenate([jnp.where(lo, v, zero), ones_a], axis=1)
    bottom = jnp.concatenate([jnp.where(lo, zero, v), ones_b], axis=1)
    return jnp.concatenate([top, bottom], axis=0)


def _merge_pair(a, b, lo):
    return jnp.where(lo, a, b), pltpu.roll(jnp.where(lo, b, a), HEAD_DIM, 1)


def _subln_pair(o, g2, lam_init, lo):
    sq = o * o
    ms_a = jnp.sum(jnp.where(lo, sq, 0.0), axis=-1, keepdims=True)
    ms_b = jnp.sum(jnp.where(lo, 0.0, sq), axis=-1, keepdims=True)
    ms = jnp.where(lo, ms_a, ms_b) * (1.0 / HEAD_DIM)
    return o * lax.rsqrt(ms + NORM_EPS) * g2 * (1.0 - lam_init)


def _diff_quarters(q, n):
    quarter = lax.broadcasted_iota(jnp.int32, (n, 128), 1) // DIFF_QK_DIM
    zero = jnp.zeros_like(q)
    return [jnp.where(quarter == i, q, zero) for i in range(4)]


def _fourier(x_bf, cl, sl, cbd, sbd, wf):
    xc = _dot(x_bf, cbd).astype(BF)
    xs = _dot(x_bf, sbd).astype(BF)
    y = _dot(cl, xc) - _dot(sl, xs)
    return _dot(y.astype(BF), wf)


CTX_UNIT_GROUP = 4
CTX_REQ_PER_STEP = 2


def _ctx_mix_kernel(lam_init, with_ada, sink_ref, z_ref, lamp_ref, subg_ref, cl_ref, sl_ref, cbd_ref, sbd_ref,
                    wf_ref, *rest):
    if with_ada:
        cond_ref, wa_ref, ba_ref, o_ref, mods_ref = rest
        _ada_kernel(cond_ref, wa_ref, ba_ref, mods_ref)
    else:
        o_ref, = rest
    for r in range(CTX_REQ_PER_STEP):
        rows = pl.ds(r * SEQ, SEQ)
        _ctx_request(lam_init, sink_ref, z_ref.at[rows], lamp_ref, subg_ref, cl_ref, sl_ref, cbd_ref, sbd_ref,
                     wf_ref, o_ref.at[rows])


def _ctx_request(lam_init, sink_ref, z_ref, lamp_ref, subg_ref, cl_ref, sl_ref, cbd_ref, sbd_ref, wf_ref, o_ref):
    lo = _lane_lo(SEQ)

    pairs = []
    for j in range(2):
        c = 128 * j
        qs = _split_pair(z_ref[:, C_NA_Q + c:C_NA_Q + c + 128], lo)
        pairs.append((qs[0], qs[1], C_NA_K + c, _v_stack(z_ref[:, C_NA_V + c:C_NA_V + c + 128], lo), None))
    for j in range(2):
        c = 128 * j
        q4 = _diff_quarters(z_ref[:, C_DQ + c:C_DQ + c + 128], SEQ)
        v2 = _v_stack(z_ref[:, C_DV + c:C_DV + c + 128], lo)
        pairs.append((q4[0], q4[2], C_DK + c, v2, None))
        pairs.append((q4[1], q4[3], C_DK + c, v2, None))
    for g in range(2):
        c = 128 * g
        qs = _split_pair(z_ref[:, C_SQ + c:C_SQ + c + 128], lo)
        pairs.append((qs[0], qs[1], ZB_SK + c, _v_stack(z_ref[:, ZB_SV + c:ZB_SV + c + 128], lo),
                      (sink_ref[2 * g], sink_ref[2 * g + 1])))

    res = []
    for p0 in range(0, len(pairs), CTX_UNIT_GROUP // 2):
        group = pairs[p0:p0 + CTX_UNIT_GROUP // 2]
        scores = [[_dot_nt(q, z_ref[:, kc:kc + 128]) for q in (qa, qb)] for qa, qb, kc, _, _ in group]
        weights, extras = [], []
        for ss, (_, _, _, _, sinks) in zip(scores, group):
            es, xs = [], []
            for i, s in enumerate(ss):
                m = _rowmax(s)
                if sinks is not None:
                    m = jnp.maximum(m, sinks[i])
                    xs.append(jnp.exp(sinks[i] - m))
                es.append(jnp.exp(s - m).astype(BF))
            weights.append(jnp.concatenate(es, axis=1))
            extras.append(xs)
        for e2, xs, (_, _, _, v2, _) in zip(weights, extras, group):
            r = _dot(e2, v2)
            num, den = r[:, 0:128], r[:, 128:256]
            if xs:
                den = den + jnp.where(lo, xs[0], xs[1])
            res.append(num / den)

    for j in range(2):
        o_ref[:, 128 * j:128 * j + 128] = res[j].astype(BF)

    lam = _diff_lambda(lamp_ref, lam_init)
    for j in range(2):
        o = res[2 + 2 * j] - lam * res[3 + 2 * j]
        o_ref[:, 256 + 128 * j:384 + 128 * j] = _subln_pair(o, subg_ref[...], lam_init, lo).astype(BF)

    o_c = _fourier(z_ref[:, C_FC:C_FC + 256], cl_ref[...], sl_ref[...], cbd_ref[...], sbd_ref[...], wf_ref[...])
    o_ref[:, 512:768] = o_c.astype(BF)

    for g in range(2):
        o_ref[:, 768 + 128 * g:896 + 128 * g] = res[6 + g].astype(BF)


def _ctx_mix(zb, sink, lamp, subg, cl, sl, cbd, sbd, wf, lam_init, ada=None):
    full = lambda shape: pl.BlockSpec(shape, lambda b, s: (0,) * len(shape))
    n_steps = BATCH // CTX_REQ_PER_STEP
    in_specs = [
        pl.BlockSpec((CTX_REQ_PER_STEP * SEQ, ZB_WIDTH), lambda b, s: (b, 0)),
        full((4, DIFF_QK_DIM)),
        full((1, 2 * HEAD_DIM)),
        full((SEQ, SEQ)), full((SEQ, SEQ)),
        full((256, 256)), full((256, 256)), full((256, 256)),
    ]
    out_specs = [pl.BlockSpec((CTX_REQ_PER_STEP * SEQ, MIX_WIDTH), lambda b, s: (b, 0))]
    out_shape = [jax.ShapeDtypeStruct((N_CTX_TOK, MIX_WIDTH), BF)]
    args = [sink, zb, lamp, subg, cl, sl, cbd, sbd, wf]
    if ada is not None:
        cond, w_ada, b_ada3, layer = ada
        ada_in, ada_out = _ada_specs(layer, n_steps, lambda b, s: b)
        in_specs += ada_in
        out_specs.append(ada_out)
        out_shape.append(jax.ShapeDtypeStruct((MOD_ROWS, 6 * D_MODEL), F32))
        args += [cond, w_ada, b_ada3]
    grid_spec = pltpu.PrefetchScalarGridSpec(
        num_scalar_prefetch=1, grid=(n_steps,), in_specs=in_specs, out_specs=out_specs)
    outs = pl.pallas_call(
        functools.partial(_ctx_mix_kernel, lam_init, ada is not None),
        grid_spec=grid_spec,
        out_shape=out_shape,
        compiler_params=_params(),
        name="ctx_mix",
    )(*args)
    return outs[0], (outs[1] if ada is not None else None)


def _cache_v_ones(cv_ref, cva, cvb):
    first = lax.broadcasted_iota(jnp.int32, (256, PAST_LEN), 0) % 128 < HEAD_DIM
    cv = cv_ref[...]
    cva[...] = jnp.where(first, cv, 1.0).astype(BF)
    cvb[...] = jnp.where(first, 1.0, cv).astype(BF)


def _na_bias_tiles(rpp_ref, t2):
    c = lax.broadcasted_iota(jnp.int32, (GRID_W, 128), 0)
    kc = lax.broadcasted_iota(jnp.int32, (GRID_W, 128), 1) % GRID_W
    c_start = jnp.clip(c - NA_KW // 2, 0, GRID_W - NA_KW)
    inside = jnp.logical_and(kc >= c_start, kc < c_start + NA_KW)
    for h in range(4):
        for dr in range(2 * NA_KH - 1):
            row = jnp.broadcast_to(rpp_ref[h, dr:dr + 1, :], (GRID_W, 128))
            toeplitz = pltpu.roll(row, 128 - (NA_KW - 1), 1, stride=1, stride_axis=0)
            t2[h, dr] = jnp.where(inside, toeplitz, NEG_INF)
        t2[h, 2 * NA_KH - 1] = jnp.full((GRID_W, 128), NEG_INF, F32)


def _lat_na_part(z_ref, ck_ref, cv_ref, t2, o_ref, ckb, cva, cvb):
    ckb[...] = ck_ref[...].astype(BF)
    _cache_v_ones(cv_ref, cva, cvb)
    n_q = NA_QROWS * GRID_W
    n_loc = NA_WIN_ROWS * GRID_W
    lo_q = _lane_lo(n_q)
    lo_w = _lane_lo(n_loc)

    lo_t = _lane_lo(GRID_W)

    def body(p, carry):
        w_row = jnp.clip(NA_QROWS * p - NA_KH // 2, 0, GRID_ROWS - NA_WIN_ROWS)
        q0 = pl.multiple_of(p * n_q, n_q)
        k0 = pl.multiple_of(w_row * GRID_W, NA_QROWS * GRID_W)

        def tile_index(r, j):
            r_start = jnp.clip(r - NA_KH // 2, 0, GRID_ROWS - NA_KH)
            kr = w_row + j
            inside = jnp.logical_and(kr >= r_start, kr < r_start + NA_KH)
            return jnp.where(inside, kr - r + NA_KH - 1, 2 * NA_KH - 1)

        def bias(h):
            rows = []
            for rl in range(NA_QROWS):
                r = NA_QROWS * p + rl
                pieces = [jnp.where(lo_t, t2[h, tile_index(r, 2 * jj)], t2[h, tile_index(r, 2 * jj + 1)])
                          for jj in range(NA_WIN_ROWS // 2)]
                rows.append(jnp.concatenate(pieces, axis=1))
            return jnp.concatenate(rows, axis=0)

        scores = []
        for h in range(4):
            c = 128 * (h // 2)
            q = _split_pair(z_ref[pl.ds(q0, n_q), C_NA_Q + c:C_NA_Q + c + 128], lo_q)[h % 2]
            s_loc = _dot_nt(q, z_ref[pl.ds(k0, n_loc), C_NA_K + c:C_NA_K + c + 128]) + bias(h)
            s_ctx = _dot(q, ckb[c:c + 128, :])
            scores.append((s_loc, s_ctx))
        weights = []
        for s_loc, s_ctx in scores:
            m = _rowmax(s_loc, s_ctx)
            weights.append((jnp.exp(s_loc - m).astype(BF), jnp.exp(s_ctx - m).astype(BF)))
        res = []
        for h, (e_loc, e_ctx) in enumerate(weights):
            c = 128 * (h // 2)
            v = _v_ones_pair(z_ref[pl.ds(k0, n_loc), C_NA_V + c:C_NA_V + c + 128], lo_w)[h % 2]
            cv = (cva, cvb)[h % 2][c:c + 128, :]
            res.append(_dot(e_loc, v) + _dot_nt(e_ctx, cv))
        for j in range(2):
            num, den = _merge_pair(res[2 * j], res[2 * j + 1], lo_q)
            o_ref[pl.ds(q0, n_q), 128 * j:128 * j + 128] = (num / den).astype(BF)
        return carry

    lax.fori_loop(0, GRID_ROWS // NA_QROWS, body, 0)


DIFF_QBLK = 512


def _lat_diff_part(lam_init, z_ref, ck_ref, cv_ref, lamp_ref, subg_ref, o_ref, ckb, cva, cvb, vla, vlb):
    ckb[...] = ck_ref[...].astype(BF)
    _cache_v_ones(cv_ref, cva, cvb)
    lo_v = _lane_lo_wide(DEC_SEQ, 256)
    v_loc = z_ref[:, C_DV:C_DV + 256]
    one = jnp.ones_like(v_loc)
    vla[...] = jnp.where(lo_v, v_loc, one)
    vlb[...] = jnp.where(lo_v, one, v_loc)
    lam = _diff_lambda(lamp_ref, lam_init)
    qblk = DIFF_QBLK
    lo_q = _lane_lo(qblk)

    def body(i, carry):
        q0 = pl.multiple_of(i * qblk, qblk)
        for j in range(2):
            c = 128 * j
            q4 = _diff_quarters(z_ref[pl.ds(q0, qblk), C_DQ + c:C_DQ + c + 128], qblk)
            scores = [(_dot_nt(q, z_ref[:, C_DK + c:C_DK + c + 128]), _dot(q, ckb[c:c + 128, :])) for q in q4]
            weights = []
            for s_loc, s_ctx in scores:
                m = _rowmax(s_loc, s_ctx)
                weights.append((jnp.exp(s_loc - m).astype(BF), jnp.exp(s_ctx - m).astype(BF)))
            res = []
            for t, (e_loc, e_ctx) in enumerate(weights):
                v_loc, v_ctx = ((vla, cva), (vlb, cvb))[t // 2]
                res.append(_dot(e_loc, v_loc[:, c:c + 128]) + _dot_nt(e_ctx, v_ctx[c:c + 128, :]))
            n1, d1 = _merge_pair(res[0], res[2], lo_q)
            n2, d2 = _merge_pair(res[1], res[3], lo_q)
            o = n1 / d1 - lam * (n2 / d2)
            o_ref[pl.ds(q0, qblk), 256 + c:256 + c + 128] = (
                _subln_pair(o, subg_ref[...], lam_init, lo_q).astype(BF))
        return carry

    lax.fori_loop(0, DEC_SEQ // qblk, body, 0)


def _lat_swa_part(sink_ref, z_ref, ck_ref, cv_ref, o_ref, ckd, cva, cvb):
    W = SWA_WINDOW
    n_win = 3 * W
    ones = jnp.ones((HEAD_DIM, PAST_LEN), BF)
    for g in range(2):
        k_g = ck_ref[HEAD_DIM * g:HEAD_DIM * (g + 1), :].astype(BF)
        v_g = cv_ref[HEAD_DIM * g:HEAD_DIM * (g + 1), :].astype(BF)
        r0, r1, r2 = 128 * g, 128 * g + HEAD_DIM, 128 * (g + 1)
        ckd[r0:r1, :] = k_g
        ckd[r1:r2, :] = k_g
        cva[r0:r1, :] = v_g
        cva[r1:r2, :] = ones
        cvb[r0:r1, :] = ones
        cvb[r1:r2, :] = v_g
    lo_q = _lane_lo(W)
    lo_w = _lane_lo(n_win)

    def body(n, carry):
        q0 = pl.multiple_of(n * W, W)
        w0 = pl.multiple_of(jnp.clip((n - 1) * W, 0, DEC_SEQ - n_win), W)
        qpos = q0 + lax.broadcasted_iota(jnp.int32, (W, n_win), 0)
        kpos = w0 + lax.broadcasted_iota(jnp.int32, (W, n_win), 1)
        valid = jnp.abs(qpos - kpos) <= W
        scores = []
        for h in range(4):
            c = 128 * (h // 2)
            q = _split_pair(z_ref[pl.ds(q0, W), C_SQ + c:C_SQ + c + 128], lo_q)[h % 2]
            s_loc = jnp.where(valid, _dot_nt(q, z_ref[pl.ds(w0, n_win), ZB_SK + c:ZB_SK + c + 128]), NEG_INF)
            scores.append((s_loc, _dot(q, ckd[c:c + 128, :])))
        weights, extras = [], []
        for h, (s_loc, s_ctx) in enumerate(scores):
            sink = sink_ref[h]
            m = jnp.maximum(_rowmax(s_loc, s_ctx), sink)
            weights.append((jnp.exp(s_loc - m).astype(BF), jnp.exp(s_ctx - m).astype(BF)))
            extras.append(jnp.exp(sink - m))
        res = []
        for h, (e_loc, e_ctx) in enumerate(weights):
            c = 128 * (h // 2)
            v = _v_ones_pair(z_ref[pl.ds(w0, n_win), ZB_SV + c:ZB_SV + c + 128], lo_w)[h % 2]
            res.append(_dot(e_loc, v) + _dot_nt(e_ctx, (cva, cvb)[h % 2][c:c + 128, :]))
        for g in range(2):
            num, den = _merge_pair(res[2 * g], res[2 * g + 1], lo_q)
            den = den + jnp.where(lo_q, extras[2 * g], extras[2 * g + 1])
            o_ref[pl.ds(q0, W), 768 + 128 * g:768 + 128 * g + 128] = (num / den).astype(BF)
        return carry

    lax.fori_loop(0, DEC_SEQ // W, body, 0)


def _lat_mix_kernel(lam_init, sink_ref, z_ref, nak_ref, nav_ref, dfk_ref, dfv_ref, swk_ref, swv_ref, rpp_ref,
                    lamp_ref, subg_ref, cl_ref, sl_ref, cbd_ref, sbd_ref, wf_ref, o_ref,
                    ck, cva, cvb, vla, vlb, t2):
    @pl.when(pl.program_id(0) == 0)
    def _():
        _na_bias_tiles(rpp_ref, t2)

    _lat_na_part(z_ref, nak_ref, nav_ref, t2, o_ref, ck, cva, cvb)
    _lat_diff_part(lam_init, z_ref, dfk_ref, dfv_ref, lamp_ref, subg_ref, o_ref, ck, cva, cvb, vla, vlb)
    o_c = _fourier(z_ref[:, C_FC:C_FC + 256], cl_ref[...], sl_ref[...], cbd_ref[...], sbd_ref[...], wf_ref[...])
    o_ref[:, 512:768] = o_c.astype(BF)
    _lat_swa_part(sink_ref, z_ref, swk_ref, swv_ref, o_ref, ck, cva, cvb)


def _lat_mix(zb, caches_in, rpp, sink, lamp, subg, cl, sl, cbd, sbd, wf, lam_init, layer):
    full = lambda shape: pl.BlockSpec(shape, lambda b, s: (0,) * len(shape), pipeline_mode=pl.Buffered(1))
    cache = lambda width: pl.BlockSpec((None, None, width, PAST_LEN), lambda b, s: (b, layer, 0, 0))
    grid_spec = pltpu.PrefetchScalarGridSpec(
        num_scalar_prefetch=1,
        grid=(DEC_BATCH,),
        in_specs=[
            pl.BlockSpec((DEC_SEQ, ZB_WIDTH), lambda b, s: (LAT_BLOCK0 + b, 0)),
            cache(256), cache(256), cache(256), cache(256), cache(128), cache(128),
            pl.BlockSpec((None, 4, 2 * NA_KH, 128), lambda b, s: (layer, 0, 0, 0), pipeline_mode=pl.Buffered(1)),
            full((4, DIFF_QK_DIM)), full((1, 2 * HEAD_DIM)),
            full((DEC_SEQ, DEC_SEQ)), full((DEC_SEQ, DEC_SEQ)),
            full((256, 256)), full((256, 256)), full((256, 256)),
        ],
        out_specs=pl.BlockSpec((DEC_SEQ, MIX_WIDTH), lambda b, s: (b, 0)),
        scratch_shapes=[pltpu.VMEM((256, PAST_LEN), BF)] * 3 + [pltpu.VMEM((DEC_SEQ, 256), BF)] * 2
        + [pltpu.VMEM((4, 2 * NA_KH, GRID_W, 128), F32)],
    )
    return pl.pallas_call(
        functools.partial(_lat_mix_kernel, lam_init),
        grid_spec=grid_spec,
        out_shape=jax.ShapeDtypeStruct((N_LAT_TOK, MIX_WIDTH), BF),
        compiler_params=_params(),
        name="lat_mix",
    )(sink, zb, *caches_in, rpp, lamp, subg, cl, sl, cbd, sbd, wf)


def _dft_tables(n):
    j = np.arange(n)
    ang = 2.0 * np.pi * ((j[:, None] * j[None, :]) % n) / n
    return np.cos(ang) / np.sqrt(n), np.sin(ang) / np.sqrt(n)


def _block_diag4(m):
    out = np.zeros((256, 256), m.dtype)
    for g in range(4):
        out[64 * g:64 * g + 64, 64 * g:64 * g + 64] = m
    return out


def _rope_tables(n_axis_dims):
    half = n_axis_dims // 2
    inv = ROPE_BASE ** (-np.arange(half, dtype=np.float64) / half)
    t = np.arange(DEC_SEQ)
    lane = np.arange(128)
    w = lane % n_axis_dims
    is_col = (lane // n_axis_dims) % 2 == 1
    pos = np.where(is_col[None, :], (t % GRID_W)[:, None], (t // GRID_W)[:, None]).astype(np.float64)
    ang = pos * inv[w % half][None, :]
    sign = np.where(w < half, -1.0, 1.0)[None, :]
    cos = np.concatenate([np.cos(ang), np.ones((ROW_TILE, 128))], axis=0)
    sin = np.concatenate([np.sin(ang) * sign, np.zeros((ROW_TILE, 128))], axis=0)
    return jnp.asarray(cos, F32), jnp.asarray(sin, F32)


def _pad_rpb_rows(rpb):
    n_dc = rpb.shape[-1]
    half = jnp.pad(rpb, ((0, 0), (0, 0), (0, 1), (0, GRID_W - n_dc)))
    return jnp.concatenate([half, half], axis=-1)


def kernel(x_prompt, x_sample, cache_na_k, cache_na_v, cache_diff_k, cache_diff_v, cache_swa_k, cache_swa_v, c, c_ctx, w_ada, b_ada, norm1_g, norm2_g, w_in, na_rpb, diff_lq1, diff_lk1, diff_lq2, diff_lk2, diff_subln_g, w_fourier, swa_sink, w_out, w_mlp1, w_mlp2, final_g):
    xp = x_prompt.reshape(N_CTX_TOK, D_MODEL)
    xs = x_sample.reshape(N_LAT_TOK, D_MODEL)
    cond = jnp.zeros((MOD_ROWS, D_MODEL), F32).at[0].set(c_ctx).at[1:1 + DEC_BATCH].set(c)
    b_ada3 = b_ada.reshape(DEPTH, 1, 6 * D_MODEL)
    mods = _ada(cond, w_ada, b_ada3, 0)

    cl_p, sl_p = _dft_tables(SEQ)
    cl_s, sl_s = _dft_tables(DEC_SEQ)
    c64, s64 = _dft_tables(64)
    cl_p, sl_p, cl_s, sl_s, cbd, sbd = (
        jnp.asarray(a, F32).astype(BF) for a in (cl_p, sl_p, cl_s, sl_s, _block_diag4(c64), _block_diag4(s64)))
    ropes = _rope_tables(16) + _rope_tables(32)
    rpp = _pad_rpb_rows(na_rpb)
    wf_bf = w_fourier.astype(BF)

    ck_na, cv_na, ck_df, cv_df, ck_sw, cv_sw = (
        a.transpose(0, 1, 3, 4, 2).reshape(DEC_BATCH, DEPTH, -1, PAST_LEN)
        for a in (cache_na_k, cache_na_v, cache_diff_k, cache_diff_v, cache_swa_k, cache_swa_v))

    _, _, zb, caches = _trunk(xp, xs, ropes, proj=(0, mods, norm1_g[0], w_in, None))
    for l in range(DEPTH):
        lam_init = 0.8 - 0.6 * math.exp(-0.3 * l)
        lamp = jnp.stack([diff_lq1[l], diff_lk1[l], diff_lq2[l], diff_lk2[l]], axis=0)
        subg = jnp.tile(diff_subln_g[l].reshape(1, HEAD_DIM), (1, 2))
        has_next = l + 1 < DEPTH

        cat_p, mods_next = _ctx_mix(zb, swa_sink[l], lamp, subg, cl_p, sl_p, cbd, sbd, wf_bf[l], lam_init,
                                    ada=(cond, w_ada, b_ada3, l + 1) if has_next else None)
        cat_s = _lat_mix(zb, (ck_na, cv_na, ck_df, cv_df, ck_sw, cv_sw), rpp, swa_sink[l], lamp, subg,
                         cl_s, sl_s, cbd, sbd, wf_bf[l], lam_init, l)
        nxt = (l + 1, mods_next, norm1_g[l + 1], w_in, caches) if has_next else None
        xp, xs, zb, new_caches = _trunk(
            xp, xs, ropes, mlp=(l, mods, cat_p, cat_s, norm2_g[l], final_g, w_out, w_mlp1, w_mlp2), proj=nxt)
        caches = new_caches if has_next else caches
        mods = mods_next

    y_prompt = xp.reshape(BATCH, SEQ, D_MODEL)
    y_sample = xs.reshape(DEC_BATCH, DEC_SEQ, D_MODEL)
    new = [a.reshape(BATCH, DEPTH, SEQ, a.shape[-1] // HEAD_DIM, HEAD_DIM) for a in caches]
    return (y_prompt, y_sample) + tuple(new)
```

```python
import functools
import math

import numpy as np
import jax
import jax.numpy as jnp
from jax import lax
from jax.experimental import pallas as pl
from jax.experimental.pallas import tpu as pltpu

D_MODEL = 1024
BATCH = 16
SEQ = 256
DEPTH = 4
DEC_BATCH = 2
DEC_SEQ = 1024
PAST_LEN = 512
GRID_W = 64
GRID_ROWS = DEC_SEQ // GRID_W
HEAD_DIM = 64
NA_KH = 8
NA_KW = 16
DIFF_QK_DIM = 32
SWA_WINDOW = 128
D_FF = 4 * D_MODEL
ROPE_BASE = 10000.0
NORM_EPS = 1e-6
NEG_INF = -1e30
IN_WIDTH = 2304
MIX_WIDTH = 1024

N_CTX_TOK = BATCH * SEQ
N_LAT_TOK = DEC_BATCH * DEC_SEQ
N_TOK = N_CTX_TOK + N_LAT_TOK
ROW_TILE = 256
N_ROW_TILES = N_TOK // ROW_TILE
N_CTX_TILES = N_CTX_TOK // ROW_TILE
LAT_TILES_PER_REQ = DEC_SEQ // ROW_TILE
LAT_BLOCK0 = N_CTX_TOK // DEC_SEQ
MOD_ROWS = 8

C_NA_Q, C_NA_K, C_NA_V = 0, 256, 512
C_DQ, C_DK, C_DV = 768, 1024, 1280
C_FC = 1536
C_SQ, C_SK, C_SV = 1792, 2048, 2176
CACHE_COLS = ((C_NA_K, 256), (C_NA_V, 256), (C_DK, 256), (C_DV, 256), (C_SK, 128), (C_SV, 128))
ZB_SK, ZB_SV = 2048, 2304
ZB_WIDTH = 2560

NA_QROWS = 2
NA_WIN_ROWS = 10

BF = jnp.bfloat16
F32 = jnp.float32
VMEM_LIMIT = 56 * 1024 * 1024


def _dot(a, b):
    return jnp.dot(a, b, preferred_element_type=F32)


def _dot_nt(a, b):
    return lax.dot_general(a, b, (((1,), (1,)), ((), ())), preferred_element_type=F32)


def _rmsnorm(x, g):
    ms = jnp.mean(x * x, axis=-1, keepdims=True)
    return x * lax.rsqrt(ms + NORM_EPS) * g


def _params(n_grid=1):
    return pltpu.CompilerParams(dimension_semantics=("arbitrary",) * n_grid, vmem_limit_bytes=VMEM_LIMIT)


def _rowmax(*parts):
    m = jnp.max(parts[0], axis=-1, keepdims=True)
    for p in parts[1:]:
        m = jnp.maximum(m, jnp.max(p, axis=-1, keepdims=True))
    return m


def _ada_kernel(cond_ref, w_ref, b_ref, o_ref):
    cnd = cond_ref[...]
    s = cnd / (1.0 + jnp.exp(-cnd))
    o_ref[...] = _dot(s.astype(BF), w_ref[...].astype(BF)) + b_ref[...]


def _ada_specs(layer, n_steps, index_of_step):
    tn = 6 * D_MODEL // n_steps
    col = lambda *g: index_of_step(*g)
    in_specs = [
        pl.BlockSpec((MOD_ROWS, D_MODEL), lambda *g: (0, 0)),
        pl.BlockSpec((None, D_MODEL, tn), lambda *g: (layer, 0, col(*g))),
        pl.BlockSpec((None, 1, tn), lambda *g: (layer, 0, col(*g))),
    ]
    return in_specs, pl.BlockSpec((MOD_ROWS, tn), lambda *g: (0, col(*g)))


def _ada(cond, w_ada, b_ada3, layer):
    in_specs, out_spec = _ada_specs(layer, 6, lambda j: j)
    return pl.pallas_call(
        _ada_kernel,
        grid=(6,),
        in_specs=in_specs,
        out_specs=out_spec,
        out_shape=jax.ShapeDtypeStruct((MOD_ROWS, 6 * D_MODEL), F32),
        compiler_params=_params(),
        name="ada",
    )(cond, w_ada, b_ada3)


def _mod_row(t):
    return jnp.where(t < N_CTX_TILES, 0, 1 + (t - N_CTX_TILES) // LAT_TILES_PER_REQ)


def _ctx_tile(t):
    return jnp.minimum(t, N_CTX_TILES - 1)


def _lat_tile(t):
    return jnp.maximum(t - N_CTX_TILES, 0)


TRUNK_CAST_STEPS = 8
PRECAST_CHUNKS = 16
FIRST_RESIDUAL_CAST_STEPS = 16
FF_CHUNK = 1024
TRUNK_VMEM_LIMIT = 60 * 1024 * 1024


def _rope(x, cos, sin_signed, shift, first):
    partner = jnp.where(first, pltpu.roll(x, 128 - shift, 1), pltpu.roll(x, shift, 1))
    return x * cos + partner * sin_signed


def _projection_stages(hb, wb, t, is_ctx, zb_ref, cache_refs, rope_refs):
    cosd_ref, sind_ref, coss_ref, sins_ref = rope_refs
    dscale = DIFF_QK_DIM ** -0.5
    p0 = pl.multiple_of(
        jnp.where(is_ctx, DEC_SEQ, ((t - N_CTX_TILES) % LAT_TILES_PER_REQ) * ROW_TILE), ROW_TILE)
    lane = lax.broadcasted_iota(jnp.int32, (ROW_TILE, 128), 1)
    first_d = (lane % 16) < 8
    first_s = (lane % 32) < 16
    lo = lane < HEAD_DIM
    cosd, sind = cosd_ref[pl.ds(p0, ROW_TILE), :], sind_ref[pl.ds(p0, ROW_TILE), :]
    coss, sins = coss_ref[pl.ds(p0, ROW_TILE), :], sins_ref[pl.ds(p0, ROW_TILE), :]

    def cols(c0, width):
        return _dot(hb(), wb[:, c0:c0 + width])

    def put(c0, width, val):
        zb_ref[:, c0:c0 + width] = val.astype(BF)

    def put_dup(c0, val):
        r = pltpu.roll(val, HEAD_DIM, 1)
        put(c0, 128, jnp.where(lo, val, r))
        put(c0 + 128, 128, jnp.where(lo, r, val))

    kept = {}

    def na_q():
        put(C_NA_Q, 256, cols(C_NA_Q, 256) * 0.125)

    def na_kv():
        kept["nakv"] = cols(C_NA_K, 512)
        put(C_NA_K, 512, kept["nakv"])

    def diff_q():
        z = cols(C_DQ, 256)
        for j in range(2):
            put(C_DQ + 128 * j, 128, _rope(z[:, 128 * j:128 * j + 128], cosd, sind, 8, first_d) * dscale)

    def diff_k():
        kept["dk"] = cols(C_DK, 256)
        for j in range(2):
            put(C_DK + 128 * j, 128, _rope(kept["dk"][:, 128 * j:128 * j + 128], cosd, sind, 8, first_d))

    def diff_v_fourier():
        kept["dvfc"] = cols(C_DV, 512)
        put(C_DV, 512, kept["dvfc"])

    def swa_q():
        z = cols(C_SQ, 256)
        for j in range(2):
            put(C_SQ + 128 * j, 128, _rope(z[:, 128 * j:128 * j + 128], coss, sins, 16, first_s) * 0.125)

    def swa_kv():
        kept["skv"] = cols(C_SK, 256)
        put_dup(ZB_SK, _rope(kept["skv"][:, 0:128], coss, sins, 16, first_s))
        put_dup(ZB_SV, kept["skv"][:, 128:256])

    def finish():
        new = (kept["nakv"][:, 0:256], kept["nakv"][:, 256:512], kept["dk"], kept["dvfc"][:, 0:256],
               kept["skv"][:, 0:128], kept["skv"][:, 128:256])
        for ref, val in zip(cache_refs, new):
            ref[...] = jnp.where(is_ctx, val, ref[...])

    return [na_q, na_kv, diff_q, diff_k, diff_v_fourier, swa_q, swa_kv], finish


def _trunk_kernel(do_mlp, do_proj, last, n_alias, n_cast, n_pre, *refs):
    refs = list(refs)
    take = lambda n: [refs.pop(0) for _ in range(n)]
    xp_ref, xs_ref = take(2)
    if do_mlp:
        catp_ref, cats_ref, mm_ref, g2_ref, fg_ref, wo_ref, w1_ref, w2_ref = take(8)
    if do_proj:
        pm_ref, g1_ref, win_ref = take(3)
        rope_refs = take(4)
        take(n_alias)
    pre_in = take(n_pre)
    if do_mlp:
        op_ref, os_ref = take(2)
    if do_proj:
        zb_ref, = take(1)
        cache_refs = take(6)
    pre_out = take(n_pre)
    if do_mlp:
        wob, w1b, w2b = take(3) if n_cast else (wo_ref, w1_ref, w2_ref)
    if do_proj:
        wb = take(1)[0] if n_cast else win_ref
    skew = do_mlp and do_proj
    if skew:
        hprev, = take(1)
    s = pl.program_id(0)

    if skew:
        @pl.when(s == 0)
        def _():
            hprev[...] = jnp.zeros_like(hprev)

    if n_cast:
        @pl.when(s < n_cast)
        def _():
            def cast(dst, src):
                rows = src.shape[0]
                dst[pl.ds(pl.multiple_of(s * rows, rows), rows), :] = src[...].astype(BF)

            if do_mlp:
                cast(wob, wo_ref)
                cast(w1b, w1_ref)
                cast(w2b, w2_ref)
            if do_proj:
                cast(wb, win_ref)

    def tile_step(t, proj_stages, proj_finish):
        pending = list(proj_stages)

        def issue(n):
            for _ in range(min(n, len(pending))):
                pending.pop(0)()

        is_ctx = t < N_CTX_TILES
        row = _mod_row(t)
        x = jnp.where(is_ctx, xp_ref[...], xs_ref[...])
        if do_mlp:
            issue(1)
            gt1 = mm_ref[pl.ds(row, 1), 2 * D_MODEL:3 * D_MODEL]
            sh2 = mm_ref[pl.ds(row, 1), 3 * D_MODEL:4 * D_MODEL]
            sc2 = mm_ref[pl.ds(row, 1), 4 * D_MODEL:5 * D_MODEL]
            gt2 = mm_ref[pl.ds(row, 1), 5 * D_MODEL:6 * D_MODEL]
            cat = jnp.where(is_ctx, catp_ref[...], cats_ref[...])
            x = x + gt1 * _dot(cat, wob[...])
            issue(1)
            hh = (_rmsnorm(x, g2_ref[...]) * (1.0 + sc2) + sh2).astype(BF)
            acc = jnp.zeros((ROW_TILE, D_MODEL), F32)
            for c in range(D_FF // FF_CHUNK):
                a = jnp.maximum(_dot(hh, w1b[:, c * FF_CHUNK:(c + 1) * FF_CHUNK]), 0.0)
                acc = acc + _dot((a * a).astype(BF), w2b[c * FF_CHUNK:(c + 1) * FF_CHUNK, :])
                if c < 2:
                    issue(1)
            issue(1)
            x = x + gt2 * acc
            out = _rmsnorm(x, fg_ref[...]) if last else x
        hb = None
        if do_proj:
            issue(1)
            shift = pm_ref[pl.ds(row, 1), 0:D_MODEL]
            scale = pm_ref[pl.ds(row, 1), D_MODEL:2 * D_MODEL]
            hb = (_rmsnorm(x, g1_ref[...]) * (1.0 + scale) + shift).astype(BF)
        issue(len(pending))
        if proj_finish is not None:
            proj_finish()
        if do_mlp:
            op_ref[...] = jnp.where(is_ctx, out, op_ref[...])
            os_ref[...] = jnp.where(is_ctx, os_ref[...], out)
        return hb

    def projection(hb, t):
        return _projection_stages(hb, wb, t, t < N_CTX_TILES, zb_ref, cache_refs, rope_refs)

    @pl.when(s >= n_cast)
    def _():
        t = s - n_cast
        for src, dst in zip(pre_in, pre_out):
            dst[...] = src[...].astype(BF)
        if not skew:
            hb = tile_step(t, [], None)
            if do_proj:
                stages, finish = projection(lambda: hb, t)
                for stage in stages:
                    stage()
                finish()
        else:
            u = jnp.maximum(t - 1, 0)

            @pl.when(t < N_ROW_TILES)
            def _():
                stages, finish = projection(lambda: hprev[...], u)
                hprev[...] = tile_step(t, stages, finish)

            @pl.when(t >= N_ROW_TILES)
            def _():
                stages, finish = projection(lambda: hprev[...], u)
                for stage in stages:
                    stage()
                finish()


def _trunk(xp, xs, ropes, mlp=None, proj=None, n_cast=TRUNK_CAST_STEPS, precast=()):
    do_mlp, do_proj = mlp is not None, proj is not None
    last = do_mlp and not do_proj
    skew = do_mlp and do_proj
    clamp_tile = lambda s: jnp.clip(s - n_cast, 0, N_ROW_TILES - 1)
    step = lambda f: (lambda s: f(clamp_tile(s)))
    step_proj = (lambda f: (lambda s: f(clamp_tile(s - 1)))) if skew else step
    const = lambda shape: pl.BlockSpec(shape, lambda s: (0,) * len(shape), pipeline_mode=pl.Buffered(1))
    if n_cast:
        weight = lambda layer, rows, cols: pl.BlockSpec(
            (None, rows // n_cast, cols), lambda s: (layer, jnp.minimum(s, n_cast - 1), 0))
    else:
        weight = lambda layer, rows, cols: const((rows, cols))
    mod_spec = const((MOD_ROWS, 6 * D_MODEL))
    x_specs = [pl.BlockSpec((ROW_TILE, D_MODEL), step(lambda t: (_ctx_tile(t), 0))),
               pl.BlockSpec((ROW_TILE, D_MODEL), step(lambda t: (_lat_tile(t), 0)))]
    args, in_specs = [xp, xs], list(x_specs)
    out_specs, out_shape, scratch, aliases = [], [], [], {}
    n_alias = 0
    if do_mlp:
        layer, mods_l, cat_p, cat_s, g2, final_g, wo, w1, w2 = mlp
        args += [cat_p, cat_s, mods_l, g2.reshape(1, D_MODEL), final_g.reshape(1, D_MODEL), wo, w1, w2]
        in_specs += [pl.BlockSpec((ROW_TILE, MIX_WIDTH), step(lambda t: (_ctx_tile(t), 0))),
                     pl.BlockSpec((ROW_TILE, MIX_WIDTH), step(lambda t: (_lat_tile(t), 0))), mod_spec,
                     const((1, D_MODEL)), const((1, D_MODEL)),
                     weight(layer, MIX_WIDTH, D_MODEL), weight(layer, D_MODEL, D_FF), weight(layer, D_FF, D_MODEL)]
    if do_proj:
        p_layer, mods_p, g1, w_in, caches = proj
        args += [mods_p, g1.reshape(1, D_MODEL), w_in, *ropes]
        in_specs += [mod_spec, const((1, D_MODEL)), weight(p_layer, D_MODEL, IN_WIDTH)]
        in_specs += [const((DEC_SEQ + ROW_TILE, 128))] * 4
        if caches is not None:
            n_alias = 6
            n_out_before = 2 if do_mlp else 0
            aliases = {len(args) + i: n_out_before + 1 + i for i in range(6)}
            args += list(caches)
            in_specs += [pl.BlockSpec(memory_space=pl.ANY)] * 6
    pre_chunk = lambda s: jnp.clip(s - n_cast, 0, PRECAST_CHUNKS - 1)
    for w, w_layer in precast:
        rows, cols = w.shape[1] // PRECAST_CHUNKS, w.shape[2]
        args.append(w)
        in_specs.append(pl.BlockSpec((None, rows, cols), lambda s, w_layer=w_layer: (w_layer, pre_chunk(s), 0)))
    if do_mlp:
        out_specs += x_specs
        out_shape += [jax.ShapeDtypeStruct((N_CTX_TOK, D_MODEL), F32),
                      jax.ShapeDtypeStruct((N_LAT_TOK, D_MODEL), F32)]
        if n_cast:
            scratch += [pltpu.VMEM((MIX_WIDTH, D_MODEL), BF), pltpu.VMEM((D_MODEL, D_FF), BF),
                        pltpu.VMEM((D_FF, D_MODEL), BF)]
    if do_proj:
        out_specs += [pl.BlockSpec((ROW_TILE, ZB_WIDTH), step_proj(lambda t: (t, 0)))] + [
            pl.BlockSpec((None, None, SEQ, width), step_proj(lambda t: (_ctx_tile(t), p_layer, 0, 0)))
            for _, width in CACHE_COLS]
        out_shape += [jax.ShapeDtypeStruct((N_TOK, ZB_WIDTH), BF)] + [
            jax.ShapeDtypeStruct((BATCH, DEPTH, SEQ, width), F32) for _, width in CACHE_COLS]
        if n_cast:
            scratch += [pltpu.VMEM((D_MODEL, IN_WIDTH), BF)]
    for w, _ in precast:
        out_specs.append(pl.BlockSpec((w.shape[1] // PRECAST_CHUNKS, w.shape[2]), lambda s: (pre_chunk(s), 0)))
        out_shape.append(jax.ShapeDtypeStruct(w.shape[1:], BF))
    if skew:
        scratch += [pltpu.VMEM((ROW_TILE, D_MODEL), BF)]
    outs = pl.pallas_call(
        functools.partial(_trunk_kernel, do_mlp, do_proj, last, n_alias, n_cast, len(precast)),
        grid=(n_cast + N_ROW_TILES + (1 if skew else 0),),
        in_specs=in_specs,
        out_specs=out_specs,
        out_shape=out_shape,
        scratch_shapes=scratch,
        input_output_aliases=aliases,
        compiler_params=pltpu.CompilerParams(dimension_semantics=("arbitrary",),
                                             vmem_limit_bytes=TRUNK_VMEM_LIMIT),
        name="trunk_mlp_proj" if (do_mlp and do_proj) else ("trunk_mlp" if do_mlp else "trunk_proj"),
    )(*args)
    outs = list(outs)
    if do_mlp:
        xp, xs = outs[:2]
        outs = outs[2:]
    zb, new_caches = (None, None)
    if do_proj:
        zb, new_caches = outs[0], tuple(outs[1:7])
        outs = outs[7:]
    return xp, xs, zb, new_caches, tuple(outs)


def _diff_lambda(lamp_ref, lam_init):
    a = jnp.sum(lamp_ref[0:1, :] * lamp_ref[1:2, :], axis=-1, keepdims=True)
    b = jnp.sum(lamp_ref[2:3, :] * lamp_ref[3:4, :], axis=-1, keepdims=True)
    return jnp.exp(a) - jnp.exp(b) + lam_init


def _lane_lo(n):
    return lax.broadcasted_iota(jnp.int32, (n, 128), 1) < HEAD_DIM


def _lane_lo_wide(n, width):
    return lax.broadcasted_iota(jnp.int32, (n, width), 1) % 128 < HEAD_DIM


def _split_pair(x, lo):
    zero = jnp.zeros_like(x)
    return jnp.where(lo, x, zero), jnp.where(lo, zero, x)


def _v_ones_pair(v, lo):
    one = jnp.ones_like(v)
    return jnp.where(lo, v, one), jnp.where(lo, one, v)


def _v_stack(v, lo):
    zero = jnp.zeros_like(v)
    ones_a = jnp.where(lo, 1.0, 0.0).astype(v.dtype)
    ones_b = jnp.where(lo, 0.0, 1.0).astype(v.dtype)
    top = jnp.concatenate([jnp.where(lo, v, zero), ones_a], axis=1)
    bottom = jnp.concatenate([jnp.where(lo, zero, v), ones_b], axis=1)
    return jnp.concatenate([top, bottom], axis=0)


def _merge_pair(a, b, lo):
    return jnp.where(lo, a, b), pltpu.roll(jnp.where(lo, b, a), HEAD_DIM, 1)


def _subln_pair(o, g2, lam_init, lo):
    sq = o * o
    ms_a = jnp.sum(jnp.where(lo, sq, 0.0), axis=-1, keepdims=True)
    ms_b = jnp.sum(jnp.where(lo, 0.0, sq), axis=-1, keepdims=True)
    ms = jnp.where(lo, ms_a, ms_b) * (1.0 / HEAD_DIM)
    return o * lax.rsqrt(ms + NORM_EPS) * g2 * (1.0 - lam_init)


def _diff_quarters(q, n):
    quarter = lax.broadcasted_iota(jnp.int32, (n, 128), 1) // DIFF_QK_DIM
    zero = jnp.zeros_like(q)
    return [jnp.where(quarter == i, q, zero) for i in range(4)]


def _fourier(x_bf, cl, sl, cbd, sbd, wf):
    xc = _dot(x_bf, cbd).astype(BF)
    xs = _dot(x_bf, sbd).astype(BF)
    y = _dot(cl, xc) - _dot(sl, xs)
    return _dot(y.astype(BF), wf)


CTX_UNIT_GROUP = 4
CTX_REQ_PER_STEP = 2


def _ctx_mix_kernel(lam_init, with_ada, sink_ref, z_ref, lamp_ref, subg_ref, cl_ref, sl_ref, cbd_ref, sbd_ref,
                    wf_ref, *rest):
    if with_ada:
        cond_ref, wa_ref, ba_ref, o_ref, mods_ref = rest
        _ada_kernel(cond_ref, wa_ref, ba_ref, mods_ref)
    else:
        o_ref, = rest
    for r in range(CTX_REQ_PER_STEP):
        rows = pl.ds(r * SEQ, SEQ)
        _ctx_request(lam_init, sink_ref, z_ref.at[rows], lamp_ref, subg_ref, cl_ref, sl_ref, cbd_ref, sbd_ref,
                     wf_ref, o_ref.at[rows])


def _ctx_request(lam_init, sink_ref, z_ref, lamp_ref, subg_ref, cl_ref, sl_ref, cbd_ref, sbd_ref, wf_ref, o_ref):
    lo = _lane_lo(SEQ)

    pairs = []
    for j in range(2):
        c = 128 * j
        qs = _split_pair(z_ref[:, C_NA_Q + c:C_NA_Q + c + 128], lo)
        pairs.append((qs[0], qs[1], C_NA_K + c, _v_stack(z_ref[:, C_NA_V + c:C_NA_V + c + 128], lo), None))
    for j in range(2):
        c = 128 * j
        q4 = _diff_quarters(z_ref[:, C_DQ + c:C_DQ + c + 128], SEQ)
        v2 = _v_stack(z_ref[:, C_DV + c:C_DV + c + 128], lo)
        pairs.append((q4[0], q4[2], C_DK + c, v2, None))
        pairs.append((q4[1], q4[3], C_DK + c, v2, None))
    for g in range(2):
        c = 128 * g
        qs = _split_pair(z_ref[:, C_SQ + c:C_SQ + c + 128], lo)
        pairs.append((qs[0], qs[1], ZB_SK + c, _v_stack(z_ref[:, ZB_SV + c:ZB_SV + c + 128], lo),
                      (sink_ref[2 * g], sink_ref[2 * g + 1])))

    res = []
    for p0 in range(0, len(pairs), CTX_UNIT_GROUP // 2):
        group = pairs[p0:p0 + CTX_UNIT_GROUP // 2]
        scores = [[_dot_nt(q, z_ref[:, kc:kc + 128]) for q in (qa, qb)] for qa, qb, kc, _, _ in group]
        weights, extras = [], []
        for ss, (_, _, _, _, sinks) in zip(scores, group):
            es, xs = [], []
            for i, s in enumerate(ss):
                m = _rowmax(s)
                if sinks is not None:
                    m = jnp.maximum(m, sinks[i])
                    xs.append(jnp.exp(sinks[i] - m))
                es.append(jnp.exp(s - m).astype(BF))
            weights.append(jnp.concatenate(es, axis=1))
            extras.append(xs)
        for e2, xs, (_, _, _, v2, _) in zip(weights, extras, group):
            r = _dot(e2, v2)
            num, den = r[:, 0:128], r[:, 128:256]
            if xs:
                den = den + jnp.where(lo, xs[0], xs[1])
            res.append(num / den)

    for j in range(2):
        o_ref[:, 128 * j:128 * j + 128] = res[j].astype(BF)

    lam = _diff_lambda(lamp_ref, lam_init)
    for j in range(2):
        o = res[2 + 2 * j] - lam * res[3 + 2 * j]
        o_ref[:, 256 + 128 * j:384 + 128 * j] = _subln_pair(o, subg_ref[...], lam_init, lo).astype(BF)

    o_c = _fourier(z_ref[:, C_FC:C_FC + 256], cl_ref[...], sl_ref[...], cbd_ref[...], sbd_ref[...], wf_ref[...])
    o_ref[:, 512:768] = o_c.astype(BF)

    for g in range(2):
        o_ref[:, 768 + 128 * g:896 + 128 * g] = res[6 + g].astype(BF)


def _ctx_mix(zb, sink, lamp, subg, cl, sl, cbd, sbd, wf, lam_init, ada=None):
    full = lambda shape: pl.BlockSpec(shape, lambda b, s: (0,) * len(shape))
    n_steps = BATCH // CTX_REQ_PER_STEP
    in_specs = [
        pl.BlockSpec((CTX_REQ_PER_STEP * SEQ, ZB_WIDTH), lambda b, s: (b, 0)),
        full((4, DIFF_QK_DIM)),
        full((1, 2 * HEAD_DIM)),
        full((SEQ, SEQ)), full((SEQ, SEQ)),
        full((256, 256)), full((256, 256)), full((256, 256)),
    ]
    out_specs = [pl.BlockSpec((CTX_REQ_PER_STEP * SEQ, MIX_WIDTH), lambda b, s: (b, 0))]
    out_shape = [jax.ShapeDtypeStruct((N_CTX_TOK, MIX_WIDTH), BF)]
    args = [sink, zb, lamp, subg, cl, sl, cbd, sbd, wf]
    if ada is not None:
        cond, w_ada, b_ada3, layer = ada
        ada_in, ada_out = _ada_specs(layer, n_steps, lambda b, s: b)
        in_specs += ada_in
        out_specs.append(ada_out)
        out_shape.append(jax.ShapeDtypeStruct((MOD_ROWS, 6 * D_MODEL), F32))
        args += [cond, w_ada, b_ada3]
    grid_spec = pltpu.PrefetchScalarGridSpec(
        num_scalar_prefetch=1, grid=(n_steps,), in_specs=in_specs, out_specs=out_specs)
    outs = pl.pallas_call(
        functools.partial(_ctx_mix_kernel, lam_init, ada is not None),
        grid_spec=grid_spec,
        out_shape=out_shape,
        compiler_params=_params(),
        name="ctx_mix",
    )(*args)
    return outs[0], (outs[1] if ada is not None else None)


def _cache_v_ones(cv_ref, cva, cvb):
    first = lax.broadcasted_iota(jnp.int32, (256, PAST_LEN), 0) % 128 < HEAD_DIM
    cv = cv_ref[...]
    cva[...] = jnp.where(first, cv, 1.0).astype(BF)
    cvb[...] = jnp.where(first, 1.0, cv).astype(BF)


def _na_bias_tiles(rpp_ref, t2):
    c = lax.broadcasted_iota(jnp.int32, (GRID_W, 128), 0)
    kc = lax.broadcasted_iota(jnp.int32, (GRID_W, 128), 1) % GRID_W
    c_start = jnp.clip(c - NA_KW // 2, 0, GRID_W - NA_KW)
    inside = jnp.logical_and(kc >= c_start, kc < c_start + NA_KW)
    for h in range(4):
        for dr in range(2 * NA_KH - 1):
            row = jnp.broadcast_to(rpp_ref[h, dr:dr + 1, :], (GRID_W, 128))
            toeplitz = pltpu.roll(row, 128 - (NA_KW - 1), 1, stride=1, stride_axis=0)
            t2[h, dr] = jnp.where(inside, toeplitz, NEG_INF)
        t2[h, 2 * NA_KH - 1] = jnp.full((GRID_W, 128), NEG_INF, F32)


def _lat_na_part(z_ref, ck_ref, cv_ref, t2, o_ref, ckb, cva, cvb):
    ckb[...] = ck_ref[...].astype(BF)
    _cache_v_ones(cv_ref, cva, cvb)
    n_q = NA_QROWS * GRID_W
    n_loc = NA_WIN_ROWS * GRID_W
    lo_q = _lane_lo(n_q)
    lo_w = _lane_lo(n_loc)

    lo_t = _lane_lo(GRID_W)

    def body(p, carry):
        w_row = jnp.clip(NA_QROWS * p - NA_KH // 2, 0, GRID_ROWS - NA_WIN_ROWS)
        q0 = pl.multiple_of(p * n_q, n_q)
        k0 = pl.multiple_of(w_row * GRID_W, NA_QROWS * GRID_W)

        def tile_index(r, j):
            r_start = jnp.clip(r - NA_KH // 2, 0, GRID_ROWS - NA_KH)
            kr = w_row + j
            inside = jnp.logical_and(kr >= r_start, kr < r_start + NA_KH)
            return jnp.where(inside, kr - r + NA_KH - 1, 2 * NA_KH - 1)

        def bias(h):
            rows = []
            for rl in range(NA_QROWS):
                r = NA_QROWS * p + rl
                pieces = [jnp.where(lo_t, t2[h, tile_index(r, 2 * jj)], t2[h, tile_index(r, 2 * jj + 1)])
                          for jj in range(NA_WIN_ROWS // 2)]
                rows.append(jnp.concatenate(pieces, axis=1))
            return jnp.concatenate(rows, axis=0)

        scores = []
        for h in range(4):
            c = 128 * (h // 2)
            q = _split_pair(z_ref[pl.ds(q0, n_q), C_NA_Q + c:C_NA_Q + c + 128], lo_q)[h % 2]
            s_loc = _dot_nt(q, z_ref[pl.ds(k0, n_loc), C_NA_K + c:C_NA_K + c + 128]) + bias(h)
            s_ctx = _dot(q, ckb[c:c + 128, :])
            scores.append((s_loc, s_ctx))
        weights = []
        for s_loc, s_ctx in scores:
            m = _rowmax(s_loc, s_ctx)
            weights.append((jnp.exp(s_loc - m).astype(BF), jnp.exp(s_ctx - m).astype(BF)))
        res = []
        for h, (e_loc, e_ctx) in enumerate(weights):
            c = 128 * (h // 2)
            v = _v_ones_pair(z_ref[pl.ds(k0, n_loc), C_NA_V + c:C_NA_V + c + 128], lo_w)[h % 2]
            cv = (cva, cvb)[h % 2][c:c + 128, :]
            res.append(_dot(e_loc, v) + _dot_nt(e_ctx, cv))
        for j in range(2):
            num, den = _merge_pair(res[2 * j], res[2 * j + 1], lo_q)
            o_ref[pl.ds(q0, n_q), 128 * j:128 * j + 128] = (num / den).astype(BF)
        return carry

    lax.fori_loop(0, GRID_ROWS // NA_QROWS, body, 0)


DIFF_QBLK = 512


def _lat_diff_part(lam_init, z_ref, ck_ref, cv_ref, lamp_ref, subg_ref, o_ref, ckb, cva, cvb, vla, vlb):
    ckb[...] = ck_ref[...].astype(BF)
    _cache_v_ones(cv_ref, cva, cvb)
    lo_v = _lane_lo_wide(DEC_SEQ, 256)
    v_loc = z_ref[:, C_DV:C_DV + 256]
    one = jnp.ones_like(v_loc)
    vla[...] = jnp.where(lo_v, v_loc, one)
    vlb[...] = jnp.where(lo_v, one, v_loc)
    lam = _diff_lambda(lamp_ref, lam_init)
    qblk = DIFF_QBLK
    lo_q = _lane_lo(qblk)

    def body(i, carry):
        q0 = pl.multiple_of(i * qblk, qblk)
        for j in range(2):
            c = 128 * j
            q4 = _diff_quarters(z_ref[pl.ds(q0, qblk), C_DQ + c:C_DQ + c + 128], qblk)
            scores = [(_dot_nt(q, z_ref[:, C_DK + c:C_DK + c + 128]), _dot(q, ckb[c:c + 128, :])) for q in q4]
            weights = []
            for s_loc, s_ctx in scores:
                m = _rowmax(s_loc, s_ctx)
                weights.append((jnp.exp(s_loc - m).astype(BF), jnp.exp(s_ctx - m).astype(BF)))
            res = []
            for t, (e_loc, e_ctx) in enumerate(weights):
                v_loc, v_ctx = ((vla, cva), (vlb, cvb))[t // 2]
                res.append(_dot(e_loc, v_loc[:, c:c + 128]) + _dot_nt(e_ctx, v_ctx[c:c + 128, :]))
            n1, d1 = _merge_pair(res[0], res[2], lo_q)
            n2, d2 = _merge_pair(res[1], res[3], lo_q)
            o = n1 / d1 - lam * (n2 / d2)
            o_ref[pl.ds(q0, qblk), 256 + c:256 + c + 128] = (
                _subln_pair(o, subg_ref[...], lam_init, lo_q).astype(BF))
        return carry

    lax.fori_loop(0, DEC_SEQ // qblk, body, 0)


def _lat_swa_part(sink_ref, z_ref, ck_ref, cv_ref, o_ref, ckd, cva, cvb):
    W = SWA_WINDOW
    n_win = 3 * W
    ones = jnp.ones((HEAD_DIM, PAST_LEN), BF)
    for g in range(2):
        k_g = ck_ref[HEAD_DIM * g:HEAD_DIM * (g + 1), :].astype(BF)
        v_g = cv_ref[HEAD_DIM * g:HEAD_DIM * (g + 1), :].astype(BF)
        r0, r1, r2 = 128 * g, 128 * g + HEAD_DIM, 128 * (g + 1)
        ckd[r0:r1, :] = k_g
        ckd[r1:r2, :] = k_g
        cva[r0:r1, :] = v_g
        cva[r1:r2, :] = ones
        cvb[r0:r1, :] = ones
        cvb[r1:r2, :] = v_g
    lo_q = _lane_lo(W)
    lo_w = _lane_lo(n_win)

    def body(n, carry):
        q0 = pl.multiple_of(n * W, W)
        w0 = pl.multiple_of(jnp.clip((n - 1) * W, 0, DEC_SEQ - n_win), W)
        qpos = q0 + lax.broadcasted_iota(jnp.int32, (W, n_win), 0)
        kpos = w0 + lax.broadcasted_iota(jnp.int32, (W, n_win), 1)
        valid = jnp.abs(qpos - kpos) <= W
        scores = []
        for h in range(4):
            c = 128 * (h // 2)
            q = _split_pair(z_ref[pl.ds(q0, W), C_SQ + c:C_SQ + c + 128], lo_q)[h % 2]
            s_loc = jnp.where(valid, _dot_nt(q, z_ref[pl.ds(w0, n_win), ZB_SK + c:ZB_SK + c + 128]), NEG_INF)
            scores.append((s_loc, _dot(q, ckd[c:c + 128, :])))
        weights, extras = [], []
        for h, (s_loc, s_ctx) in enumerate(scores):
            sink = sink_ref[h]
            m = jnp.maximum(_rowmax(s_loc, s_ctx), sink)
            weights.append((jnp.exp(s_loc - m).astype(BF), jnp.exp(s_ctx - m).astype(BF)))
            extras.append(jnp.exp(sink - m))
        res = []
        for h, (e_loc, e_ctx) in enumerate(weights):
            c = 128 * (h // 2)
            v = _v_ones_pair(z_ref[pl.ds(w0, n_win), ZB_SV + c:ZB_SV + c + 128], lo_w)[h % 2]
            res.append(_dot(e_loc, v) + _dot_nt(e_ctx, (cva, cvb)[h % 2][c:c + 128, :]))
        for g in range(2):
            num, den = _merge_pair(res[2 * g], res[2 * g + 1], lo_q)
            den = den + jnp.where(lo_q, extras[2 * g], extras[2 * g + 1])
            o_ref[pl.ds(q0, W), 768 + 128 * g:768 + 128 * g + 128] = (num / den).astype(BF)
        return carry

    lax.fori_loop(0, DEC_SEQ // W, body, 0)


def _lat_mix_kernel(lam_init, sink_ref, z_ref, nak_ref, nav_ref, dfk_ref, dfv_ref, swk_ref, swv_ref, rpp_ref,
                    lamp_ref, subg_ref, cl_ref, sl_ref, cbd_ref, sbd_ref, wf_ref, o_ref,
                    ck, cva, cvb, vla, vlb, t2):
    @pl.when(pl.program_id(0) == 0)
    def _():
        _na_bias_tiles(rpp_ref, t2)

    _lat_na_part(z_ref, nak_ref, nav_ref, t2, o_ref, ck, cva, cvb)
    _lat_diff_part(lam_init, z_ref, dfk_ref, dfv_ref, lamp_ref, subg_ref, o_ref, ck, cva, cvb, vla, vlb)
    o_c = _fourier(z_ref[:, C_FC:C_FC + 256], cl_ref[...], sl_ref[...], cbd_ref[...], sbd_ref[...], wf_ref[...])
    o_ref[:, 512:768] = o_c.astype(BF)
    _lat_swa_part(sink_ref, z_ref, swk_ref, swv_ref, o_ref, ck, cva, cvb)


def _lat_mix(zb, caches_in, rpp, sink, lamp, subg, cl, sl, cbd, sbd, wf, lam_init, layer):
    full = lambda shape: pl.BlockSpec(shape, lambda b, s: (0,) * len(shape), pipeline_mode=pl.Buffered(1))
    cache = lambda width: pl.BlockSpec((None, None, width, PAST_LEN), lambda b, s: (b, layer, 0, 0))
    grid_spec = pltpu.PrefetchScalarGridSpec(
        num_scalar_prefetch=1,
        grid=(DEC_BATCH,),
        in_specs=[
            pl.BlockSpec((DEC_SEQ, ZB_WIDTH), lambda b, s: (LAT_BLOCK0 + b, 0)),
            cache(256), cache(256), cache(256), cache(256), cache(128), cache(128),
            pl.BlockSpec((None, 4, 2 * NA_KH, 128), lambda b, s: (layer, 0, 0, 0), pipeline_mode=pl.Buffered(1)),
            full((4, DIFF_QK_DIM)), full((1, 2 * HEAD_DIM)),
            full((DEC_SEQ, DEC_SEQ)), full((DEC_SEQ, DEC_SEQ)),
            full((256, 256)), full((256, 256)), full((256, 256)),
        ],
        out_specs=pl.BlockSpec((DEC_SEQ, MIX_WIDTH), lambda b, s: (b, 0)),
        scratch_shapes=[pltpu.VMEM((256, PAST_LEN), BF)] * 3 + [pltpu.VMEM((DEC_SEQ, 256), BF)] * 2
        + [pltpu.VMEM((4, 2 * NA_KH, GRID_W, 128), F32)],
    )
    return pl.pallas_call(
        functools.partial(_lat_mix_kernel, lam_init),
        grid_spec=grid_spec,
        out_shape=jax.ShapeDtypeStruct((N_LAT_TOK, MIX_WIDTH), BF),
        compiler_params=_params(),
        name="lat_mix",
    )(sink, zb, *caches_in, rpp, lamp, subg, cl, sl, cbd, sbd, wf)


def _dft_tables(n):
    j = np.arange(n)
    ang = 2.0 * np.pi * ((j[:, None] * j[None, :]) % n) / n
    return np.cos(ang) / np.sqrt(n), np.sin(ang) / np.sqrt(n)


def _block_diag4(m):
    out = np.zeros((256, 256), m.dtype)
    for g in range(4):
        out[64 * g:64 * g + 64, 64 * g:64 * g + 64] = m
    return out


def _rope_tables(n_axis_dims):
    half = n_axis_dims // 2
    inv = ROPE_BASE ** (-np.arange(half, dtype=np.float64) / half)
    t = np.arange(DEC_SEQ)
    lane = np.arange(128)
    w = lane % n_axis_dims
    is_col = (lane // n_axis_dims) % 2 == 1
    pos = np.where(is_col[None, :], (t % GRID_W)[:, None], (t // GRID_W)[:, None]).astype(np.float64)
    ang = pos * inv[w % half][None, :]
    sign = np.where(w < half, -1.0, 1.0)[None, :]
    cos = np.concatenate([np.cos(ang), np.ones((ROW_TILE, 128))], axis=0)
    sin = np.concatenate([np.sin(ang) * sign, np.zeros((ROW_TILE, 128))], axis=0)
    return jnp.asarray(cos, F32), jnp.asarray(sin, F32)


def _pad_rpb_rows(rpb):
    n_dc = rpb.shape[-1]
    half = jnp.pad(rpb, ((0, 0), (0, 0), (0, 1), (0, GRID_W - n_dc)))
    return jnp.concatenate([half, half], axis=-1)


def kernel(x_prompt, x_sample, cache_na_k, cache_na_v, cache_diff_k, cache_diff_v, cache_swa_k, cache_swa_v, c, c_ctx, w_ada, b_ada, norm1_g, norm2_g, w_in, na_rpb, diff_lq1, diff_lk1, diff_lq2, diff_lk2, diff_subln_g, w_fourier, swa_sink, w_out, w_mlp1, w_mlp2, final_g):
    xp = x_prompt.reshape(N_CTX_TOK, D_MODEL)
    xs = x_sample.reshape(N_LAT_TOK, D_MODEL)
    cond = jnp.zeros((MOD_ROWS, D_MODEL), F32).at[0].set(c_ctx).at[1:1 + DEC_BATCH].set(c)
    b_ada3 = b_ada.reshape(DEPTH, 1, 6 * D_MODEL)
    mods = _ada(cond, w_ada, b_ada3, 0)

    cl_p, sl_p = _dft_tables(SEQ)
    cl_s, sl_s = _dft_tables(DEC_SEQ)
    c64, s64 = _dft_tables(64)
    cl_p, sl_p, cl_s, sl_s, cbd, sbd = (
        jnp.asarray(a, F32).astype(BF) for a in (cl_p, sl_p, cl_s, sl_s, _block_diag4(c64), _block_diag4(s64)))
    ropes = _rope_tables(16) + _rope_tables(32)
    rpp = _pad_rpb_rows(na_rpb)
    wf_bf = w_fourier.astype(BF)

    ck_na, cv_na, ck_df, cv_df, ck_sw, cv_sw = (
        a.transpose(0, 1, 3, 4, 2).reshape(DEC_BATCH, DEPTH, -1, PAST_LEN)
        for a in (cache_na_k, cache_na_v, cache_diff_k, cache_diff_v, cache_swa_k, cache_swa_v))

    _, _, zb, caches, _ = _trunk(xp, xs, ropes, proj=(0, mods, norm1_g[0], w_in, None))
    ready = None
    for l in range(DEPTH):
        lam_init = 0.8 - 0.6 * math.exp(-0.3 * l)
        lamp = jnp.stack([diff_lq1[l], diff_lk1[l], diff_lq2[l], diff_lk2[l]], axis=0)
        subg = jnp.tile(diff_subln_g[l].reshape(1, HEAD_DIM), (1, 2))
        has_next = l + 1 < DEPTH

        cat_p, mods_next = _ctx_mix(zb, swa_sink[l], lamp, subg, cl_p, sl_p, cbd, sbd, wf_bf[l], lam_init,
                                    ada=(cond, w_ada, b_ada3, l + 1) if has_next else None)
        cat_s = _lat_mix(zb, (ck_na, cv_na, ck_df, cv_df, ck_sw, cv_sw), rpp, swa_sink[l], lamp, subg,
                         cl_s, sl_s, cbd, sbd, wf_bf[l], lam_init, l)
        precast = ()
        if has_next:
            precast = ((w_out, l + 1), (w_mlp1, l + 1), (w_mlp2, l + 1))
            precast += ((w_in, l + 2),) if l + 2 < DEPTH else ()
        wo, w1, w2, wi = (w_out, w_mlp1, w_mlp2, w_in) if ready is None else (ready + (None,))[:4]
        nxt = (l + 1, mods_next, norm1_g[l + 1], wi, caches) if has_next else None
        xp, xs, zb, new_caches, ready = _trunk(
            xp, xs, ropes, mlp=(l, mods, cat_p, cat_s, norm2_g[l], final_g, wo, w1, w2), proj=nxt,
            n_cast=FIRST_RESIDUAL_CAST_STEPS if ready is None else 0, precast=precast)
        caches = new_caches if has_next else caches
        mods = mods_next

    y_prompt = xp.reshape(BATCH, SEQ, D_MODEL)
    y_sample = xs.reshape(DEC_BATCH, DEC_SEQ, D_MODEL)
    new = [a.reshape(BATCH, DEPTH, SEQ, a.shape[-1] // HEAD_DIM, HEAD_DIM) for a in caches]
    return (y_prompt, y_sample) + tuple(new)
```

```python
import functools
import math

import numpy as np
import jax
import jax.numpy as jnp
from jax import lax
from jax.experimental import pallas as pl
from jax.experimental.pallas import tpu as pltpu

D_MODEL = 1024
BATCH = 16
SEQ = 256
DEPTH = 4
DEC_BATCH = 2
DEC_SEQ = 1024
PAST_LEN = 512
GRID_W = 64
GRID_ROWS = DEC_SEQ // GRID_W
HEAD_DIM = 64
NA_KH = 8
NA_KW = 16
DIFF_QK_DIM = 32
SWA_WINDOW = 128
D_FF = 4 * D_MODEL
ROPE_BASE = 10000.0
NORM_EPS = 1e-6
NEG_INF = -1e30
IN_WIDTH = 2304
MIX_WIDTH = 1024

N_CTX_TOK = BATCH * SEQ
N_LAT_TOK = DEC_BATCH * DEC_SEQ
N_TOK = N_CTX_TOK + N_LAT_TOK
ROW_TILE = 256
N_ROW_TILES = N_TOK // ROW_TILE
N_CTX_TILES = N_CTX_TOK // ROW_TILE
LAT_TILES_PER_REQ = DEC_SEQ // ROW_TILE
LAT_BLOCK0 = N_CTX_TOK // DEC_SEQ
MOD_ROWS = 8

C_NA_Q, C_NA_K, C_NA_V = 0, 256, 512
C_DQ, C_DK, C_DV = 768, 1024, 1280
C_FC = 1536
C_SQ, C_SK, C_SV = 1792, 2048, 2176
CACHE_COLS = ((C_NA_K, 256), (C_NA_V, 256), (C_DK, 256), (C_DV, 256), (C_SK, 128), (C_SV, 128))
ZB_SK, ZB_SV = 2048, 2304
ZB_WIDTH = 2560

NA_QROWS = 2
NA_WIN_ROWS = 10
NA_BLOCKS_PER_TRIP = 2

BF = jnp.bfloat16
F32 = jnp.float32
VMEM_LIMIT = 56 * 1024 * 1024


def _dot(a, b):
    return jnp.dot(a, b, preferred_element_type=F32)


def _dot_nt(a, b):
    return lax.dot_general(a, b, (((1,), (1,)), ((), ())), preferred_element_type=F32)


def _rmsnorm(x, g):
    ms = jnp.mean(x * x, axis=-1, keepdims=True)
    return x * lax.rsqrt(ms + NORM_EPS) * g


def _params(n_grid=1):
    return pltpu.CompilerParams(dimension_semantics=("arbitrary",) * n_grid, vmem_limit_bytes=VMEM_LIMIT)


def _rowmax(*parts):
    m = jnp.max(parts[0], axis=-1, keepdims=True)
    for p in parts[1:]:
        m = jnp.maximum(m, jnp.max(p, axis=-1, keepdims=True))
    return m


def _ada_kernel(cond_ref, w_ref, b_ref, o_ref):
    cnd = cond_ref[...]
    s = cnd / (1.0 + jnp.exp(-cnd))
    o_ref[...] = _dot(s.astype(BF), w_ref[...].astype(BF)) + b_ref[...]


def _ada_specs(layer, n_steps, index_of_step):
    tn = 6 * D_MODEL // n_steps
    col = lambda *g: index_of_step(*g)
    in_specs = [
        pl.BlockSpec((MOD_ROWS, D_MODEL), lambda *g: (0, 0)),
        pl.BlockSpec((None, D_MODEL, tn), lambda *g: (layer, 0, col(*g))),
        pl.BlockSpec((None, 1, tn), lambda *g: (layer, 0, col(*g))),
    ]
    return in_specs, pl.BlockSpec((MOD_ROWS, tn), lambda *g: (0, col(*g)))


def _ada(cond, w_ada, b_ada3, layer):
    in_specs, out_spec = _ada_specs(layer, 6, lambda j: j)
    return pl.pallas_call(
        _ada_kernel,
        grid=(6,),
        in_specs=in_specs,
        out_specs=out_spec,
        out_shape=jax.ShapeDtypeStruct((MOD_ROWS, 6 * D_MODEL), F32),
        compiler_params=_params(),
        name="ada",
    )(cond, w_ada, b_ada3)


def _mod_row(t):
    return jnp.where(t < N_CTX_TILES, 0, 1 + (t - N_CTX_TILES) // LAT_TILES_PER_REQ)


def _ctx_tile(t):
    return jnp.minimum(t, N_CTX_TILES - 1)


def _lat_tile(t):
    return jnp.maximum(t - N_CTX_TILES, 0)


TRUNK_CAST_STEPS = 8
FF_CHUNK = 1024
TRUNK_VMEM_LIMIT = 60 * 1024 * 1024


def _rope(x, cos, sin_signed, shift, first):
    partner = jnp.where(first, pltpu.roll(x, 128 - shift, 1), pltpu.roll(x, shift, 1))
    return x * cos + partner * sin_signed


def _projection_stages(hb, wb, t, is_ctx, zb_ref, cache_refs, rope_refs):
    cosd_ref, sind_ref, coss_ref, sins_ref = rope_refs
    dscale = DIFF_QK_DIM ** -0.5
    p0 = pl.multiple_of(
        jnp.where(is_ctx, DEC_SEQ, ((t - N_CTX_TILES) % LAT_TILES_PER_REQ) * ROW_TILE), ROW_TILE)
    lane = lax.broadcasted_iota(jnp.int32, (ROW_TILE, 128), 1)
    first_d = (lane % 16) < 8
    first_s = (lane % 32) < 16
    lo = lane < HEAD_DIM
    cosd, sind = cosd_ref[pl.ds(p0, ROW_TILE), :], sind_ref[pl.ds(p0, ROW_TILE), :]
    coss, sins = coss_ref[pl.ds(p0, ROW_TILE), :], sins_ref[pl.ds(p0, ROW_TILE), :]

    def cols(c0, width):
        return _dot(hb(), wb[:, c0:c0 + width])

    def put(c0, width, val):
        zb_ref[:, c0:c0 + width] = val.astype(BF)

    def put_dup(c0, val):
        r = pltpu.roll(val, HEAD_DIM, 1)
        put(c0, 128, jnp.where(lo, val, r))
        put(c0 + 128, 128, jnp.where(lo, r, val))

    kept = {}

    def na_q():
        put(C_NA_Q, 256, cols(C_NA_Q, 256) * 0.125)

    def na_kv():
        kept["nakv"] = cols(C_NA_K, 512)
        put(C_NA_K, 512, kept["nakv"])

    def diff_q():
        z = cols(C_DQ, 256)
        for j in range(2):
            put(C_DQ + 128 * j, 128, _rope(z[:, 128 * j:128 * j + 128], cosd, sind, 8, first_d) * dscale)

    def diff_k():
        kept["dk"] = cols(C_DK, 256)
        for j in range(2):
            put(C_DK + 128 * j, 128, _rope(kept["dk"][:, 128 * j:128 * j + 128], cosd, sind, 8, first_d))

    def diff_v_fourier():
        kept["dvfc"] = cols(C_DV, 512)
        put(C_DV, 512, kept["dvfc"])

    def swa_q():
        z = cols(C_SQ, 256)
        for j in range(2):
            put(C_SQ + 128 * j, 128, _rope(z[:, 128 * j:128 * j + 128], coss, sins, 16, first_s) * 0.125)

    def swa_kv():
        kept["skv"] = cols(C_SK, 256)
        put_dup(ZB_SK, _rope(kept["skv"][:, 0:128], coss, sins, 16, first_s))
        put_dup(ZB_SV, kept["skv"][:, 128:256])

    def finish():
        new = (kept["nakv"][:, 0:256], kept["nakv"][:, 256:512], kept["dk"], kept["dvfc"][:, 0:256],
               kept["skv"][:, 0:128], kept["skv"][:, 128:256])
        for ref, val in zip(cache_refs, new):
            ref[...] = jnp.where(is_ctx, val, ref[...])

    return [na_q, na_kv, diff_q, diff_k, diff_v_fourier, swa_q, swa_kv], finish


def _trunk_kernel(do_mlp, do_proj, last, n_alias, *refs):
    xp_ref, xs_ref = refs[:2]
    pos = 2
    if do_mlp:
        catp_ref, cats_ref, mm_ref, g2_ref, fg_ref, wo_ref, w1_ref, w2_ref = refs[pos:pos + 8]
        pos += 8
    if do_proj:
        pm_ref, g1_ref, win_ref = refs[pos:pos + 3]
        rope_refs = refs[pos + 3:pos + 7]
        pos += 7 + n_alias
    if do_mlp:
        op_ref, os_ref = refs[pos:pos + 2]
        pos += 2
    if do_proj:
        zb_ref = refs[pos]
        cache_refs = refs[pos + 1:pos + 7]
        pos += 7
    if do_mlp:
        wob, w1b, w2b = refs[pos:pos + 3]
        pos += 3
    if do_proj:
        wb = refs[pos]
        pos += 1
    skew = do_proj
    if skew:
        hprev = refs[pos]
    s = pl.program_id(0)

    @pl.when(s < TRUNK_CAST_STEPS)
    def _():
        def cast(dst, src):
            rows = src.shape[0]
            dst[pl.ds(pl.multiple_of(s * rows, rows), rows), :] = src[...].astype(BF)

        if do_mlp:
            cast(wob, wo_ref)
            cast(w1b, w1_ref)
            cast(w2b, w2_ref)
        if do_proj:
            cast(wb, win_ref)
        if skew:
            @pl.when(s == 0)
            def _():
                hprev[...] = jnp.zeros_like(hprev)

    def tile_step(t, proj_stages, proj_finish):
        pending = list(proj_stages)

        def issue(n):
            for _ in range(min(n, len(pending))):
                pending.pop(0)()

        is_ctx = t < N_CTX_TILES
        row = _mod_row(t)
        x = jnp.where(is_ctx, xp_ref[...], xs_ref[...])
        if do_mlp:
            issue(1)
            gt1 = mm_ref[pl.ds(row, 1), 2 * D_MODEL:3 * D_MODEL]
            sh2 = mm_ref[pl.ds(row, 1), 3 * D_MODEL:4 * D_MODEL]
            sc2 = mm_ref[pl.ds(row, 1), 4 * D_MODEL:5 * D_MODEL]
            gt2 = mm_ref[pl.ds(row, 1), 5 * D_MODEL:6 * D_MODEL]
            cat = jnp.where(is_ctx, catp_ref[...], cats_ref[...])
            x = x + gt1 * _dot(cat, wob[...])
            issue(1)
            hh = (_rmsnorm(x, g2_ref[...]) * (1.0 + sc2) + sh2).astype(BF)
            acc = jnp.zeros((ROW_TILE, D_MODEL), F32)
            for c in range(D_FF // FF_CHUNK):
                a = jnp.maximum(_dot(hh, w1b[:, c * FF_CHUNK:(c + 1) * FF_CHUNK]), 0.0)
                acc = acc + _dot((a * a).astype(BF), w2b[c * FF_CHUNK:(c + 1) * FF_CHUNK, :])
                if c < 2:
                    issue(1)
            issue(1)
            x = x + gt2 * acc
            out = _rmsnorm(x, fg_ref[...]) if last else x
        hb = None
        if do_proj:
            issue(1)
            shift = pm_ref[pl.ds(row, 1), 0:D_MODEL]
            scale = pm_ref[pl.ds(row, 1), D_MODEL:2 * D_MODEL]
            hb = (_rmsnorm(x, g1_ref[...]) * (1.0 + scale) + shift).astype(BF)
        issue(len(pending))
        if proj_finish is not None:
            proj_finish()
        if do_mlp:
            op_ref[...] = jnp.where(is_ctx, out, op_ref[...])
            os_ref[...] = jnp.where(is_ctx, os_ref[...], out)
        return hb

    def projection(hb, t):
        return _projection_stages(hb, wb, t, t < N_CTX_TILES, zb_ref, cache_refs, rope_refs)

    @pl.when(s >= TRUNK_CAST_STEPS)
    def _():
        t = s - TRUNK_CAST_STEPS
        if not skew:
            hb = tile_step(t, [], None)
            if do_proj:
                stages, finish = projection(lambda: hb, t)
                for stage in stages:
                    stage()
                finish()
        else:
            u = jnp.maximum(t - 1, 0)

            @pl.when(t < N_ROW_TILES)
            def _():
                stages, finish = projection(lambda: hprev[...], u)
                hprev[...] = tile_step(t, stages, finish)

            @pl.when(t >= N_ROW_TILES)
            def _():
                stages, finish = projection(lambda: hprev[...], u)
                for stage in stages:
                    stage()
                finish()


def _trunk(xp, xs, ropes, mlp=None, proj=None):
    do_mlp, do_proj = mlp is not None, proj is not None
    last = do_mlp and not do_proj
    skew = do_proj
    clamp_tile = lambda s: jnp.clip(s - TRUNK_CAST_STEPS, 0, N_ROW_TILES - 1)
    step = lambda f: (lambda s: f(clamp_tile(s)))
    step_proj = (lambda f: (lambda s: f(clamp_tile(s - 1)))) if skew else step
    const = lambda shape: pl.BlockSpec(shape, lambda s: (0,) * len(shape), pipeline_mode=pl.Buffered(1))
    chunk = lambda layer, rows, cols: pl.BlockSpec(
        (None, rows // TRUNK_CAST_STEPS, cols), lambda s: (layer, jnp.minimum(s, TRUNK_CAST_STEPS - 1), 0))
    mod_spec = const((MOD_ROWS, 6 * D_MODEL))
    x_specs = [pl.BlockSpec((ROW_TILE, D_MODEL), step(lambda t: (_ctx_tile(t), 0))),
               pl.BlockSpec((ROW_TILE, D_MODEL), step(lambda t: (_lat_tile(t), 0)))]
    args, in_specs = [xp, xs], list(x_specs)
    out_specs, out_shape, scratch, aliases = [], [], [], {}
    n_alias = 0
    if do_mlp:
        layer, mods_l, cat_p, cat_s, g2, final_g, wo, w1, w2 = mlp
        args += [cat_p, cat_s, mods_l, g2.reshape(1, D_MODEL), final_g.reshape(1, D_MODEL), wo, w1, w2]
        in_specs += [pl.BlockSpec((ROW_TILE, MIX_WIDTH), step(lambda t: (_ctx_tile(t), 0))),
                     pl.BlockSpec((ROW_TILE, MIX_WIDTH), step(lambda t: (_lat_tile(t), 0))), mod_spec,
                     const((1, D_MODEL)), const((1, D_MODEL)),
                     chunk(layer, MIX_WIDTH, D_MODEL), chunk(layer, D_MODEL, D_FF), chunk(layer, D_FF, D_MODEL)]
    if do_proj:
        p_layer, mods_p, g1, w_in, caches = proj
        args += [mods_p, g1.reshape(1, D_MODEL), w_in, *ropes]
        in_specs += [mod_spec, const((1, D_MODEL)), chunk(p_layer, D_MODEL, IN_WIDTH)]
        in_specs += [const((DEC_SEQ + ROW_TILE, 128))] * 4
        if caches is not None:
            n_alias = 6
            n_out_before = 2 if do_mlp else 0
            aliases = {len(args) + i: n_out_before + 1 + i for i in range(6)}
            args += list(caches)
            in_specs += [pl.BlockSpec(memory_space=pl.ANY)] * 6
    if do_mlp:
        out_specs += x_specs
        out_shape += [jax.ShapeDtypeStruct((N_CTX_TOK, D_MODEL), F32),
                      jax.ShapeDtypeStruct((N_LAT_TOK, D_MODEL), F32)]
        scratch += [pltpu.VMEM((MIX_WIDTH, D_MODEL), BF), pltpu.VMEM((D_MODEL, D_FF), BF),
                    pltpu.VMEM((D_FF, D_MODEL), BF)]
    if do_proj:
        out_specs += [pl.BlockSpec((ROW_TILE, ZB_WIDTH), step_proj(lambda t: (t, 0)))] + [
            pl.BlockSpec((None, None, SEQ, width), step_proj(lambda t: (_ctx_tile(t), p_layer, 0, 0)))
            for _, width in CACHE_COLS]
        out_shape += [jax.ShapeDtypeStruct((N_TOK, ZB_WIDTH), BF)] + [
            jax.ShapeDtypeStruct((BATCH, DEPTH, SEQ, width), F32) for _, width in CACHE_COLS]
        scratch += [pltpu.VMEM((D_MODEL, IN_WIDTH), BF)]
    if skew:
        scratch += [pltpu.VMEM((ROW_TILE, D_MODEL), BF)]
    outs = pl.pallas_call(
        functools.partial(_trunk_kernel, do_mlp, do_proj, last, n_alias),
        grid=(TRUNK_CAST_STEPS + N_ROW_TILES + (1 if skew else 0),),
        in_specs=in_specs,
        out_specs=out_specs,
        out_shape=out_shape,
        scratch_shapes=scratch,
        input_output_aliases=aliases,
        compiler_params=pltpu.CompilerParams(dimension_semantics=("arbitrary",),
                                             vmem_limit_bytes=TRUNK_VMEM_LIMIT),
        name="trunk_mlp_proj" if (do_mlp and do_proj) else ("trunk_mlp" if do_mlp else "trunk_proj"),
    )(*args)
    outs = list(outs)
    if do_mlp:
        xp, xs = outs[:2]
        outs = outs[2:]
    zb, new_caches = (outs[0], tuple(outs[1:7])) if do_proj else (None, None)
    return xp, xs, zb, new_caches


def _diff_lambda(lamp_ref, lam_init):
    a = jnp.sum(lamp_ref[0:1, :] * lamp_ref[1:2, :], axis=-1, keepdims=True)
    b = jnp.sum(lamp_ref[2:3, :] * lamp_ref[3:4, :], axis=-1, keepdims=True)
    return jnp.exp(a) - jnp.exp(b) + lam_init


def _lane_lo(n):
    return lax.broadcasted_iota(jnp.int32, (n, 128), 1) < HEAD_DIM


def _lane_lo_wide(n, width):
    return lax.broadcasted_iota(jnp.int32, (n, width), 1) % 128 < HEAD_DIM


def _split_pair(x, lo):
    zero = jnp.zeros_like(x)
    return jnp.where(lo, x, zero), jnp.where(lo, zero, x)


def _v_ones_pair(v, lo):
    one = jnp.ones_like(v)
    return jnp.where(lo, v, one), jnp.where(lo, one, v)


def _v_stack(v, lo):
    zero = jnp.zeros_like(v)
    ones_a = jnp.where(lo, 1.0, 0.0).astype(v.dtype)
    ones_b = jnp.where(lo, 0.0, 1.0).astype(v.dtype)
    top = jnp.concatenate([jnp.where(lo, v, zero), ones_a], axis=1)
    bottom = jnp.concatenate([jnp.where(lo, zero, v), ones_b], axis=1)
    return jnp.concatenate([top, bottom], axis=0)


def _merge_pair(a, b, lo):
    return jnp.where(lo, a, b), pltpu.roll(jnp.where(lo, b, a), HEAD_DIM, 1)


def _subln_pair(o, g2, lam_init, lo):
    sq = o * o
    ms_a = jnp.sum(jnp.where(lo, sq, 0.0), axis=-1, keepdims=True)
    ms_b = jnp.sum(jnp.where(lo, 0.0, sq), axis=-1, keepdims=True)
    ms = jnp.where(lo, ms_a, ms_b) * (1.0 / HEAD_DIM)
    return o * lax.rsqrt(ms + NORM_EPS) * g2 * (1.0 - lam_init)


def _diff_quarters(q, n):
    quarter = lax.broadcasted_iota(jnp.int32, (n, 128), 1) // DIFF_QK_DIM
    zero = jnp.zeros_like(q)
    return [jnp.where(quarter == i, q, zero) for i in range(4)]


def _fourier(x_bf, cl, sl, cbd, sbd, wf):
    xc = _dot(x_bf, cbd).astype(BF)
    xs = _dot(x_bf, sbd).astype(BF)
    y = _dot(cl, xc) - _dot(sl, xs)
    return _dot(y.astype(BF), wf)


CTX_UNIT_GROUP = 8
CTX_REQ_PER_STEP = 2


def _ctx_mix_kernel(lam_init, with_ada, sink_ref, z_ref, lamp_ref, subg_ref, cl_ref, sl_ref, cbd_ref, sbd_ref,
                    wf_ref, *rest):
    if with_ada:
        cond_ref, wa_ref, ba_ref, o_ref, mods_ref = rest
        _ada_kernel(cond_ref, wa_ref, ba_ref, mods_ref)
    else:
        o_ref, = rest
    for r in range(CTX_REQ_PER_STEP):
        rows = pl.ds(r * SEQ, SEQ)
        _ctx_request(lam_init, sink_ref, z_ref.at[rows], lamp_ref, subg_ref, cl_ref, sl_ref, cbd_ref, sbd_ref,
                     wf_ref, o_ref.at[rows])


def _ctx_request(lam_init, sink_ref, z_ref, lamp_ref, subg_ref, cl_ref, sl_ref, cbd_ref, sbd_ref, wf_ref, o_ref):
    lo = _lane_lo(SEQ)

    pairs = []
    for j in range(2):
        c = 128 * j
        qs = _split_pair(z_ref[:, C_NA_Q + c:C_NA_Q + c + 128], lo)
        pairs.append((qs[0], qs[1], C_NA_K + c, _v_stack(z_ref[:, C_NA_V + c:C_NA_V + c + 128], lo), None))
    for j in range(2):
        c = 128 * j
        q4 = _diff_quarters(z_ref[:, C_DQ + c:C_DQ + c + 128], SEQ)
        v2 = _v_stack(z_ref[:, C_DV + c:C_DV + c + 128], lo)
        pairs.append((q4[0], q4[2], C_DK + c, v2, None))
        pairs.append((q4[1], q4[3], C_DK + c, v2, None))
    for g in range(2):
        c = 128 * g
        qs = _split_pair(z_ref[:, C_SQ + c:C_SQ + c + 128], lo)
        pairs.append((qs[0], qs[1], ZB_SK + c, _v_stack(z_ref[:, ZB_SV + c:ZB_SV + c + 128], lo),
                      (sink_ref[2 * g], sink_ref[2 * g + 1])))

    res = []
    for p0 in range(0, len(pairs), CTX_UNIT_GROUP // 2):
        group = pairs[p0:p0 + CTX_UNIT_GROUP // 2]
        scores = [[_dot_nt(q, z_ref[:, kc:kc + 128]) for q in (qa, qb)] for qa, qb, kc, _, _ in group]
        weights, extras = [], []
        for ss, (_, _, _, _, sinks) in zip(scores, group):
            es, xs = [], []
            for i, s in enumerate(ss):
                m = _rowmax(s)
                if sinks is not None:
                    m = jnp.maximum(m, sinks[i])
                    xs.append(jnp.exp(sinks[i] - m))
                es.append(jnp.exp(s - m).astype(BF))
            weights.append(jnp.concatenate(es, axis=1))
            extras.append(xs)
        for e2, xs, (_, _, _, v2, _) in zip(weights, extras, group):
            r = _dot(e2, v2)
            num, den = r[:, 0:128], r[:, 128:256]
            if xs:
                den = den + jnp.where(lo, xs[0], xs[1])
            res.append(num / den)

    for j in range(2):
        o_ref[:, 128 * j:128 * j + 128] = res[j].astype(BF)

    lam = _diff_lambda(lamp_ref, lam_init)
    for j in range(2):
        o = res[2 + 2 * j] - lam * res[3 + 2 * j]
        o_ref[:, 256 + 128 * j:384 + 128 * j] = _subln_pair(o, subg_ref[...], lam_init, lo).astype(BF)

    o_c = _fourier(z_ref[:, C_FC:C_FC + 256], cl_ref[...], sl_ref[...], cbd_ref[...], sbd_ref[...], wf_ref[...])
    o_ref[:, 512:768] = o_c.astype(BF)

    for g in range(2):
        o_ref[:, 768 + 128 * g:896 + 128 * g] = res[6 + g].astype(BF)


def _ctx_mix(zb, sink, lamp, subg, cl, sl, cbd, sbd, wf, lam_init, ada=None):
    full = lambda shape: pl.BlockSpec(shape, lambda b, s: (0,) * len(shape))
    n_steps = BATCH // CTX_REQ_PER_STEP
    in_specs = [
        pl.BlockSpec((CTX_REQ_PER_STEP * SEQ, ZB_WIDTH), lambda b, s: (b, 0)),
        full((4, DIFF_QK_DIM)),
        full((1, 2 * HEAD_DIM)),
        full((SEQ, SEQ)), full((SEQ, SEQ)),
        full((256, 256)), full((256, 256)), full((256, 256)),
    ]
    out_specs = [pl.BlockSpec((CTX_REQ_PER_STEP * SEQ, MIX_WIDTH), lambda b, s: (b, 0))]
    out_shape = [jax.ShapeDtypeStruct((N_CTX_TOK, MIX_WIDTH), BF)]
    args = [sink, zb, lamp, subg, cl, sl, cbd, sbd, wf]
    if ada is not None:
        cond, w_ada, b_ada3, layer = ada
        ada_in, ada_out = _ada_specs(layer, n_steps, lambda b, s: b)
        in_specs += ada_in
        out_specs.append(ada_out)
        out_shape.append(jax.ShapeDtypeStruct((MOD_ROWS, 6 * D_MODEL), F32))
        args += [cond, w_ada, b_ada3]
    grid_spec = pltpu.PrefetchScalarGridSpec(
        num_scalar_prefetch=1, grid=(n_steps,), in_specs=in_specs, out_specs=out_specs)
    outs = pl.pallas_call(
        functools.partial(_ctx_mix_kernel, lam_init, ada is not None),
        grid_spec=grid_spec,
        out_shape=out_shape,
        compiler_params=_params(),
        name="ctx_mix",
    )(*args)
    return outs[0], (outs[1] if ada is not None else None)


def _cache_v_ones(cv_ref, cva, cvb):
    first = lax.broadcasted_iota(jnp.int32, (256, PAST_LEN), 0) % 128 < HEAD_DIM
    cv = cv_ref[...]
    cva[...] = jnp.where(first, cv, 1.0).astype(BF)
    cvb[...] = jnp.where(first, 1.0, cv).astype(BF)


def _na_bias_tiles(rpp_ref, t2):
    c = lax.broadcasted_iota(jnp.int32, (GRID_W, 128), 0)
    kc = lax.broadcasted_iota(jnp.int32, (GRID_W, 128), 1) % GRID_W
    c_start = jnp.clip(c - NA_KW // 2, 0, GRID_W - NA_KW)
    inside = jnp.logical_and(kc >= c_start, kc < c_start + NA_KW)
    for h in range(4):
        for dr in range(2 * NA_KH - 1):
            row = jnp.broadcast_to(rpp_ref[h, dr:dr + 1, :], (GRID_W, 128))
            toeplitz = pltpu.roll(row, 128 - (NA_KW - 1), 1, stride=1, stride_axis=0)
            t2[h, dr] = jnp.where(inside, toeplitz, NEG_INF)
        t2[h, 2 * NA_KH - 1] = jnp.full((GRID_W, 128), NEG_INF, F32)


def _lat_na_part(z_ref, ck_ref, cv_ref, t2, o_ref, ckb, cva, cvb):
    ckb[...] = ck_ref[...].astype(BF)
    _cache_v_ones(cv_ref, cva, cvb)
    n_q = NA_QROWS * GRID_W
    n_loc = NA_WIN_ROWS * GRID_W
    lo_q = _lane_lo(n_q)
    lo_w = _lane_lo(n_loc)

    lo_t = _lane_lo(GRID_W)

    def block(p):
        w_row = jnp.clip(NA_QROWS * p - NA_KH // 2, 0, GRID_ROWS - NA_WIN_ROWS)
        q0 = pl.multiple_of(p * n_q, n_q)
        k0 = pl.multiple_of(w_row * GRID_W, NA_QROWS * GRID_W)

        def tile_index(r, j):
            r_start = jnp.clip(r - NA_KH // 2, 0, GRID_ROWS - NA_KH)
            kr = w_row + j
            inside = jnp.logical_and(kr >= r_start, kr < r_start + NA_KH)
            return jnp.where(inside, kr - r + NA_KH - 1, 2 * NA_KH - 1)

        def bias(h):
            rows = []
            for rl in range(NA_QROWS):
                r = NA_QROWS * p + rl
                pieces = [jnp.where(lo_t, t2[h, tile_index(r, 2 * jj)], t2[h, tile_index(r, 2 * jj + 1)])
                          for jj in range(NA_WIN_ROWS // 2)]
                rows.append(jnp.concatenate(pieces, axis=1))
            return jnp.concatenate(rows, axis=0)

        return q0, k0, bias

    def body(i, carry):
        blocks = [block(NA_BLOCKS_PER_TRIP * i + b) for b in range(NA_BLOCKS_PER_TRIP)]
        units = [(q0, k0, bias, h) for q0, k0, bias in blocks for h in range(4)]
        scores = []
        for q0, k0, bias, h in units:
            c = 128 * (h // 2)
            q = _split_pair(z_ref[pl.ds(q0, n_q), C_NA_Q + c:C_NA_Q + c + 128], lo_q)[h % 2]
            s_loc = _dot_nt(q, z_ref[pl.ds(k0, n_loc), C_NA_K + c:C_NA_K + c + 128]) + bias(h)
            s_ctx = _dot(q, ckb[c:c + 128, :])
            scores.append((s_loc, s_ctx))
        weights = []
        for s_loc, s_ctx in scores:
            m = _rowmax(s_loc, s_ctx)
            weights.append((jnp.exp(s_loc - m).astype(BF), jnp.exp(s_ctx - m).astype(BF)))
        res = []
        for (q0, k0, bias, h), (e_loc, e_ctx) in zip(units, weights):
            c = 128 * (h // 2)
            v = _v_ones_pair(z_ref[pl.ds(k0, n_loc), C_NA_V + c:C_NA_V + c + 128], lo_w)[h % 2]
            cv = (cva, cvb)[h % 2][c:c + 128, :]
            res.append(_dot(e_loc, v) + _dot_nt(e_ctx, cv))
        for b, (q0, _, _) in enumerate(blocks):
            for j in range(2):
                num, den = _merge_pair(res[4 * b + 2 * j], res[4 * b + 2 * j + 1], lo_q)
                o_ref[pl.ds(q0, n_q), 128 * j:128 * j + 128] = (num / den).astype(BF)
        return carry

    lax.fori_loop(0, GRID_ROWS // NA_QROWS // NA_BLOCKS_PER_TRIP, body, 0)


DIFF_QBLK = 1024
DIFF_GROUP = 2


def _lat_diff_part(lam_init, z_ref, ck_ref, cv_ref, lamp_ref, subg_ref, o_ref, ckb, cva, cvb, vla, vlb):
    ckb[...] = ck_ref[...].astype(BF)
    _cache_v_ones(cv_ref, cva, cvb)
    lo_v = _lane_lo_wide(DEC_SEQ, 256)
    v_loc = z_ref[:, C_DV:C_DV + 256]
    one = jnp.ones_like(v_loc)
    vla[...] = jnp.where(lo_v, v_loc, one)
    vlb[...] = jnp.where(lo_v, one, v_loc)
    lam = _diff_lambda(lamp_ref, lam_init)
    qblk = DIFF_QBLK
    lo_q = _lane_lo(qblk)

    def body(i, carry):
        q0 = pl.multiple_of(i * qblk, qblk)
        for j in range(2):
            c = 128 * j
            q4 = _diff_quarters(z_ref[pl.ds(q0, qblk), C_DQ + c:C_DQ + c + 128], qblk)
            res = []
            for t0 in range(0, 4, DIFF_GROUP):
                scores = [(_dot_nt(q, z_ref[:, C_DK + c:C_DK + c + 128]), _dot(q, ckb[c:c + 128, :]))
                          for q in q4[t0:t0 + DIFF_GROUP]]
                weights = []
                for s_loc, s_ctx in scores:
                    m = _rowmax(s_loc, s_ctx)
                    weights.append((jnp.exp(s_loc - m).astype(BF), jnp.exp(s_ctx - m).astype(BF)))
                for t, (e_loc, e_ctx) in enumerate(weights, start=t0):
                    v_loc, v_ctx = ((vla, cva), (vlb, cvb))[t // 2]
                    res.append(_dot(e_loc, v_loc[:, c:c + 128]) + _dot_nt(e_ctx, v_ctx[c:c + 128, :]))
            n1, d1 = _merge_pair(res[0], res[2], lo_q)
            n2, d2 = _merge_pair(res[1], res[3], lo_q)
            o = n1 / d1 - lam * (n2 / d2)
            o_ref[pl.ds(q0, qblk), 256 + c:256 + c + 128] = (
                _subln_pair(o, subg_ref[...], lam_init, lo_q).astype(BF))
        return carry

    lax.fori_loop(0, DEC_SEQ // qblk, body, 0)


def _lat_swa_part(sink_ref, z_ref, ck_ref, cv_ref, o_ref, ckd, cva, cvb):
    W = SWA_WINDOW
    n_win = 3 * W
    ones = jnp.ones((HEAD_DIM, PAST_LEN), BF)
    for g in range(2):
        k_g = ck_ref[HEAD_DIM * g:HEAD_DIM * (g + 1), :].astype(BF)
        v_g = cv_ref[HEAD_DIM * g:HEAD_DIM * (g + 1), :].astype(BF)
        r0, r1, r2 = 128 * g, 128 * g + HEAD_DIM, 128 * (g + 1)
        ckd[r0:r1, :] = k_g
        ckd[r1:r2, :] = k_g
        cva[r0:r1, :] = v_g
        cva[r1:r2, :] = ones
        cvb[r0:r1, :] = ones
        cvb[r1:r2, :] = v_g
    lo_q = _lane_lo(W)
    lo_w = _lane_lo(n_win)

    def body(n, carry):
        q0 = pl.multiple_of(n * W, W)
        w0 = pl.multiple_of(jnp.clip((n - 1) * W, 0, DEC_SEQ - n_win), W)
        qpos = q0 + lax.broadcasted_iota(jnp.int32, (W, n_win), 0)
        kpos = w0 + lax.broadcasted_iota(jnp.int32, (W, n_win), 1)
        valid = jnp.abs(qpos - kpos) <= W
        scores = []
        for h in range(4):
            c = 128 * (h // 2)
            q = _split_pair(z_ref[pl.ds(q0, W), C_SQ + c:C_SQ + c + 128], lo_q)[h % 2]
            s_loc = jnp.where(valid, _dot_nt(q, z_ref[pl.ds(w0, n_win), ZB_SK + c:ZB_SK + c + 128]), NEG_INF)
            scores.append((s_loc, _dot(q, ckd[c:c + 128, :])))
        weights, extras = [], []
        for h, (s_loc, s_ctx) in enumerate(scores):
            sink = sink_ref[h]
            m = jnp.maximum(_rowmax(s_loc, s_ctx), sink)
            weights.append((jnp.exp(s_loc - m).astype(BF), jnp.exp(s_ctx - m).astype(BF)))
            extras.append(jnp.exp(sink - m))
        res = []
        for h, (e_loc, e_ctx) in enumerate(weights):
            c = 128 * (h // 2)
            v = _v_ones_pair(z_ref[pl.ds(w0, n_win), ZB_SV + c:ZB_SV + c + 128], lo_w)[h % 2]
            res.append(_dot(e_loc, v) + _dot_nt(e_ctx, (cva, cvb)[h % 2][c:c + 128, :]))
        for g in range(2):
            num, den = _merge_pair(res[2 * g], res[2 * g + 1], lo_q)
            den = den + jnp.where(lo_q, extras[2 * g], extras[2 * g + 1])
            o_ref[pl.ds(q0, W), 768 + 128 * g:768 + 128 * g + 128] = (num / den).astype(BF)
        return carry

    lax.fori_loop(0, DEC_SEQ // W, body, 0)


def _lat_mix_kernel(lam_init, sink_ref, z_ref, nak_ref, nav_ref, dfk_ref, dfv_ref, swk_ref, swv_ref, rpp_ref,
                    lamp_ref, subg_ref, cl_ref, sl_ref, cbd_ref, sbd_ref, wf_ref, o_ref,
                    ck, cva, cvb, vla, vlb, t2):
    @pl.when(pl.program_id(0) == 0)
    def _():
        _na_bias_tiles(rpp_ref, t2)

    _lat_na_part(z_ref, nak_ref, nav_ref, t2, o_ref, ck, cva, cvb)
    _lat_diff_part(lam_init, z_ref, dfk_ref, dfv_ref, lamp_ref, subg_ref, o_ref, ck, cva, cvb, vla, vlb)
    o_c = _fourier(z_ref[:, C_FC:C_FC + 256], cl_ref[...], sl_ref[...], cbd_ref[...], sbd_ref[...], wf_ref[...])
    o_ref[:, 512:768] = o_c.astype(BF)
    _lat_swa_part(sink_ref, z_ref, swk_ref, swv_ref, o_ref, ck, cva, cvb)


def _lat_mix(zb, caches_in, rpp, sink, lamp, subg, cl, sl, cbd, sbd, wf, lam_init, layer):
    full = lambda shape: pl.BlockSpec(shape, lambda b, s: (0,) * len(shape), pipeline_mode=pl.Buffered(1))
    cache = lambda width: pl.BlockSpec((None, None, width, PAST_LEN), lambda b, s: (b, layer, 0, 0))
    grid_spec = pltpu.PrefetchScalarGridSpec(
        num_scalar_prefetch=1,
        grid=(DEC_BATCH,),
        in_specs=[
            pl.BlockSpec((DEC_SEQ, ZB_WIDTH), lambda b, s: (LAT_BLOCK0 + b, 0)),
            cache(256), cache(256), cache(256), cache(256), cache(128), cache(128),
            pl.BlockSpec((None, 4, 2 * NA_KH, 128), lambda b, s: (layer, 0, 0, 0), pipeline_mode=pl.Buffered(1)),
            full((4, DIFF_QK_DIM)), full((1, 2 * HEAD_DIM)),
            full((DEC_SEQ, DEC_SEQ)), full((DEC_SEQ, DEC_SEQ)),
            full((256, 256)), full((256, 256)), full((256, 256)),
        ],
        out_specs=pl.BlockSpec((DEC_SEQ, MIX_WIDTH), lambda b, s: (b, 0)),
        scratch_shapes=[pltpu.VMEM((256, PAST_LEN), BF)] * 3 + [pltpu.VMEM((DEC_SEQ, 256), BF)] * 2
        + [pltpu.VMEM((4, 2 * NA_KH, GRID_W, 128), F32)],
    )
    return pl.pallas_call(
        functools.partial(_lat_mix_kernel, lam_init),
        grid_spec=grid_spec,
        out_shape=jax.ShapeDtypeStruct((N_LAT_TOK, MIX_WIDTH), BF),
        compiler_params=_params(),
        name="lat_mix",
    )(sink, zb, *caches_in, rpp, lamp, subg, cl, sl, cbd, sbd, wf)


def _dft_tables(n):
    j = np.arange(n)
    ang = 2.0 * np.pi * ((j[:, None] * j[None, :]) % n) / n
    return np.cos(ang) / np.sqrt(n), np.sin(ang) / np.sqrt(n)


def _block_diag4(m):
    out = np.zeros((256, 256), m.dtype)
    for g in range(4):
        out[64 * g:64 * g + 64, 64 * g:64 * g + 64] = m
    return out


def _rope_tables(n_axis_dims):
    half = n_axis_dims // 2
    inv = ROPE_BASE ** (-np.arange(half, dtype=np.float64) / half)
    t = np.arange(DEC_SEQ)
    lane = np.arange(128)
    w = lane % n_axis_dims
    is_col = (lane // n_axis_dims) % 2 == 1
    pos = np.where(is_col[None, :], (t % GRID_W)[:, None], (t // GRID_W)[:, None]).astype(np.float64)
    ang = pos * inv[w % half][None, :]
    sign = np.where(w < half, -1.0, 1.0)[None, :]
    cos = np.concatenate([np.cos(ang), np.ones((ROW_TILE, 128))], axis=0)
    sin = np.concatenate([np.sin(ang) * sign, np.zeros((ROW_TILE, 128))], axis=0)
    return jnp.asarray(cos, F32), jnp.asarray(sin, F32)


def _pad_rpb_rows(rpb):
    n_dc = rpb.shape[-1]
    half = jnp.pad(rpb, ((0, 0), (0, 0), (0, 1), (0, GRID_W - n_dc)))
    return jnp.concatenate([half, half], axis=-1)


def kernel(x_prompt, x_sample, cache_na_k, cache_na_v, cache_diff_k, cache_diff_v, cache_swa_k, cache_swa_v, c, c_ctx, w_ada, b_ada, norm1_g, norm2_g, w_in, na_rpb, diff_lq1, diff_lk1, diff_lq2, diff_lk2, diff_subln_g, w_fourier, swa_sink, w_out, w_mlp1, w_mlp2, final_g):
    xp = x_prompt.reshape(N_CTX_TOK, D_MODEL)
    xs = x_sample.reshape(N_LAT_TOK, D_MODEL)
    cond = jnp.zeros((MOD_ROWS, D_MODEL), F32).at[0].set(c_ctx).at[1:1 + DEC_BATCH].set(c)
    b_ada3 = b_ada.reshape(DEPTH, 1, 6 * D_MODEL)
    mods = _ada(cond, w_ada, b_ada3, 0)

    cl_p, sl_p = _dft_tables(SEQ)
    cl_s, sl_s = _dft_tables(DEC_SEQ)
    c64, s64 = _dft_tables(64)
    cl_p, sl_p, cl_s, sl_s, cbd, sbd = (
        jnp.asarray(a, F32).astype(BF) for a in (cl_p, sl_p, cl_s, sl_s, _block_diag4(c64), _block_diag4(s64)))
    ropes = _rope_tables(16) + _rope_tables(32)
    rpp = _pad_rpb_rows(na_rpb)
    wf_bf = w_fourier.astype(BF)

    ck_na, cv_na, ck_df, cv_df, ck_sw, cv_sw = (
        a.transpose(0, 1, 3, 4, 2).reshape(DEC_BATCH, DEPTH, -1, PAST_LEN)
        for a in (cache_na_k, cache_na_v, cache_diff_k, cache_diff_v, cache_swa_k, cache_swa_v))

    _, _, zb, caches = _trunk(xp, xs, ropes, proj=(0, mods, norm1_g[0], w_in, None))
    for l in range(DEPTH):
        lam_init = 0.8 - 0.6 * math.exp(-0.3 * l)
        lamp = jnp.stack([diff_lq1[l], diff_lk1[l], diff_lq2[l], diff_lk2[l]], axis=0)
        subg = jnp.tile(diff_subln_g[l].reshape(1, HEAD_DIM), (1, 2))
        has_next = l + 1 < DEPTH

        cat_p, mods_next = _ctx_mix(zb, swa_sink[l], lamp, subg, cl_p, sl_p, cbd, sbd, wf_bf[l], lam_init,
                                    ada=(cond, w_ada, b_ada3, l + 1) if has_next else None)
        cat_s = _lat_mix(zb, (ck_na, cv_na, ck_df, cv_df, ck_sw, cv_sw), rpp, swa_sink[l], lamp, subg,
                         cl_s, sl_s, cbd, sbd, wf_bf[l], lam_init, l)
        nxt = (l + 1, mods_next, norm1_g[l + 1], w_in, caches) if has_next else None
        xp, xs, zb, new_caches = _trunk(
            xp, xs, ropes, mlp=(l, mods, cat_p, cat_s, norm2_g[l], final_g, w_out, w_mlp1, w_mlp2), proj=nxt)
        caches = new_caches if has_next else caches
        mods = mods_next

    y_prompt = xp.reshape(BATCH, SEQ, D_MODEL)
    y_sample = xs.reshape(DEC_BATCH, DEC_SEQ, D_MODEL)
    new = [a.reshape(BATCH, DEPTH, SEQ, a.shape[-1] // HEAD_DIM, HEAD_DIM) for a in caches]
    return (y_prompt, y_sample) + tuple(new)
```

```python
import functools
import math

import numpy as np
import jax
import jax.numpy as jnp
from jax import lax
from jax.experimental import pallas as pl
from jax.experimental.pallas import tpu as pltpu

D_MODEL = 1024
BATCH = 16
SEQ = 256
DEPTH = 4
DEC_BATCH = 2
DEC_SEQ = 1024
PAST_LEN = 512
GRID_W = 64
GRID_ROWS = DEC_SEQ // GRID_W
HEAD_DIM = 64
NA_KH = 8
NA_KW = 16
DIFF_QK_DIM = 32
SWA_WINDOW = 128
D_FF = 4 * D_MODEL
ROPE_BASE = 10000.0
NORM_EPS = 1e-6
NEG_INF = -1e30
IN_WIDTH = 2304
MIX_WIDTH = 1024

N_CTX_TOK = BATCH * SEQ
N_LAT_TOK = DEC_BATCH * DEC_SEQ
N_TOK = N_CTX_TOK + N_LAT_TOK
ROW_TILE = 256
N_ROW_TILES = N_TOK // ROW_TILE
N_CTX_TILES = N_CTX_TOK // ROW_TILE
LAT_TILES_PER_REQ = DEC_SEQ // ROW_TILE
LAT_BLOCK0 = N_CTX_TOK // DEC_SEQ
MOD_ROWS = 8

C_NA_Q, C_NA_K, C_NA_V = 0, 256, 512
C_DQ, C_DK, C_DV = 768, 1024, 1280
C_FC = 1536
C_SQ, C_SK, C_SV = 1792, 2048, 2176
CACHE_COLS = ((C_NA_K, 256), (C_NA_V, 256), (C_DK, 256), (C_DV, 256), (C_SK, 128), (C_SV, 128))
ZB_SK, ZB_SV = 2048, 2304
ZB_WIDTH = 2560

NA_QROWS = 2
NA_WIN_ROWS = 10
NA_BLOCKS_PER_TRIP = 8

BF = jnp.bfloat16
F32 = jnp.float32
VMEM_LIMIT = 56 * 1024 * 1024


def _dot(a, b):
    return jnp.dot(a, b, preferred_element_type=F32)


def _dot_nt(a, b):
    return lax.dot_general(a, b, (((1,), (1,)), ((), ())), preferred_element_type=F32)


def _rmsnorm(x, g):
    ms = jnp.mean(x * x, axis=-1, keepdims=True)
    return x * lax.rsqrt(ms + NORM_EPS) * g


def _params(n_grid=1):
    return pltpu.CompilerParams(dimension_semantics=("arbitrary",) * n_grid, vmem_limit_bytes=VMEM_LIMIT)


def _rowmax(*parts):
    m = jnp.max(parts[0], axis=-1, keepdims=True)
    for p in parts[1:]:
        m = jnp.maximum(m, jnp.max(p, axis=-1, keepdims=True))
    return m


def _ada_kernel(cond_ref, w_ref, b_ref, o_ref):
    cnd = cond_ref[...]
    s = cnd / (1.0 + jnp.exp(-cnd))
    o_ref[...] = _dot(s.astype(BF), w_ref[...].astype(BF)) + b_ref[...]


def _ada_specs(layer, n_steps, index_of_step):
    tn = 6 * D_MODEL // n_steps
    col = lambda *g: index_of_step(*g)
    in_specs = [
        pl.BlockSpec((MOD_ROWS, D_MODEL), lambda *g: (0, 0)),
        pl.BlockSpec((None, D_MODEL, tn), lambda *g: (layer, 0, col(*g))),
        pl.BlockSpec((None, 1, tn), lambda *g: (layer, 0, col(*g))),
    ]
    return in_specs, pl.BlockSpec((MOD_ROWS, tn), lambda *g: (0, col(*g)))


def _ada(cond, w_ada, b_ada3, layer):
    in_specs, out_spec = _ada_specs(layer, 6, lambda j: j)
    return pl.pallas_call(
        _ada_kernel,
        grid=(6,),
        in_specs=in_specs,
        out_specs=out_spec,
        out_shape=jax.ShapeDtypeStruct((MOD_ROWS, 6 * D_MODEL), F32),
        compiler_params=_params(),
        name="ada",
    )(cond, w_ada, b_ada3)


def _mod_row(t):
    return jnp.where(t < N_CTX_TILES, 0, 1 + (t - N_CTX_TILES) // LAT_TILES_PER_REQ)


def _ctx_tile(t):
    return jnp.minimum(t, N_CTX_TILES - 1)


def _lat_tile(t):
    return jnp.maximum(t - N_CTX_TILES, 0)


TRUNK_CAST_STEPS = 8
FF_CHUNK = 1024
TRUNK_VMEM_LIMIT = 60 * 1024 * 1024


def _rope(x, cos, sin_signed, shift, first):
    partner = jnp.where(first, pltpu.roll(x, 128 - shift, 1), pltpu.roll(x, shift, 1))
    return x * cos + partner * sin_signed


def _projection_stages(hb, wb, t, is_ctx, zb_ref, cache_refs, rope_refs):
    cosd_ref, sind_ref, coss_ref, sins_ref = rope_refs
    dscale = DIFF_QK_DIM ** -0.5
    p0 = pl.multiple_of(
        jnp.where(is_ctx, DEC_SEQ, ((t - N_CTX_TILES) % LAT_TILES_PER_REQ) * ROW_TILE), ROW_TILE)
    lane = lax.broadcasted_iota(jnp.int32, (ROW_TILE, 128), 1)
    first_d = (lane % 16) < 8
    first_s = (lane % 32) < 16
    lo = lane < HEAD_DIM
    cosd, sind = cosd_ref[pl.ds(p0, ROW_TILE), :], sind_ref[pl.ds(p0, ROW_TILE), :]
    coss, sins = coss_ref[pl.ds(p0, ROW_TILE), :], sins_ref[pl.ds(p0, ROW_TILE), :]

    def cols(c0, width):
        return _dot(hb(), wb[:, c0:c0 + width])

    def put(c0, width, val):
        zb_ref[:, c0:c0 + width] = val.astype(BF)

    def put_dup(c0, val):
        r = pltpu.roll(val, HEAD_DIM, 1)
        put(c0, 128, jnp.where(lo, val, r))
        put(c0 + 128, 128, jnp.where(lo, r, val))

    kept = {}

    def na_q():
        put(C_NA_Q, 256, cols(C_NA_Q, 256) * 0.125)

    def na_kv():
        kept["nakv"] = cols(C_NA_K, 512)
        put(C_NA_K, 512, kept["nakv"])

    def diff_q():
        z = cols(C_DQ, 256)
        for j in range(2):
            put(C_DQ + 128 * j, 128, _rope(z[:, 128 * j:128 * j + 128], cosd, sind, 8, first_d) * dscale)

    def diff_k():
        kept["dk"] = cols(C_DK, 256)
        for j in range(2):
            put(C_DK + 128 * j, 128, _rope(kept["dk"][:, 128 * j:128 * j + 128], cosd, sind, 8, first_d))

    def diff_v_fourier():
        kept["dvfc"] = cols(C_DV, 512)
        put(C_DV, 512, kept["dvfc"])

    def swa_q():
        z = cols(C_SQ, 256)
        for j in range(2):
            put(C_SQ + 128 * j, 128, _rope(z[:, 128 * j:128 * j + 128], coss, sins, 16, first_s) * 0.125)

    def swa_kv():
        kept["skv"] = cols(C_SK, 256)
        put_dup(ZB_SK, _rope(kept["skv"][:, 0:128], coss, sins, 16, first_s))
        put_dup(ZB_SV, kept["skv"][:, 128:256])

    def finish():
        new = (kept["nakv"][:, 0:256], kept["nakv"][:, 256:512], kept["dk"], kept["dvfc"][:, 0:256],
               kept["skv"][:, 0:128], kept["skv"][:, 128:256])
        for ref, val in zip(cache_refs, new):
            ref[...] = jnp.where(is_ctx, val, ref[...])

    return [na_q, na_kv, diff_q, diff_k, diff_v_fourier, swa_q, swa_kv], finish


def _trunk_kernel(do_mlp, do_proj, last, n_alias, *refs):
    xp_ref, xs_ref = refs[:2]
    pos = 2
    if do_mlp:
        catp_ref, cats_ref, mm_ref, g2_ref, fg_ref, wo_ref, w1_ref, w2_ref = refs[pos:pos + 8]
        pos += 8
    if do_proj:
        pm_ref, g1_ref, win_ref = refs[pos:pos + 3]
        rope_refs = refs[pos + 3:pos + 7]
        pos += 7 + n_alias
    if do_mlp:
        op_ref, os_ref = refs[pos:pos + 2]
        pos += 2
    if do_proj:
        zb_ref = refs[pos]
        cache_refs = refs[pos + 1:pos + 7]
        pos += 7
    if do_mlp:
        wob, w1b, w2b = refs[pos:pos + 3]
        pos += 3
    if do_proj:
        wb = refs[pos]
        pos += 1
    skew = do_proj
    if skew:
        hprev = refs[pos]
    s = pl.program_id(0)

    @pl.when(s < TRUNK_CAST_STEPS)
    def _():
        def cast(dst, src):
            rows = src.shape[0]
            dst[pl.ds(pl.multiple_of(s * rows, rows), rows), :] = src[...].astype(BF)

        if do_mlp:
            cast(wob, wo_ref)
            cast(w1b, w1_ref)
            cast(w2b, w2_ref)
        if do_proj:
            cast(wb, win_ref)
        if skew:
            @pl.when(s == 0)
            def _():
                hprev[...] = jnp.zeros_like(hprev)

    def tile_step(t, proj_stages, proj_finish):
        pending = list(proj_stages)

        def issue(n):
            for _ in range(min(n, len(pending))):
                pending.pop(0)()

        is_ctx = t < N_CTX_TILES
        row = _mod_row(t)
        x = jnp.where(is_ctx, xp_ref[...], xs_ref[...])
        if do_mlp:
            issue(1)
            gt1 = mm_ref[pl.ds(row, 1), 2 * D_MODEL:3 * D_MODEL]
            sh2 = mm_ref[pl.ds(row, 1), 3 * D_MODEL:4 * D_MODEL]
            sc2 = mm_ref[pl.ds(row, 1), 4 * D_MODEL:5 * D_MODEL]
            gt2 = mm_ref[pl.ds(row, 1), 5 * D_MODEL:6 * D_MODEL]
            cat = jnp.where(is_ctx, catp_ref[...], cats_ref[...])
            x = x + gt1 * _dot(cat, wob[...])
            issue(1)
            hh = (_rmsnorm(x, g2_ref[...]) * (1.0 + sc2) + sh2).astype(BF)
            acc = jnp.zeros((ROW_TILE, D_MODEL), F32)
            for c in range(D_FF // FF_CHUNK):
                a = jnp.maximum(_dot(hh, w1b[:, c * FF_CHUNK:(c + 1) * FF_CHUNK]), 0.0)
                acc = acc + _dot((a * a).astype(BF), w2b[c * FF_CHUNK:(c + 1) * FF_CHUNK, :])
                if c < 2:
                    issue(1)
            issue(1)
            x = x + gt2 * acc
            out = _rmsnorm(x, fg_ref[...]) if last else x
        hb = None
        if do_proj:
            issue(1)
            shift = pm_ref[pl.ds(row, 1), 0:D_MODEL]
            scale = pm_ref[pl.ds(row, 1), D_MODEL:2 * D_MODEL]
            hb = (_rmsnorm(x, g1_ref[...]) * (1.0 + scale) + shift).astype(BF)
        issue(len(pending))
        if proj_finish is not None:
            proj_finish()
        if do_mlp:
            op_ref[...] = jnp.where(is_ctx, out, op_ref[...])
            os_ref[...] = jnp.where(is_ctx, os_ref[...], out)
        return hb

    def projection(hb, t):
        return _projection_stages(hb, wb, t, t < N_CTX_TILES, zb_ref, cache_refs, rope_refs)

    @pl.when(s >= TRUNK_CAST_STEPS)
    def _():
        t = s - TRUNK_CAST_STEPS
        if not skew:
            hb = tile_step(t, [], None)
            if do_proj:
                stages, finish = projection(lambda: hb, t)
                for stage in stages:
                    stage()
                finish()
        else:
            u = jnp.maximum(t - 1, 0)

            @pl.when(t < N_ROW_TILES)
            def _():
                stages, finish = projection(lambda: hprev[...], u)
                hprev[...] = tile_step(t, stages, finish)

            @pl.when(t >= N_ROW_TILES)
            def _():
                stages, finish = projection(lambda: hprev[...], u)
                for stage in stages:
                    stage()
                finish()


def _trunk(xp, xs, ropes, mlp=None, proj=None):
    do_mlp, do_proj = mlp is not None, proj is not None
    last = do_mlp and not do_proj
    skew = do_proj
    clamp_tile = lambda s: jnp.clip(s - TRUNK_CAST_STEPS, 0, N_ROW_TILES - 1)
    step = lambda f: (lambda s: f(clamp_tile(s)))
    step_proj = (lambda f: (lambda s: f(clamp_tile(s - 1)))) if skew else step
    const = lambda shape: pl.BlockSpec(shape, lambda s: (0,) * len(shape), pipeline_mode=pl.Buffered(1))
    chunk = lambda layer, rows, cols: pl.BlockSpec(
        (None, rows // TRUNK_CAST_STEPS, cols), lambda s: (layer, jnp.minimum(s, TRUNK_CAST_STEPS - 1), 0))
    mod_spec = const((MOD_ROWS, 6 * D_MODEL))
    x_specs = [pl.BlockSpec((ROW_TILE, D_MODEL), step(lambda t: (_ctx_tile(t), 0))),
               pl.BlockSpec((ROW_TILE, D_MODEL), step(lambda t: (_lat_tile(t), 0)))]
    args, in_specs = [xp, xs], list(x_specs)
    out_specs, out_shape, scratch, aliases = [], [], [], {}
    n_alias = 0
    if do_mlp:
        layer, mods_l, cat_p, cat_s, g2, final_g, wo, w1, w2 = mlp
        args += [cat_p, cat_s, mods_l, g2.reshape(1, D_MODEL), final_g.reshape(1, D_MODEL), wo, w1, w2]
        in_specs += [pl.BlockSpec((ROW_TILE, MIX_WIDTH), step(lambda t: (_ctx_tile(t), 0))),
                     pl.BlockSpec((ROW_TILE, MIX_WIDTH), step(lambda t: (_lat_tile(t), 0))), mod_spec,
                     const((1, D_MODEL)), const((1, D_MODEL)),
                     chunk(layer, MIX_WIDTH, D_MODEL), chunk(layer, D_MODEL, D_FF), chunk(layer, D_FF, D_MODEL)]
    if do_proj:
        p_layer, mods_p, g1, w_in, caches = proj
        args += [mods_p, g1.reshape(1, D_MODEL), w_in, *ropes]
        in_specs += [mod_spec, const((1, D_MODEL)), chunk(p_layer, D_MODEL, IN_WIDTH)]
        in_specs += [const((DEC_SEQ + ROW_TILE, 128))] * 4
        if caches is not None:
            n_alias = 6
            n_out_before = 2 if do_mlp else 0
            aliases = {len(args) + i: n_out_before + 1 + i for i in range(6)}
            args += list(caches)
            in_specs += [pl.BlockSpec(memory_space=pl.ANY)] * 6
    if do_mlp:
        out_specs += x_specs
        out_shape += [jax.ShapeDtypeStruct((N_CTX_TOK, D_MODEL), F32),
                      jax.ShapeDtypeStruct((N_LAT_TOK, D_MODEL), F32)]
        scratch += [pltpu.VMEM((MIX_WIDTH, D_MODEL), BF), pltpu.VMEM((D_MODEL, D_FF), BF),
                    pltpu.VMEM((D_FF, D_MODEL), BF)]
    if do_proj:
        out_specs += [pl.BlockSpec((ROW_TILE, ZB_WIDTH), step_proj(lambda t: (t, 0)))] + [
            pl.BlockSpec((None, None, SEQ, width), step_proj(lambda t: (_ctx_tile(t), p_layer, 0, 0)))
            for _, width in CACHE_COLS]
        out_shape += [jax.ShapeDtypeStruct((N_TOK, ZB_WIDTH), BF)] + [
            jax.ShapeDtypeStruct((BATCH, DEPTH, SEQ, width), F32) for _, width in CACHE_COLS]
        scratch += [pltpu.VMEM((D_MODEL, IN_WIDTH), BF)]
    if skew:
        scratch += [pltpu.VMEM((ROW_TILE, D_MODEL), BF)]
    outs = pl.pallas_call(
        functools.partial(_trunk_kernel, do_mlp, do_proj, last, n_alias),
        grid=(TRUNK_CAST_STEPS + N_ROW_TILES + (1 if skew else 0),),
        in_specs=in_specs,
        out_specs=out_specs,
        out_shape=out_shape,
        scratch_shapes=scratch,
        input_output_aliases=aliases,
        compiler_params=pltpu.CompilerParams(dimension_semantics=("arbitrary",),
                                             vmem_limit_bytes=TRUNK_VMEM_LIMIT),
        name="trunk_mlp_proj" if (do_mlp and do_proj) else ("trunk_mlp" if do_mlp else "trunk_proj"),
    )(*args)
    outs = list(outs)
    if do_mlp:
        xp, xs = outs[:2]
        outs = outs[2:]
    zb, new_caches = (outs[0], tuple(outs[1:7])) if do_proj else (None, None)
    return xp, xs, zb, new_caches


def _diff_lambda(lamp_ref, lam_init):
    a = jnp.sum(lamp_ref[0:1, :] * lamp_ref[1:2, :], axis=-1, keepdims=True)
    b = jnp.sum(lamp_ref[2:3, :] * lamp_ref[3:4, :], axis=-1, keepdims=True)
    return jnp.exp(a) - jnp.exp(b) + lam_init


def _lane_lo(n):
    return lax.broadcasted_iota(jnp.int32, (n, 128), 1) < HEAD_DIM


def _lane_lo_wide(n, width):
    return lax.broadcasted_iota(jnp.int32, (n, width), 1) % 128 < HEAD_DIM


def _split_pair(x, lo):
    zero = jnp.zeros_like(x)
    return jnp.where(lo, x, zero), jnp.where(lo, zero, x)


def _v_ones_pair(v, lo):
    one = jnp.ones_like(v)
    return jnp.where(lo, v, one), jnp.where(lo, one, v)


def _v_stack(v, lo):
    zero = jnp.zeros_like(v)
    ones_a = jnp.where(lo, 1.0, 0.0).astype(v.dtype)
    ones_b = jnp.where(lo, 0.0, 1.0).astype(v.dtype)
    top = jnp.concatenate([jnp.where(lo, v, zero), ones_a], axis=1)
    bottom = jnp.concatenate([jnp.where(lo, zero, v), ones_b], axis=1)
    return jnp.concatenate([top, bottom], axis=0)


def _merge_pair(a, b, lo):
    return jnp.where(lo, a, b), pltpu.roll(jnp.where(lo, b, a), HEAD_DIM, 1)


def _subln_pair(o, g2, lam_init, lo):
    sq = o * o
    ms_a = jnp.sum(jnp.where(lo, sq, 0.0), axis=-1, keepdims=True)
    ms_b = jnp.sum(jnp.where(lo, 0.0, sq), axis=-1, keepdims=True)
    ms = jnp.where(lo, ms_a, ms_b) * (1.0 / HEAD_DIM)
    return o * lax.rsqrt(ms + NORM_EPS) * g2 * (1.0 - lam_init)


def _diff_quarters(q, n):
    quarter = lax.broadcasted_iota(jnp.int32, (n, 128), 1) // DIFF_QK_DIM
    zero = jnp.zeros_like(q)
    return [jnp.where(quarter == i, q, zero) for i in range(4)]


def _attention_pipeline(units, score_phase, softmax_phase, value_phase, on_result=None):
    res, scores, weights = [], {}, {}
    n = len(units)
    for i in range(n + 2):
        if i < n:
            scores[i] = score_phase(units[i])
        if 0 <= i - 1 < n:
            weights[i - 1] = softmax_phase(units[i - 1], scores.pop(i - 1))
        if 0 <= i - 2 < n:
            res.append(value_phase(units[i - 2], weights.pop(i - 2)))
            if on_result is not None:
                on_result(res)
    return res


def _fourier(x_bf, cl, sl, cbd, sbd, wf):
    xc = _dot(x_bf, cbd).astype(BF)
    xs = _dot(x_bf, sbd).astype(BF)
    y = _dot(cl, xc) - _dot(sl, xs)
    return _dot(y.astype(BF), wf)


CTX_REQ_PER_STEP = 2


def _ctx_mix_kernel(lam_init, with_ada, sink_ref, z_ref, lamp_ref, subg_ref, cl_ref, sl_ref, cbd_ref, sbd_ref,
                    wf_ref, *rest):
    if with_ada:
        cond_ref, wa_ref, ba_ref, o_ref, mods_ref = rest
        _ada_kernel(cond_ref, wa_ref, ba_ref, mods_ref)
    else:
        o_ref, = rest
    lo = _lane_lo(SEQ)
    z_refs = [z_ref.at[pl.ds(r * SEQ, SEQ)] for r in range(CTX_REQ_PER_STEP)]
    o_refs = [o_ref.at[pl.ds(r * SEQ, SEQ)] for r in range(CTX_REQ_PER_STEP)]

    pairs = []
    for zr in z_refs:
        for j in range(2):
            c = 128 * j
            qs = _split_pair(zr[:, C_NA_Q + c:C_NA_Q + c + 128], lo)
            pairs.append((zr, qs[0], qs[1], C_NA_K + c, _v_stack(zr[:, C_NA_V + c:C_NA_V + c + 128], lo), None))
        for j in range(2):
            c = 128 * j
            q4 = _diff_quarters(zr[:, C_DQ + c:C_DQ + c + 128], SEQ)
            v2 = _v_stack(zr[:, C_DV + c:C_DV + c + 128], lo)
            pairs.append((zr, q4[0], q4[2], C_DK + c, v2, None))
            pairs.append((zr, q4[1], q4[3], C_DK + c, v2, None))
        for g in range(2):
            c = 128 * g
            qs = _split_pair(zr[:, C_SQ + c:C_SQ + c + 128], lo)
            pairs.append((zr, qs[0], qs[1], ZB_SK + c, _v_stack(zr[:, ZB_SV + c:ZB_SV + c + 128], lo),
                          (sink_ref[2 * g], sink_ref[2 * g + 1])))

    def score_phase(pair):
        zr, qa, qb, kc, _, _ = pair
        return [_dot_nt(q, zr[:, kc:kc + 128]) for q in (qa, qb)]

    def softmax_phase(pair, scores):
        sinks = pair[5]
        es, xs = [], []
        for i, s in enumerate(scores):
            m = _rowmax(s)
            if sinks is not None:
                m = jnp.maximum(m, sinks[i])
                xs.append(jnp.exp(sinks[i] - m))
            es.append(jnp.exp(s - m).astype(BF))
        return jnp.concatenate(es, axis=1), xs

    def value_phase(pair, weights):
        e2, xs = weights
        r = _dot(e2, pair[4])
        num, den = r[:, 0:128], r[:, 128:256]
        if xs:
            den = den + jnp.where(lo, xs[0], xs[1])
        return num / den

    lam = _diff_lambda(lamp_ref, lam_init)

    def write_request(zr, out, rr):
        for j in range(2):
            out[:, 128 * j:128 * j + 128] = rr[j].astype(BF)
        for j in range(2):
            o = rr[2 + 2 * j] - lam * rr[3 + 2 * j]
            out[:, 256 + 128 * j:384 + 128 * j] = _subln_pair(o, subg_ref[...], lam_init, lo).astype(BF)
        o_c = _fourier(zr[:, C_FC:C_FC + 256], cl_ref[...], sl_ref[...], cbd_ref[...], sbd_ref[...], wf_ref[...])
        out[:, 512:768] = o_c.astype(BF)
        for g in range(2):
            out[:, 768 + 128 * g:896 + 128 * g] = rr[6 + g].astype(BF)

    def on_result(res):
        if len(res) % 8 == 0:
            r = len(res) // 8 - 1
            write_request(z_refs[r], o_refs[r], res[8 * r:8 * r + 8])

    _attention_pipeline(pairs, score_phase, softmax_phase, value_phase, on_result)


def _ctx_mix(zb, sink, lamp, subg, cl, sl, cbd, sbd, wf, lam_init, ada=None):
    full = lambda shape: pl.BlockSpec(shape, lambda b, s: (0,) * len(shape))
    n_steps = BATCH // CTX_REQ_PER_STEP
    in_specs = [
        pl.BlockSpec((CTX_REQ_PER_STEP * SEQ, ZB_WIDTH), lambda b, s: (b, 0)),
        full((4, DIFF_QK_DIM)),
        full((1, 2 * HEAD_DIM)),
        full((SEQ, SEQ)), full((SEQ, SEQ)),
        full((256, 256)), full((256, 256)), full((256, 256)),
    ]
    out_specs = [pl.BlockSpec((CTX_REQ_PER_STEP * SEQ, MIX_WIDTH), lambda b, s: (b, 0))]
    out_shape = [jax.ShapeDtypeStruct((N_CTX_TOK, MIX_WIDTH), BF)]
    args = [sink, zb, lamp, subg, cl, sl, cbd, sbd, wf]
    if ada is not None:
        cond, w_ada, b_ada3, layer = ada
        ada_in, ada_out = _ada_specs(layer, n_steps, lambda b, s: b)
        in_specs += ada_in
        out_specs.append(ada_out)
        out_shape.append(jax.ShapeDtypeStruct((MOD_ROWS, 6 * D_MODEL), F32))
        args += [cond, w_ada, b_ada3]
    grid_spec = pltpu.PrefetchScalarGridSpec(
        num_scalar_prefetch=1, grid=(n_steps,), in_specs=in_specs, out_specs=out_specs)
    outs = pl.pallas_call(
        functools.partial(_ctx_mix_kernel, lam_init, ada is not None),
        grid_spec=grid_spec,
        out_shape=out_shape,
        compiler_params=_params(),
        name="ctx_mix",
    )(*args)
    return outs[0], (outs[1] if ada is not None else None)


def _cache_v_ones(cv_ref, cva, cvb):
    first = lax.broadcasted_iota(jnp.int32, (256, PAST_LEN), 0) % 128 < HEAD_DIM
    cv = cv_ref[...]
    cva[...] = jnp.where(first, cv, 1.0).astype(BF)
    cvb[...] = jnp.where(first, 1.0, cv).astype(BF)


def _na_bias_tiles(rpp_ref, t2):
    c = lax.broadcasted_iota(jnp.int32, (GRID_W, 128), 0)
    kc = lax.broadcasted_iota(jnp.int32, (GRID_W, 128), 1) % GRID_W
    c_start = jnp.clip(c - NA_KW // 2, 0, GRID_W - NA_KW)
    inside = jnp.logical_and(kc >= c_start, kc < c_start + NA_KW)
    for h in range(4):
        for dr in range(2 * NA_KH - 1):
            row = jnp.broadcast_to(rpp_ref[h, dr:dr + 1, :], (GRID_W, 128))
            toeplitz = pltpu.roll(row, 128 - (NA_KW - 1), 1, stride=1, stride_axis=0)
            t2[h, dr] = jnp.where(inside, toeplitz, NEG_INF)
        t2[h, 2 * NA_KH - 1] = jnp.full((GRID_W, 128), NEG_INF, F32)


def _lat_na_part(z_ref, ck_ref, cv_ref, t2, o_ref, ckb, cva, cvb):
    ckb[...] = ck_ref[...].astype(BF)
    _cache_v_ones(cv_ref, cva, cvb)
    n_q = NA_QROWS * GRID_W
    n_loc = NA_WIN_ROWS * GRID_W
    lo_q = _lane_lo(n_q)
    lo_w = _lane_lo(n_loc)

    lo_t = _lane_lo(GRID_W)

    def block(p):
        w_row = jnp.clip(NA_QROWS * p - NA_KH // 2, 0, GRID_ROWS - NA_WIN_ROWS)
        q0 = pl.multiple_of(p * n_q, n_q)
        k0 = pl.multiple_of(w_row * GRID_W, NA_QROWS * GRID_W)

        def tile_index(r, j):
            r_start = jnp.clip(r - NA_KH // 2, 0, GRID_ROWS - NA_KH)
            kr = w_row + j
            inside = jnp.logical_and(kr >= r_start, kr < r_start + NA_KH)
            return jnp.where(inside, kr - r + NA_KH - 1, 2 * NA_KH - 1)

        def bias(h):
            rows = []
            for rl in range(NA_QROWS):
                r = NA_QROWS * p + rl
                pieces = [jnp.where(lo_t, t2[h, tile_index(r, 2 * jj)], t2[h, tile_index(r, 2 * jj + 1)])
                          for jj in range(NA_WIN_ROWS // 2)]
                rows.append(jnp.concatenate(pieces, axis=1))
            return jnp.concatenate(rows, axis=0)

        return q0, k0, bias

    def score_phase(unit):
        q0, k0, bias, h = unit
        c = 128 * (h // 2)
        q = _split_pair(z_ref[pl.ds(q0, n_q), C_NA_Q + c:C_NA_Q + c + 128], lo_q)[h % 2]
        s_loc = _dot_nt(q, z_ref[pl.ds(k0, n_loc), C_NA_K + c:C_NA_K + c + 128]) + bias(h)
        return s_loc, _dot(q, ckb[c:c + 128, :])

    def softmax_phase(unit, scores):
        s_loc, s_ctx = scores
        m = _rowmax(s_loc, s_ctx)
        return jnp.exp(s_loc - m).astype(BF), jnp.exp(s_ctx - m).astype(BF)

    def value_phase(unit, weights):
        q0, k0, bias, h = unit
        e_loc, e_ctx = weights
        c = 128 * (h // 2)
        v = _v_ones_pair(z_ref[pl.ds(k0, n_loc), C_NA_V + c:C_NA_V + c + 128], lo_w)[h % 2]
        return _dot(e_loc, v) + _dot_nt(e_ctx, (cva, cvb)[h % 2][c:c + 128, :])

    def body(i, carry):
        blocks = [block(NA_BLOCKS_PER_TRIP * i + b) for b in range(NA_BLOCKS_PER_TRIP)]
        units = [(q0, k0, bias, h) for q0, k0, bias in blocks for h in range(4)]
        res = _attention_pipeline(units, score_phase, softmax_phase, value_phase)
        for b, (q0, _, _) in enumerate(blocks):
            for j in range(2):
                num, den = _merge_pair(res[4 * b + 2 * j], res[4 * b + 2 * j + 1], lo_q)
                o_ref[pl.ds(q0, n_q), 128 * j:128 * j + 128] = (num / den).astype(BF)
        return carry

    lax.fori_loop(0, GRID_ROWS // NA_QROWS // NA_BLOCKS_PER_TRIP, body, 0)


DIFF_QBLK = 1024
DIFF_GROUP = 2


def _lat_diff_part(lam_init, z_ref, ck_ref, cv_ref, lamp_ref, subg_ref, o_ref, ckb, cva, cvb, vla, vlb):
    ckb[...] = ck_ref[...].astype(BF)
    _cache_v_ones(cv_ref, cva, cvb)
    lo_v = _lane_lo_wide(DEC_SEQ, 256)
    v_loc = z_ref[:, C_DV:C_DV + 256]
    one = jnp.ones_like(v_loc)
    vla[...] = jnp.where(lo_v, v_loc, one)
    vlb[...] = jnp.where(lo_v, one, v_loc)
    lam = _diff_lambda(lamp_ref, lam_init)
    qblk = DIFF_QBLK
    lo_q = _lane_lo(qblk)

    def body(i, carry):
        q0 = pl.multiple_of(i * qblk, qblk)
        for j in range(2):
            c = 128 * j
            q4 = _diff_quarters(z_ref[pl.ds(q0, qblk), C_DQ + c:C_DQ + c + 128], qblk)
            res = []
            for t0 in range(0, 4, DIFF_GROUP):
                scores = [(_dot_nt(q, z_ref[:, C_DK + c:C_DK + c + 128]), _dot(q, ckb[c:c + 128, :]))
                          for q in q4[t0:t0 + DIFF_GROUP]]
                weights = []
                for s_loc, s_ctx in scores:
                    m = _rowmax(s_loc, s_ctx)
                    weights.append((jnp.exp(s_loc - m).astype(BF), jnp.exp(s_ctx - m).astype(BF)))
                for t, (e_loc, e_ctx) in enumerate(weights, start=t0):
                    v_loc, v_ctx = ((vla, cva), (vlb, cvb))[t // 2]
                    res.append(_dot(e_loc, v_loc[:, c:c + 128]) + _dot_nt(e_ctx, v_ctx[c:c + 128, :]))
            n1, d1 = _merge_pair(res[0], res[2], lo_q)
            n2, d2 = _merge_pair(res[1], res[3], lo_q)
            o = n1 / d1 - lam * (n2 / d2)
            o_ref[pl.ds(q0, qblk), 256 + c:256 + c + 128] = (
                _subln_pair(o, subg_ref[...], lam_init, lo_q).astype(BF))
        return carry

    lax.fori_loop(0, DEC_SEQ // qblk, body, 0)


SWA_BLOCKS_PER_TRIP = 8


def _lat_swa_part(sink_ref, z_ref, ck_ref, cv_ref, o_ref, ckd, cva, cvb):
    W = SWA_WINDOW
    n_win = 3 * W
    ones = jnp.ones((HEAD_DIM, PAST_LEN), BF)
    for g in range(2):
        k_g = ck_ref[HEAD_DIM * g:HEAD_DIM * (g + 1), :].astype(BF)
        v_g = cv_ref[HEAD_DIM * g:HEAD_DIM * (g + 1), :].astype(BF)
        r0, r1, r2 = 128 * g, 128 * g + HEAD_DIM, 128 * (g + 1)
        ckd[r0:r1, :] = k_g
        ckd[r1:r2, :] = k_g
        cva[r0:r1, :] = v_g
        cva[r1:r2, :] = ones
        cvb[r0:r1, :] = ones
        cvb[r1:r2, :] = v_g
    lo_q = _lane_lo(W)
    lo_w = _lane_lo(n_win)

    def block(n):
        q0 = pl.multiple_of(n * W, W)
        w0 = pl.multiple_of(jnp.clip((n - 1) * W, 0, DEC_SEQ - n_win), W)
        qpos = q0 + lax.broadcasted_iota(jnp.int32, (W, n_win), 0)
        kpos = w0 + lax.broadcasted_iota(jnp.int32, (W, n_win), 1)
        return q0, w0, jnp.abs(qpos - kpos) <= W

    def score_phase(unit):
        q0, w0, valid, h = unit
        c = 128 * (h // 2)
        q = _split_pair(z_ref[pl.ds(q0, W), C_SQ + c:C_SQ + c + 128], lo_q)[h % 2]
        s_loc = jnp.where(valid, _dot_nt(q, z_ref[pl.ds(w0, n_win), ZB_SK + c:ZB_SK + c + 128]), NEG_INF)
        return s_loc, _dot(q, ckd[c:c + 128, :])

    def softmax_phase(unit, scores):
        s_loc, s_ctx = scores
        sink = sink_ref[unit[3]]
        m = jnp.maximum(_rowmax(s_loc, s_ctx), sink)
        return jnp.exp(s_loc - m).astype(BF), jnp.exp(s_ctx - m).astype(BF), jnp.exp(sink - m)

    def value_phase(unit, weights):
        q0, w0, valid, h = unit
        e_loc, e_ctx, extra = weights
        c = 128 * (h // 2)
        v = _v_ones_pair(z_ref[pl.ds(w0, n_win), ZB_SV + c:ZB_SV + c + 128], lo_w)[h % 2]
        return _dot(e_loc, v) + _dot_nt(e_ctx, (cva, cvb)[h % 2][c:c + 128, :]), extra

    def body(i, carry):
        blocks = [block(SWA_BLOCKS_PER_TRIP * i + b) for b in range(SWA_BLOCKS_PER_TRIP)]
        units = [(q0, w0, valid, h) for q0, w0, valid in blocks for h in range(4)]
        res = _attention_pipeline(units, score_phase, softmax_phase, value_phase)
        for b, (q0, _, _) in enumerate(blocks):
            for g in range(2):
                (r_a, x_a), (r_b, x_b) = res[4 * b + 2 * g], res[4 * b + 2 * g + 1]
                num, den = _merge_pair(r_a, r_b, lo_q)
                den = den + jnp.where(lo_q, x_a, x_b)
                o_ref[pl.ds(q0, W), 768 + 128 * g:768 + 128 * g + 128] = (num / den).astype(BF)
        return carry

    lax.fori_loop(0, DEC_SEQ // W // SWA_BLOCKS_PER_TRIP, body, 0)


def _lat_mix_kernel(lam_init, sink_ref, z_ref, nak_ref, nav_ref, dfk_ref, dfv_ref, swk_ref, swv_ref, rpp_ref,
                    lamp_ref, subg_ref, cl_ref, sl_ref, cbd_ref, sbd_ref, wf_ref, o_ref,
                    ck, cva, cvb, vla, vlb, t2):
    @pl.when(pl.program_id(0) == 0)
    def _():
        _na_bias_tiles(rpp_ref, t2)

    _lat_na_part(z_ref, nak_ref, nav_ref, t2, o_ref, ck, cva, cvb)
    _lat_diff_part(lam_init, z_ref, dfk_ref, dfv_ref, lamp_ref, subg_ref, o_ref, ck, cva, cvb, vla, vlb)
    o_c = _fourier(z_ref[:, C_FC:C_FC + 256], cl_ref[...], sl_ref[...], cbd_ref[...], sbd_ref[...], wf_ref[...])
    o_ref[:, 512:768] = o_c.astype(BF)
    _lat_swa_part(sink_ref, z_ref, swk_ref, swv_ref, o_ref, ck, cva, cvb)


def _lat_mix(zb, caches_in, rpp, sink, lamp, subg, cl, sl, cbd, sbd, wf, lam_init, layer):
    full = lambda shape: pl.BlockSpec(shape, lambda b, s: (0,) * len(shape), pipeline_mode=pl.Buffered(1))
    cache = lambda width: pl.BlockSpec((None, None, width, PAST_LEN), lambda b, s: (b, layer, 0, 0))
    grid_spec = pltpu.PrefetchScalarGridSpec(
        num_scalar_prefetch=1,
        grid=(DEC_BATCH,),
        in_specs=[
            pl.BlockSpec((DEC_SEQ, ZB_WIDTH), lambda b, s: (LAT_BLOCK0 + b, 0)),
            cache(256), cache(256), cache(256), cache(256), cache(128), cache(128),
            pl.BlockSpec((None, 4, 2 * NA_KH, 128), lambda b, s: (layer, 0, 0, 0), pipeline_mode=pl.Buffered(1)),
            full((4, DIFF_QK_DIM)), full((1, 2 * HEAD_DIM)),
            full((DEC_SEQ, DEC_SEQ)), full((DEC_SEQ, DEC_SEQ)),
            full((256, 256)), full((256, 256)), full((256, 256)),
        ],
        out_specs=pl.BlockSpec((DEC_SEQ, MIX_WIDTH), lambda b, s: (b, 0)),
        scratch_shapes=[pltpu.VMEM((256, PAST_LEN), BF)] * 3 + [pltpu.VMEM((DEC_SEQ, 256), BF)] * 2
        + [pltpu.VMEM((4, 2 * NA_KH, GRID_W, 128), F32)],
    )
    return pl.pallas_call(
        functools.partial(_lat_mix_kernel, lam_init),
        grid_spec=grid_spec,
        out_shape=jax.ShapeDtypeStruct((N_LAT_TOK, MIX_WIDTH), BF),
        compiler_params=_params(),
        name="lat_mix",
    )(sink, zb, *caches_in, rpp, lamp, subg, cl, sl, cbd, sbd, wf)


def _dft_tables(n):
    j = np.arange(n)
    ang = 2.0 * np.pi * ((j[:, None] * j[None, :]) % n) / n
    return np.cos(ang) / np.sqrt(n), np.sin(ang) / np.sqrt(n)


def _block_diag4(m):
    out = np.zeros((256, 256), m.dtype)
    for g in range(4):
        out[64 * g:64 * g + 64, 64 * g:64 * g + 64] = m
    return out


def _rope_tables(n_axis_dims):
    half = n_axis_dims // 2
    inv = ROPE_BASE ** (-np.arange(half, dtype=np.float64) / half)
    t = np.arange(DEC_SEQ)
    lane = np.arange(128)
    w = lane % n_axis_dims
    is_col = (lane // n_axis_dims) % 2 == 1
    pos = np.where(is_col[None, :], (t % GRID_W)[:, None], (t // GRID_W)[:, None]).astype(np.float64)
    ang = pos * inv[w % half][None, :]
    sign = np.where(w < half, -1.0, 1.0)[None, :]
    cos = np.concatenate([np.cos(ang), np.ones((ROW_TILE, 128))], axis=0)
    sin = np.concatenate([np.sin(ang) * sign, np.zeros((ROW_TILE, 128))], axis=0)
    return jnp.asarray(cos, F32), jnp.asarray(sin, F32)


def _pad_rpb_rows(rpb):
    n_dc = rpb.shape[-1]
    half = jnp.pad(rpb, ((0, 0), (0, 0), (0, 1), (0, GRID_W - n_dc)))
    return jnp.concatenate([half, half], axis=-1)


def kernel(x_prompt, x_sample, cache_na_k, cache_na_v, cache_diff_k, cache_diff_v, cache_swa_k, cache_swa_v, c, c_ctx, w_ada, b_ada, norm1_g, norm2_g, w_in, na_rpb, diff_lq1, diff_lk1, diff_lq2, diff_lk2, diff_subln_g, w_fourier, swa_sink, w_out, w_mlp1, w_mlp2, final_g):
    xp = x_prompt.reshape(N_CTX_TOK, D_MODEL)
    xs = x_sample.reshape(N_LAT_TOK, D_MODEL)
    cond = jnp.zeros((MOD_ROWS, D_MODEL), F32).at[0].set(c_ctx).at[1:1 + DEC_BATCH].set(c)
    b_ada3 = b_ada.reshape(DEPTH, 1, 6 * D_MODEL)
    mods = _ada(cond, w_ada, b_ada3, 0)

    cl_p, sl_p = _dft_tables(SEQ)
    cl_s, sl_s = _dft_tables(DEC_SEQ)
    c64, s64 = _dft_tables(64)
    cl_p, sl_p, cl_s, sl_s, cbd, sbd = (
        jnp.asarray(a, F32).astype(BF) for a in (cl_p, sl_p, cl_s, sl_s, _block_diag4(c64), _block_diag4(s64)))
    ropes = _rope_tables(16) + _rope_tables(32)
    rpp = _pad_rpb_rows(na_rpb)
    wf_bf = w_fourier.astype(BF)

    ck_na, cv_na, ck_df, cv_df, ck_sw, cv_sw = (
        a.transpose(0, 1, 3, 4, 2).reshape(DEC_BATCH, DEPTH, -1, PAST_LEN)
        for a in (cache_na_k, cache_na_v, cache_diff_k, cache_diff_v, cache_swa_k, cache_swa_v))

    _, _, zb, caches = _trunk(xp, xs, ropes, proj=(0, mods, norm1_g[0], w_in, None))
    for l in range(DEPTH):
        lam_init = 0.8 - 0.6 * math.exp(-0.3 * l)
        lamp = jnp.stack([diff_lq1[l], diff_lk1[l], diff_lq2[l], diff_lk2[l]], axis=0)
        subg = jnp.tile(diff_subln_g[l].reshape(1, HEAD_DIM), (1, 2))
        has_next = l + 1 < DEPTH

        cat_p, mods_next = _ctx_mix(zb, swa_sink[l], lamp, subg, cl_p, sl_p, cbd, sbd, wf_bf[l], lam_init,
                                    ada=(cond, w_ada, b_ada3, l + 1) if has_next else None)
        cat_s = _lat_mix(zb, (ck_na, cv_na, ck_df, cv_df, ck_sw, cv_sw), rpp, swa_sink[l], lamp, subg,
                         cl_s, sl_s, cbd, sbd, wf_bf[l], lam_init, l)
        nxt = (l + 1, mods_next, norm1_g[l + 1], w_in, caches) if has_next else None
        xp, xs, zb, new_caches = _trunk(
            xp, xs, ropes, mlp=(l, mods, cat_p, cat_s, norm2_g[l], final_g, w_out, w_mlp1, w_mlp2), proj=nxt)
        caches = new_caches if has_next else caches
        mods = mods_next

    y_prompt = xp.reshape(BATCH, SEQ, D_MODEL)
    y_sample = xs.reshape(DEC_BATCH, DEC_SEQ, D_MODEL)
    new = [a.reshape(BATCH, DEPTH, SEQ, a.shape[-1] // HEAD_DIM, HEAD_DIM) for a in caches]
    return (y_prompt, y_sample) + tuple(new)
```

```python
import functools
import math

import numpy as np
import jax
import jax.numpy as jnp
from jax import lax
from jax.experimental import pallas as pl
from jax.experimental.pallas import tpu as pltpu

D_MODEL = 1024
BATCH = 16
SEQ = 256
DEPTH = 4
DEC_BATCH = 2
DEC_SEQ = 1024
PAST_LEN = 512
GRID_W = 64
GRID_ROWS = DEC_SEQ // GRID_W
HEAD_DIM = 64
NA_KH = 8
NA_KW = 16
DIFF_QK_DIM = 32
SWA_WINDOW = 128
D_FF = 4 * D_MODEL
ROPE_BASE = 10000.0
NORM_EPS = 1e-6
NEG_INF = -1e30
IN_WIDTH = 2304
MIX_WIDTH = 1024

N_CTX_TOK = BATCH * SEQ
N_LAT_TOK = DEC_BATCH * DEC_SEQ
N_TOK = N_CTX_TOK + N_LAT_TOK
ROW_TILE = 256
N_ROW_TILES = N_TOK // ROW_TILE
N_CTX_TILES = N_CTX_TOK // ROW_TILE
LAT_TILES_PER_REQ = DEC_SEQ // ROW_TILE
LAT_BLOCK0 = N_CTX_TOK // DEC_SEQ
MOD_ROWS = 8

C_NA_Q, C_NA_K, C_NA_V = 0, 256, 512
C_DQ, C_DK, C_DV = 768, 1024, 1280
C_FC = 1536
C_SQ, C_SK, C_SV = 1792, 2048, 2176
CACHE_COLS = ((C_NA_K, 256), (C_NA_V, 256), (C_DK, 256), (C_DV, 256), (C_SK, 128), (C_SV, 128))
ZB_SK, ZB_SV = 2048, 2304
ZB_WIDTH = 2560

NA_QROWS = 2
NA_WIN_ROWS = 10
NA_BLOCKS_PER_TRIP = 2

BF = jnp.bfloat16
F32 = jnp.float32
VMEM_LIMIT = 56 * 1024 * 1024


def _dot(a, b):
    return jnp.dot(a, b, preferred_element_type=F32)


def _dot_nt(a, b):
    return lax.dot_general(a, b, (((1,), (1,)), ((), ())), preferred_element_type=F32)


def _rmsnorm(x, g):
    ms = jnp.mean(x * x, axis=-1, keepdims=True)
    return x * lax.rsqrt(ms + NORM_EPS) * g


def _params(n_grid=1):
    return pltpu.CompilerParams(dimension_semantics=("arbitrary",) * n_grid, vmem_limit_bytes=VMEM_LIMIT)


def _rowmax(*parts):
    m = jnp.max(parts[0], axis=-1, keepdims=True)
    for p in parts[1:]:
        m = jnp.maximum(m, jnp.max(p, axis=-1, keepdims=True))
    return m


def _ada_kernel(cond_ref, w_ref, b_ref, o_ref):
    cnd = cond_ref[...]
    s = cnd / (1.0 + jnp.exp(-cnd))
    o_ref[...] = _dot(s.astype(BF), w_ref[...].astype(BF)) + b_ref[...]


def _ada_specs(layer, n_steps, index_of_step):
    tn = 6 * D_MODEL // n_steps
    col = lambda *g: index_of_step(*g)
    in_specs = [
        pl.BlockSpec((MOD_ROWS, D_MODEL), lambda *g: (0, 0)),
        pl.BlockSpec((None, D_MODEL, tn), lambda *g: (layer, 0, col(*g))),
        pl.BlockSpec((None, 1, tn), lambda *g: (layer, 0, col(*g))),
    ]
    return in_specs, pl.BlockSpec((MOD_ROWS, tn), lambda *g: (0, col(*g)))


def _ada(cond, w_ada, b_ada3, layer):
    in_specs, out_spec = _ada_specs(layer, 6, lambda j: j)
    return pl.pallas_call(
        _ada_kernel,
        grid=(6,),
        in_specs=in_specs,
        out_specs=out_spec,
        out_shape=jax.ShapeDtypeStruct((MOD_ROWS, 6 * D_MODEL), F32),
        compiler_params=_params(),
        name="ada",
    )(cond, w_ada, b_ada3)


def _mod_row(t):
    return jnp.where(t < N_CTX_TILES, 0, 1 + (t - N_CTX_TILES) // LAT_TILES_PER_REQ)


def _ctx_tile(t):
    return jnp.minimum(t, N_CTX_TILES - 1)


def _lat_tile(t):
    return jnp.maximum(t - N_CTX_TILES, 0)


TRUNK_CAST_STEPS = 8
FF_CHUNK = 1024
TRUNK_VMEM_LIMIT = 60 * 1024 * 1024


def _rope(x, cos, sin_signed, shift, first):
    partner = jnp.where(first, pltpu.roll(x, 128 - shift, 1), pltpu.roll(x, shift, 1))
    return x * cos + partner * sin_signed


def _projection_stages(hb, wb, t, is_ctx, zb_ref, cache_refs, rope_refs):
    cosd_ref, sind_ref, coss_ref, sins_ref = rope_refs
    dscale = DIFF_QK_DIM ** -0.5
    p0 = pl.multiple_of(
        jnp.where(is_ctx, DEC_SEQ, ((t - N_CTX_TILES) % LAT_TILES_PER_REQ) * ROW_TILE), ROW_TILE)
    lane = lax.broadcasted_iota(jnp.int32, (ROW_TILE, 128), 1)
    first_d = (lane % 16) < 8
    first_s = (lane % 32) < 16
    lo = lane < HEAD_DIM
    cosd, sind = cosd_ref[pl.ds(p0, ROW_TILE), :], sind_ref[pl.ds(p0, ROW_TILE), :]
    coss, sins = coss_ref[pl.ds(p0, ROW_TILE), :], sins_ref[pl.ds(p0, ROW_TILE), :]

    def cols(c0, width):
        return _dot(hb(), wb[:, c0:c0 + width])

    def put(c0, width, val):
        zb_ref[:, c0:c0 + width] = val.astype(BF)

    def put_dup(c0, val):
        r = pltpu.roll(val, HEAD_DIM, 1)
        put(c0, 128, jnp.where(lo, val, r))
        put(c0 + 128, 128, jnp.where(lo, r, val))

    kept = {}

    def na_q():
        put(C_NA_Q, 256, cols(C_NA_Q, 256) * 0.125)

    def na_kv():
        kept["nakv"] = cols(C_NA_K, 512)
        put(C_NA_K, 512, kept["nakv"])

    def diff_q():
        z = cols(C_DQ, 256)
        for j in range(2):
            put(C_DQ + 128 * j, 128, _rope(z[:, 128 * j:128 * j + 128], cosd, sind, 8, first_d) * dscale)

    def diff_k():
        kept["dk"] = cols(C_DK, 256)
        for j in range(2):
            put(C_DK + 128 * j, 128, _rope(kept["dk"][:, 128 * j:128 * j + 128], cosd, sind, 8, first_d))

    def diff_v_fourier():
        kept["dvfc"] = cols(C_DV, 512)
        put(C_DV, 512, kept["dvfc"])

    def swa_q():
        z = cols(C_SQ, 256)
        for j in range(2):
            put(C_SQ + 128 * j, 128, _rope(z[:, 128 * j:128 * j + 128], coss, sins, 16, first_s) * 0.125)

    def swa_kv():
        kept["skv"] = cols(C_SK, 256)
        put_dup(ZB_SK, _rope(kept["skv"][:, 0:128], coss, sins, 16, first_s))
        put_dup(ZB_SV, kept["skv"][:, 128:256])

    def finish():
        new = (kept["nakv"][:, 0:256], kept["nakv"][:, 256:512], kept["dk"], kept["dvfc"][:, 0:256],
               kept["skv"][:, 0:128], kept["skv"][:, 128:256])
        for ref, val in zip(cache_refs, new):
            ref[...] = jnp.where(is_ctx, val, ref[...])

    return [na_q, na_kv, diff_q, diff_k, diff_v_fourier, swa_q, swa_kv], finish


def _trunk_kernel(do_mlp, do_proj, last, n_alias, *refs):
    xp_ref, xs_ref = refs[:2]
    pos = 2
    if do_mlp:
        catp_ref, cats_ref, mm_ref, g2_ref, fg_ref, wo_ref, w1_ref, w2_ref = refs[pos:pos + 8]
        pos += 8
    if do_proj:
        pm_ref, g1_ref, win_ref = refs[pos:pos + 3]
        rope_refs = refs[pos + 3:pos + 7]
        pos += 7 + n_alias
    if do_mlp:
        op_ref, os_ref = refs[pos:pos + 2]
        pos += 2
    if do_proj:
        zb_ref = refs[pos]
        cache_refs = refs[pos + 1:pos + 7]
        pos += 7
    if do_mlp:
        wob, w1b, w2b = refs[pos:pos + 3]
        pos += 3
    if do_proj:
        wb = refs[pos]
        pos += 1
    skew = do_proj
    if skew:
        hprev = refs[pos]
    s = pl.program_id(0)

    @pl.when(s < TRUNK_CAST_STEPS)
    def _():
        def cast(dst, src):
            rows = src.shape[0]
            dst[pl.ds(pl.multiple_of(s * rows, rows), rows), :] = src[...].astype(BF)

        if do_mlp:
            cast(wob, wo_ref)
            cast(w1b, w1_ref)
            cast(w2b, w2_ref)
        if do_proj:
            cast(wb, win_ref)
        if skew:
            @pl.when(s == 0)
            def _():
                hprev[...] = jnp.zeros_like(hprev)

    def tile_step(t, proj_stages, proj_finish):
        pending = list(proj_stages)

        def issue(n):
            for _ in range(min(n, len(pending))):
                pending.pop(0)()

        is_ctx = t < N_CTX_TILES
        row = _mod_row(t)
        x = jnp.where(is_ctx, xp_ref[...], xs_ref[...])
        if do_mlp:
            issue(1)
            gt1 = mm_ref[pl.ds(row, 1), 2 * D_MODEL:3 * D_MODEL]
            sh2 = mm_ref[pl.ds(row, 1), 3 * D_MODEL:4 * D_MODEL]
            sc2 = mm_ref[pl.ds(row, 1), 4 * D_MODEL:5 * D_MODEL]
            gt2 = mm_ref[pl.ds(row, 1), 5 * D_MODEL:6 * D_MODEL]
            cat = jnp.where(is_ctx, catp_ref[...], cats_ref[...])
            x = x + gt1 * _dot(cat, wob[...])
            issue(1)
            hh = (_rmsnorm(x, g2_ref[...]) * (1.0 + sc2) + sh2).astype(BF)
            acc = jnp.zeros((ROW_TILE, D_MODEL), F32)
            for c in range(D_FF // FF_CHUNK):
                a = jnp.maximum(_dot(hh, w1b[:, c * FF_CHUNK:(c + 1) * FF_CHUNK]), 0.0)
                acc = acc + _dot((a * a).astype(BF), w2b[c * FF_CHUNK:(c + 1) * FF_CHUNK, :])
                if c < 2:
                    issue(1)
            issue(1)
            x = x + gt2 * acc
            out = _rmsnorm(x, fg_ref[...]) if last else x
        hb = None
        if do_proj:
            issue(1)
            shift = pm_ref[pl.ds(row, 1), 0:D_MODEL]
            scale = pm_ref[pl.ds(row, 1), D_MODEL:2 * D_MODEL]
            hb = (_rmsnorm(x, g1_ref[...]) * (1.0 + scale) + shift).astype(BF)
        issue(len(pending))
        if proj_finish is not None:
            proj_finish()
        if do_mlp:
            op_ref[...] = jnp.where(is_ctx, out, op_ref[...])
            os_ref[...] = jnp.where(is_ctx, os_ref[...], out)
        return hb

    def projection(hb, t):
        return _projection_stages(hb, wb, t, t < N_CTX_TILES, zb_ref, cache_refs, rope_refs)

    @pl.when(s >= TRUNK_CAST_STEPS)
    def _():
        t = s - TRUNK_CAST_STEPS
        if not skew:
            hb = tile_step(t, [], None)
            if do_proj:
                stages, finish = projection(lambda: hb, t)
                for stage in stages:
                    stage()
                finish()
        else:
            u = jnp.maximum(t - 1, 0)

            @pl.when(t < N_ROW_TILES)
            def _():
                stages, finish = projection(lambda: hprev[...], u)
                hprev[...] = tile_step(t, stages, finish)

            @pl.when(t >= N_ROW_TILES)
            def _():
                stages, finish = projection(lambda: hprev[...], u)
                for stage in stages:
                    stage()
                finish()


def _trunk(xp, xs, ropes, mlp=None, proj=None):
    do_mlp, do_proj = mlp is not None, proj is not None
    last = do_mlp and not do_proj
    skew = do_proj
    clamp_tile = lambda s: jnp.clip(s - TRUNK_CAST_STEPS, 0, N_ROW_TILES - 1)
    step = lambda f: (lambda s: f(clamp_tile(s)))
    step_proj = (lambda f: (lambda s: f(clamp_tile(s - 1)))) if skew else step
    const = lambda shape: pl.BlockSpec(shape, lambda s: (0,) * len(shape), pipeline_mode=pl.Buffered(1))
    chunk = lambda layer, rows, cols: pl.BlockSpec(
        (None, rows // TRUNK_CAST_STEPS, cols), lambda s: (layer, jnp.minimum(s, TRUNK_CAST_STEPS - 1), 0))
    mod_spec = const((MOD_ROWS, 6 * D_MODEL))
    x_specs = [pl.BlockSpec((ROW_TILE, D_MODEL), step(lambda t: (_ctx_tile(t), 0))),
               pl.BlockSpec((ROW_TILE, D_MODEL), step(lambda t: (_lat_tile(t), 0)))]
    args, in_specs = [xp, xs], list(x_specs)
    out_specs, out_shape, scratch, aliases = [], [], [], {}
    n_alias = 0
    if do_mlp:
        layer, mods_l, cat_p, cat_s, g2, final_g, wo, w1, w2 = mlp
        args += [cat_p, cat_s, mods_l, g2.reshape(1, D_MODEL), final_g.reshape(1, D_MODEL), wo, w1, w2]
        in_specs += [pl.BlockSpec((ROW_TILE, MIX_WIDTH), step(lambda t: (_ctx_tile(t), 0))),
                     pl.BlockSpec((ROW_TILE, MIX_WIDTH), step(lambda t: (_lat_tile(t), 0))), mod_spec,
                     const((1, D_MODEL)), const((1, D_MODEL)),
                     chunk(layer, MIX_WIDTH, D_MODEL), chunk(layer, D_MODEL, D_FF), chunk(layer, D_FF, D_MODEL)]
    if do_proj:
        p_layer, mods_p, g1, w_in, caches = proj
        args += [mods_p, g1.reshape(1, D_MODEL), w_in, *ropes]
        in_specs += [mod_spec, const((1, D_MODEL)), chunk(p_layer, D_MODEL, IN_WIDTH)]
        in_specs += [const((DEC_SEQ + ROW_TILE, 128))] * 4
        if caches is not None:
            n_alias = 6
            n_out_before = 2 if do_mlp else 0
            aliases = {len(args) + i: n_out_before + 1 + i for i in range(6)}
            args += list(caches)
            in_specs += [pl.BlockSpec(memory_space=pl.ANY)] * 6
    if do_mlp:
        out_specs += x_specs
        out_shape += [jax.ShapeDtypeStruct((N_CTX_TOK, D_MODEL), F32),
                      jax.ShapeDtypeStruct((N_LAT_TOK, D_MODEL), F32)]
        scratch += [pltpu.VMEM((MIX_WIDTH, D_MODEL), BF), pltpu.VMEM((D_MODEL, D_FF), BF),
                    pltpu.VMEM((D_FF, D_MODEL), BF)]
    if do_proj:
        out_specs += [pl.BlockSpec((ROW_TILE, ZB_WIDTH), step_proj(lambda t: (t, 0)))] + [
            pl.BlockSpec((None, None, SEQ, width), step_proj(lambda t: (_ctx_tile(t), p_layer, 0, 0)))
            for _, width in CACHE_COLS]
        out_shape += [jax.ShapeDtypeStruct((N_TOK, ZB_WIDTH), BF)] + [
            jax.ShapeDtypeStruct((BATCH, DEPTH, SEQ, width), F32) for _, width in CACHE_COLS]
        scratch += [pltpu.VMEM((D_MODEL, IN_WIDTH), BF)]
    if skew:
        scratch += [pltpu.VMEM((ROW_TILE, D_MODEL), BF)]
    outs = pl.pallas_call(
        functools.partial(_trunk_kernel, do_mlp, do_proj, last, n_alias),
        grid=(TRUNK_CAST_STEPS + N_ROW_TILES + (1 if skew else 0),),
        in_specs=in_specs,
        out_specs=out_specs,
        out_shape=out_shape,
        scratch_shapes=scratch,
        input_output_aliases=aliases,
        compiler_params=pltpu.CompilerParams(dimension_semantics=("arbitrary",),
                                             vmem_limit_bytes=TRUNK_VMEM_LIMIT),
        name="trunk_mlp_proj" if (do_mlp and do_proj) else ("trunk_mlp" if do_mlp else "trunk_proj"),
    )(*args)
    outs = list(outs)
    if do_mlp:
        xp, xs = outs[:2]
        outs = outs[2:]
    zb, new_caches = (outs[0], tuple(outs[1:7])) if do_proj else (None, None)
    return xp, xs, zb, new_caches


def _diff_lambda(lamp_ref, lam_init):
    a = jnp.sum(lamp_ref[0:1, :] * lamp_ref[1:2, :], axis=-1, keepdims=True)
    b = jnp.sum(lamp_ref[2:3, :] * lamp_ref[3:4, :], axis=-1, keepdims=True)
    return jnp.exp(a) - jnp.exp(b) + lam_init


def _lane_lo(n):
    return lax.broadcasted_iota(jnp.int32, (n, 128), 1) < HEAD_DIM


def _lane_lo_wide(n, width):
    return lax.broadcasted_iota(jnp.int32, (n, width), 1) % 128 < HEAD_DIM


def _split_pair(x, lo):
    zero = jnp.zeros_like(x)
    return jnp.where(lo, x, zero), jnp.where(lo, zero, x)


def _v_ones_pair(v, lo):
    one = jnp.ones_like(v)
    return jnp.where(lo, v, one), jnp.where(lo, one, v)


def _v_stack(v, lo):
    zero = jnp.zeros_like(v)
    ones_a = jnp.where(lo, 1.0, 0.0).astype(v.dtype)
    ones_b = jnp.where(lo, 0.0, 1.0).astype(v.dtype)
    top = jnp.concatenate([jnp.where(lo, v, zero), ones_a], axis=1)
    bottom = jnp.concatenate([jnp.where(lo, zero, v), ones_b], axis=1)
    return jnp.concatenate([top, bottom], axis=0)


def _merge_pair(a, b, lo):
    return jnp.where(lo, a, b), pltpu.roll(jnp.where(lo, b, a), HEAD_DIM, 1)


def _subln_pair(o, g2, lam_init, lo):
    sq = o * o
    ms_a = jnp.sum(jnp.where(lo, sq, 0.0), axis=-1, keepdims=True)
    ms_b = jnp.sum(jnp.where(lo, 0.0, sq), axis=-1, keepdims=True)
    ms = jnp.where(lo, ms_a, ms_b) * (1.0 / HEAD_DIM)
    return o * lax.rsqrt(ms + NORM_EPS) * g2 * (1.0 - lam_init)


def _diff_quarters(q, n):
    quarter = lax.broadcasted_iota(jnp.int32, (n, 128), 1) // DIFF_QK_DIM
    zero = jnp.zeros_like(q)
    return [jnp.where(quarter == i, q, zero) for i in range(4)]


def _attention_pipeline(units, score_phase, softmax_phase, value_phase, on_result=None):
    res, scores, weights = [], {}, {}
    n = len(units)
    for i in range(n + 2):
        if i < n:
            scores[i] = score_phase(units[i])
        if 0 <= i - 1 < n:
            weights[i - 1] = softmax_phase(units[i - 1], scores.pop(i - 1))
        if 0 <= i - 2 < n:
            res.append(value_phase(units[i - 2], weights.pop(i - 2)))
            if on_result is not None:
                on_result(res)
    return res


def _fourier(x_bf, cl, sl, cbd, sbd, wf):
    xc = _dot(x_bf, cbd).astype(BF)
    xs = _dot(x_bf, sbd).astype(BF)
    y = _dot(cl, xc) - _dot(sl, xs)
    return _dot(y.astype(BF), wf)


CTX_REQ_PER_STEP = 2


def _ctx_mix_kernel(lam_init, with_ada, sink_ref, z_ref, lamp_ref, subg_ref, cl_ref, sl_ref, cbd_ref, sbd_ref,
                    wf_ref, *rest):
    if with_ada:
        cond_ref, wa_ref, ba_ref, o_ref, mods_ref = rest
        _ada_kernel(cond_ref, wa_ref, ba_ref, mods_ref)
    else:
        o_ref, = rest
    lo = _lane_lo(SEQ)
    z_refs = [z_ref.at[pl.ds(r * SEQ, SEQ)] for r in range(CTX_REQ_PER_STEP)]
    o_refs = [o_ref.at[pl.ds(r * SEQ, SEQ)] for r in range(CTX_REQ_PER_STEP)]

    pairs = []
    for zr in z_refs:
        for j in range(2):
            c = 128 * j
            qs = _split_pair(zr[:, C_NA_Q + c:C_NA_Q + c + 128], lo)
            pairs.append((zr, qs[0], qs[1], C_NA_K + c, _v_stack(zr[:, C_NA_V + c:C_NA_V + c + 128], lo), None))
        for j in range(2):
            c = 128 * j
            q4 = _diff_quarters(zr[:, C_DQ + c:C_DQ + c + 128], SEQ)
            v2 = _v_stack(zr[:, C_DV + c:C_DV + c + 128], lo)
            pairs.append((zr, q4[0], q4[2], C_DK + c, v2, None))
            pairs.append((zr, q4[1], q4[3], C_DK + c, v2, None))
        for g in range(2):
            c = 128 * g
            qs = _split_pair(zr[:, C_SQ + c:C_SQ + c + 128], lo)
            pairs.append((zr, qs[0], qs[1], ZB_SK + c, _v_stack(zr[:, ZB_SV + c:ZB_SV + c + 128], lo),
                          (sink_ref[2 * g], sink_ref[2 * g + 1])))

    def score_phase(pair):
        zr, qa, qb, kc, _, _ = pair
        return [_dot_nt(q, zr[:, kc:kc + 128]) for q in (qa, qb)]

    def softmax_phase(pair, scores):
        sinks = pair[5]
        es, xs = [], []
        for i, s in enumerate(scores):
            m = _rowmax(s)
            if sinks is not None:
                m = jnp.maximum(m, sinks[i])
                xs.append(jnp.exp(sinks[i] - m))
            es.append(jnp.exp(s - m).astype(BF))
        return jnp.concatenate(es, axis=1), xs

    def value_phase(pair, weights):
        e2, xs = weights
        r = _dot(e2, pair[4])
        num, den = r[:, 0:128], r[:, 128:256]
        if xs:
            den = den + jnp.where(lo, xs[0], xs[1])
        return num / den

    lam = _diff_lambda(lamp_ref, lam_init)

    def write_request(zr, out, rr):
        for j in range(2):
            out[:, 128 * j:128 * j + 128] = rr[j].astype(BF)
        for j in range(2):
            o = rr[2 + 2 * j] - lam * rr[3 + 2 * j]
            out[:, 256 + 128 * j:384 + 128 * j] = _subln_pair(o, subg_ref[...], lam_init, lo).astype(BF)
        o_c = _fourier(zr[:, C_FC:C_FC + 256], cl_ref[...], sl_ref[...], cbd_ref[...], sbd_ref[...], wf_ref[...])
        out[:, 512:768] = o_c.astype(BF)
        for g in range(2):
            out[:, 768 + 128 * g:896 + 128 * g] = rr[6 + g].astype(BF)

    def on_result(res):
        if len(res) % 8 == 0:
            r = len(res) // 8 - 1
            write_request(z_refs[r], o_refs[r], res[8 * r:8 * r + 8])

    _attention_pipeline(pairs, score_phase, softmax_phase, value_phase, on_result)


def _ctx_mix(zb, sink, lamp, subg, cl, sl, cbd, sbd, wf, lam_init, ada=None):
    full = lambda shape: pl.BlockSpec(shape, lambda b, s: (0,) * len(shape))
    n_steps = BATCH // CTX_REQ_PER_STEP
    in_specs = [
        pl.BlockSpec((CTX_REQ_PER_STEP * SEQ, ZB_WIDTH), lambda b, s: (b, 0)),
        full((4, DIFF_QK_DIM)),
        full((1, 2 * HEAD_DIM)),
        full((SEQ, SEQ)), full((SEQ, SEQ)),
        full((256, 256)), full((256, 256)), full((256, 256)),
    ]
    out_specs = [pl.BlockSpec((CTX_REQ_PER_STEP * SEQ, MIX_WIDTH), lambda b, s: (b, 0))]
    out_shape = [jax.ShapeDtypeStruct((N_CTX_TOK, MIX_WIDTH), BF)]
    args = [sink, zb, lamp, subg, cl, sl, cbd, sbd, wf]
    if ada is not None:
        cond, w_ada, b_ada3, layer = ada
        ada_in, ada_out = _ada_specs(layer, n_steps, lambda b, s: b)
        in_specs += ada_in
        out_specs.append(ada_out)
        out_shape.append(jax.ShapeDtypeStruct((MOD_ROWS, 6 * D_MODEL), F32))
        args += [cond, w_ada, b_ada3]
    grid_spec = pltpu.PrefetchScalarGridSpec(
        num_scalar_prefetch=1, grid=(n_steps,), in_specs=in_specs, out_specs=out_specs)
    outs = pl.pallas_call(
        functools.partial(_ctx_mix_kernel, lam_init, ada is not None),
        grid_spec=grid_spec,
        out_shape=out_shape,
        compiler_params=_params(),
        name="ctx_mix",
    )(*args)
    return outs[0], (outs[1] if ada is not None else None)


def _cache_v_ones(cv_ref, cva, cvb):
    first = lax.broadcasted_iota(jnp.int32, (256, PAST_LEN), 0) % 128 < HEAD_DIM
    cv = cv_ref[...]
    cva[...] = jnp.where(first, cv, 1.0).astype(BF)
    cvb[...] = jnp.where(first, 1.0, cv).astype(BF)


def _na_bias_tiles(rpp_ref, t2):
    c = lax.broadcasted_iota(jnp.int32, (GRID_W, 128), 0)
    kc = lax.broadcasted_iota(jnp.int32, (GRID_W, 128), 1) % GRID_W
    c_start = jnp.clip(c - NA_KW // 2, 0, GRID_W - NA_KW)
    inside = jnp.logical_and(kc >= c_start, kc < c_start + NA_KW)
    for h in range(4):
        for dr in range(2 * NA_KH - 1):
            row = jnp.broadcast_to(rpp_ref[h, dr:dr + 1, :], (GRID_W, 128))
            toeplitz = pltpu.roll(row, 128 - (NA_KW - 1), 1, stride=1, stride_axis=0)
            t2[h, dr] = jnp.where(inside, toeplitz, NEG_INF)
        t2[h, 2 * NA_KH - 1] = jnp.full((GRID_W, 128), NEG_INF, F32)


def _lat_na_part(z_ref, ck_ref, cv_ref, t2, o_ref, ckb, cva, cvb):
    ckb[...] = ck_ref[...].astype(BF)
    _cache_v_ones(cv_ref, cva, cvb)
    n_q = NA_QROWS * GRID_W
    n_loc = NA_WIN_ROWS * GRID_W
    lo_q = _lane_lo(n_q)
    lo_w = _lane_lo(n_loc)

    lo_t = _lane_lo(GRID_W)

    def block(p):
        w_row = jnp.clip(NA_QROWS * p - NA_KH // 2, 0, GRID_ROWS - NA_WIN_ROWS)
        q0 = pl.multiple_of(p * n_q, n_q)
        k0 = pl.multiple_of(w_row * GRID_W, NA_QROWS * GRID_W)

        def tile_index(r, j):
            r_start = jnp.clip(r - NA_KH // 2, 0, GRID_ROWS - NA_KH)
            kr = w_row + j
            inside = jnp.logical_and(kr >= r_start, kr < r_start + NA_KH)
            return jnp.where(inside, kr - r + NA_KH - 1, 2 * NA_KH - 1)

        def bias(h):
            rows = []
            for rl in range(NA_QROWS):
                r = NA_QROWS * p + rl
                pieces = [jnp.where(lo_t, t2[h, tile_index(r, 2 * jj)], t2[h, tile_index(r, 2 * jj + 1)])
                          for jj in range(NA_WIN_ROWS // 2)]
                rows.append(jnp.concatenate(pieces, axis=1))
            return jnp.concatenate(rows, axis=0)

        return q0, k0, bias

    def score_phase(unit):
        q0, k0, bias, h = unit
        c = 128 * (h // 2)
        q = _split_pair(z_ref[pl.ds(q0, n_q), C_NA_Q + c:C_NA_Q + c + 128], lo_q)[h % 2]
        s_loc = _dot_nt(q, z_ref[pl.ds(k0, n_loc), C_NA_K + c:C_NA_K + c + 128]) + bias(h)
        return s_loc, _dot(q, ckb[c:c + 128, :])

    def softmax_phase(unit, scores):
        s_loc, s_ctx = scores
        m = _rowmax(s_loc, s_ctx)
        return jnp.exp(s_loc - m).astype(BF), jnp.exp(s_ctx - m).astype(BF)

    def value_phase(unit, weights):
        q0, k0, bias, h = unit
        e_loc, e_ctx = weights
        c = 128 * (h // 2)
        v = _v_ones_pair(z_ref[pl.ds(k0, n_loc), C_NA_V + c:C_NA_V + c + 128], lo_w)[h % 2]
        return _dot(e_loc, v) + _dot_nt(e_ctx, (cva, cvb)[h % 2][c:c + 128, :])

    def body(i, carry):
        blocks = [block(NA_BLOCKS_PER_TRIP * i + b) for b in range(NA_BLOCKS_PER_TRIP)]
        units = [(q0, k0, bias, h) for q0, k0, bias in blocks for h in range(4)]
        res = _attention_pipeline(units, score_phase, softmax_phase, value_phase)
        for b, (q0, _, _) in enumerate(blocks):
            for j in range(2):
                num, den = _merge_pair(res[4 * b + 2 * j], res[4 * b + 2 * j + 1], lo_q)
                o_ref[pl.ds(q0, n_q), 128 * j:128 * j + 128] = (num / den).astype(BF)
        return carry

    lax.fori_loop(0, GRID_ROWS // NA_QROWS // NA_BLOCKS_PER_TRIP, body, 0)


DIFF_QBLK = 1024
DIFF_GROUP = 2


def _lat_diff_part(lam_init, z_ref, ck_ref, cv_ref, lamp_ref, subg_ref, o_ref, ckb, cva, cvb, vla, vlb):
    ckb[...] = ck_ref[...].astype(BF)
    _cache_v_ones(cv_ref, cva, cvb)
    lo_v = _lane_lo_wide(DEC_SEQ, 256)
    v_loc = z_ref[:, C_DV:C_DV + 256]
    one = jnp.ones_like(v_loc)
    vla[...] = jnp.where(lo_v, v_loc, one)
    vlb[...] = jnp.where(lo_v, one, v_loc)
    lam = _diff_lambda(lamp_ref, lam_init)
    qblk = DIFF_QBLK
    lo_q = _lane_lo(qblk)

    def body(i, carry):
        q0 = pl.multiple_of(i * qblk, qblk)
        for j in range(2):
            c = 128 * j
            q4 = _diff_quarters(z_ref[pl.ds(q0, qblk), C_DQ + c:C_DQ + c + 128], qblk)
            res = []
            for t0 in range(0, 4, DIFF_GROUP):
                scores = [(_dot_nt(q, z_ref[:, C_DK + c:C_DK + c + 128]), _dot(q, ckb[c:c + 128, :]))
                          for q in q4[t0:t0 + DIFF_GROUP]]
                weights = []
                for s_loc, s_ctx in scores:
                    m = _rowmax(s_loc, s_ctx)
                    weights.append((jnp.exp(s_loc - m).astype(BF), jnp.exp(s_ctx - m).astype(BF)))
                for t, (e_loc, e_ctx) in enumerate(weights, start=t0):
                    v_loc, v_ctx = ((vla, cva), (vlb, cvb))[t // 2]
                    res.append(_dot(e_loc, v_loc[:, c:c + 128]) + _dot_nt(e_ctx, v_ctx[c:c + 128, :]))
            n1, d1 = _merge_pair(res[0], res[2], lo_q)
            n2, d2 = _merge_pair(res[1], res[3], lo_q)
            o = n1 / d1 - lam * (n2 / d2)
            o_ref[pl.ds(q0, qblk), 256 + c:256 + c + 128] = (
                _subln_pair(o, subg_ref[...], lam_init, lo_q).astype(BF))
        return carry

    lax.fori_loop(0, DEC_SEQ // qblk, body, 0)


SWA_BLOCKS_PER_TRIP = 2


def _lat_swa_part(sink_ref, z_ref, ck_ref, cv_ref, o_ref, ckd, cva, cvb):
    W = SWA_WINDOW
    n_win = 3 * W
    ones = jnp.ones((HEAD_DIM, PAST_LEN), BF)
    for g in range(2):
        k_g = ck_ref[HEAD_DIM * g:HEAD_DIM * (g + 1), :].astype(BF)
        v_g = cv_ref[HEAD_DIM * g:HEAD_DIM * (g + 1), :].astype(BF)
        r0, r1, r2 = 128 * g, 128 * g + HEAD_DIM, 128 * (g + 1)
        ckd[r0:r1, :] = k_g
        ckd[r1:r2, :] = k_g
        cva[r0:r1, :] = v_g
        cva[r1:r2, :] = ones
        cvb[r0:r1, :] = ones
        cvb[r1:r2, :] = v_g
    lo_q = _lane_lo(W)
    lo_w = _lane_lo(n_win)

    def block(n):
        q0 = pl.multiple_of(n * W, W)
        w0 = pl.multiple_of(jnp.clip((n - 1) * W, 0, DEC_SEQ - n_win), W)
        qpos = q0 + lax.broadcasted_iota(jnp.int32, (W, n_win), 0)
        kpos = w0 + lax.broadcasted_iota(jnp.int32, (W, n_win), 1)
        return q0, w0, jnp.abs(qpos - kpos) <= W

    def score_phase(unit):
        q0, w0, valid, h = unit
        c = 128 * (h // 2)
        q = _split_pair(z_ref[pl.ds(q0, W), C_SQ + c:C_SQ + c + 128], lo_q)[h % 2]
        s_loc = jnp.where(valid, _dot_nt(q, z_ref[pl.ds(w0, n_win), ZB_SK + c:ZB_SK + c + 128]), NEG_INF)
        return s_loc, _dot(q, ckd[c:c + 128, :])

    def softmax_phase(unit, scores):
        s_loc, s_ctx = scores
        sink = sink_ref[unit[3]]
        m = jnp.maximum(_rowmax(s_loc, s_ctx), sink)
        return jnp.exp(s_loc - m).astype(BF), jnp.exp(s_ctx - m).astype(BF), jnp.exp(sink - m)

    def value_phase(unit, weights):
        q0, w0, valid, h = unit
        e_loc, e_ctx, extra = weights
        c = 128 * (h // 2)
        v = _v_ones_pair(z_ref[pl.ds(w0, n_win), ZB_SV + c:ZB_SV + c + 128], lo_w)[h % 2]
        return _dot(e_loc, v) + _dot_nt(e_ctx, (cva, cvb)[h % 2][c:c + 128, :]), extra

    def body(i, carry):
        blocks = [block(SWA_BLOCKS_PER_TRIP * i + b) for b in range(SWA_BLOCKS_PER_TRIP)]
        units = [(q0, w0, valid, h) for q0, w0, valid in blocks for h in range(4)]
        res = _attention_pipeline(units, score_phase, softmax_phase, value_phase)
        for b, (q0, _, _) in enumerate(blocks):
            for g in range(2):
                (r_a, x_a), (r_b, x_b) = res[4 * b + 2 * g], res[4 * b + 2 * g + 1]
                num, den = _merge_pair(r_a, r_b, lo_q)
                den = den + jnp.where(lo_q, x_a, x_b)
                o_ref[pl.ds(q0, W), 768 + 128 * g:768 + 128 * g + 128] = (num / den).astype(BF)
        return carry

    lax.fori_loop(0, DEC_SEQ // W // SWA_BLOCKS_PER_TRIP, body, 0)


def _lat_mix_kernel(lam_init, sink_ref, z_ref, nak_ref, nav_ref, dfk_ref, dfv_ref, swk_ref, swv_ref, rpp_ref,
                    lamp_ref, subg_ref, cl_ref, sl_ref, cbd_ref, sbd_ref, wf_ref, o_ref,
                    ck, cva, cvb, vla, vlb, t2):
    @pl.when(pl.program_id(0) == 0)
    def _():
        _na_bias_tiles(rpp_ref, t2)

    _lat_na_part(z_ref, nak_ref, nav_ref, t2, o_ref, ck, cva, cvb)
    _lat_diff_part(lam_init, z_ref, dfk_ref, dfv_ref, lamp_ref, subg_ref, o_ref, ck, cva, cvb, vla, vlb)
    o_c = _fourier(z_ref[:, C_FC:C_FC + 256], cl_ref[...], sl_ref[...], cbd_ref[...], sbd_ref[...], wf_ref[...])
    o_ref[:, 512:768] = o_c.astype(BF)
    _lat_swa_part(sink_ref, z_ref, swk_ref, swv_ref, o_ref, ck, cva, cvb)


def _lat_mix(zb, caches_in, rpp, sink, lamp, subg, cl, sl, cbd, sbd, wf, lam_init, layer):
    full = lambda shape: pl.BlockSpec(shape, lambda b, s: (0,) * len(shape), pipeline_mode=pl.Buffered(1))
    cache = lambda width: pl.BlockSpec((None, None, width, PAST_LEN), lambda b, s: (b, layer, 0, 0))
    grid_spec = pltpu.PrefetchScalarGridSpec(
        num_scalar_prefetch=1,
        grid=(DEC_BATCH,),
        in_specs=[
            pl.BlockSpec((DEC_SEQ, ZB_WIDTH), lambda b, s: (LAT_BLOCK0 + b, 0)),
            cache(256), cache(256), cache(256), cache(256), cache(128), cache(128),
            pl.BlockSpec((None, 4, 2 * NA_KH, 128), lambda b, s: (layer, 0, 0, 0), pipeline_mode=pl.Buffered(1)),
            full((4, DIFF_QK_DIM)), full((1, 2 * HEAD_DIM)),
            full((DEC_SEQ, DEC_SEQ)), full((DEC_SEQ, DEC_SEQ)),
            full((256, 256)), full((256, 256)), full((256, 256)),
        ],
        out_specs=pl.BlockSpec((DEC_SEQ, MIX_WIDTH), lambda b, s: (b, 0)),
        scratch_shapes=[pltpu.VMEM((256, PAST_LEN), BF)] * 3 + [pltpu.VMEM((DEC_SEQ, 256), BF)] * 2
        + [pltpu.VMEM((4, 2 * NA_KH, GRID_W, 128), F32)],
    )
    return pl.pallas_call(
        functools.partial(_lat_mix_kernel, lam_init),
        grid_spec=grid_spec,
        out_shape=jax.ShapeDtypeStruct((N_LAT_TOK, MIX_WIDTH), BF),
        compiler_params=_params(),
        name="lat_mix",
    )(sink, zb, *caches_in, rpp, lamp, subg, cl, sl, cbd, sbd, wf)


def _dft_tables(n):
    j = np.arange(n)
    ang = 2.0 * np.pi * ((j[:, None] * j[None, :]) % n) / n
    return np.cos(ang) / np.sqrt(n), np.sin(ang) / np.sqrt(n)


def _block_diag4(m):
    out = np.zeros((256, 256), m.dtype)
    for g in range(4):
        out[64 * g:64 * g + 64, 64 * g:64 * g + 64] = m
    return out


def _rope_tables(n_axis_dims):
    half = n_axis_dims // 2
    inv = ROPE_BASE ** (-np.arange(half, dtype=np.float64) / half)
    t = np.arange(DEC_SEQ)
    lane = np.arange(128)
    w = lane % n_axis_dims
    is_col = (lane // n_axis_dims) % 2 == 1
    pos = np.where(is_col[None, :], (t % GRID_W)[:, None], (t // GRID_W)[:, None]).astype(np.float64)
    ang = pos * inv[w % half][None, :]
    sign = np.where(w < half, -1.0, 1.0)[None, :]
    cos = np.concatenate([np.cos(ang), np.ones((ROW_TILE, 128))], axis=0)
    sin = np.concatenate([np.sin(ang) * sign, np.zeros((ROW_TILE, 128))], axis=0)
    return jnp.asarray(cos, F32), jnp.asarray(sin, F32)


def _pad_rpb_rows(rpb):
    n_dc = rpb.shape[-1]
    half = jnp.pad(rpb, ((0, 0), (0, 0), (0, 1), (0, GRID_W - n_dc)))
    return jnp.concatenate([half, half], axis=-1)


def kernel(x_prompt, x_sample, cache_na_k, cache_na_v, cache_diff_k, cache_diff_v, cache_swa_k, cache_swa_v, c, c_ctx, w_ada, b_ada, norm1_g, norm2_g, w_in, na_rpb, diff_lq1, diff_lk1, diff_lq2, diff_lk2, diff_subln_g, w_fourier, swa_sink, w_out, w_mlp1, w_mlp2, final_g):
    xp = x_prompt.reshape(N_CTX_TOK, D_MODEL)
    xs = x_sample.reshape(N_LAT_TOK, D_MODEL)
    cond = jnp.zeros((MOD_ROWS, D_MODEL), F32).at[0].set(c_ctx).at[1:1 + DEC_BATCH].set(c)
    b_ada3 = b_ada.reshape(DEPTH, 1, 6 * D_MODEL)
    mods = _ada(cond, w_ada, b_ada3, 0)

    cl_p, sl_p = _dft_tables(SEQ)
    cl_s, sl_s = _dft_tables(DEC_SEQ)
    c64, s64 = _dft_tables(64)
    cl_p, sl_p, cl_s, sl_s, cbd, sbd = (
        jnp.asarray(a, F32).astype(BF) for a in (cl_p, sl_p, cl_s, sl_s, _block_diag4(c64), _block_diag4(s64)))
    ropes = _rope_tables(16) + _rope_tables(32)
    rpp = _pad_rpb_rows(na_rpb)
    wf_bf = w_fourier.astype(BF)

    ck_na, cv_na, ck_df, cv_df, ck_sw, cv_sw = (
        a.transpose(0, 1, 3, 4, 2).reshape(DEC_BATCH, DEPTH, -1, PAST_LEN)
        for a in (cache_na_k, cache_na_v, cache_diff_k, cache_diff_v, cache_swa_k, cache_swa_v))

    _, _, zb, caches = _trunk(xp, xs, ropes, proj=(0, mods, norm1_g[0], w_in, None))
    for l in range(DEPTH):
        lam_init = 0.8 - 0.6 * math.exp(-0.3 * l)
        lamp = jnp.stack([diff_lq1[l], diff_lk1[l], diff_lq2[l], diff_lk2[l]], axis=0)
        subg = jnp.tile(diff_subln_g[l].reshape(1, HEAD_DIM), (1, 2))
        has_next = l + 1 < DEPTH

        cat_p, mods_next = _ctx_mix(zb, swa_sink[l], lamp, subg, cl_p, sl_p, cbd, sbd, wf_bf[l], lam_init,
                                    ada=(cond, w_ada, b_ada3, l + 1) if has_next else None)
        cat_s = _lat_mix(zb, (ck_na, cv_na, ck_df, cv_df, ck_sw, cv_sw), rpp, swa_sink[l], lamp, subg,
                         cl_s, sl_s, cbd, sbd, wf_bf[l], lam_init, l)
        nxt = (l + 1, mods_next, norm1_g[l + 1], w_in, caches) if has_next else None
        xp, xs, zb, new_caches = _trunk(
            xp, xs, ropes, mlp=(l, mods, cat_p, cat_s, norm2_g[l], final_g, w_out, w_mlp1, w_mlp2), proj=nxt)
        caches = new_caches if has_next else caches
        mods = mods_next

    y_prompt = xp.reshape(BATCH, SEQ, D_MODEL)
    y_sample = xs.reshape(DEC_BATCH, DEC_SEQ, D_MODEL)
    new = [a.reshape(BATCH, DEPTH, SEQ, a.shape[-1] // HEAD_DIM, HEAD_DIM) for a in caches]
    return (y_prompt, y_sample) + tuple(new)
```

```python
import functools
import math

import numpy as np
import jax
import jax.numpy as jnp
from jax import lax
from jax.experimental import pallas as pl
from jax.experimental.pallas import tpu as pltpu

D_MODEL = 1024
BATCH = 16
SEQ = 256
DEPTH = 4
DEC_BATCH = 2
DEC_SEQ = 1024
PAST_LEN = 512
GRID_W = 64
GRID_ROWS = DEC_SEQ // GRID_W
HEAD_DIM = 64
NA_KH = 8
NA_KW = 16
DIFF_QK_DIM = 32
SWA_WINDOW = 128
D_FF = 4 * D_MODEL
ROPE_BASE = 10000.0
NORM_EPS = 1e-6
NEG_INF = -1e30
IN_WIDTH = 2304
MIX_WIDTH = 1024

N_CTX_TOK = BATCH * SEQ
N_LAT_TOK = DEC_BATCH * DEC_SEQ
N_TOK = N_CTX_TOK + N_LAT_TOK
ROW_TILE = 256
N_ROW_TILES = N_TOK // ROW_TILE
N_CTX_TILES = N_CTX_TOK // ROW_TILE
LAT_TILES_PER_REQ = DEC_SEQ // ROW_TILE
LAT_BLOCK0 = N_CTX_TOK // DEC_SEQ
MOD_ROWS = 8

C_NA_Q, C_NA_K, C_NA_V = 0, 256, 512
C_DQ, C_DK, C_DV = 768, 1024, 1280
C_FC = 1536
C_SQ, C_SK, C_SV = 1792, 2048, 2176
CACHE_COLS = ((C_NA_K, 256), (C_NA_V, 256), (C_DK, 256), (C_DV, 256), (C_SK, 128), (C_SV, 128))
ZB_SK, ZB_SV = 2048, 2304
ZB_WIDTH = 2560

NA_QROWS = 2
NA_WIN_ROWS = 10
NA_BLOCKS_PER_TRIP = 4

BF = jnp.bfloat16
F32 = jnp.float32
VMEM_LIMIT = 56 * 1024 * 1024


def _dot(a, b):
    return jnp.dot(a, b, preferred_element_type=F32)


def _dot_nt(a, b):
    return lax.dot_general(a, b, (((1,), (1,)), ((), ())), preferred_element_type=F32)


def _rmsnorm(x, g):
    ms = jnp.mean(x * x, axis=-1, keepdims=True)
    return x * lax.rsqrt(ms + NORM_EPS) * g


def _params(n_grid=1):
    return pltpu.CompilerParams(dimension_semantics=("arbitrary",) * n_grid, vmem_limit_bytes=VMEM_LIMIT)


def _rowmax(*parts):
    m = jnp.max(parts[0], axis=-1, keepdims=True)
    for p in parts[1:]:
        m = jnp.maximum(m, jnp.max(p, axis=-1, keepdims=True))
    return m


def _ada_kernel(cond_ref, w_ref, b_ref, o_ref):
    cnd = cond_ref[...]
    s = cnd / (1.0 + jnp.exp(-cnd))
    o_ref[...] = _dot(s.astype(BF), w_ref[...].astype(BF)) + b_ref[...]


def _ada_specs(layer, n_steps, index_of_step):
    tn = 6 * D_MODEL // n_steps
    col = lambda *g: index_of_step(*g)
    in_specs = [
        pl.BlockSpec((MOD_ROWS, D_MODEL), lambda *g: (0, 0)),
        pl.BlockSpec((None, D_MODEL, tn), lambda *g: (layer, 0, col(*g))),
        pl.BlockSpec((None, 1, tn), lambda *g: (layer, 0, col(*g))),
    ]
    return in_specs, pl.BlockSpec((MOD_ROWS, tn), lambda *g: (0, col(*g)))


def _ada(cond, w_ada, b_ada3, layer):
    in_specs, out_spec = _ada_specs(layer, 6, lambda j: j)
    return pl.pallas_call(
        _ada_kernel,
        grid=(6,),
        in_specs=in_specs,
        out_specs=out_spec,
        out_shape=jax.ShapeDtypeStruct((MOD_ROWS, 6 * D_MODEL), F32),
        compiler_params=_params(),
        name="ada",
    )(cond, w_ada, b_ada3)


def _mod_row(t):
    return jnp.where(t < N_CTX_TILES, 0, 1 + (t - N_CTX_TILES) // LAT_TILES_PER_REQ)


def _ctx_tile(t):
    return jnp.minimum(t, N_CTX_TILES - 1)


def _lat_tile(t):
    return jnp.maximum(t - N_CTX_TILES, 0)


TRUNK_CAST_STEPS = 8
FF_CHUNK = 1024
TRUNK_VMEM_LIMIT = 60 * 1024 * 1024


def _rope(x, cos, sin_signed, shift, first):
    partner = jnp.where(first, pltpu.roll(x, 128 - shift, 1), pltpu.roll(x, shift, 1))
    return x * cos + partner * sin_signed


def _projection_stages(hb, wb, t, is_ctx, zb_ref, cache_refs, rope_refs):
    cosd_ref, sind_ref, coss_ref, sins_ref = rope_refs
    dscale = DIFF_QK_DIM ** -0.5
    p0 = pl.multiple_of(
        jnp.where(is_ctx, DEC_SEQ, ((t - N_CTX_TILES) % LAT_TILES_PER_REQ) * ROW_TILE), ROW_TILE)
    lane = lax.broadcasted_iota(jnp.int32, (ROW_TILE, 128), 1)
    first_d = (lane % 16) < 8
    first_s = (lane % 32) < 16
    lo = lane < HEAD_DIM
    cosd, sind = cosd_ref[pl.ds(p0, ROW_TILE), :], sind_ref[pl.ds(p0, ROW_TILE), :]
    coss, sins = coss_ref[pl.ds(p0, ROW_TILE), :], sins_ref[pl.ds(p0, ROW_TILE), :]

    def cols(c0, width):
        return _dot(hb(), wb[:, c0:c0 + width])

    def put(c0, width, val):
        zb_ref[:, c0:c0 + width] = val.astype(BF)

    def put_dup(c0, val):
        r = pltpu.roll(val, HEAD_DIM, 1)
        put(c0, 128, jnp.where(lo, val, r))
        put(c0 + 128, 128, jnp.where(lo, r, val))

    kept = {}

    def na_q():
        put(C_NA_Q, 256, cols(C_NA_Q, 256) * 0.125)

    def na_kv():
        kept["nakv"] = cols(C_NA_K, 512)
        put(C_NA_K, 512, kept["nakv"])

    def diff_q():
        z = cols(C_DQ, 256)
        for j in range(2):
            put(C_DQ + 128 * j, 128, _rope(z[:, 128 * j:128 * j + 128], cosd, sind, 8, first_d) * dscale)

    def diff_k():
        kept["dk"] = cols(C_DK, 256)
        for j in range(2):
            put(C_DK + 128 * j, 128, _rope(kept["dk"][:, 128 * j:128 * j + 128], cosd, sind, 8, first_d))

    def diff_v_fourier():
        kept["dvfc"] = cols(C_DV, 512)
        put(C_DV, 512, kept["dvfc"])

    def swa_q():
        z = cols(C_SQ, 256)
        for j in range(2):
            put(C_SQ + 128 * j, 128, _rope(z[:, 128 * j:128 * j + 128], coss, sins, 16, first_s) * 0.125)

    def swa_kv():
        kept["skv"] = cols(C_SK, 256)
        put_dup(ZB_SK, _rope(kept["skv"][:, 0:128], coss, sins, 16, first_s))
        put_dup(ZB_SV, kept["skv"][:, 128:256])

    def finish():
        new = (kept["nakv"][:, 0:256], kept["nakv"][:, 256:512], kept["dk"], kept["dvfc"][:, 0:256],
               kept["skv"][:, 0:128], kept["skv"][:, 128:256])
        for ref, val in zip(cache_refs, new):
            ref[...] = jnp.where(is_ctx, val, ref[...])

    return [na_q, na_kv, diff_q, diff_k, diff_v_fourier, swa_q, swa_kv], finish


def _trunk_kernel(do_mlp, do_proj, last, n_alias, *refs):
    xp_ref, xs_ref = refs[:2]
    pos = 2
    if do_mlp:
        catp_ref, cats_ref, mm_ref, g2_ref, fg_ref, wo_ref, w1_ref, w2_ref = refs[pos:pos + 8]
        pos += 8
    if do_proj:
        pm_ref, g1_ref, win_ref = refs[pos:pos + 3]
        rope_refs = refs[pos + 3:pos + 7]
        pos += 7 + n_alias
    if do_mlp:
        op_ref, os_ref = refs[pos:pos + 2]
        pos += 2
    if do_proj:
        zb_ref = refs[pos]
        cache_refs = refs[pos + 1:pos + 7]
        pos += 7
    if do_mlp:
        wob, w1b, w2b = refs[pos:pos + 3]
        pos += 3
    if do_proj:
        wb = refs[pos]
        pos += 1
    skew = do_proj
    if skew:
        hprev = refs[pos]
    s = pl.program_id(0)

    @pl.when(s < TRUNK_CAST_STEPS)
    def _():
        def cast(dst, src):
            rows = src.shape[0]
            dst[pl.ds(pl.multiple_of(s * rows, rows), rows), :] = src[...].astype(BF)

        if do_mlp:
            cast(wob, wo_ref)
            cast(w1b, w1_ref)
            cast(w2b, w2_ref)
        if do_proj:
            cast(wb, win_ref)
        if skew:
            @pl.when(s == 0)
            def _():
                hprev[...] = jnp.zeros_like(hprev)

    def tile_step(t, proj_stages, proj_finish):
        pending = list(proj_stages)

        def issue(n):
            for _ in range(min(n, len(pending))):
                pending.pop(0)()

        is_ctx = t < N_CTX_TILES
        row = _mod_row(t)
        x = jnp.where(is_ctx, xp_ref[...], xs_ref[...])
        if do_mlp:
            issue(1)
            gt1 = mm_ref[pl.ds(row, 1), 2 * D_MODEL:3 * D_MODEL]
            sh2 = mm_ref[pl.ds(row, 1), 3 * D_MODEL:4 * D_MODEL]
            sc2 = mm_ref[pl.ds(row, 1), 4 * D_MODEL:5 * D_MODEL]
            gt2 = mm_ref[pl.ds(row, 1), 5 * D_MODEL:6 * D_MODEL]
            cat = jnp.where(is_ctx, catp_ref[...], cats_ref[...])
            x = x + gt1 * _dot(cat, wob[...])
            issue(1)
            hh = (_rmsnorm(x, g2_ref[...]) * (1.0 + sc2) + sh2).astype(BF)
            acc = jnp.zeros((ROW_TILE, D_MODEL), F32)
            for c in range(D_FF // FF_CHUNK):
                a = jnp.maximum(_dot(hh, w1b[:, c * FF_CHUNK:(c + 1) * FF_CHUNK]), 0.0)
                acc = acc + _dot((a * a).astype(BF), w2b[c * FF_CHUNK:(c + 1) * FF_CHUNK, :])
                if c < 2:
                    issue(1)
            issue(1)
            x = x + gt2 * acc
            out = _rmsnorm(x, fg_ref[...]) if last else x
        hb = None
        if do_proj:
            issue(1)
            shift = pm_ref[pl.ds(row, 1), 0:D_MODEL]
            scale = pm_ref[pl.ds(row, 1), D_MODEL:2 * D_MODEL]
            hb = (_rmsnorm(x, g1_ref[...]) * (1.0 + scale) + shift).astype(BF)
        issue(len(pending))
        if proj_finish is not None:
            proj_finish()
        if do_mlp:
            op_ref[...] = jnp.where(is_ctx, out, op_ref[...])
            os_ref[...] = jnp.where(is_ctx, os_ref[...], out)
        return hb

    def projection(hb, t):
        return _projection_stages(hb, wb, t, t < N_CTX_TILES, zb_ref, cache_refs, rope_refs)

    @pl.when(s >= TRUNK_CAST_STEPS)
    def _():
        t = s - TRUNK_CAST_STEPS
        if not skew:
            hb = tile_step(t, [], None)
            if do_proj:
                stages, finish = projection(lambda: hb, t)
                for stage in stages:
                    stage()
                finish()
        else:
            u = jnp.maximum(t - 1, 0)

            @pl.when(t < N_ROW_TILES)
            def _():
                stages, finish = projection(lambda: hprev[...], u)
                hprev[...] = tile_step(t, stages, finish)

            @pl.when(t >= N_ROW_TILES)
            def _():
                stages, finish = projection(lambda: hprev[...], u)
                for stage in stages:
                    stage()
                finish()


def _trunk(xp, xs, ropes, mlp=None, proj=None):
    do_mlp, do_proj = mlp is not None, proj is not None
    last = do_mlp and not do_proj
    skew = do_proj
    clamp_tile = lambda s: jnp.clip(s - TRUNK_CAST_STEPS, 0, N_ROW_TILES - 1)
    step = lambda f: (lambda s: f(clamp_tile(s)))
    step_proj = (lambda f: (lambda s: f(clamp_tile(s - 1)))) if skew else step
    const = lambda shape: pl.BlockSpec(shape, lambda s: (0,) * len(shape), pipeline_mode=pl.Buffered(1))
    chunk = lambda layer, rows, cols: pl.BlockSpec(
        (None, rows // TRUNK_CAST_STEPS, cols), lambda s: (layer, jnp.minimum(s, TRUNK_CAST_STEPS - 1), 0))
    mod_spec = const((MOD_ROWS, 6 * D_MODEL))
    x_specs = [pl.BlockSpec((ROW_TILE, D_MODEL), step(lambda t: (_ctx_tile(t), 0))),
               pl.BlockSpec((ROW_TILE, D_MODEL), step(lambda t: (_lat_tile(t), 0)))]
    args, in_specs = [xp, xs], list(x_specs)
    out_specs, out_shape, scratch, aliases = [], [], [], {}
    n_alias = 0
    if do_mlp:
        layer, mods_l, cat_p, cat_s, g2, final_g, wo, w1, w2 = mlp
        args += [cat_p, cat_s, mods_l, g2.reshape(1, D_MODEL), final_g.reshape(1, D_MODEL), wo, w1, w2]
        in_specs += [pl.BlockSpec((ROW_TILE, MIX_WIDTH), step(lambda t: (_ctx_tile(t), 0))),
                     pl.BlockSpec((ROW_TILE, MIX_WIDTH), step(lambda t: (_lat_tile(t), 0))), mod_spec,
                     const((1, D_MODEL)), const((1, D_MODEL)),
                     chunk(layer, MIX_WIDTH, D_MODEL), chunk(layer, D_MODEL, D_FF), chunk(layer, D_FF, D_MODEL)]
    if do_proj:
        p_layer, mods_p, g1, w_in, caches = proj
        args += [mods_p, g1.reshape(1, D_MODEL), w_in, *ropes]
        in_specs += [mod_spec, const((1, D_MODEL)), chunk(p_layer, D_MODEL, IN_WIDTH)]
        in_specs += [const((DEC_SEQ + ROW_TILE, 128))] * 4
        if caches is not None:
            n_alias = 6
            n_out_before = 2 if do_mlp else 0
            aliases = {len(args) + i: n_out_before + 1 + i for i in range(6)}
            args += list(caches)
            in_specs += [pl.BlockSpec(memory_space=pl.ANY)] * 6
    if do_mlp:
        out_specs += x_specs
        out_shape += [jax.ShapeDtypeStruct((N_CTX_TOK, D_MODEL), F32),
                      jax.ShapeDtypeStruct((N_LAT_TOK, D_MODEL), F32)]
        scratch += [pltpu.VMEM((MIX_WIDTH, D_MODEL), BF), pltpu.VMEM((D_MODEL, D_FF), BF),
                    pltpu.VMEM((D_FF, D_MODEL), BF)]
    if do_proj:
        out_specs += [pl.BlockSpec((ROW_TILE, ZB_WIDTH), step_proj(lambda t: (t, 0)))] + [
            pl.BlockSpec((None, None, SEQ, width), step_proj(lambda t: (_ctx_tile(t), p_layer, 0, 0)))
            for _, width in CACHE_COLS]
        out_shape += [jax.ShapeDtypeStruct((N_TOK, ZB_WIDTH), BF)] + [
            jax.ShapeDtypeStruct((BATCH, DEPTH, SEQ, width), F32) for _, width in CACHE_COLS]
        scratch += [pltpu.VMEM((D_MODEL, IN_WIDTH), BF)]
    if skew:
        scratch += [pltpu.VMEM((ROW_TILE, D_MODEL), BF)]
    outs = pl.pallas_call(
        functools.partial(_trunk_kernel, do_mlp, do_proj, last, n_alias),
        grid=(TRUNK_CAST_STEPS + N_ROW_TILES + (1 if skew else 0),),
        in_specs=in_specs,
        out_specs=out_specs,
        out_shape=out_shape,
        scratch_shapes=scratch,
        input_output_aliases=aliases,
        compiler_params=pltpu.CompilerParams(dimension_semantics=("arbitrary",),
                                             vmem_limit_bytes=TRUNK_VMEM_LIMIT),
        name="trunk_mlp_proj" if (do_mlp and do_proj) else ("trunk_mlp" if do_mlp else "trunk_proj"),
    )(*args)
    outs = list(outs)
    if do_mlp:
        xp, xs = outs[:2]
        outs = outs[2:]
    zb, new_caches = (outs[0], tuple(outs[1:7])) if do_proj else (None, None)
    return xp, xs, zb, new_caches


def _diff_lambda(lamp_ref, lam_init):
    a = jnp.sum(lamp_ref[0:1, :] * lamp_ref[1:2, :], axis=-1, keepdims=True)
    b = jnp.sum(lamp_ref[2:3, :] * lamp_ref[3:4, :], axis=-1, keepdims=True)
    return jnp.exp(a) - jnp.exp(b) + lam_init


def _lane_lo(n):
    return lax.broadcasted_iota(jnp.int32, (n, 128), 1) < HEAD_DIM


def _lane_lo_wide(n, width):
    return lax.broadcasted_iota(jnp.int32, (n, width), 1) % 128 < HEAD_DIM


def _split_pair(x, lo):
    zero = jnp.zeros_like(x)
    return jnp.where(lo, x, zero), jnp.where(lo, zero, x)


def _v_ones_pair(v, lo):
    one = jnp.ones_like(v)
    return jnp.where(lo, v, one), jnp.where(lo, one, v)


def _v_stack(v, lo):
    zero = jnp.zeros_like(v)
    ones_a = jnp.where(lo, 1.0, 0.0).astype(v.dtype)
    ones_b = jnp.where(lo, 0.0, 1.0).astype(v.dtype)
    top = jnp.concatenate([jnp.where(lo, v, zero), ones_a], axis=1)
    bottom = jnp.concatenate([jnp.where(lo, zero, v), ones_b], axis=1)
    return jnp.concatenate([top, bottom], axis=0)


def _merge_pair(a, b, lo):
    return jnp.where(lo, a, b), pltpu.roll(jnp.where(lo, b, a), HEAD_DIM, 1)


def _subln_pair(o, g2, lam_init, lo):
    sq = o * o
    ms_a = jnp.sum(jnp.where(lo, sq, 0.0), axis=-1, keepdims=True)
    ms_b = jnp.sum(jnp.where(lo, 0.0, sq), axis=-1, keepdims=True)
    ms = jnp.where(lo, ms_a, ms_b) * (1.0 / HEAD_DIM)
    return o * lax.rsqrt(ms + NORM_EPS) * g2 * (1.0 - lam_init)


def _diff_quarters(q, n):
    quarter = lax.broadcasted_iota(jnp.int32, (n, 128), 1) // DIFF_QK_DIM
    zero = jnp.zeros_like(q)
    return [jnp.where(quarter == i, q, zero) for i in range(4)]


def _attention_pipeline(units, score_phase, softmax_phase, value_phase, on_result=None):
    res, scores, weights = [], {}, {}
    n = len(units)
    for i in range(n + 2):
        if i < n:
            scores[i] = score_phase(units[i])
        if 0 <= i - 1 < n:
            weights[i - 1] = softmax_phase(units[i - 1], scores.pop(i - 1))
        if 0 <= i - 2 < n:
            res.append(value_phase(units[i - 2], weights.pop(i - 2)))
            if on_result is not None:
                on_result(res)
    return res


def _fourier(x_bf, cl, sl, cbd, sbd, wf):
    xc = _dot(x_bf, cbd).astype(BF)
    xs = _dot(x_bf, sbd).astype(BF)
    y = _dot(cl, xc) - _dot(sl, xs)
    return _dot(y.astype(BF), wf)


CTX_REQ_PER_STEP = 2


def _ctx_mix_kernel(lam_init, with_ada, sink_ref, z_ref, lamp_ref, subg_ref, cl_ref, sl_ref, cbd_ref, sbd_ref,
                    wf_ref, *rest):
    if with_ada:
        cond_ref, wa_ref, ba_ref, o_ref, mods_ref = rest
        _ada_kernel(cond_ref, wa_ref, ba_ref, mods_ref)
    else:
        o_ref, = rest
    lo = _lane_lo(SEQ)
    z_refs = [z_ref.at[pl.ds(r * SEQ, SEQ)] for r in range(CTX_REQ_PER_STEP)]
    o_refs = [o_ref.at[pl.ds(r * SEQ, SEQ)] for r in range(CTX_REQ_PER_STEP)]

    pairs = []
    for zr in z_refs:
        for j in range(2):
            c = 128 * j
            qs = _split_pair(zr[:, C_NA_Q + c:C_NA_Q + c + 128], lo)
            pairs.append((zr, qs[0], qs[1], C_NA_K + c, _v_stack(zr[:, C_NA_V + c:C_NA_V + c + 128], lo), None))
        for j in range(2):
            c = 128 * j
            q4 = _diff_quarters(zr[:, C_DQ + c:C_DQ + c + 128], SEQ)
            v2 = _v_stack(zr[:, C_DV + c:C_DV + c + 128], lo)
            pairs.append((zr, q4[0], q4[2], C_DK + c, v2, None))
            pairs.append((zr, q4[1], q4[3], C_DK + c, v2, None))
        for g in range(2):
            c = 128 * g
            qs = _split_pair(zr[:, C_SQ + c:C_SQ + c + 128], lo)
            pairs.append((zr, qs[0], qs[1], ZB_SK + c, _v_stack(zr[:, ZB_SV + c:ZB_SV + c + 128], lo),
                          (sink_ref[2 * g], sink_ref[2 * g + 1])))

    def score_phase(pair):
        zr, qa, qb, kc, _, _ = pair
        return [_dot_nt(q, zr[:, kc:kc + 128]) for q in (qa, qb)]

    def softmax_phase(pair, scores):
        sinks = pair[5]
        es, xs = [], []
        for i, s in enumerate(scores):
            m = _rowmax(s)
            if sinks is not None:
                m = jnp.maximum(m, sinks[i])
                xs.append(jnp.exp(sinks[i] - m))
            es.append(jnp.exp(s - m).astype(BF))
        return jnp.concatenate(es, axis=1), xs

    def value_phase(pair, weights):
        e2, xs = weights
        r = _dot(e2, pair[4])
        num, den = r[:, 0:128], r[:, 128:256]
        if xs:
            den = den + jnp.where(lo, xs[0], xs[1])
        return num / den

    lam = _diff_lambda(lamp_ref, lam_init)

    def write_request(zr, out, rr):
        for j in range(2):
            out[:, 128 * j:128 * j + 128] = rr[j].astype(BF)
        for j in range(2):
            o = rr[2 + 2 * j] - lam * rr[3 + 2 * j]
            out[:, 256 + 128 * j:384 + 128 * j] = _subln_pair(o, subg_ref[...], lam_init, lo).astype(BF)
        o_c = _fourier(zr[:, C_FC:C_FC + 256], cl_ref[...], sl_ref[...], cbd_ref[...], sbd_ref[...], wf_ref[...])
        out[:, 512:768] = o_c.astype(BF)
        for g in range(2):
            out[:, 768 + 128 * g:896 + 128 * g] = rr[6 + g].astype(BF)

    def on_result(res):
        if len(res) % 8 == 0:
            r = len(res) // 8 - 1
            write_request(z_refs[r], o_refs[r], res[8 * r:8 * r + 8])

    _attention_pipeline(pairs, score_phase, softmax_phase, value_phase, on_result)


def _ctx_mix(zb, sink, lamp, subg, cl, sl, cbd, sbd, wf, lam_init, ada=None):
    full = lambda shape: pl.BlockSpec(shape, lambda b, s: (0,) * len(shape))
    n_steps = BATCH // CTX_REQ_PER_STEP
    in_specs = [
        pl.BlockSpec((CTX_REQ_PER_STEP * SEQ, ZB_WIDTH), lambda b, s: (b, 0)),
        full((4, DIFF_QK_DIM)),
        full((1, 2 * HEAD_DIM)),
        full((SEQ, SEQ)), full((SEQ, SEQ)),
        full((256, 256)), full((256, 256)), full((256, 256)),
    ]
    out_specs = [pl.BlockSpec((CTX_REQ_PER_STEP * SEQ, MIX_WIDTH), lambda b, s: (b, 0))]
    out_shape = [jax.ShapeDtypeStruct((N_CTX_TOK, MIX_WIDTH), BF)]
    args = [sink, zb, lamp, subg, cl, sl, cbd, sbd, wf]
    if ada is not None:
        cond, w_ada, b_ada3, layer = ada
        ada_in, ada_out = _ada_specs(layer, n_steps, lambda b, s: b)
        in_specs += ada_in
        out_specs.append(ada_out)
        out_shape.append(jax.ShapeDtypeStruct((MOD_ROWS, 6 * D_MODEL), F32))
        args += [cond, w_ada, b_ada3]
    grid_spec = pltpu.PrefetchScalarGridSpec(
        num_scalar_prefetch=1, grid=(n_steps,), in_specs=in_specs, out_specs=out_specs)
    outs = pl.pallas_call(
        functools.partial(_ctx_mix_kernel, lam_init, ada is not None),
        grid_spec=grid_spec,
        out_shape=out_shape,
        compiler_params=_params(),
        name="ctx_mix",
    )(*args)
    return outs[0], (outs[1] if ada is not None else None)


def _cache_v_ones(cv_ref, cva, cvb):
    first = lax.broadcasted_iota(jnp.int32, (256, PAST_LEN), 0) % 128 < HEAD_DIM
    cv = cv_ref[...]
    cva[...] = jnp.where(first, cv, 1.0).astype(BF)
    cvb[...] = jnp.where(first, 1.0, cv).astype(BF)


def _na_bias_tiles(rpp_ref, t2):
    c = lax.broadcasted_iota(jnp.int32, (GRID_W, 128), 0)
    kc = lax.broadcasted_iota(jnp.int32, (GRID_W, 128), 1) % GRID_W
    c_start = jnp.clip(c - NA_KW // 2, 0, GRID_W - NA_KW)
    inside = jnp.logical_and(kc >= c_start, kc < c_start + NA_KW)
    for h in range(4):
        for dr in range(2 * NA_KH - 1):
            row = jnp.broadcast_to(rpp_ref[h, dr:dr + 1, :], (GRID_W, 128))
            toeplitz = pltpu.roll(row, 128 - (NA_KW - 1), 1, stride=1, stride_axis=0)
            t2[h, dr] = jnp.where(inside, toeplitz, NEG_INF)
        t2[h, 2 * NA_KH - 1] = jnp.full((GRID_W, 128), NEG_INF, F32)


def _lat_na_part(z_ref, ck_ref, cv_ref, t2, o_ref, ckb, cva, cvb):
    ckb[...] = ck_ref[...].astype(BF)
    _cache_v_ones(cv_ref, cva, cvb)
    n_q = NA_QROWS * GRID_W
    n_loc = NA_WIN_ROWS * GRID_W
    lo_q = _lane_lo(n_q)
    lo_w = _lane_lo(n_loc)

    lo_t = _lane_lo(GRID_W)

    def block(p):
        w_row = jnp.clip(NA_QROWS * p - NA_KH // 2, 0, GRID_ROWS - NA_WIN_ROWS)
        q0 = pl.multiple_of(p * n_q, n_q)
        k0 = pl.multiple_of(w_row * GRID_W, NA_QROWS * GRID_W)

        def tile_index(r, j):
            r_start = jnp.clip(r - NA_KH // 2, 0, GRID_ROWS - NA_KH)
            kr = w_row + j
            inside = jnp.logical_and(kr >= r_start, kr < r_start + NA_KH)
            return jnp.where(inside, kr - r + NA_KH - 1, 2 * NA_KH - 1)

        def bias(h):
            rows = []
            for rl in range(NA_QROWS):
                r = NA_QROWS * p + rl
                pieces = [jnp.where(lo_t, t2[h, tile_index(r, 2 * jj)], t2[h, tile_index(r, 2 * jj + 1)])
                          for jj in range(NA_WIN_ROWS // 2)]
                rows.append(jnp.concatenate(pieces, axis=1))
            return jnp.concatenate(rows, axis=0)

        return q0, k0, bias

    def score_phase(unit):
        q0, k0, bias, h = unit
        c = 128 * (h // 2)
        q = _split_pair(z_ref[pl.ds(q0, n_q), C_NA_Q + c:C_NA_Q + c + 128], lo_q)[h % 2]
        s_loc = _dot_nt(q, z_ref[pl.ds(k0, n_loc), C_NA_K + c:C_NA_K + c + 128]) + bias(h)
        return s_loc, _dot(q, ckb[c:c + 128, :])

    def softmax_phase(unit, scores):
        s_loc, s_ctx = scores
        m = _rowmax(s_loc, s_ctx)
        return jnp.exp(s_loc - m).astype(BF), jnp.exp(s_ctx - m).astype(BF)

    def value_phase(unit, weights):
        q0, k0, bias, h = unit
        e_loc, e_ctx = weights
        c = 128 * (h // 2)
        v = _v_ones_pair(z_ref[pl.ds(k0, n_loc), C_NA_V + c:C_NA_V + c + 128], lo_w)[h % 2]
        return _dot(e_loc, v) + _dot_nt(e_ctx, (cva, cvb)[h % 2][c:c + 128, :])

    def body(i, carry):
        blocks = [block(NA_BLOCKS_PER_TRIP * i + b) for b in range(NA_BLOCKS_PER_TRIP)]
        units = [(q0, k0, bias, h) for q0, k0, bias in blocks for h in range(4)]
        res = _attention_pipeline(units, score_phase, softmax_phase, value_phase)
        for b, (q0, _, _) in enumerate(blocks):
            for j in range(2):
                num, den = _merge_pair(res[4 * b + 2 * j], res[4 * b + 2 * j + 1], lo_q)
                o_ref[pl.ds(q0, n_q), 128 * j:128 * j + 128] = (num / den).astype(BF)
        return carry

    lax.fori_loop(0, GRID_ROWS // NA_QROWS // NA_BLOCKS_PER_TRIP, body, 0)


DIFF_QBLK = 1024
DIFF_GROUP = 2


def _lat_diff_part(lam_init, z_ref, ck_ref, cv_ref, lamp_ref, subg_ref, o_ref, ckb, cva, cvb, vla, vlb):
    ckb[...] = ck_ref[...].astype(BF)
    _cache_v_ones(cv_ref, cva, cvb)
    lo_v = _lane_lo_wide(DEC_SEQ, 256)
    v_loc = z_ref[:, C_DV:C_DV + 256]
    one = jnp.ones_like(v_loc)
    vla[...] = jnp.where(lo_v, v_loc, one)
    vlb[...] = jnp.where(lo_v, one, v_loc)
    lam = _diff_lambda(lamp_ref, lam_init)
    qblk = DIFF_QBLK
    lo_q = _lane_lo(qblk)

    def body(i, carry):
        q0 = pl.multiple_of(i * qblk, qblk)
        for j in range(2):
            c = 128 * j
            q4 = _diff_quarters(z_ref[pl.ds(q0, qblk), C_DQ + c:C_DQ + c + 128], qblk)
            res = []
            for t0 in range(0, 4, DIFF_GROUP):
                scores = [(_dot_nt(q, z_ref[:, C_DK + c:C_DK + c + 128]), _dot(q, ckb[c:c + 128, :]))
                          for q in q4[t0:t0 + DIFF_GROUP]]
                weights = []
                for s_loc, s_ctx in scores:
                    m = _rowmax(s_loc, s_ctx)
                    weights.append((jnp.exp(s_loc - m).astype(BF), jnp.exp(s_ctx - m).astype(BF)))
                for t, (e_loc, e_ctx) in enumerate(weights, start=t0):
                    v_loc, v_ctx = ((vla, cva), (vlb, cvb))[t // 2]
                    res.append(_dot(e_loc, v_loc[:, c:c + 128]) + _dot_nt(e_ctx, v_ctx[c:c + 128, :]))
            n1, d1 = _merge_pair(res[0], res[2], lo_q)
            n2, d2 = _merge_pair(res[1], res[3], lo_q)
            o = n1 / d1 - lam * (n2 / d2)
            o_ref[pl.ds(q0, qblk), 256 + c:256 + c + 128] = (
                _subln_pair(o, subg_ref[...], lam_init, lo_q).astype(BF))
        return carry

    lax.fori_loop(0, DEC_SEQ // qblk, body, 0)


SWA_BLOCKS_PER_TRIP = 4


def _lat_swa_part(sink_ref, z_ref, ck_ref, cv_ref, o_ref, ckd, cva, cvb):
    W = SWA_WINDOW
    n_win = 3 * W
    ones = jnp.ones((HEAD_DIM, PAST_LEN), BF)
    for g in range(2):
        k_g = ck_ref[HEAD_DIM * g:HEAD_DIM * (g + 1), :].astype(BF)
        v_g = cv_ref[HEAD_DIM * g:HEAD_DIM * (g + 1), :].astype(BF)
        r0, r1, r2 = 128 * g, 128 * g + HEAD_DIM, 128 * (g + 1)
        ckd[r0:r1, :] = k_g
        ckd[r1:r2, :] = k_g
        cva[r0:r1, :] = v_g
        cva[r1:r2, :] = ones
        cvb[r0:r1, :] = ones
        cvb[r1:r2, :] = v_g
    lo_q = _lane_lo(W)
    lo_w = _lane_lo(n_win)

    def block(n):
        q0 = pl.multiple_of(n * W, W)
        w0 = pl.multiple_of(jnp.clip((n - 1) * W, 0, DEC_SEQ - n_win), W)
        qpos = q0 + lax.broadcasted_iota(jnp.int32, (W, n_win), 0)
        kpos = w0 + lax.broadcasted_iota(jnp.int32, (W, n_win), 1)
        return q0, w0, jnp.abs(qpos - kpos) <= W

    def score_phase(unit):
        q0, w0, valid, h = unit
        c = 128 * (h // 2)
        q = _split_pair(z_ref[pl.ds(q0, W), C_SQ + c:C_SQ + c + 128], lo_q)[h % 2]
        s_loc = jnp.where(valid, _dot_nt(q, z_ref[pl.ds(w0, n_win), ZB_SK + c:ZB_SK + c + 128]), NEG_INF)
        return s_loc, _dot(q, ckd[c:c + 128, :])

    def softmax_phase(unit, scores):
        s_loc, s_ctx = scores
        sink = sink_ref[unit[3]]
        m = jnp.maximum(_rowmax(s_loc, s_ctx), sink)
        return jnp.exp(s_loc - m).astype(BF), jnp.exp(s_ctx - m).astype(BF), jnp.exp(sink - m)

    def value_phase(unit, weights):
        q0, w0, valid, h = unit
        e_loc, e_ctx, extra = weights
        c = 128 * (h // 2)
        v = _v_ones_pair(z_ref[pl.ds(w0, n_win), ZB_SV + c:ZB_SV + c + 128], lo_w)[h % 2]
        return _dot(e_loc, v) + _dot_nt(e_ctx, (cva, cvb)[h % 2][c:c + 128, :]), extra

    def body(i, carry):
        blocks = [block(SWA_BLOCKS_PER_TRIP * i + b) for b in range(SWA_BLOCKS_PER_TRIP)]
        units = [(q0, w0, valid, h) for q0, w0, valid in blocks for h in range(4)]
        res = _attention_pipeline(units, score_phase, softmax_phase, value_phase)
        for b, (q0, _, _) in enumerate(blocks):
            for g in range(2):
                (r_a, x_a), (r_b, x_b) = res[4 * b + 2 * g], res[4 * b + 2 * g + 1]
                num, den = _merge_pair(r_a, r_b, lo_q)
                den = den + jnp.where(lo_q, x_a, x_b)
                o_ref[pl.ds(q0, W), 768 + 128 * g:768 + 128 * g + 128] = (num / den).astype(BF)
        return carry

    lax.fori_loop(0, DEC_SEQ // W // SWA_BLOCKS_PER_TRIP, body, 0)


def _lat_mix_kernel(lam_init, sink_ref, z_ref, nak_ref, nav_ref, dfk_ref, dfv_ref, swk_ref, swv_ref, rpp_ref,
                    lamp_ref, subg_ref, cl_ref, sl_ref, cbd_ref, sbd_ref, wf_ref, o_ref,
                    ck, cva, cvb, vla, vlb, t2):
    @pl.when(pl.program_id(0) == 0)
    def _():
        _na_bias_tiles(rpp_ref, t2)

    _lat_na_part(z_ref, nak_ref, nav_ref, t2, o_ref, ck, cva, cvb)
    _lat_diff_part(lam_init, z_ref, dfk_ref, dfv_ref, lamp_ref, subg_ref, o_ref, ck, cva, cvb, vla, vlb)
    o_c = _fourier(z_ref[:, C_FC:C_FC + 256], cl_ref[...], sl_ref[...], cbd_ref[...], sbd_ref[...], wf_ref[...])
    o_ref[:, 512:768] = o_c.astype(BF)
    _lat_swa_part(sink_ref, z_ref, swk_ref, swv_ref, o_ref, ck, cva, cvb)


def _lat_mix(zb, caches_in, rpp, sink, lamp, subg, cl, sl, cbd, sbd, wf, lam_init, layer):
    full = lambda shape: pl.BlockSpec(shape, lambda b, s: (0,) * len(shape), pipeline_mode=pl.Buffered(1))
    cache = lambda width: pl.BlockSpec((None, None, width, PAST_LEN), lambda b, s: (b, layer, 0, 0))
    grid_spec = pltpu.PrefetchScalarGridSpec(
        num_scalar_prefetch=1,
        grid=(DEC_BATCH,),
        in_specs=[
            pl.BlockSpec((DEC_SEQ, ZB_WIDTH), lambda b, s: (LAT_BLOCK0 + b, 0)),
            cache(256), cache(256), cache(256), cache(256), cache(128), cache(128),
            pl.BlockSpec((None, 4, 2 * NA_KH, 128), lambda b, s: (layer, 0, 0, 0), pipeline_mode=pl.Buffered(1)),
            full((4, DIFF_QK_DIM)), full((1, 2 * HEAD_DIM)),
            full((DEC_SEQ, DEC_SEQ)), full((DEC_SEQ, DEC_SEQ)),
            full((256, 256)), full((256, 256)), full((256, 256)),
        ],
        out_specs=pl.BlockSpec((DEC_SEQ, MIX_WIDTH), lambda b, s: (b, 0)),
        scratch_shapes=[pltpu.VMEM((256, PAST_LEN), BF)] * 3 + [pltpu.VMEM((DEC_SEQ, 256), BF)] * 2
        + [pltpu.VMEM((4, 2 * NA_KH, GRID_W, 128), F32)],
    )
    return pl.pallas_call(
        functools.partial(_lat_mix_kernel, lam_init),
        grid_spec=grid_spec,
        out_shape=jax.ShapeDtypeStruct((N_LAT_TOK, MIX_WIDTH), BF),
        compiler_params=_params(),
        name="lat_mix",
    )(sink, zb, *caches_in, rpp, lamp, subg, cl, sl, cbd, sbd, wf)


def _dft_tables(n):
    j = np.arange(n)
    ang = 2.0 * np.pi * ((j[:, None] * j[None, :]) % n) / n
    return np.cos(ang) / np.sqrt(n), np.sin(ang) / np.sqrt(n)


def _block_diag4(m):
    out = np.zeros((256, 256), m.dtype)
    for g in range(4):
        out[64 * g:64 * g + 64, 64 * g:64 * g + 64] = m
    return out


def _rope_tables(n_axis_dims):
    half = n_axis_dims // 2
    inv = ROPE_BASE ** (-np.arange(half, dtype=np.float64) / half)
    t = np.arange(DEC_SEQ)
    lane = np.arange(128)
    w = lane % n_axis_dims
    is_col = (lane // n_axis_dims) % 2 == 1
    pos = np.where(is_col[None, :], (t % GRID_W)[:, None], (t // GRID_W)[:, None]).astype(np.float64)
    ang = pos * inv[w % half][None, :]
    sign = np.where(w < half, -1.0, 1.0)[None, :]
    cos = np.concatenate([np.cos(ang), np.ones((ROW_TILE, 128))], axis=0)
    sin = np.concatenate([np.sin(ang) * sign, np.zeros((ROW_TILE, 128))], axis=0)
    return jnp.asarray(cos, F32), jnp.asarray(sin, F32)


def _pad_rpb_rows(rpb):
    n_dc = rpb.shape[-1]
    half = jnp.pad(rpb, ((0, 0), (0, 0), (0, 1), (0, GRID_W - n_dc)))
    return jnp.concatenate([half, half], axis=-1)


def kernel(x_prompt, x_sample, cache_na_k, cache_na_v, cache_diff_k, cache_diff_v, cache_swa_k, cache_swa_v, c, c_ctx, w_ada, b_ada, norm1_g, norm2_g, w_in, na_rpb, diff_lq1, diff_lk1, diff_lq2, diff_lk2, diff_subln_g, w_fourier, swa_sink, w_out, w_mlp1, w_mlp2, final_g):
    xp = x_prompt.reshape(N_CTX_TOK, D_MODEL)
    xs = x_sample.reshape(N_LAT_TOK, D_MODEL)
    cond = jnp.zeros((MOD_ROWS, D_MODEL), F32).at[0].set(c_ctx).at[1:1 + DEC_BATCH].set(c)
    b_ada3 = b_ada.reshape(DEPTH, 1, 6 * D_MODEL)
    mods = _ada(cond, w_ada, b_ada3, 0)

    cl_p, sl_p = _dft_tables(SEQ)
    cl_s, sl_s = _dft_tables(DEC_SEQ)
    c64, s64 = _dft_tables(64)
    cl_p, sl_p, cl_s, sl_s, cbd, sbd = (
        jnp.asarray(a, F32).astype(BF) for a in (cl_p, sl_p, cl_s, sl_s, _block_diag4(c64), _block_diag4(s64)))
    ropes = _rope_tables(16) + _rope_tables(32)
    rpp = _pad_rpb_rows(na_rpb)
    wf_bf = w_fourier.astype(BF)

    ck_na, cv_na, ck_df, cv_df, ck_sw, cv_sw = (
        a.transpose(0, 1, 3, 4, 2).reshape(DEC_BATCH, DEPTH, -1, PAST_LEN)
        for a in (cache_na_k, cache_na_v, cache_diff_k, cache_diff_v, cache_swa_k, cache_swa_v))

    _, _, zb, caches = _trunk(xp, xs, ropes, proj=(0, mods, norm1_g[0], w_in, None))
    for l in range(DEPTH):
        lam_init = 0.8 - 0.6 * math.exp(-0.3 * l)
        lamp = jnp.stack([diff_lq1[l], diff_lk1[l], diff_lq2[l], diff_lk2[l]], axis=0)
        subg = jnp.tile(diff_subln_g[l].reshape(1, HEAD_DIM), (1, 2))
        has_next = l + 1 < DEPTH

        cat_p, mods_next = _ctx_mix(zb, swa_sink[l], lamp, subg, cl_p, sl_p, cbd, sbd, wf_bf[l], lam_init,
                                    ada=(cond, w_ada, b_ada3, l + 1) if has_next else None)
        cat_s = _lat_mix(zb, (ck_na, cv_na, ck_df, cv_df, ck_sw, cv_sw), rpp, swa_sink[l], lamp, subg,
                         cl_s, sl_s, cbd, sbd, wf_bf[l], lam_init, l)
        nxt = (l + 1, mods_next, norm1_g[l + 1], w_in, caches) if has_next else None
        xp, xs, zb, new_caches = _trunk(
            xp, xs, ropes, mlp=(l, mods, cat_p, cat_s, norm2_g[l], final_g, w_out, w_mlp1, w_mlp2), proj=nxt)
        caches = new_caches if has_next else caches
        mods = mods_next

    y_prompt = xp.reshape(BATCH, SEQ, D_MODEL)
    y_sample = xs.reshape(DEC_BATCH, DEC_SEQ, D_MODEL)
    new = [a.reshape(BATCH, DEPTH, SEQ, a.shape[-1] // HEAD_DIM, HEAD_DIM) for a in caches]
    return (y_prompt, y_sample) + tuple(new)
```

```python
import functools
import math

import numpy as np
import jax
import jax.numpy as jnp
from jax import lax
from jax.experimental import pallas as pl
from jax.experimental.pallas import tpu as pltpu

D_MODEL = 1024
BATCH = 16
SEQ = 256
DEPTH = 4
DEC_BATCH = 2
DEC_SEQ = 1024
PAST_LEN = 512
GRID_W = 64
GRID_ROWS = DEC_SEQ // GRID_W
HEAD_DIM = 64
NA_KH = 8
NA_KW = 16
DIFF_QK_DIM = 32
SWA_WINDOW = 128
D_FF = 4 * D_MODEL
ROPE_BASE = 10000.0
NORM_EPS = 1e-6
NEG_INF = -1e30
IN_WIDTH = 2304
MIX_WIDTH = 1024

N_CTX_TOK = BATCH * SEQ
N_LAT_TOK = DEC_BATCH * DEC_SEQ
N_TOK = N_CTX_TOK + N_LAT_TOK
ROW_TILE = 256
N_ROW_TILES = N_TOK // ROW_TILE
N_CTX_TILES = N_CTX_TOK // ROW_TILE
LAT_TILES_PER_REQ = DEC_SEQ // ROW_TILE
LAT_BLOCK0 = N_CTX_TOK // DEC_SEQ
MOD_ROWS = 8

C_NA_Q, C_NA_K, C_NA_V = 0, 256, 512
C_DQ, C_DK, C_DV = 768, 1024, 1280
C_FC = 1536
C_SQ, C_SK, C_SV = 1792, 2048, 2176
CACHE_COLS = ((C_NA_K, 256), (C_NA_V, 256), (C_DK, 256), (C_DV, 256), (C_SK, 128), (C_SV, 128))
ZB_SK, ZB_SV = 2048, 2304
ZB_WIDTH = 2560

NA_QROWS = 2
NA_WIN_ROWS = 10
NA_BLOCKS_PER_TRIP = 4

BF = jnp.bfloat16
F32 = jnp.float32
VMEM_LIMIT = 56 * 1024 * 1024


def _dot(a, b):
    return jnp.dot(a, b, preferred_element_type=F32)


def _dot_nt(a, b):
    return lax.dot_general(a, b, (((1,), (1,)), ((), ())), preferred_element_type=F32)


def _rmsnorm(x, g):
    ms = jnp.mean(x * x, axis=-1, keepdims=True)
    return x * lax.rsqrt(ms + NORM_EPS) * g


def _params(n_grid=1):
    return pltpu.CompilerParams(dimension_semantics=("arbitrary",) * n_grid, vmem_limit_bytes=VMEM_LIMIT)


def _rowmax(*parts):
    m = jnp.max(parts[0], axis=-1, keepdims=True)
    for p in parts[1:]:
        m = jnp.maximum(m, jnp.max(p, axis=-1, keepdims=True))
    return m


def _ada_kernel(cond_ref, w_ref, b_ref, o_ref):
    cnd = cond_ref[...]
    s = cnd / (1.0 + jnp.exp(-cnd))
    o_ref[...] = _dot(s.astype(BF), w_ref[...].astype(BF)) + b_ref[...]


def _ada_specs(layer, n_steps, index_of_step):
    tn = 6 * D_MODEL // n_steps
    col = lambda *g: index_of_step(*g)
    in_specs = [
        pl.BlockSpec((MOD_ROWS, D_MODEL), lambda *g: (0, 0)),
        pl.BlockSpec((None, D_MODEL, tn), lambda *g: (layer, 0, col(*g))),
        pl.BlockSpec((None, 1, tn), lambda *g: (layer, 0, col(*g))),
    ]
    return in_specs, pl.BlockSpec((MOD_ROWS, tn), lambda *g: (0, col(*g)))


def _ada(cond, w_ada, b_ada3, layer):
    in_specs, out_spec = _ada_specs(layer, 6, lambda j: j)
    return pl.pallas_call(
        _ada_kernel,
        grid=(6,),
        in_specs=in_specs,
        out_specs=out_spec,
        out_shape=jax.ShapeDtypeStruct((MOD_ROWS, 6 * D_MODEL), F32),
        compiler_params=_params(),
        name="ada",
    )(cond, w_ada, b_ada3)


def _mod_row(t):
    return jnp.where(t < N_CTX_TILES, 0, 1 + (t - N_CTX_TILES) // LAT_TILES_PER_REQ)


def _ctx_tile(t):
    return jnp.minimum(t, N_CTX_TILES - 1)


def _lat_tile(t):
    return jnp.maximum(t - N_CTX_TILES, 0)


TRUNK_CAST_STEPS = 8
FF_CHUNK = 1024
TRUNK_VMEM_LIMIT = 60 * 1024 * 1024


def _rope(x, cos, sin_signed, shift, first):
    partner = jnp.where(first, pltpu.roll(x, 128 - shift, 1), pltpu.roll(x, shift, 1))
    return x * cos + partner * sin_signed


def _projection_stages(hb, wb, t, is_ctx, zb_ref, cache_refs, rope_refs):
    cosd_ref, sind_ref, coss_ref, sins_ref = rope_refs
    dscale = DIFF_QK_DIM ** -0.5
    p0 = pl.multiple_of(
        jnp.where(is_ctx, DEC_SEQ, ((t - N_CTX_TILES) % LAT_TILES_PER_REQ) * ROW_TILE), ROW_TILE)
    lane = lax.broadcasted_iota(jnp.int32, (ROW_TILE, 128), 1)
    first_d = (lane % 16) < 8
    first_s = (lane % 32) < 16
    lo = lane < HEAD_DIM
    cosd, sind = cosd_ref[pl.ds(p0, ROW_TILE), :], sind_ref[pl.ds(p0, ROW_TILE), :]
    coss, sins = coss_ref[pl.ds(p0, ROW_TILE), :], sins_ref[pl.ds(p0, ROW_TILE), :]

    def cols(c0, width):
        return _dot(hb(), wb[:, c0:c0 + width])

    def put(c0, width, val):
        zb_ref[:, c0:c0 + width] = val.astype(BF)

    def put_dup(c0, val):
        r = pltpu.roll(val, HEAD_DIM, 1)
        put(c0, 128, jnp.where(lo, val, r))
        put(c0 + 128, 128, jnp.where(lo, r, val))

    kept = {}

    def na_q():
        put(C_NA_Q, 256, cols(C_NA_Q, 256) * 0.125)

    def na_kv():
        kept["nakv"] = cols(C_NA_K, 512)
        put(C_NA_K, 512, kept["nakv"])

    def diff_q():
        z = cols(C_DQ, 256)
        for j in range(2):
            put(C_DQ + 128 * j, 128, _rope(z[:, 128 * j:128 * j + 128], cosd, sind, 8, first_d) * dscale)

    def diff_k():
        kept["dk"] = cols(C_DK, 256)
        for j in range(2):
            put(C_DK + 128 * j, 128, _rope(kept["dk"][:, 128 * j:128 * j + 128], cosd, sind, 8, first_d))

    def diff_v_fourier():
        kept["dvfc"] = cols(C_DV, 512)
        put(C_DV, 512, kept["dvfc"])

    def swa_q():
        z = cols(C_SQ, 256)
        for j in range(2):
            put(C_SQ + 128 * j, 128, _rope(z[:, 128 * j:128 * j + 128], coss, sins, 16, first_s) * 0.125)

    def swa_kv():
        kept["skv"] = cols(C_SK, 256)
        put_dup(ZB_SK, _rope(kept["skv"][:, 0:128], coss, sins, 16, first_s))
        put_dup(ZB_SV, kept["skv"][:, 128:256])

    def finish():
        new = (kept["nakv"][:, 0:256], kept["nakv"][:, 256:512], kept["dk"], kept["dvfc"][:, 0:256],
               kept["skv"][:, 0:128], kept["skv"][:, 128:256])
        for ref, val in zip(cache_refs, new):
            ref[...] = jnp.where(is_ctx, val, ref[...])

    return [na_q, na_kv, diff_q, diff_k, diff_v_fourier, swa_q, swa_kv], finish


def _trunk_kernel(do_mlp, do_proj, last, n_alias, *refs):
    xp_ref, xs_ref = refs[:2]
    pos = 2
    if do_mlp:
        catp_ref, cats_ref, mm_ref, g2_ref, fg_ref, wo_ref, w1_ref, w2_ref = refs[pos:pos + 8]
        pos += 8
    if do_proj:
        pm_ref, g1_ref, win_ref = refs[pos:pos + 3]
        rope_refs = refs[pos + 3:pos + 7]
        pos += 7 + n_alias
    if do_mlp:
        op_ref, os_ref = refs[pos:pos + 2]
        pos += 2
    if do_proj:
        zb_ref = refs[pos]
        cache_refs = refs[pos + 1:pos + 7]
        pos += 7
    if do_mlp:
        wob, w1b, w2b = refs[pos:pos + 3]
        pos += 3
    if do_proj:
        wb = refs[pos]
        pos += 1
    skew = do_proj
    if skew:
        hprev = refs[pos]
    s = pl.program_id(0)

    @pl.when(s < TRUNK_CAST_STEPS)
    def _():
        def cast(dst, src):
            rows = src.shape[0]
            dst[pl.ds(pl.multiple_of(s * rows, rows), rows), :] = src[...].astype(BF)

        if do_mlp:
            cast(wob, wo_ref)
            cast(w1b, w1_ref)
            cast(w2b, w2_ref)
        if do_proj:
            cast(wb, win_ref)
        if skew:
            @pl.when(s == 0)
            def _():
                hprev[...] = jnp.zeros_like(hprev)

    def tile_step(t, proj_stages, proj_finish):
        pending = list(proj_stages)

        def issue(n):
            for _ in range(min(n, len(pending))):
                pending.pop(0)()

        is_ctx = t < N_CTX_TILES
        row = _mod_row(t)
        x = jnp.where(is_ctx, xp_ref[...], xs_ref[...])
        if do_mlp:
            issue(1)
            gt1 = mm_ref[pl.ds(row, 1), 2 * D_MODEL:3 * D_MODEL]
            sh2 = mm_ref[pl.ds(row, 1), 3 * D_MODEL:4 * D_MODEL]
            sc2 = mm_ref[pl.ds(row, 1), 4 * D_MODEL:5 * D_MODEL]
            gt2 = mm_ref[pl.ds(row, 1), 5 * D_MODEL:6 * D_MODEL]
            cat = jnp.where(is_ctx, catp_ref[...], cats_ref[...])
            x = x + gt1 * _dot(cat, wob[...])
            issue(1)
            hh = (_rmsnorm(x, g2_ref[...]) * (1.0 + sc2) + sh2).astype(BF)
            acc = jnp.zeros((ROW_TILE, D_MODEL), F32)
            for c in range(D_FF // FF_CHUNK):
                a = jnp.maximum(_dot(hh, w1b[:, c * FF_CHUNK:(c + 1) * FF_CHUNK]), 0.0)
                acc = acc + _dot((a * a).astype(BF), w2b[c * FF_CHUNK:(c + 1) * FF_CHUNK, :])
                if c < 2:
                    issue(1)
            issue(1)
            x = x + gt2 * acc
            out = _rmsnorm(x, fg_ref[...]) if last else x
        hb = None
        if do_proj:
            issue(1)
            shift = pm_ref[pl.ds(row, 1), 0:D_MODEL]
            scale = pm_ref[pl.ds(row, 1), D_MODEL:2 * D_MODEL]
            hb = (_rmsnorm(x, g1_ref[...]) * (1.0 + scale) + shift).astype(BF)
        issue(len(pending))
        if proj_finish is not None:
            proj_finish()
        if do_mlp:
            op_ref[...] = jnp.where(is_ctx, out, op_ref[...])
            os_ref[...] = jnp.where(is_ctx, os_ref[...], out)
        return hb

    def projection(hb, t):
        return _projection_stages(hb, wb, t, t < N_CTX_TILES, zb_ref, cache_refs, rope_refs)

    @pl.when(s >= TRUNK_CAST_STEPS)
    def _():
        t = s - TRUNK_CAST_STEPS
        if not skew:
            hb = tile_step(t, [], None)
            if do_proj:
                stages, finish = projection(lambda: hb, t)
                for stage in stages:
                    stage()
                finish()
        else:
            u = jnp.maximum(t - 1, 0)

            @pl.when(t < N_ROW_TILES)
            def _():
                stages, finish = projection(lambda: hprev[...], u)
                hprev[...] = tile_step(t, stages, finish)

            @pl.when(t >= N_ROW_TILES)
            def _():
                stages, finish = projection(lambda: hprev[...], u)
                for stage in stages:
                    stage()
                finish()


def _trunk(xp, xs, ropes, mlp=None, proj=None):
    do_mlp, do_proj = mlp is not None, proj is not None
    last = do_mlp and not do_proj
    skew = do_proj
    clamp_tile = lambda s: jnp.clip(s - TRUNK_CAST_STEPS, 0, N_ROW_TILES - 1)
    step = lambda f: (lambda s: f(clamp_tile(s)))
    step_proj = (lambda f: (lambda s: f(clamp_tile(s - 1)))) if skew else step
    const = lambda shape: pl.BlockSpec(shape, lambda s: (0,) * len(shape), pipeline_mode=pl.Buffered(1))
    chunk = lambda layer, rows, cols: pl.BlockSpec(
        (None, rows // TRUNK_CAST_STEPS, cols), lambda s: (layer, jnp.minimum(s, TRUNK_CAST_STEPS - 1), 0))
    mod_spec = const((MOD_ROWS, 6 * D_MODEL))
    x_specs = [pl.BlockSpec((ROW_TILE, D_MODEL), step(lambda t: (_ctx_tile(t), 0))),
               pl.BlockSpec((ROW_TILE, D_MODEL), step(lambda t: (_lat_tile(t), 0)))]
    args, in_specs = [xp, xs], list(x_specs)
    out_specs, out_shape, scratch, aliases = [], [], [], {}
    n_alias = 0
    if do_mlp:
        layer, mods_l, cat_p, cat_s, g2, final_g, wo, w1, w2 = mlp
        args += [cat_p, cat_s, mods_l, g2.reshape(1, D_MODEL), final_g.reshape(1, D_MODEL), wo, w1, w2]
        in_specs += [pl.BlockSpec((ROW_TILE, MIX_WIDTH), step(lambda t: (_ctx_tile(t), 0))),
                     pl.BlockSpec((ROW_TILE, MIX_WIDTH), step(lambda t: (_lat_tile(t), 0))), mod_spec,
                     const((1, D_MODEL)), const((1, D_MODEL)),
                     chunk(layer, MIX_WIDTH, D_MODEL), chunk(layer, D_MODEL, D_FF), chunk(layer, D_FF, D_MODEL)]
    if do_proj:
        p_layer, mods_p, g1, w_in, caches = proj
        args += [mods_p, g1.reshape(1, D_MODEL), w_in, *ropes]
        in_specs += [mod_spec, const((1, D_MODEL)), chunk(p_layer, D_MODEL, IN_WIDTH)]
        in_specs += [const((DEC_SEQ + ROW_TILE, 128))] * 4
        if caches is not None:
            n_alias = 6
            n_out_before = 2 if do_mlp else 0
            aliases = {len(args) + i: n_out_before + 1 + i for i in range(6)}
            args += list(caches)
            in_specs += [pl.BlockSpec(memory_space=pl.ANY)] * 6
    if do_mlp:
        out_specs += x_specs
        out_shape += [jax.ShapeDtypeStruct((N_CTX_TOK, D_MODEL), F32),
                      jax.ShapeDtypeStruct((N_LAT_TOK, D_MODEL), F32)]
        scratch += [pltpu.VMEM((MIX_WIDTH, D_MODEL), BF), pltpu.VMEM((D_MODEL, D_FF), BF),
                    pltpu.VMEM((D_FF, D_MODEL), BF)]
    if do_proj:
        out_specs += [pl.BlockSpec((ROW_TILE, ZB_WIDTH), step_proj(lambda t: (t, 0)))] + [
            pl.BlockSpec((None, None, SEQ, width), step_proj(lambda t: (_ctx_tile(t), p_layer, 0, 0)))
            for _, width in CACHE_COLS]
        out_shape += [jax.ShapeDtypeStruct((N_TOK, ZB_WIDTH), BF)] + [
            jax.ShapeDtypeStruct((BATCH, DEPTH, SEQ, width), F32) for _, width in CACHE_COLS]
        scratch += [pltpu.VMEM((D_MODEL, IN_WIDTH), BF)]
    if skew:
        scratch += [pltpu.VMEM((ROW_TILE, D_MODEL), BF)]
    outs = pl.pallas_call(
        functools.partial(_trunk_kernel, do_mlp, do_proj, last, n_alias),
        grid=(TRUNK_CAST_STEPS + N_ROW_TILES + (1 if skew else 0),),
        in_specs=in_specs,
        out_specs=out_specs,
        out_shape=out_shape,
        scratch_shapes=scratch,
        input_output_aliases=aliases,
        compiler_params=pltpu.CompilerParams(dimension_semantics=("arbitrary",),
                                             vmem_limit_bytes=TRUNK_VMEM_LIMIT),
        name="trunk_mlp_proj" if (do_mlp and do_proj) else ("trunk_mlp" if do_mlp else "trunk_proj"),
    )(*args)
    outs = list(outs)
    if do_mlp:
        xp, xs = outs[:2]
        outs = outs[2:]
    zb, new_caches = (outs[0], tuple(outs[1:7])) if do_proj else (None, None)
    return xp, xs, zb, new_caches


def _diff_lambda(lamp_ref, lam_init):
    a = jnp.sum(lamp_ref[0:1, :] * lamp_ref[1:2, :], axis=-1, keepdims=True)
    b = jnp.sum(lamp_ref[2:3, :] * lamp_ref[3:4, :], axis=-1, keepdims=True)
    return jnp.exp(a) - jnp.exp(b) + lam_init


def _lane_lo(n):
    return lax.broadcasted_iota(jnp.int32, (n, 128), 1) < HEAD_DIM


def _lane_lo_wide(n, width):
    return lax.broadcasted_iota(jnp.int32, (n, width), 1) % 128 < HEAD_DIM


def _split_pair(x, lo):
    zero = jnp.zeros_like(x)
    return jnp.where(lo, x, zero), jnp.where(lo, zero, x)


def _v_ones_pair(v, lo):
    one = jnp.ones_like(v)
    return jnp.where(lo, v, one), jnp.where(lo, one, v)


def _v_stack(v, lo):
    zero = jnp.zeros_like(v)
    ones_a = jnp.where(lo, 1.0, 0.0).astype(v.dtype)
    ones_b = jnp.where(lo, 0.0, 1.0).astype(v.dtype)
    top = jnp.concatenate([jnp.where(lo, v, zero), ones_a], axis=1)
    bottom = jnp.concatenate([jnp.where(lo, zero, v), ones_b], axis=1)
    return jnp.concatenate([top, bottom], axis=0)


def _merge_pair(a, b, lo):
    return jnp.where(lo, a, b), pltpu.roll(jnp.where(lo, b, a), HEAD_DIM, 1)


def _subln_pair(o, g2, lam_init, lo):
    sq = o * o
    ms_a = jnp.sum(jnp.where(lo, sq, 0.0), axis=-1, keepdims=True)
    ms_b = jnp.sum(jnp.where(lo, 0.0, sq), axis=-1, keepdims=True)
    ms = jnp.where(lo, ms_a, ms_b) * (1.0 / HEAD_DIM)
    return o * lax.rsqrt(ms + NORM_EPS) * g2 * (1.0 - lam_init)


def _diff_quarters(q, n):
    quarter = lax.broadcasted_iota(jnp.int32, (n, 128), 1) // DIFF_QK_DIM
    zero = jnp.zeros_like(q)
    return [jnp.where(quarter == i, q, zero) for i in range(4)]


def _attention_pipeline(units, score_phase, softmax_phase, value_phase, on_result=None):
    res, scores, weights = [], {}, {}
    n = len(units)
    for i in range(n + 2):
        if i < n:
            scores[i] = score_phase(units[i])
        if 0 <= i - 1 < n:
            weights[i - 1] = softmax_phase(units[i - 1], scores.pop(i - 1))
        if 0 <= i - 2 < n:
            res.append(value_phase(units[i - 2], weights.pop(i - 2)))
            if on_result is not None:
                on_result(res)
    return res


def _fourier(x_bf, cl, sl, cbd, sbd, wf):
    xc = _dot(x_bf, cbd).astype(BF)
    xs = _dot(x_bf, sbd).astype(BF)
    y = _dot(cl, xc) + _dot(sl, -xs)
    return _dot(y.astype(BF), wf)


CTX_REQ_PER_STEP = 4


def _ctx_mix_kernel(lam_init, with_ada, sink_ref, z_ref, lamp_ref, subg_ref, cl_ref, sl_ref, cbd_ref, sbd_ref,
                    wf_ref, *rest):
    if with_ada:
        cond_ref, wa_ref, ba_ref, o_ref, mods_ref = rest
        _ada_kernel(cond_ref, wa_ref, ba_ref, mods_ref)
    else:
        o_ref, = rest
    lo = _lane_lo(SEQ)
    z_refs = [z_ref.at[pl.ds(r * SEQ, SEQ)] for r in range(CTX_REQ_PER_STEP)]
    o_refs = [o_ref.at[pl.ds(r * SEQ, SEQ)] for r in range(CTX_REQ_PER_STEP)]

    pairs = []
    for zr in z_refs:
        for j in range(2):
            c = 128 * j
            qs = _split_pair(zr[:, C_NA_Q + c:C_NA_Q + c + 128], lo)
            pairs.append((zr, qs[0], qs[1], C_NA_K + c, _v_stack(zr[:, C_NA_V + c:C_NA_V + c + 128], lo), None))
        for j in range(2):
            c = 128 * j
            q4 = _diff_quarters(zr[:, C_DQ + c:C_DQ + c + 128], SEQ)
            v2 = _v_stack(zr[:, C_DV + c:C_DV + c + 128], lo)
            pairs.append((zr, q4[0], q4[2], C_DK + c, v2, None))
            pairs.append((zr, q4[1], q4[3], C_DK + c, v2, None))
        for g in range(2):
            c = 128 * g
            qs = _split_pair(zr[:, C_SQ + c:C_SQ + c + 128], lo)
            pairs.append((zr, qs[0], qs[1], ZB_SK + c, _v_stack(zr[:, ZB_SV + c:ZB_SV + c + 128], lo),
                          (sink_ref[2 * g], sink_ref[2 * g + 1])))

    def score_phase(pair):
        zr, qa, qb, kc, _, _ = pair
        return [_dot_nt(q, zr[:, kc:kc + 128]) for q in (qa, qb)]

    def softmax_phase(pair, scores):
        sinks = pair[5]
        es, xs = [], []
        for i, s in enumerate(scores):
            m = _rowmax(s)
            if sinks is not None:
                m = jnp.maximum(m, sinks[i])
                xs.append(jnp.exp(sinks[i] - m))
            es.append(jnp.exp(s - m).astype(BF))
        return jnp.concatenate(es, axis=1), xs

    def value_phase(pair, weights):
        e2, xs = weights
        r = _dot(e2, pair[4])
        num, den = r[:, 0:128], r[:, 128:256]
        if xs:
            den = den + jnp.where(lo, xs[0], xs[1])
        return num / den

    lam = _diff_lambda(lamp_ref, lam_init)

    def write_request(zr, out, rr):
        for j in range(2):
            out[:, 128 * j:128 * j + 128] = rr[j].astype(BF)
        for j in range(2):
            o = rr[2 + 2 * j] - lam * rr[3 + 2 * j]
            out[:, 256 + 128 * j:384 + 128 * j] = _subln_pair(o, subg_ref[...], lam_init, lo).astype(BF)
        o_c = _fourier(zr[:, C_FC:C_FC + 256], cl_ref[...], sl_ref[...], cbd_ref[...], sbd_ref[...], wf_ref[...])
        out[:, 512:768] = o_c.astype(BF)
        for g in range(2):
            out[:, 768 + 128 * g:896 + 128 * g] = rr[6 + g].astype(BF)

    def on_result(res):
        if len(res) % 8 == 0:
            r = len(res) // 8 - 1
            write_request(z_refs[r], o_refs[r], res[8 * r:8 * r + 8])

    _attention_pipeline(pairs, score_phase, softmax_phase, value_phase, on_result)


def _ctx_mix(zb, sink, lamp, subg, cl, sl, cbd, sbd, wf, lam_init, ada=None):
    full = lambda shape: pl.BlockSpec(shape, lambda b, s: (0,) * len(shape))
    n_steps = BATCH // CTX_REQ_PER_STEP
    in_specs = [
        pl.BlockSpec((CTX_REQ_PER_STEP * SEQ, ZB_WIDTH), lambda b, s: (b, 0)),
        full((4, DIFF_QK_DIM)),
        full((1, 2 * HEAD_DIM)),
        full((SEQ, SEQ)), full((SEQ, SEQ)),
        full((256, 256)), full((256, 256)), full((256, 256)),
    ]
    out_specs = [pl.BlockSpec((CTX_REQ_PER_STEP * SEQ, MIX_WIDTH), lambda b, s: (b, 0))]
    out_shape = [jax.ShapeDtypeStruct((N_CTX_TOK, MIX_WIDTH), BF)]
    args = [sink, zb, lamp, subg, cl, sl, cbd, sbd, wf]
    if ada is not None:
        cond, w_ada, b_ada3, layer = ada
        ada_in, ada_out = _ada_specs(layer, n_steps, lambda b, s: b)
        in_specs += ada_in
        out_specs.append(ada_out)
        out_shape.append(jax.ShapeDtypeStruct((MOD_ROWS, 6 * D_MODEL), F32))
        args += [cond, w_ada, b_ada3]
    grid_spec = pltpu.PrefetchScalarGridSpec(
        num_scalar_prefetch=1, grid=(n_steps,), in_specs=in_specs, out_specs=out_specs)
    outs = pl.pallas_call(
        functools.partial(_ctx_mix_kernel, lam_init, ada is not None),
        grid_spec=grid_spec,
        out_shape=out_shape,
        compiler_params=_params(),
        name="ctx_mix",
    )(*args)
    return outs[0], (outs[1] if ada is not None else None)


def _cache_v_ones(cv_ref, cva, cvb):
    first = lax.broadcasted_iota(jnp.int32, (256, PAST_LEN), 0) % 128 < HEAD_DIM
    cv = cv_ref[...]
    cva[...] = jnp.where(first, cv, 1.0).astype(BF)
    cvb[...] = jnp.where(first, 1.0, cv).astype(BF)


def _na_bias_tiles(rpp_ref, t2):
    c = lax.broadcasted_iota(jnp.int32, (GRID_W, 128), 0)
    kc = lax.broadcasted_iota(jnp.int32, (GRID_W, 128), 1) % GRID_W
    c_start = jnp.clip(c - NA_KW // 2, 0, GRID_W - NA_KW)
    inside = jnp.logical_and(kc >= c_start, kc < c_start + NA_KW)
    for h in range(4):
        for dr in range(2 * NA_KH - 1):
            row = jnp.broadcast_to(rpp_ref[h, dr:dr + 1, :], (GRID_W, 128))
            toeplitz = pltpu.roll(row, 128 - (NA_KW - 1), 1, stride=1, stride_axis=0)
            t2[h, dr] = jnp.where(inside, toeplitz, NEG_INF)
        t2[h, 2 * NA_KH - 1] = jnp.full((GRID_W, 128), NEG_INF, F32)


def _lat_na_part(z_ref, ck_ref, cv_ref, t2, o_ref, ckb, cva, cvb):
    ckb[...] = ck_ref[...].astype(BF)
    _cache_v_ones(cv_ref, cva, cvb)
    n_q = NA_QROWS * GRID_W
    n_loc = NA_WIN_ROWS * GRID_W
    lo_q = _lane_lo(n_q)
    lo_w = _lane_lo(n_loc)

    lo_t = _lane_lo(GRID_W)

    def block(p):
        w_row = jnp.clip(NA_QROWS * p - NA_KH // 2, 0, GRID_ROWS - NA_WIN_ROWS)
        q0 = pl.multiple_of(p * n_q, n_q)
        k0 = pl.multiple_of(w_row * GRID_W, NA_QROWS * GRID_W)

        def tile_index(r, j):
            r_start = jnp.clip(r - NA_KH // 2, 0, GRID_ROWS - NA_KH)
            kr = w_row + j
            inside = jnp.logical_and(kr >= r_start, kr < r_start + NA_KH)
            return jnp.where(inside, kr - r + NA_KH - 1, 2 * NA_KH - 1)

        def bias(h):
            rows = []
            for rl in range(NA_QROWS):
                r = NA_QROWS * p + rl
                pieces = [jnp.where(lo_t, t2[h, tile_index(r, 2 * jj)], t2[h, tile_index(r, 2 * jj + 1)])
                          for jj in range(NA_WIN_ROWS // 2)]
                rows.append(jnp.concatenate(pieces, axis=1))
            return jnp.concatenate(rows, axis=0)

        return q0, k0, bias

    def score_phase(unit):
        q0, k0, bias, h = unit
        c = 128 * (h // 2)
        q = _split_pair(z_ref[pl.ds(q0, n_q), C_NA_Q + c:C_NA_Q + c + 128], lo_q)[h % 2]
        s_loc = _dot_nt(q, z_ref[pl.ds(k0, n_loc), C_NA_K + c:C_NA_K + c + 128]) + bias(h)
        return s_loc, _dot(q, ckb[c:c + 128, :])

    def softmax_phase(unit, scores):
        s_loc, s_ctx = scores
        m = _rowmax(s_loc, s_ctx)
        return jnp.exp(s_loc - m).astype(BF), jnp.exp(s_ctx - m).astype(BF)

    def value_phase(unit, weights):
        q0, k0, bias, h = unit
        e_loc, e_ctx = weights
        c = 128 * (h // 2)
        v = _v_ones_pair(z_ref[pl.ds(k0, n_loc), C_NA_V + c:C_NA_V + c + 128], lo_w)[h % 2]
        return _dot(e_loc, v) + _dot_nt(e_ctx, (cva, cvb)[h % 2][c:c + 128, :])

    def body(i, carry):
        blocks = [block(NA_BLOCKS_PER_TRIP * i + b) for b in range(NA_BLOCKS_PER_TRIP)]
        units = [(q0, k0, bias, h) for q0, k0, bias in blocks for h in range(4)]
        res = _attention_pipeline(units, score_phase, softmax_phase, value_phase)
        for b, (q0, _, _) in enumerate(blocks):
            for j in range(2):
                num, den = _merge_pair(res[4 * b + 2 * j], res[4 * b + 2 * j + 1], lo_q)
                o_ref[pl.ds(q0, n_q), 128 * j:128 * j + 128] = (num / den).astype(BF)
        return carry

    lax.fori_loop(0, GRID_ROWS // NA_QROWS // NA_BLOCKS_PER_TRIP, body, 0)


DIFF_QBLK = 1024
DIFF_GROUP = 2


def _lat_diff_part(lam_init, z_ref, ck_ref, cv_ref, lamp_ref, subg_ref, o_ref, ckb, cva, cvb, vla, vlb):
    ckb[...] = ck_ref[...].astype(BF)
    _cache_v_ones(cv_ref, cva, cvb)
    lo_v = _lane_lo_wide(DEC_SEQ, 256)
    v_loc = z_ref[:, C_DV:C_DV + 256]
    one = jnp.ones_like(v_loc)
    vla[...] = jnp.where(lo_v, v_loc, one)
    vlb[...] = jnp.where(lo_v, one, v_loc)
    lam = _diff_lambda(lamp_ref, lam_init)
    qblk = DIFF_QBLK
    lo_q = _lane_lo(qblk)

    def body(i, carry):
        q0 = pl.multiple_of(i * qblk, qblk)
        for j in range(2):
            c = 128 * j
            q4 = _diff_quarters(z_ref[pl.ds(q0, qblk), C_DQ + c:C_DQ + c + 128], qblk)
            res = []
            for t0 in range(0, 4, DIFF_GROUP):
                scores = [(_dot_nt(q, z_ref[:, C_DK + c:C_DK + c + 128]), _dot(q, ckb[c:c + 128, :]))
                          for q in q4[t0:t0 + DIFF_GROUP]]
                weights = []
                for s_loc, s_ctx in scores:
                    m = _rowmax(s_loc, s_ctx)
                    weights.append((jnp.exp(s_loc - m).astype(BF), jnp.exp(s_ctx - m).astype(BF)))
                for t, (e_loc, e_ctx) in enumerate(weights, start=t0):
                    v_loc, v_ctx = ((vla, cva), (vlb, cvb))[t // 2]
                    res.append(_dot(e_loc, v_loc[:, c:c + 128]) + _dot_nt(e_ctx, v_ctx[c:c + 128, :]))
            n1, d1 = _merge_pair(res[0], res[2], lo_q)
            n2, d2 = _merge_pair(res[1], res[3], lo_q)
            o = n1 / d1 - lam * (n2 / d2)
            o_ref[pl.ds(q0, qblk), 256 + c:256 + c + 128] = (
                _subln_pair(o, subg_ref[...], lam_init, lo_q).astype(BF))
        return carry

    lax.fori_loop(0, DEC_SEQ // qblk, body, 0)


SWA_BLOCKS_PER_TRIP = 4


def _lat_swa_part(sink_ref, z_ref, ck_ref, cv_ref, o_ref, ckd, cva, cvb):
    W = SWA_WINDOW
    n_win = 3 * W
    ones = jnp.ones((HEAD_DIM, PAST_LEN), BF)
    for g in range(2):
        k_g = ck_ref[HEAD_DIM * g:HEAD_DIM * (g + 1), :].astype(BF)
        v_g = cv_ref[HEAD_DIM * g:HEAD_DIM * (g + 1), :].astype(BF)
        r0, r1, r2 = 128 * g, 128 * g + HEAD_DIM, 128 * (g + 1)
        ckd[r0:r1, :] = k_g
        ckd[r1:r2, :] = k_g
        cva[r0:r1, :] = v_g
        cva[r1:r2, :] = ones
        cvb[r0:r1, :] = ones
        cvb[r1:r2, :] = v_g
    lo_q = _lane_lo(W)
    lo_w = _lane_lo(n_win)

    def block(n):
        q0 = pl.multiple_of(n * W, W)
        w0 = pl.multiple_of(jnp.clip((n - 1) * W, 0, DEC_SEQ - n_win), W)
        qpos = q0 + lax.broadcasted_iota(jnp.int32, (W, n_win), 0)
        kpos = w0 + lax.broadcasted_iota(jnp.int32, (W, n_win), 1)
        return q0, w0, jnp.abs(qpos - kpos) <= W

    def score_phase(unit):
        q0, w0, valid, h = unit
        c = 128 * (h // 2)
        q = _split_pair(z_ref[pl.ds(q0, W), C_SQ + c:C_SQ + c + 128], lo_q)[h % 2]
        s_loc = jnp.where(valid, _dot_nt(q, z_ref[pl.ds(w0, n_win), ZB_SK + c:ZB_SK + c + 128]), NEG_INF)
        return s_loc, _dot(q, ckd[c:c + 128, :])

    def softmax_phase(unit, scores):
        s_loc, s_ctx = scores
        sink = sink_ref[unit[3]]
        m = jnp.maximum(_rowmax(s_loc, s_ctx), sink)
        return jnp.exp(s_loc - m).astype(BF), jnp.exp(s_ctx - m).astype(BF), jnp.exp(sink - m)

    def value_phase(unit, weights):
        q0, w0, valid, h = unit
        e_loc, e_ctx, extra = weights
        c = 128 * (h // 2)
        v = _v_ones_pair(z_ref[pl.ds(w0, n_win), ZB_SV + c:ZB_SV + c + 128], lo_w)[h % 2]
        return _dot(e_loc, v) + _dot_nt(e_ctx, (cva, cvb)[h % 2][c:c + 128, :]), extra

    def body(i, carry):
        blocks = [block(SWA_BLOCKS_PER_TRIP * i + b) for b in range(SWA_BLOCKS_PER_TRIP)]
        units = [(q0, w0, valid, h) for q0, w0, valid in blocks for h in range(4)]
        res = _attention_pipeline(units, score_phase, softmax_phase, value_phase)
        for b, (q0, _, _) in enumerate(blocks):
            for g in range(2):
                (r_a, x_a), (r_b, x_b) = res[4 * b + 2 * g], res[4 * b + 2 * g + 1]
                num, den = _merge_pair(r_a, r_b, lo_q)
                den = den + jnp.where(lo_q, x_a, x_b)
                o_ref[pl.ds(q0, W), 768 + 128 * g:768 + 128 * g + 128] = (num / den).astype(BF)
        return carry

    lax.fori_loop(0, DEC_SEQ // W // SWA_BLOCKS_PER_TRIP, body, 0)


def _lat_mix_kernel(lam_init, sink_ref, z_ref, nak_ref, nav_ref, dfk_ref, dfv_ref, swk_ref, swv_ref, rpp_ref,
                    lamp_ref, subg_ref, cl_ref, sl_ref, cbd_ref, sbd_ref, wf_ref, o_ref,
                    ck, cva, cvb, vla, vlb, t2):
    @pl.when(pl.program_id(0) == 0)
    def _():
        _na_bias_tiles(rpp_ref, t2)

    _lat_na_part(z_ref, nak_ref, nav_ref, t2, o_ref, ck, cva, cvb)
    _lat_diff_part(lam_init, z_ref, dfk_ref, dfv_ref, lamp_ref, subg_ref, o_ref, ck, cva, cvb, vla, vlb)
    o_c = _fourier(z_ref[:, C_FC:C_FC + 256], cl_ref[...], sl_ref[...], cbd_ref[...], sbd_ref[...], wf_ref[...])
    o_ref[:, 512:768] = o_c.astype(BF)
    _lat_swa_part(sink_ref, z_ref, swk_ref, swv_ref, o_ref, ck, cva, cvb)


def _lat_mix(zb, caches_in, rpp, sink, lamp, subg, cl, sl, cbd, sbd, wf, lam_init, layer):
    full = lambda shape: pl.BlockSpec(shape, lambda b, s: (0,) * len(shape), pipeline_mode=pl.Buffered(1))
    cache = lambda width: pl.BlockSpec((None, None, width, PAST_LEN), lambda b, s: (b, layer, 0, 0))
    grid_spec = pltpu.PrefetchScalarGridSpec(
        num_scalar_prefetch=1,
        grid=(DEC_BATCH,),
        in_specs=[
            pl.BlockSpec((DEC_SEQ, ZB_WIDTH), lambda b, s: (LAT_BLOCK0 + b, 0)),
            cache(256), cache(256), cache(256), cache(256), cache(128), cache(128),
            pl.BlockSpec((None, 4, 2 * NA_KH, 128), lambda b, s: (layer, 0, 0, 0), pipeline_mode=pl.Buffered(1)),
            full((4, DIFF_QK_DIM)), full((1, 2 * HEAD_DIM)),
            full((DEC_SEQ, DEC_SEQ)), full((DEC_SEQ, DEC_SEQ)),
            full((256, 256)), full((256, 256)), full((256, 256)),
        ],
        out_specs=pl.BlockSpec((DEC_SEQ, MIX_WIDTH), lambda b, s: (b, 0)),
        scratch_shapes=[pltpu.VMEM((256, PAST_LEN), BF)] * 3 + [pltpu.VMEM((DEC_SEQ, 256), BF)] * 2
        + [pltpu.VMEM((4, 2 * NA_KH, GRID_W, 128), F32)],
    )
    return pl.pallas_call(
        functools.partial(_lat_mix_kernel, lam_init),
        grid_spec=grid_spec,
        out_shape=jax.ShapeDtypeStruct((N_LAT_TOK, MIX_WIDTH), BF),
        compiler_params=_params(),
        name="lat_mix",
    )(sink, zb, *caches_in, rpp, lamp, subg, cl, sl, cbd, sbd, wf)


def _dft_tables(n):
    j = np.arange(n)
    ang = 2.0 * np.pi * ((j[:, None] * j[None, :]) % n) / n
    return np.cos(ang) / np.sqrt(n), np.sin(ang) / np.sqrt(n)


def _block_diag4(m):
    out = np.zeros((256, 256), m.dtype)
    for g in range(4):
        out[64 * g:64 * g + 64, 64 * g:64 * g + 64] = m
    return out


def _rope_tables(n_axis_dims):
    half = n_axis_dims // 2
    inv = ROPE_BASE ** (-np.arange(half, dtype=np.float64) / half)
    t = np.arange(DEC_SEQ)
    lane = np.arange(128)
    w = lane % n_axis_dims
    is_col = (lane // n_axis_dims) % 2 == 1
    pos = np.where(is_col[None, :], (t % GRID_W)[:, None], (t // GRID_W)[:, None]).astype(np.float64)
    ang = pos * inv[w % half][None, :]
    sign = np.where(w < half, -1.0, 1.0)[None, :]
    cos = np.concatenate([np.cos(ang), np.ones((ROW_TILE, 128))], axis=0)
    sin = np.concatenate([np.sin(ang) * sign, np.zeros((ROW_TILE, 128))], axis=0)
    return jnp.asarray(cos, F32), jnp.asarray(sin, F32)


def _pad_rpb_rows(rpb):
    n_dc = rpb.shape[-1]
    half = jnp.pad(rpb, ((0, 0), (0, 0), (0, 1), (0, GRID_W - n_dc)))
    return jnp.concatenate([half, half], axis=-1)


def kernel(x_prompt, x_sample, cache_na_k, cache_na_v, cache_diff_k, cache_diff_v, cache_swa_k, cache_swa_v, c, c_ctx, w_ada, b_ada, norm1_g, norm2_g, w_in, na_rpb, diff_lq1, diff_lk1, diff_lq2, diff_lk2, diff_subln_g, w_fourier, swa_sink, w_out, w_mlp1, w_mlp2, final_g):
    xp = x_prompt.reshape(N_CTX_TOK, D_MODEL)
    xs = x_sample.reshape(N_LAT_TOK, D_MODEL)
    cond = jnp.zeros((MOD_ROWS, D_MODEL), F32).at[0].set(c_ctx).at[1:1 + DEC_BATCH].set(c)
    b_ada3 = b_ada.reshape(DEPTH, 1, 6 * D_MODEL)
    mods = _ada(cond, w_ada, b_ada3, 0)

    cl_p, sl_p = _dft_tables(SEQ)
    cl_s, sl_s = _dft_tables(DEC_SEQ)
    c64, s64 = _dft_tables(64)
    cl_p, sl_p, cl_s, sl_s, cbd, sbd = (
        jnp.asarray(a, F32).astype(BF) for a in (cl_p, sl_p, cl_s, sl_s, _block_diag4(c64), _block_diag4(s64)))
    ropes = _rope_tables(16) + _rope_tables(32)
    rpp = _pad_rpb_rows(na_rpb)
    wf_bf = w_fourier.astype(BF)

    ck_na, cv_na, ck_df, cv_df, ck_sw, cv_sw = (
        a.transpose(0, 1, 3, 4, 2).reshape(DEC_BATCH, DEPTH, -1, PAST_LEN)
        for a in (cache_na_k, cache_na_v, cache_diff_k, cache_diff_v, cache_swa_k, cache_swa_v))

    _, _, zb, caches = _trunk(xp, xs, ropes, proj=(0, mods, norm1_g[0], w_in, None))
    for l in range(DEPTH):
        lam_init = 0.8 - 0.6 * math.exp(-0.3 * l)
        lamp = jnp.stack([diff_lq1[l], diff_lk1[l], diff_lq2[l], diff_lk2[l]], axis=0)
        subg = jnp.tile(diff_subln_g[l].reshape(1, HEAD_DIM), (1, 2))
        has_next = l + 1 < DEPTH

        cat_p, mods_next = _ctx_mix(zb, swa_sink[l], lamp, subg, cl_p, sl_p, cbd, sbd, wf_bf[l], lam_init,
                                    ada=(cond, w_ada, b_ada3, l + 1) if has_next else None)
        cat_s = _lat_mix(zb, (ck_na, cv_na, ck_df, cv_df, ck_sw, cv_sw), rpp, swa_sink[l], lamp, subg,
                         cl_s, sl_s, cbd, sbd, wf_bf[l], lam_init, l)
        nxt = (l + 1, mods_next, norm1_g[l + 1], w_in, caches) if has_next else None
        xp, xs, zb, new_caches = _trunk(
            xp, xs, ropes, mlp=(l, mods, cat_p, cat_s, norm2_g[l], final_g, w_out, w_mlp1, w_mlp2), proj=nxt)
        caches = new_caches if has_next else caches
        mods = mods_next

    y_prompt = xp.reshape(BATCH, SEQ, D_MODEL)
    y_sample = xs.reshape(DEC_BATCH, DEC_SEQ, D_MODEL)
    new = [a.reshape(BATCH, DEPTH, SEQ, a.shape[-1] // HEAD_DIM, HEAD_DIM) for a in caches]
    return (y_prompt, y_sample) + tuple(new)
```
